```python
import math
import jax, jax.numpy as jnp
from jax import lax
import numpy as np

D_MODEL = 1024
BATCH = 8
SEQ = 4096
DEPTH = 1

D_MIX = D_MODEL
D_POOL = D_MIX // 2
D_ATTN = D_MIX - D_POOL
POOL_WINDOWS = (2, 4, 8, 16)
N_POOL_GROUPS = len(POOL_WINDOWS)
POOL_GROUP_DIM = D_POOL // N_POOL_GROUPS
HEAD_DIM = 64
N_HEADS = D_ATTN // HEAD_DIM
Q_BLOCK = 128
D_FF = 4 * D_MODEL
D_IN_PROJ = D_POOL + 3 * D_ATTN
EPS = 1e-6

kernel_name = "hymba_pool_stickbreak_block"


def rmsnorm(x, g):
    xf = x.astype(jnp.float32)
    r = lax.rsqrt(jnp.mean(xf * xf, axis=-1, keepdims=True) + EPS)
    return (xf * r * g.astype(jnp.float32)).astype(x.dtype)


def pool_mixer(u, pool_w, pool_scale):
    B, S, _ = u.shape
    ug = u.reshape(B, S, N_POOL_GROUPS, POOL_GROUP_DIM)
    pos = jnp.arange(S, dtype=jnp.int32)
    outs = []
    for g, w in enumerate(POOL_WINDOWS):
        xg = ug[:, :, g, :].astype(jnp.float32)
        cs = jnp.cumsum(xg, axis=1)
        cs_lag = jnp.pad(cs, ((0, 0), (w, 0), (0, 0)))[:, :S]
        count = jnp.minimum(pos + 1, w).astype(jnp.float32)[None, :, None]
        mean = (cs - cs_lag) / count
        outs.append(mean - xg)
    pooled = jnp.stack(outs, axis=2)
    mapped = jnp.einsum('bsgc,gcd->bsgd', pooled, pool_w.astype(jnp.float32))
    y = mapped.reshape(B, S, D_POOL) * pool_scale.astype(jnp.float32)
    return y.astype(u.dtype)


def stick_breaking_attention(q, k, v):
    B, H, S, Dh = q.shape
    scale = 1.0 / math.sqrt(Dh)
    n_blocks = S // Q_BLOCK
    outs = []
    for i in range(n_blocks):
        q0, end = i * Q_BLOCK, (i + 1) * Q_BLOCK
        qb = q[:, :, q0:end]
        kb = k[:, :, :end]
        vb = v[:, :, :end]
        z = jnp.einsum('bhqd,bhkd->bhqk', qb, kb).astype(jnp.float32) * scale
        t_idx = q0 + jnp.arange(Q_BLOCK, dtype=jnp.int32)
        s_idx = jnp.arange(end, dtype=jnp.int32)
        mask = s_idx[None, :] < t_idx[:, None]
        log1m = jnp.where(mask, jax.nn.log_sigmoid(-z), 0.0)
        tail = lax.cumsum(log1m, axis=3, reverse=True) - log1m
        log_a = jax.nn.log_sigmoid(z) + tail
        a = jnp.where(mask, jnp.exp(log_a), 0.0)
        ob = jnp.einsum('bhqk,bhkd->bhqd', a, vb.astype(jnp.float32))
        outs.append(ob)
    o = jnp.concatenate(outs, axis=2)
    return o.astype(q.dtype)


def _fwd_setup_inputs(seed: int = 0) -> dict:
    key = jax.random.key(seed)
    ks = jax.random.split(key, 12)
    f32 = jnp.float32
    x = jax.random.normal(ks[0], (BATCH, SEQ, D_MODEL), f32)
    norm1_g = 1.0 + 0.02 * jax.random.normal(ks[1], (D_MODEL,), f32)
    w_in = jax.random.normal(ks[2], (D_MODEL, D_IN_PROJ), f32) * D_MODEL ** -0.5
    pool_w = jax.random.normal(ks[3], (N_POOL_GROUPS, POOL_GROUP_DIM, POOL_GROUP_DIM), f32) * POOL_GROUP_DIM ** -0.5
    pool_scale = 0.5 + 0.02 * jax.random.normal(ks[4], (D_POOL,), f32)
    pool_out_g = 1.0 + 0.02 * jax.random.normal(ks[5], (D_POOL,), f32)
    attn_out_g = 1.0 + 0.02 * jax.random.normal(ks[6], (D_ATTN,), f32)
    w_out = jax.random.normal(ks[7], (D_MIX, D_MODEL), f32) * D_MIX ** -0.5
    norm2_g = 1.0 + 0.02 * jax.random.normal(ks[8], (D_MODEL,), f32)
    w_up = jax.random.normal(ks[9], (D_MODEL, D_FF), f32) * D_MODEL ** -0.5
    w_down = jax.random.normal(ks[10], (D_FF, D_MODEL), f32) * D_FF ** -0.5
    final_g = 1.0 + 0.02 * jax.random.normal(ks[11], (D_MODEL,), f32)
    return {"x": x, "norm1_g": norm1_g, "w_in": w_in, "pool_w": pool_w,
            "pool_scale": pool_scale, "pool_out_g": pool_out_g, "attn_out_g": attn_out_g,
            "w_out": w_out, "norm2_g": norm2_g, "w_up": w_up, "w_down": w_down,
            "final_g": final_g}


def _fwd_reference(x, norm1_g, w_in, pool_w, pool_scale, pool_out_g, attn_out_g,
              w_out, norm2_g, w_up, w_down, final_g):
    B, S, _ = x.shape
    h = x
    for _ in range(DEPTH):
        hn = rmsnorm(h, norm1_g)
        proj = jnp.einsum('bsd,de->bse', hn, w_in)
        u_pool = proj[..., :D_POOL]
        q = proj[..., D_POOL:D_POOL + D_ATTN]
        k = proj[..., D_POOL + D_ATTN:D_POOL + 2 * D_ATTN]
        v = proj[..., D_POOL + 2 * D_ATTN:]
        to_heads = lambda t: t.reshape(B, S, N_HEADS, HEAD_DIM).transpose(0, 2, 1, 3)
        y_pool = pool_mixer(u_pool, pool_w, pool_scale)
        o = stick_breaking_attention(to_heads(q), to_heads(k), to_heads(v))
        y_attn = o.transpose(0, 2, 1, 3).reshape(B, S, D_ATTN)
        mixed = jnp.concatenate([rmsnorm(y_pool, pool_out_g),
                                 rmsnorm(y_attn, attn_out_g)], axis=-1)
        h = h + jnp.einsum('bse,ed->bsd', mixed, w_out)
        hn2 = rmsnorm(h, norm2_g)
        up = jnp.einsum('bsd,df->bsf', hn2, w_up)
        act = jnp.square(jax.nn.relu(up))
        h = h + jnp.einsum('bsf,fd->bsd', act, w_down)
    return rmsnorm(h, final_g)


import jax as _jax
import jax.numpy as _jnp

TWIN_FORMAT = 'train_step'
FWD_PARAMS = ['x', 'norm1_g', 'w_in', 'pool_w', 'pool_scale', 'pool_out_g', 'attn_out_g', 'w_out', 'norm2_g', 'w_up', 'w_down', 'final_g']
TWIN_WEIGHTS = ['norm1_g', 'w_in', 'pool_w', 'pool_scale', 'pool_out_g', 'attn_out_g', 'w_out', 'norm2_g', 'w_up', 'w_down', 'final_g']
TWIN_DIFF_INPUT = 'x'
TWIN_INPUTS = ['x', 'norm1_g', 'w_in', 'pool_w', 'pool_scale', 'pool_out_g', 'attn_out_g', 'w_out', 'norm2_g', 'w_up', 'w_down', 'final_g', 'loss_target', 'm_norm1_g', 'm_w_in', 'm_pool_w', 'm_pool_scale', 'm_pool_out_g', 'm_attn_out_g', 'm_w_out', 'm_norm2_g', 'm_w_up', 'm_w_down', 'm_final_g', 'v_norm1_g', 'v_w_in', 'v_pool_w', 'v_pool_scale', 'v_pool_out_g', 'v_attn_out_g', 'v_w_out', 'v_norm2_g', 'v_w_up', 'v_w_down', 'v_final_g']
TWIN_OUTPUTS = ['loss', 'grad_x', 'grad_norm1_g', 'grad_w_in', 'grad_pool_w', 'grad_pool_scale', 'grad_pool_out_g', 'grad_attn_out_g', 'grad_w_out', 'grad_norm2_g', 'grad_w_up', 'grad_w_down', 'grad_final_g', 'delta_norm1_g', 'delta_w_in', 'delta_pool_w', 'delta_pool_scale', 'delta_pool_out_g', 'delta_attn_out_g', 'delta_w_out', 'delta_norm2_g', 'delta_w_up', 'delta_w_down', 'delta_final_g', 'new_m_norm1_g', 'new_m_w_in', 'new_m_pool_w', 'new_m_pool_scale', 'new_m_pool_out_g', 'new_m_attn_out_g', 'new_m_w_out', 'new_m_norm2_g', 'new_m_w_up', 'new_m_w_down', 'new_m_final_g', 'new_v_norm1_g', 'new_v_w_in', 'new_v_pool_w', 'new_v_pool_scale', 'new_v_pool_out_g', 'new_v_attn_out_g', 'new_v_w_out', 'new_v_norm2_g', 'new_v_w_up', 'new_v_w_down', 'new_v_final_g']
TWIN_LEAF_KINDS = {'loss': 'loss', 'grad_x': 'grad_x', 'grad_norm1_g': 'grad_w', 'grad_w_in': 'grad_w', 'grad_pool_w': 'grad_w', 'grad_pool_scale': 'grad_w', 'grad_pool_out_g': 'grad_w', 'grad_attn_out_g': 'grad_w', 'grad_w_out': 'grad_w', 'grad_norm2_g': 'grad_w', 'grad_w_up': 'grad_w', 'grad_w_down': 'grad_w', 'grad_final_g': 'grad_w', 'delta_norm1_g': 'delta_w', 'delta_w_in': 'delta_w', 'delta_pool_w': 'delta_w', 'delta_pool_scale': 'delta_w', 'delta_pool_out_g': 'delta_w', 'delta_attn_out_g': 'delta_w', 'delta_w_out': 'delta_w', 'delta_norm2_g': 'delta_w', 'delta_w_up': 'delta_w', 'delta_w_down': 'delta_w', 'delta_final_g': 'delta_w', 'new_m_norm1_g': 'new_m', 'new_m_w_in': 'new_m', 'new_m_pool_w': 'new_m', 'new_m_pool_scale': 'new_m', 'new_m_pool_out_g': 'new_m', 'new_m_attn_out_g': 'new_m', 'new_m_w_out': 'new_m', 'new_m_norm2_g': 'new_m', 'new_m_w_up': 'new_m', 'new_m_w_down': 'new_m', 'new_m_final_g': 'new_m', 'new_v_norm1_g': 'new_v', 'new_v_w_in': 'new_v', 'new_v_pool_w': 'new_v', 'new_v_pool_scale': 'new_v', 'new_v_pool_out_g': 'new_v', 'new_v_attn_out_g': 'new_v', 'new_v_w_out': 'new_v', 'new_v_norm2_g': 'new_v', 'new_v_w_up': 'new_v', 'new_v_w_down': 'new_v', 'new_v_final_g': 'new_v'}


def _forward(args):
    return _fwd_reference(*[args[k] for k in FWD_PARAMS])


def _output_shape():
    def fwd():
        inp = _fwd_setup_inputs(0)
        return _fwd_reference(*[inp[k] for k in FWD_PARAMS])
    out = _jax.eval_shape(fwd)
    return out.shape, out.dtype

N_MICROBATCH = 1
ADAM_LR = 0.001
ADAM_B1 = 0.9
ADAM_B2 = 0.999
ADAM_EPS = 1e-08
ADAM_WD = 0.01
ADAM_STEP = 10
PER_EXAMPLE_BATCH_AXIS = {'x': 0, 'loss_target': 0}
SHARED_INPUTS = []
_WEIGHT_DTYPES = {'norm1_g': _jnp.float32, 'w_in': _jnp.float32, 'pool_w': _jnp.float32, 'pool_scale': _jnp.float32, 'pool_out_g': _jnp.float32, 'attn_out_g': _jnp.float32, 'w_out': _jnp.float32, 'norm2_g': _jnp.float32, 'w_up': _jnp.float32, 'w_down': _jnp.float32, 'final_g': _jnp.float32}
MOMENT_SCALE = {'norm1_g': 1.566351e-01, 'w_in': 1.048203e-01, 'pool_w': 1.349112e-01, 'pool_scale': 3.096212e-01, 'pool_out_g': 1.541438e-01, 'attn_out_g': 1.396068e-01, 'w_out': 1.355630e-01, 'norm2_g': 1.381348e-01, 'w_up': 6.747298e-02, 'w_down': 1.352843e-01, 'final_g': 3.218213e+01}


def _to_microbatches(a, axis):
    t = _jnp.moveaxis(a, axis, 0)
    t = t.reshape((N_MICROBATCH, t.shape[0] // N_MICROBATCH) + t.shape[1:])
    return _jnp.moveaxis(t, 1, axis + 1)


def setup_inputs(seed: int = 0) -> dict:
    inp = _fwd_setup_inputs(seed)
    key = _jax.random.fold_in(_jax.random.key(seed), 7919)
    shape, _ = _output_shape()
    out = dict(inp)
    out["loss_target"] = _jax.random.normal(_jax.random.fold_in(key, 0), shape, _jnp.float32)
    for i, name in enumerate(TWIN_WEIGHTS):
        w = inp[name].astype(_jnp.float32)
        if MOMENT_SCALE is None:
            s = _jnp.sqrt(_jnp.mean(_jnp.square(w)) + 1e-30)
        else:
            s = MOMENT_SCALE[name]
        km, kv = _jax.random.split(_jax.random.fold_in(key, i + 1))
        out[name] = w
        out["m_" + name] = s * _jax.random.normal(km, w.shape, _jnp.float32)
        out["v_" + name] = (s * s) * _jax.random.uniform(kv, w.shape, _jnp.float32, 0.5, 1.5)
    if N_MICROBATCH > 1:
        for name, axis in PER_EXAMPLE_BATCH_AXIS.items():
            out[name] = _to_microbatches(out[name], axis)
    return {'x': out['x'], 'norm1_g': out['norm1_g'], 'w_in': out['w_in'], 'pool_w': out['pool_w'], 'pool_scale': out['pool_scale'], 'pool_out_g': out['pool_out_g'], 'attn_out_g': out['attn_out_g'], 'w_out': out['w_out'], 'norm2_g': out['norm2_g'], 'w_up': out['w_up'], 'w_down': out['w_down'], 'final_g': out['final_g'], 'loss_target': out['loss_target'], 'm_norm1_g': out['m_norm1_g'], 'm_w_in': out['m_w_in'], 'm_pool_w': out['m_pool_w'], 'm_pool_scale': out['m_pool_scale'], 'm_pool_out_g': out['m_pool_out_g'], 'm_attn_out_g': out['m_attn_out_g'], 'm_w_out': out['m_w_out'], 'm_norm2_g': out['m_norm2_g'], 'm_w_up': out['m_w_up'], 'm_w_down': out['m_w_down'], 'm_final_g': out['m_final_g'], 'v_norm1_g': out['v_norm1_g'], 'v_w_in': out['v_w_in'], 'v_pool_w': out['v_pool_w'], 'v_pool_scale': out['v_pool_scale'], 'v_pool_out_g': out['v_pool_out_g'], 'v_attn_out_g': out['v_attn_out_g'], 'v_w_out': out['v_w_out'], 'v_norm2_g': out['v_norm2_g'], 'v_w_up': out['v_w_up'], 'v_w_down': out['v_w_down'], 'v_final_g': out['v_final_g']}


def _loss(weights, diff, rest, loss_target):
    with _jax.named_scope("forward"):
        args = {**rest, TWIN_DIFF_INPUT: diff, **{k: w.astype(_WEIGHT_DTYPES[k]) for k, w in weights.items()}}
        y = _forward(args)
    with _jax.named_scope("loss_head"):
        err = _jnp.square(y.astype(_jnp.float32) - loss_target)
        return 0.5 * _jnp.sum(_jnp.mean(err, axis=-1)) if err.ndim else 0.5 * err


def _adamw(w, g, m, v):
    m = ADAM_B1 * m + (1.0 - ADAM_B1) * g
    v = ADAM_B2 * v + (1.0 - ADAM_B2) * _jnp.square(g)
    m_hat = m / (1.0 - ADAM_B1 ** ADAM_STEP)
    v_hat = v / (1.0 - ADAM_B2 ** ADAM_STEP)
    delta = -ADAM_LR * (m_hat / (_jnp.sqrt(v_hat) + ADAM_EPS) + ADAM_WD * w)
    return delta, m, v


def reference(x, norm1_g, w_in, pool_w, pool_scale, pool_out_g, attn_out_g, w_out, norm2_g, w_up, w_down, final_g, loss_target, m_norm1_g, m_w_in, m_pool_w, m_pool_scale, m_pool_out_g, m_attn_out_g, m_w_out, m_norm2_g, m_w_up, m_w_down, m_final_g, v_norm1_g, v_w_in, v_pool_w, v_pool_scale, v_pool_out_g, v_attn_out_g, v_w_out, v_norm2_g, v_w_up, v_w_down, v_final_g):
    given = dict(x=x, norm1_g=norm1_g, w_in=w_in, pool_w=pool_w, pool_scale=pool_scale, pool_out_g=pool_out_g, attn_out_g=attn_out_g, w_out=w_out, norm2_g=norm2_g, w_up=w_up, w_down=w_down, final_g=final_g, loss_target=loss_target, m_norm1_g=m_norm1_g, m_w_in=m_w_in, m_pool_w=m_pool_w, m_pool_scale=m_pool_scale, m_pool_out_g=m_pool_out_g, m_attn_out_g=m_attn_out_g, m_w_out=m_w_out, m_norm2_g=m_norm2_g, m_w_up=m_w_up, m_w_down=m_w_down, m_final_g=m_final_g, v_norm1_g=v_norm1_g, v_w_in=v_w_in, v_pool_w=v_pool_w, v_pool_scale=v_pool_scale, v_pool_out_g=v_pool_out_g, v_attn_out_g=v_attn_out_g, v_w_out=v_w_out, v_norm2_g=v_norm2_g, v_w_up=v_w_up, v_w_down=v_w_down, v_final_g=v_final_g)
    weights = {n: given[n] for n in TWIN_WEIGHTS}
    shared = {n: given[n] for n in SHARED_INPUTS}
    per_example = {n: given[n] for n in ['x']}
    grad_fn = _jax.value_and_grad(_loss, argnums=(0, 1))

    def one_microbatch(ex, loss_target):
        ex = dict(ex)
        diff = ex.pop(TWIN_DIFF_INPUT)
        return grad_fn(weights, diff, {**shared, **ex}, loss_target)

    if N_MICROBATCH == 1:
        loss, (grad_w, grad_x) = one_microbatch(per_example, given["loss_target"])
    else:
        def body(carry, xs):
            loss_sum, grad_sum = carry
            l_k, (gw_k, gx_k) = one_microbatch(xs[0], xs[1])
            with _jax.named_scope("update"):
                return (loss_sum + l_k, _jax.tree.map(_jnp.add, grad_sum, gw_k)), gx_k

        init = (_jnp.zeros((), _jnp.float32), _jax.tree.map(_jnp.zeros_like, weights))
        (loss, grad_w), grad_x = _jax.lax.scan(body, init, (per_example, given["loss_target"]))
    with _jax.named_scope("update"):
        delta_w, new_m, new_v = {}, {}, {}
        for n in TWIN_WEIGHTS:
            delta_w[n], new_m[n], new_v[n] = _adamw(weights[n], grad_w[n], given["m_" + n], given["v_" + n])
    return (loss, grad_x, *[grad_w[n] for n in TWIN_WEIGHTS], *[delta_w[n] for n in TWIN_WEIGHTS],
            *[new_m[n] for n in TWIN_WEIGHTS], *[new_v[n] for n in TWIN_WEIGHTS])
```

```python
import functools

import jax
import jax.numpy as jnp
from jax import lax
from jax.experimental import pallas as pl
from jax.experimental.pallas import tpu as pltpu

F32 = jnp.float32
BF16 = jnp.bfloat16

D_MODEL = 1024
D_POOL = 512
D_ATTN = 512
POOL_WINDOWS = (2, 4, 8, 16)
POOL_GROUP_DIM = 128
POOL_HALO = 16
HEAD_DIM = 64
HEADS_PER_BLOCK = 2
D_FF = 4096
N_CHIPS = 4
N_DEV = 8
EPS = 1e-6
ATTN_SCALE = 0.125
ATTN_TILE = 256
ROW_TILE = 512
LANES = 128

ADAM_LR = 0.001
ADAM_B1 = 0.9
ADAM_B2 = 0.999
ADAM_EPS = 1e-08
ADAM_WD = 0.01
ADAM_STEP = 10

MESH = pl.DeviceIdType.MESH
ANY = pl.BlockSpec(memory_space=pl.ANY)
VMEM_LIMIT = 56 * 1024 * 1024


def _nn(a, b):
    return jnp.dot(a, b, preferred_element_type=F32)


def _nt(a, b):
    return lax.dot_general(a, b, (((1,), (1,)), ((), ())), preferred_element_type=F32)


def _tn(a, b):
    return lax.dot_general(a, b, (((0,), (0,)), ((), ())), preferred_element_type=F32)


def _rms(x):
    return lax.rsqrt(jnp.mean(x * x, axis=-1, keepdims=True) + EPS)


def _rms_bwd(dy, n, r, g):
    dn = dy * g
    return r * (dn - n * jnp.mean(dn * n, axis=-1, keepdims=True))


def _params(**kw):
    return pltpu.CompilerParams(vmem_limit_bytes=VMEM_LIMIT, **kw)


def _in_proj(x, g1, w_in):
    S = x.shape[0]
    ts = ROW_TILE

    def body(x_ref, g_ref, w_ref, hn_ref, u_ref, qkv_ref):
        j = pl.program_id(1)

        @pl.when(j == 0)
        def _():
            xf = x_ref[...]
            hn_ref[...] = (xf * _rms(xf) * g_ref[...]).astype(BF16)

        p = _nn(hn_ref[...], w_ref[0])

        @pl.when(j == 0)
        def _():
            u_ref[...] = p

        @pl.when(j > 0)
        def _():
            qkv_ref[0] = p.astype(BF16)

    return pl.pallas_call(
        body,
        name="in_proj",
        grid=(S // ts, 4),
        in_specs=[
            pl.BlockSpec((ts, D_MODEL), lambda i, j: (i, 0)),
            pl.BlockSpec((1, D_MODEL), lambda i, j: (0, 0)),
            pl.BlockSpec((1, D_MODEL, 512), lambda i, j: (j, 0, 0)),
        ],
        out_specs=[
            pl.BlockSpec((ts, D_MODEL), lambda i, j: (i, 0)),
            pl.BlockSpec((ts, D_POOL), lambda i, j: (i, 0)),
            pl.BlockSpec((1, ts, 512), lambda i, j: (jnp.maximum(j - 1, 0), i, 0)),
        ],
        out_shape=[
            jax.ShapeDtypeStruct((S, D_MODEL), BF16),
            jax.ShapeDtypeStruct((S, D_POOL), F32),
            jax.ShapeDtypeStruct((3, S, 512), BF16),
        ],
        compiler_params=_params(),
    )(x, g1, w_in)


def _pool_counts(first_row, rows):
    t = first_row + lax.broadcasted_iota(jnp.int32, (rows, 1), 0)
    return [1.0 / jnp.minimum(t + 1, w).astype(F32) for w in POOL_WINDOWS]


def _pooled(u_tile, halo, first_row):
    ts = u_tile.shape[0]
    inv = _pool_counts(first_row, ts)
    outs = []
    for g, w in enumerate(POOL_WINDOWS):
        lanes = slice(g * POOL_GROUP_DIM, (g + 1) * POOL_GROUP_DIM)
        xg = u_tile[:, lanes]
        acc = jnp.concatenate([halo[:, lanes], xg], axis=0)
        shift = 1
        while shift < w:
            acc = acc + pltpu.roll(acc, shift, axis=0)
            shift *= 2
        outs.append(acc[POOL_HALO:, :] * inv[g] - xg)
    return outs


def _pool_fwd(u, pool_w, pool_scale):
    S = u.shape[0]
    ts = ROW_TILE
    hb = ts // POOL_HALO

    def body(u_ref, halo_ref, w_ref, s_ref, y_ref):
        i = pl.program_id(0)
        halo = jnp.where(i == 0, 0.0, halo_ref[...])
        pooled = _pooled(u_ref[...], halo, i * ts)
        for g in range(len(POOL_WINDOWS)):
            lanes = slice(g * POOL_GROUP_DIM, (g + 1) * POOL_GROUP_DIM)
            y_ref[:, lanes] = _nn(pooled[g].astype(BF16), w_ref[g]) * s_ref[:, lanes]

    return pl.pallas_call(
        body,
        name="pool_fwd",
        grid=(S // ts,),
        in_specs=[
            pl.BlockSpec((ts, D_POOL), lambda i: (i, 0)),
            pl.BlockSpec((POOL_HALO, D_POOL), lambda i: (jnp.maximum(i * hb - 1, 0), 0)),
            pl.BlockSpec((4, POOL_GROUP_DIM, POOL_GROUP_DIM), lambda i: (0, 0, 0)),
            pl.BlockSpec((1, D_POOL), lambda i: (0, 0)),
        ],
        out_specs=pl.BlockSpec((ts, D_POOL), lambda i: (i, 0)),
        out_shape=jax.ShapeDtypeStruct((S, D_POOL), F32),
        compiler_params=_params(),
    )(u, u, pool_w, pool_scale)


def _head_masks():
    lane = lax.broadcasted_iota(jnp.int32, (1, LANES), 1)
    first = lane < HEAD_DIM
    return [first, jnp.logical_not(first)]


def _tri_masks(t):
    row = lax.broadcasted_iota(jnp.int32, (t, t), 0)
    col = lax.broadcasted_iota(jnp.int32, (t, t), 1)
    return row, col


def _split_bf16(x):
    hi = x.astype(BF16)
    lo = (x - hi.astype(F32)).astype(BF16)
    return hi, lo


def _log_sigmoids(z):
    sp = jnp.log(1.0 + jnp.exp(-jnp.abs(z)))
    ls = jnp.minimum(z, 0.0) - sp
    return ls, ls - z


def _attn_fwd(qkv):
    S = qkv.shape[1]
    t = ATTN_TILE

    def body(q_ref, k_ref, v_ref, o_ref, vh_ref):
        i = pl.program_id(1)
        masks = _head_masks()

        @pl.when(i == 0)
        def _():
            vv = v_ref[0]
            for h in range(HEADS_PER_BLOCK):
                vh_ref[h] = jnp.where(masks[h], vv, jnp.zeros_like(vv))

        row, col = _tri_masks(t)
        later = (row > col).astype(BF16)
        causal = col < row
        qs = q_ref[0] * ATTN_SCALE
        out = jnp.zeros((t, LANES), F32)

        for h in range(HEADS_PER_BLOCK):
            qh = jnp.where(masks[h], qs, jnp.zeros_like(qs))

            def tile(j, carry, diag, h=h, qh=qh):
                c, o = carry
                keys = pl.ds(pl.multiple_of(j * t, t), t)
                z = _nt(qh, k_ref[0, keys, :])
                ls, l1m = _log_sigmoids(z)
                if diag:
                    l1m = jnp.where(causal, l1m, 0.0)
                hi, lo = _split_bf16(l1m)
                tail = _nn(hi, later) + _nn(lo, later) + c
                a = jnp.exp(ls + tail)
                if diag:
                    a = jnp.where(causal, a, 0.0)
                o = o + _nn(a.astype(BF16), vh_ref[h, keys, :])
                c = c + jnp.sum(l1m, axis=1, keepdims=True)
                return c, o

            carry = tile(i, (jnp.zeros((t, 1), F32), out), True)
            carry = lax.fori_loop(0, i, lambda jj, cr: tile(i - 1 - jj, cr, False), carry)
            out = carry[1]

        o_ref[...] = out

    nblk = D_ATTN // LANES
    return pl.pallas_call(
        body,
        name="attn_fwd",
        grid=(nblk, S // t),
        in_specs=[
            pl.BlockSpec((1, t, LANES), lambda hp, i: (0, i, hp)),
            pl.BlockSpec((1, S, LANES), lambda hp, i: (1, 0, hp)),
            pl.BlockSpec((1, S, LANES), lambda hp, i: (2, 0, hp)),
        ],
        out_specs=pl.BlockSpec((t, LANES), lambda hp, i: (i, hp)),
        out_shape=jax.ShapeDtypeStruct((S, D_ATTN), F32),
        scratch_shapes=[pltpu.VMEM((HEADS_PER_BLOCK, S, LANES), BF16)],
        compiler_params=_params(),
    )(qkv, qkv, qkv)


def _out_proj(y_pool, y_attn, x, g_pool, g_attn, w_out, g2):
    S = x.shape[0]
    ts = ROW_TILE

    def body(yp_ref, ya_ref, x_ref, gp_ref, ga_ref, w_ref, g2_ref, mixed_ref, h1_ref, hn2_ref):
        yp = yp_ref[...]
        ya = ya_ref[...]
        mixed = jnp.concatenate([yp * _rms(yp) * gp_ref[...], ya * _rms(ya) * ga_ref[...]], axis=-1).astype(BF16)
        mixed_ref[...] = mixed
        h1 = x_ref[...] + _nn(mixed, w_ref[...])
        h1_ref[...] = h1
        hn2_ref[...] = (h1 * _rms(h1) * g2_ref[...]).astype(BF16)

    row = lambda w: pl.BlockSpec((ts, w), lambda i: (i, 0))
    vec = lambda w: pl.BlockSpec((1, w), lambda i: (0, 0))
    return pl.pallas_call(
        body,
        name="out_proj",
        grid=(S // ts,),
        in_specs=[row(D_POOL), row(D_ATTN), row(D_MODEL), vec(D_POOL), vec(D_ATTN),
                  pl.BlockSpec((D_MODEL, D_MODEL), lambda i: (0, 0)), vec(D_MODEL)],
        out_specs=[row(D_MODEL), row(D_MODEL), row(D_MODEL)],
        out_shape=[
            jax.ShapeDtypeStruct((S, D_MODEL), BF16),
            jax.ShapeDtypeStruct((S, D_MODEL), F32),
            jax.ShapeDtypeStruct((S, D_MODEL), BF16),
        ],
        compiler_params=_params(),
    )(y_pool, y_attn, x, g_pool, g_attn, w_out, g2)


def _mlp_fwd(hn2, h1, w_up, w_down, g_final, target):
    S = hn2.shape[0]
    ts = ROW_TILE
    nf = D_FF // 1024

    def body(hn2_ref, h1_ref, wu_ref, wd_ref, gf_ref, tg_ref, r_ref, dh2_ref, dh2b_ref, lsq_ref, dgf_ref, acc_ref):
        i = pl.program_id(0)
        c = pl.program_id(1)
        r = jnp.maximum(_nn(hn2_ref[...], wu_ref[0]), 0.0)
        r_ref[...] = r.astype(BF16)
        contrib = _nn((r * r).astype(BF16), wd_ref[0])

        @pl.when(c == 0)
        def _():
            acc_ref[...] = contrib

        @pl.when(c > 0)
        def _():
            acc_ref[...] += contrib

        @pl.when(jnp.logical_and(i == 0, c == 0))
        def _():
            lsq_ref[...] = jnp.zeros_like(lsq_ref)
            dgf_ref[...] = jnp.zeros_like(dgf_ref)

        @pl.when(c == nf - 1)
        def _():
            h2 = h1_ref[...] + acc_ref[...]
            rf = _rms(h2)
            n = h2 * rf
            gf = gf_ref[...]
            e = n * gf - tg_ref[...]
            lsq_ref[...] += jnp.sum(e * e, axis=0, keepdims=True)
            dy = e * (1.0 / D_MODEL)
            dgf_ref[...] += jnp.sum(dy * n, axis=0, keepdims=True)
            dh2 = _rms_bwd(dy, n, rf, gf)
            dh2_ref[...] = dh2
            dh2b_ref[...] = dh2.astype(BF16)

    row = lambda w: pl.BlockSpec((ts, w), lambda i, c: (i, 0))
    vec = lambda w: pl.BlockSpec((1, w), lambda i, c: (0, 0))
    return pl.pallas_call(
        body,
        name="mlp_fwd",
        grid=(S // ts, nf),
        in_specs=[row(D_MODEL), row(D_MODEL),
                  pl.BlockSpec((1, D_MODEL, 1024), lambda i, c: (c, 0, 0)),
                  pl.BlockSpec((1, 1024, D_MODEL), lambda i, c: (c, 0, 0)),
                  vec(D_MODEL), row(D_MODEL)],
        out_specs=[pl.BlockSpec((ts, 1024), lambda i, c: (i, c)), row(D_MODEL), row(D_MODEL),
                   vec(D_MODEL), vec(D_MODEL)],
        out_shape=[
            jax.ShapeDtypeStruct((S, D_FF), BF16),
            jax.ShapeDtypeStruct((S, D_MODEL), F32),
            jax.ShapeDtypeStruct((S, D_MODEL), BF16),
            jax.ShapeDtypeStruct((1, D_MODEL), F32),
            jax.ShapeDtypeStruct((1, D_MODEL), F32),
        ],
        scratch_shapes=[pltpu.VMEM((ts, D_MODEL), F32)],
        compiler_params=_params(),
    )(hn2, h1, w_up, w_down, g_final, target)


def _mlp_bwd_dx(dh2b, r_act, w_down, w_up, h1, dh2, g2):
    S = h1.shape[0]
    ts = ROW_TILE
    nf = D_FF // 1024

    def body(dh2b_ref, r_ref, wd_ref, wu_ref, h1_ref, dh2_ref, g2_ref, dup_ref, dh1_ref, dg2_ref, acc_ref):
        i = pl.program_id(0)
        c = pl.program_id(1)
        dup = (_nt(dh2b_ref[...], wd_ref[0]) * (2.0 * r_ref[...].astype(F32))).astype(BF16)
        dup_ref[...] = dup
        contrib = _nt(dup, wu_ref[0])

        @pl.when(c == 0)
        def _():
            acc_ref[...] = contrib

        @pl.when(c > 0)
        def _():
            acc_ref[...] += contrib

        @pl.when(jnp.logical_and(i == 0, c == 0))
        def _():
            dg2_ref[...] = jnp.zeros_like(dg2_ref)

        @pl.when(c == nf - 1)
        def _():
            dhn2 = acc_ref[...]
            h1v = h1_ref[...]
            r2 = _rms(h1v)
            n2 = h1v * r2
            dg2_ref[...] += jnp.sum(dhn2 * n2, axis=0, keepdims=True)
            dh1_ref[...] = dh2_ref[...] + _rms_bwd(dhn2, n2, r2, g2_ref[...])

    row = lambda w: pl.BlockSpec((ts, w), lambda i, c: (i, 0))
    vec = lambda w: pl.BlockSpec((1, w), lambda i, c: (0, 0))
    chunk = pl.BlockSpec((ts, 1024), lambda i, c: (i, c))
    return pl.pallas_call(
        body,
        name="mlp_bwd_dx",
        grid=(S // ts, nf),
        in_specs=[row(D_MODEL), chunk,
                  pl.BlockSpec((1, 1024, D_MODEL), lambda i, c: (c, 0, 0)),
                  pl.BlockSpec((1, D_MODEL, 1024), lambda i, c: (c, 0, 0)),
                  row(D_MODEL), row(D_MODEL), vec(D_MODEL)],
        out_specs=[chunk, row(D_MODEL), vec(D_MODEL)],
        out_shape=[
            jax.ShapeDtypeStruct((S, D_FF), BF16),
            jax.ShapeDtypeStruct((S, D_MODEL), F32),
            jax.ShapeDtypeStruct((1, D_MODEL), F32),
        ],
        scratch_shapes=[pltpu.VMEM((ts, D_MODEL), F32)],
        compiler_params=_params(),
    )(dh2b, r_act, w_down, w_up, h1, dh2, g2)


def _mlp_bwd_dw(r_act, dh2b, hn2, dup):
    S = hn2.shape[0]
    ts = ROW_TILE
    nf = D_FF // 1024

    def body(r_ref, dh2b_ref, hn2_ref, dup_ref, dwd_ref, dwu_ref):
        tt = pl.program_id(1)
        r = r_ref[...].astype(F32)
        dwd = _tn((r * r).astype(BF16), dh2b_ref[...])
        dwu = _tn(hn2_ref[...], dup_ref[...])

        @pl.when(tt == 0)
        def _():
            dwd_ref[0] = dwd
            dwu_ref[0] = dwu

        @pl.when(tt > 0)
        def _():
            dwd_ref[0] += dwd
            dwu_ref[0] += dwu

    row = pl.BlockSpec((ts, D_MODEL), lambda c, tt: (tt, 0))
    chunk = pl.BlockSpec((ts, 1024), lambda c, tt: (tt, c))
    return pl.pallas_call(
        body,
        name="mlp_bwd_dw",
        grid=(nf, S // ts),
        in_specs=[chunk, row, row, chunk],
        out_specs=[pl.BlockSpec((1, 1024, D_MODEL), lambda c, tt: (c, 0, 0)),
                   pl.BlockSpec((1, D_MODEL, 1024), lambda c, tt: (c, 0, 0))],
        out_shape=[
            jax.ShapeDtypeStruct((nf, 1024, D_MODEL), F32),
            jax.ShapeDtypeStruct((nf, D_MODEL, 1024), F32),
        ],
        compiler_params=_params(),
    )(r_act, dh2b, hn2, dup)


def _out_bwd(dh1, w_out, mixed, y_pool, y_attn, g_pool, g_attn):
    S = dh1.shape[0]
    ts = ROW_TILE

    def body(dh1_ref, w_ref, mixed_ref, yp_ref, ya_ref, gp_ref, ga_ref,
             dyp_ref, dya_ref, dw_ref, dgp_ref, dga_ref):
        i = pl.program_id(0)
        dh1b = dh1_ref[...].astype(BF16)
        dmixed = _nt(dh1b, w_ref[...])
        dw = _tn(mixed_ref[...], dh1b)

        @pl.when(i == 0)
        def _():
            dw_ref[...] = dw
            dgp_ref[...] = jnp.zeros_like(dgp_ref)
            dga_ref[...] = jnp.zeros_like(dga_ref)

        @pl.when(i > 0)
        def _():
            dw_ref[...] += dw

        for y_ref, g_ref, dy_ref, dg_ref, lanes in (
                (yp_ref, gp_ref, dyp_ref, dgp_ref, slice(0, D_POOL)),
                (ya_ref, ga_ref, dya_ref, dga_ref, slice(D_POOL, D_MODEL))):
            y = y_ref[...]
            r = _rms(y)
            n = y * r
            dm = dmixed[:, lanes]
            dg_ref[...] += jnp.sum(dm * n, axis=0, keepdims=True)
            dy_ref[...] = _rms_bwd(dm, n, r, g_ref[...])

    row = lambda w: pl.BlockSpec((ts, w), lambda i: (i, 0))
    vec = lambda w: pl.BlockSpec((1, w), lambda i: (0, 0))
    full = pl.BlockSpec((D_MODEL, D_MODEL), lambda i: (0, 0))
    return pl.pallas_call(
        body,
        name="out_bwd",
        grid=(S // ts,),
        in_specs=[row(D_MODEL), full, row(D_MODEL), row(D_POOL), row(D_ATTN), vec(D_POOL), vec(D_ATTN)],
        out_specs=[row(D_POOL), row(D_ATTN), full, vec(D_POOL), vec(D_ATTN)],
        out_shape=[
            jax.ShapeDtypeStruct((S, D_POOL), F32),
            jax.ShapeDtypeStruct((S, D_ATTN), F32),
            jax.ShapeDtypeStruct((D_MODEL, D_MODEL), F32),
            jax.ShapeDtypeStruct((1, D_POOL), F32),
            jax.ShapeDtypeStruct((1, D_ATTN), F32),
        ],
        compiler_params=_params(),
    )(dh1, w_out, mixed, y_pool, y_attn, g_pool, g_attn)


def _attn_bwd(qkv, o, do):
    S = qkv.shape[1]
    t = ATTN_TILE

    def body(q_ref, k_ref, v_ref, o_ref, do_ref, dq_ref, dk_ref, dv_ref, kh_ref, dk_acc, dv_acc):
        i = pl.program_id(1)
        masks = _head_masks()

        @pl.when(i == 0)
        def _():
            kk = k_ref[0]
            for h in range(HEADS_PER_BLOCK):
                kh_ref[h] = jnp.where(masks[h], kk, jnp.zeros_like(kk))
            dk_acc[...] = jnp.zeros_like(dk_acc)
            dv_acc[...] = jnp.zeros_like(dv_acc)

        row, col = _tri_masks(t)
        later = (row > col).astype(BF16)
        from_s = (row >= col).astype(BF16)
        causal = col < row
        qs = q_ref[0] * ATTN_SCALE
        dob = do_ref[...].astype(BF16)
        d_all = dob.astype(F32) * o_ref[...]
        dq = jnp.zeros((t, LANES), F32)

        for h in range(HEADS_PER_BLOCK):
            qh = jnp.where(masks[h], qs, jnp.zeros_like(qs))
            doh = jnp.where(masks[h], dob, jnp.zeros_like(dob))
            d_row = jnp.sum(jnp.where(masks[h], d_all, 0.0), axis=1, keepdims=True)

            def tile(j, carry, diag, h=h, qh=qh, doh=doh, d_row=d_row):
                c_l, c_g, dq = carry
                keys = pl.ds(pl.multiple_of(j * t, t), t)
                z = _nt(qh, k_ref[0, keys, :])
                ls, l1m = _log_sigmoids(z)
                if diag:
                    l1m = jnp.where(causal, l1m, 0.0)
                hi, lo = _split_bf16(l1m)
                tail = _nn(hi, later) + _nn(lo, later) + c_l
                a = jnp.exp(ls + tail)
                if diag:
                    a = jnp.where(causal, a, 0.0)
                ab = a.astype(BF16)
                g = ab.astype(F32) * _nt(doh, v_ref[0, keys, :])
                ghi, glo = _split_bf16(g)
                before = d_row - (_nn(ghi, from_s) + _nn(glo, from_s) + c_g)
                beta = jnp.exp(ls)
                dz = g * (1.0 - beta) - before * beta
                if diag:
                    dz = jnp.where(causal, dz, 0.0)
                dzb = dz.astype(BF16)
                dq = dq + _nn(dzb, kh_ref[h, keys, :])
                dk_acc[keys, :] += _tn(dzb, qh)
                dv_acc[keys, :] += _tn(ab, doh)
                c_l = c_l + jnp.sum(l1m, axis=1, keepdims=True)
                c_g = c_g + jnp.sum(g, axis=1, keepdims=True)
                return c_l, c_g, dq

            zero = jnp.zeros((t, 1), F32)
            carry = tile(i, (zero, zero, dq), True)
            carry = lax.fori_loop(0, i, lambda jj, cr: tile(i - 1 - jj, cr, False), carry)
            dq = carry[2]

        dq_ref[...] = (dq * ATTN_SCALE).astype(BF16)

        @pl.when(i == pl.num_programs(1) - 1)
        def _():
            dk_ref[...] = dk_acc[...].astype(BF16)
            dv_ref[...] = dv_acc[...].astype(BF16)

    nblk = D_ATTN // LANES
    qtile = pl.BlockSpec((t, LANES), lambda hp, i: (i, hp))
    whole = pl.BlockSpec((S, LANES), lambda hp, i: (0, hp))
    return pl.pallas_call(
        body,
        name="attn_bwd",
        grid=(nblk, S // t),
        in_specs=[
            pl.BlockSpec((1, t, LANES), lambda hp, i: (0, i, hp)),
            pl.BlockSpec((1, S, LANES), lambda hp, i: (1, 0, hp)),
            pl.BlockSpec((1, S, LANES), lambda hp, i: (2, 0, hp)),
            qtile, qtile,
        ],
        out_specs=[qtile, whole, whole],
        out_shape=[jax.ShapeDtypeStruct((S, D_ATTN), BF16)] * 3,
        scratch_shapes=[
            pltpu.VMEM((HEADS_PER_BLOCK, S, LANES), BF16),
            pltpu.VMEM((S, LANES), F32),
            pltpu.VMEM((S, LANES), F32),
        ],
        compiler_params=_params(),
    )(qkv, qkv, qkv, o, do)


def _pool_bwd_w(u, dyp, pool_w, pool_scale):
    S = u.shape[0]
    ts = ROW_TILE
    hb = ts // POOL_HALO

    def body(u_ref, halo_ref, dy_ref, w_ref, s_ref, dp_ref, dw_ref, ds_ref):
        i = pl.program_id(0)
        halo = jnp.where(i == 0, 0.0, halo_ref[...])
        pooled = _pooled(u_ref[...], halo, i * ts)

        @pl.when(i == 0)
        def _():
            dw_ref[...] = jnp.zeros_like(dw_ref)
            ds_ref[...] = jnp.zeros_like(ds_ref)

        for g in range(len(POOL_WINDOWS)):
            lanes = slice(g * POOL_GROUP_DIM, (g + 1) * POOL_GROUP_DIM)
            pg = pooled[g].astype(BF16)
            dy = dy_ref[:, lanes]
            ds_ref[:, lanes] += jnp.sum(dy * _nn(pg, w_ref[g]), axis=0, keepdims=True)
            dmapped = (dy * s_ref[:, lanes]).astype(BF16)
            dp_ref[:, lanes] = _nt(dmapped, w_ref[g])
            dw_ref[g] += _tn(pg, dmapped)

    row = pl.BlockSpec((ts, D_POOL), lambda i: (i, 0))
    vec = pl.BlockSpec((1, D_POOL), lambda i: (0, 0))
    wspec = pl.BlockSpec((4, POOL_GROUP_DIM, POOL_GROUP_DIM), lambda i: (0, 0, 0))
    return pl.pallas_call(
        body,
        name="pool_bwd_w",
        grid=(S // ts,),
        in_specs=[row, pl.BlockSpec((POOL_HALO, D_POOL), lambda i: (jnp.maximum(i * hb - 1, 0), 0)),
                  row, wspec, vec],
        out_specs=[row, wspec, vec],
        out_shape=[
            jax.ShapeDtypeStruct((S, D_POOL), F32),
            jax.ShapeDtypeStruct((4, POOL_GROUP_DIM, POOL_GROUP_DIM), F32),
            jax.ShapeDtypeStruct((1, D_POOL), F32),
        ],
        compiler_params=_params(),
    )(u, u, dyp, pool_w, pool_scale)


def _pool_bwd_u(dpooled):
    S = dpooled.shape[0]
    ts = ROW_TILE
    hb = ts // POOL_HALO
    last = S // ts - 1

    def body(dp_ref, halo_ref, du_ref):
        i = pl.program_id(0)
        dp = dp_ref[...]
        halo = jnp.where(i == last, 0.0, halo_ref[...])
        inv = _pool_counts(i * ts, ts)
        n = ts + POOL_HALO
        for g, w in enumerate(POOL_WINDOWS):
            lanes = slice(g * POOL_GROUP_DIM, (g + 1) * POOL_GROUP_DIM)
            dg = dp[:, lanes]
            acc = jnp.concatenate([dg * inv[g], halo[:, lanes] * (1.0 / w)], axis=0)
            shift = 1
            while shift < w:
                acc = acc + pltpu.roll(acc, n - shift, axis=0)
                shift *= 2
            du_ref[:, lanes] = (acc[:ts, :] - dg).astype(BF16)

    return pl.pallas_call(
        body,
        name="pool_bwd_u",
        grid=(S // ts,),
        in_specs=[pl.BlockSpec((ts, D_POOL), lambda i: (i, 0)),
                  pl.BlockSpec((POOL_HALO, D_POOL), lambda i: (jnp.minimum((i + 1) * hb, (last + 1) * hb - 1), 0))],
        out_specs=pl.BlockSpec((ts, D_POOL), lambda i: (i, 0)),
        out_shape=jax.ShapeDtypeStruct((S, D_POOL), BF16),
        compiler_params=_params(),
    )(dpooled, dpooled)


def _in_bwd(du, dq, dk, dv, hn, w_in, x, dh1, g1):
    S = x.shape[0]
    ts = ROW_TILE // 2

    def body(du_ref, dq_ref, dk_ref, dv_ref, hn_ref, w_ref, x_ref, dh1_ref, g_ref, dx_ref, dw_ref, dg_ref):
        i = pl.program_id(0)

        @pl.when(i == 0)
        def _():
            dw_ref[...] = jnp.zeros_like(dw_ref)
            dg_ref[...] = jnp.zeros_like(dg_ref)

        hn = hn_ref[...]
        dhn = jnp.zeros((ts, D_MODEL), F32)
        for j, dp_ref in enumerate((du_ref, dq_ref, dk_ref, dv_ref)):
            dp = dp_ref[...]
            dhn = dhn + _nt(dp, w_ref[j])
            dw_ref[j] += _tn(hn, dp)
        xv = x_ref[...]
        r1 = _rms(xv)
        n1 = xv * r1
        dg_ref[...] += jnp.sum(dhn * n1, axis=0, keepdims=True)
        dx_ref[...] = dh1_ref[...] + _rms_bwd(dhn, n1, r1, g_ref[...])

    row = lambda w: pl.BlockSpec((ts, w), lambda i: (i, 0))
    vec = pl.BlockSpec((1, D_MODEL), lambda i: (0, 0))
    wspec = pl.BlockSpec((4, D_MODEL, 512), lambda i: (0, 0, 0))
    return pl.pallas_call(
        body,
        name="in_bwd",
        grid=(S // ts,),
        in_specs=[row(512), row(512), row(512), row(512), row(D_MODEL), wspec, row(D_MODEL), row(D_MODEL), vec],
        out_specs=[row(D_MODEL), wspec, vec],
        out_shape=[
            jax.ShapeDtypeStruct((S, D_MODEL), F32),
            jax.ShapeDtypeStruct((4, D_MODEL, 512), F32),
            jax.ShapeDtypeStruct((1, D_MODEL), F32),
        ],
        compiler_params=_params(),
    )(du, dq, dk, dv, hn, w_in, x, dh1, g1)


def _local_step(x, target, w_in, w_out, w_up, w_down, pool_w, small):
    hn, u, qkv = _in_proj(x, small["norm1_g"], w_in)
    y_pool = _pool_fwd(u, pool_w, small["pool_scale"])
    y_attn = _attn_fwd(qkv)
    mixed, h1, hn2 = _out_proj(y_pool, y_attn, x, small["pool_out_g"], small["attn_out_g"], w_out, small["norm2_g"])
    r_act, dh2, dh2b, lsq, dgf = _mlp_fwd(hn2, h1, w_up, w_down, small["final_g"], target)

    dup, dh1, dg2 = _mlp_bwd_dx(dh2b, r_act, w_down, w_up, h1, dh2, small["norm2_g"])
    dw_down, dw_up = _mlp_bwd_dw(r_act, dh2b, hn2, dup)
    dyp, dya, dw_out, dgp, dga = _out_bwd(dh1, w_out, mixed, y_pool, y_attn, small["pool_out_g"], small["attn_out_g"])
    dq, dk, dv = _attn_bwd(qkv, y_attn, dya)
    dpooled, dpool_w, dpool_scale = _pool_bwd_w(u, dyp, pool_w, small["pool_scale"])
    du = _pool_bwd_u(dpooled)
    dx, dw_in, dg1 = _in_bwd(du, dq, dk, dv, hn, w_in, x, dh1, small["norm1_g"])

    big = {"w_in": dw_in, "w_out": dw_out, "w_up": dw_up, "w_down": dw_down}
    little = {"norm1_g": dg1, "pool_w": dpool_w, "pool_scale": dpool_scale, "pool_out_g": dgp,
              "attn_out_g": dga, "norm2_g": dg2, "final_g": dgf, "loss_sq": lsq}
    return dx, big, little


def _place():
    x, y, c = lax.axis_index("x"), lax.axis_index("y"), lax.axis_index("c")
    other_chips = [(1 - x, y), (x, 1 - y), (1 - x, 1 - y)]
    return x, y, c, other_chips


def _chip_index(chip):
    return 2 * chip[0] + chip[1]


def _gather_weights(shards):
    n = len(shards)

    def body(*refs):
        ins, outs = refs[:n], refs[n:2 * n]
        send_sems, recv_sems, local_sems = refs[2 * n:]
        x, y, c, chips = _place()
        me = _chip_index((x, y))
        sibling = (x, y, 1 - c)

        def copy(a, k, chip_idx, half, to, src=None):
            dst = outs[a].at[chip_idx, half]
            return pltpu.make_async_remote_copy(
                src_ref=dst if src is None else src, dst_ref=dst,
                send_sem=send_sems.at[a, k], recv_sem=recv_sems.at[a, k],
                device_id=to, device_id_type=MESH)

        started = []
        local = []
        for a in range(n):
            mine = pltpu.make_async_copy(ins[a].at[c], outs[a].at[me, c], local_sems.at[a])
            mine.start()
            local.append(mine)
            first = [copy(a, 0, me, c, sibling, src=ins[a].at[c])]
            first += [copy(a, 1 + j, me, c, (*chip, c), src=ins[a].at[c]) for j, chip in enumerate(chips)]
            for cp in first:
                cp.start()
            started += first
        for a in range(n):
            for j, chip in enumerate(chips):
                copy(a, 1 + j, _chip_index(chip), c, (x, y, c)).wait_recv()
                passed = copy(a, 4 + j, _chip_index(chip), c, sibling)
                passed.start()
                started.append(passed)
        for a in range(n):
            copy(a, 0, me, 1 - c, (x, y, c)).wait_recv()
            for j, chip in enumerate(chips):
                copy(a, 4 + j, _chip_index(chip), 1 - c, (x, y, c)).wait_recv()
        for cp in started:
            cp.wait_send()
        for cp in local:
            cp.wait()

    return pl.pallas_call(
        body,
        name="gather_weights",
        in_specs=[ANY] * n,
        out_specs=[ANY] * n,
        out_shape=[jax.ShapeDtypeStruct((N_CHIPS,) + s.shape, s.dtype) for s in shards],
        scratch_shapes=[pltpu.SemaphoreType.DMA((n, 7)), pltpu.SemaphoreType.DMA((n, 7)),
                        pltpu.SemaphoreType.DMA((n,))],
    )(*shards)


def _swap_halves(grads):
    n = len(grads)

    def body(*refs):
        ins, outs = refs[:n], refs[n:2 * n]
        send_sems, recv_sems = refs[2 * n:]
        x, y, c, _ = _place()
        copies = [
            pltpu.make_async_remote_copy(
                src_ref=ins[a].at[:, 1 - c], dst_ref=outs[a],
                send_sem=send_sems.at[a], recv_sem=recv_sems.at[a],
                device_id=(x, y, 1 - c), device_id_type=MESH)
            for a in range(n)]
        for cp in copies:
            cp.start()
        for cp in copies:
            cp.wait()

    return pl.pallas_call(
        body,
        name="swap_halves",
        in_specs=[ANY] * n,
        out_specs=[ANY] * n,
        out_shape=[jax.ShapeDtypeStruct((g.shape[0],) + g.shape[2:], g.dtype) for g in grads],
        scratch_shapes=[pltpu.SemaphoreType.DMA((n,)), pltpu.SemaphoreType.DMA((n,))],
    )(*grads)


def _add_own_half(core, grad, received):
    _, _, h, cols = grad.shape
    hb = min(h, 256)

    def body(c_ref, g_ref, r_ref, o_ref):
        o_ref[...] = g_ref[0] + r_ref[...]

    return pl.pallas_call(
        body,
        name="add_own_half",
        grid_spec=pltpu.PrefetchScalarGridSpec(
            num_scalar_prefetch=1,
            grid=(N_CHIPS, h // hb),
            in_specs=[pl.BlockSpec((1, 1, hb, cols), lambda j, r, c_ref: (j, c_ref[0], r, 0)),
                      pl.BlockSpec((1, hb, cols), lambda j, r, c_ref: (j, r, 0))],
            out_specs=pl.BlockSpec((1, hb, cols), lambda j, r, c_ref: (j, r, 0)),
        ),
        out_shape=jax.ShapeDtypeStruct(received.shape, F32),
    )(core, grad, received)


def _scatter_to_owners(sums):
    n = len(sums)

    def body(*refs):
        ins, outs = refs[:n], refs[n:2 * n]
        send_sems, recv_sems, local_sems = refs[2 * n:]
        x, y, c, chips = _place()
        me = _chip_index((x, y))
        sends = []
        local = []
        for a in range(n):
            mine = pltpu.make_async_copy(ins[a].at[me], outs[a].at[me], local_sems.at[a])
            mine.start()
            local.append(mine)
            for j, chip in enumerate(chips):
                cp = pltpu.make_async_remote_copy(
                    src_ref=ins[a].at[_chip_index(chip)], dst_ref=outs[a].at[me],
                    send_sem=send_sems.at[a, j], recv_sem=recv_sems.at[a, j],
                    device_id=(*chip, c), device_id_type=MESH)
                cp.start()
                sends.append(cp)
        for a in range(n):
            for j, chip in enumerate(chips):
                pltpu.make_async_remote_copy(
                    src_ref=ins[a].at[me], dst_ref=outs[a].at[_chip_index(chip)],
                    send_sem=send_sems.at[a, j], recv_sem=recv_sems.at[a, j],
                    device_id=(x, y, c), device_id_type=MESH).wait_recv()
        for cp in sends:
            cp.wait_send()
        for cp in local:
            cp.wait()

    return pl.pallas_call(
        body,
        name="scatter_to_owners",
        in_specs=[ANY] * n,
        out_specs=[ANY] * n,
        out_shape=[jax.ShapeDtypeStruct(s.shape, s.dtype) for s in sums],
        scratch_shapes=[pltpu.SemaphoreType.DMA((n, 3)), pltpu.SemaphoreType.DMA((n, 3)),
                        pltpu.SemaphoreType.DMA((n,))],
    )(*sums)


def _sum_chips(parts):
    _, h, cols = parts.shape
    hb = min(h, 256)

    def body(p0, p1, p2, p3, o_ref):
        o_ref[...] = ((p0[0] + p1[0]) + p2[0]) + p3[0]

    return pl.pallas_call(
        body,
        name="sum_chips",
        grid=(h // hb,),
        in_specs=[pl.BlockSpec((1, hb, cols), functools.partial(lambda r, k: (k, r, 0), k=k)) for k in range(N_CHIPS)],
        out_specs=pl.BlockSpec((hb, cols), lambda r: (r, 0)),
        out_shape=jax.ShapeDtypeStruct((h, cols), F32),
    )(parts, parts, parts, parts)


def _join_halves(halves):
    n = len(halves)

    def body(*refs):
        ins, outs = refs[:n], refs[n:2 * n]
        send_sems, recv_sems, local_sems = refs[2 * n:]
        x, y, c, _ = _place()
        local = []
        sends = []
        for a in range(n):
            mine = pltpu.make_async_copy(ins[a], outs[a].at[c], local_sems.at[a])
            mine.start()
            local.append(mine)
            cp = pltpu.make_async_remote_copy(
                src_ref=ins[a], dst_ref=outs[a].at[c],
                send_sem=send_sems.at[a], recv_sem=recv_sems.at[a],
                device_id=(x, y, 1 - c), device_id_type=MESH)
            cp.start()
            sends.append(cp)
        for a in range(n):
            pltpu.make_async_remote_copy(
                src_ref=ins[a], dst_ref=outs[a].at[1 - c],
                send_sem=send_sems.at[a], recv_sem=recv_sems.at[a],
                device_id=(x, y, c), device_id_type=MESH).wait_recv()
        for cp in sends:
            cp.wait_send()
        for cp in local:
            cp.wait()

    return pl.pallas_call(
        body,
        name="join_halves",
        in_specs=[ANY] * n,
        out_specs=[ANY] * n,
        out_shape=[jax.ShapeDtypeStruct((2,) + s.shape, s.dtype) for s in halves],
        scratch_shapes=[pltpu.SemaphoreType.DMA((n,)), pltpu.SemaphoreType.DMA((n,)),
                        pltpu.SemaphoreType.DMA((n,))],
    )(*halves)


def _adamw(w, g, m, v):
    m = ADAM_B1 * m + (1.0 - ADAM_B1) * g
    v = ADAM_B2 * v + (1.0 - ADAM_B2) * jnp.square(g)
    m_hat = m / (1.0 - ADAM_B1 ** ADAM_STEP)
    v_hat = v / (1.0 - ADAM_B2 ** ADAM_STEP)
    delta = -ADAM_LR * (m_hat / (jnp.sqrt(v_hat) + ADAM_EPS) + ADAM_WD * w)
    return delta, m, v


def _adamw_big(w, g, m, v):
    rows, cols = w.shape
    rb = min(rows, 256)

    def body(w_ref, g_ref, m_ref, v_ref, d_ref, mo_ref, vo_ref):
        d_ref[...], mo_ref[...], vo_ref[...] = _adamw(w_ref[...], g_ref[...], m_ref[...], v_ref[...])

    blk = pl.BlockSpec((rb, cols), lambda r: (r, 0))
    return pl.pallas_call(
        body,
        name="adamw_big",
        grid=(rows // rb,),
        in_specs=[blk] * 4,
        out_specs=[blk] * 3,
        out_shape=[jax.ShapeDtypeStruct(w.shape, F32)] * 3,
    )(w, g, m, v)


SMALL_ORDER = ("pool_w", "norm1_g", "pool_scale", "pool_out_g", "attn_out_g", "norm2_g", "final_g")
SUBLANES = 8


def _pack(parts):
    rows = []
    for p in parts:
        p = p.reshape(-1, LANES)
        pad = (-p.shape[0]) % SUBLANES
        if pad:
            p = jnp.pad(p, ((0, pad), (0, 0)))
        rows.append(p)
    return jnp.concatenate(rows, axis=0)


def _unpack(slab, shapes):
    out, r = [], 0
    for shp in shapes:
        size = 1
        for d in shp:
            size *= d
        nrow = size // LANES
        out.append(slab[r:r + nrow].reshape(shp))
        r += nrow + (-nrow) % SUBLANES
    return out


def _small_step(partials, w, m, v, loss_rows):
    rows = partials.shape[0]

    def body(p_ref, w_ref, m_ref, v_ref, g_ref, d_ref, mo_ref, vo_ref, loss_ref, buf, send_sems, recv_sems):
        x, y, c, _ = _place()
        me = 4 * x + 2 * y + c
        for k in range(N_DEV):
            @pl.when(k != me)
            def _(k=k):
                pltpu.make_async_remote_copy(
                    src_ref=p_ref, dst_ref=buf.at[me],
                    send_sem=send_sems.at[k], recv_sem=recv_sems.at[me],
                    device_id=(k // 4, (k // 2) % 2, k % 2), device_id_type=MESH).start()
        buf[me] = p_ref[...]
        for k in range(N_DEV):
            @pl.when(k != me)
            def _(k=k):
                pltpu.make_async_remote_copy(
                    src_ref=p_ref, dst_ref=buf.at[k],
                    send_sem=send_sems.at[k], recv_sem=recv_sems.at[k],
                    device_id=(x, y, c), device_id_type=MESH).wait()
        g = buf[0]
        for k in range(1, N_DEV):
            g = g + buf[k]
        g_ref[...] = g
        d_ref[...], mo_ref[...], vo_ref[...] = _adamw(w_ref[...], g, m_ref[...], v_ref[...])
        loss = (0.5 / D_MODEL) * jnp.sum(g[rows - loss_rows:, :])
        loss_ref[...] = jnp.full(loss_ref.shape, loss, F32)

    vm = pl.BlockSpec(memory_space=pltpu.VMEM)
    slab = jax.ShapeDtypeStruct((rows, LANES), F32)
    return pl.pallas_call(
        body,
        name="small_step",
        in_specs=[vm] * 4,
        out_specs=[vm] * 5,
        out_shape=[slab, slab, slab, slab, jax.ShapeDtypeStruct((SUBLANES, LANES), F32)],
        scratch_shapes=[pltpu.VMEM((N_DEV, rows, LANES), F32),
                        pltpu.SemaphoreType.DMA((N_DEV,)), pltpu.SemaphoreType.DMA((N_DEV,))],
    )(partials, w, m, v)


BIG_ORDER = ("w_in", "w_out", "w_up", "w_down")
WEIGHT_ORDER = ("norm1_g", "w_in", "pool_w", "pool_scale", "pool_out_g", "attn_out_g", "w_out", "norm2_g",
                "w_up", "w_down", "final_g")


def _halves(a):
    return a.reshape(2, a.shape[0] // 2, a.shape[1])


def kernel(x, norm1_g, w_in, pool_w, pool_scale, pool_out_g, attn_out_g, w_out, norm2_g, w_up, w_down, final_g, loss_target, m_norm1_g, m_w_in, m_pool_w, m_pool_scale, m_pool_out_g, m_attn_out_g, m_w_out, m_norm2_g, m_w_up, m_w_down, m_final_g, v_norm1_g, v_w_in, v_pool_w, v_pool_scale, v_pool_out_g, v_attn_out_g, v_w_out, v_norm2_g, v_w_up, v_w_down, v_final_g):
    w = dict(norm1_g=norm1_g, w_in=w_in, pool_w=pool_w, pool_scale=pool_scale, pool_out_g=pool_out_g,
             attn_out_g=attn_out_g, w_out=w_out, norm2_g=norm2_g, w_up=w_up, w_down=w_down, final_g=final_g)
    m = dict(norm1_g=m_norm1_g, w_in=m_w_in, pool_w=m_pool_w, pool_scale=m_pool_scale, pool_out_g=m_pool_out_g,
             attn_out_g=m_attn_out_g, w_out=m_w_out, norm2_g=m_norm2_g, w_up=m_w_up, w_down=m_w_down,
             final_g=m_final_g)
    v = dict(norm1_g=v_norm1_g, w_in=v_w_in, pool_w=v_pool_w, pool_scale=v_pool_scale, pool_out_g=v_pool_out_g,
             attn_out_g=v_attn_out_g, w_out=v_w_out, norm2_g=v_norm2_g, w_up=v_w_up, w_down=v_w_down,
             final_g=v_final_g)

    gathered = _gather_weights([_halves(w[n].astype(BF16)) for n in BIG_ORDER])
    w_in_g = gathered[0].reshape(N_CHIPS, D_MODEL, 512)
    w_out_g = gathered[1].reshape(D_MODEL, D_MODEL)
    w_up_g = gathered[2].reshape(N_CHIPS, D_MODEL, 1024)
    w_down_g = gathered[3].reshape(N_CHIPS, 1024, D_MODEL)

    small = {n: w[n].reshape(1, -1) for n in ("norm1_g", "pool_scale", "pool_out_g", "attn_out_g", "norm2_g", "final_g")}
    dx, big, little = _local_step(x[0], loss_target[0], w_in_g, w_out_g, w_up_g, w_down_g, pool_w.astype(BF16), small)

    core = lax.axis_index("c").astype(jnp.int32).reshape(1)
    parts = [big[n].reshape(N_CHIPS, 2, -1, big[n].shape[-1]) for n in BIG_ORDER]
    from_sibling = _swap_halves(parts)
    chip_sums = [_add_own_half(core, p, r) for p, r in zip(parts, from_sibling)]
    by_chip = _scatter_to_owners(chip_sums)
    halves = [_sum_chips(p) for p in by_chip]
    full = _join_halves(halves)

    grads, deltas, new_m, new_v = {}, {}, {}, {}
    for n, g in zip(BIG_ORDER, full):
        grads[n] = g.reshape(w[n].shape)
        deltas[n], new_m[n], new_v[n] = _adamw_big(w[n], grads[n], m[n], v[n])

    loss_rows = D_MODEL // LANES
    slab_g = _pack([little[n] for n in SMALL_ORDER] + [little["loss_sq"]])
    zeros = jnp.zeros((loss_rows, LANES), F32)
    slab_w = _pack([w[n] for n in SMALL_ORDER] + [zeros])
    slab_m = _pack([m[n] for n in SMALL_ORDER] + [zeros])
    slab_v = _pack([v[n] for n in SMALL_ORDER] + [zeros])
    g_s, d_s, m_s, v_s, loss = _small_step(slab_g, slab_w, slab_m, slab_v, loss_rows)
    shapes = [w[n].shape for n in SMALL_ORDER]
    for slab, dst in ((g_s, grads), (d_s, deltas), (m_s, new_m), (v_s, new_v)):
        for n, val in zip(SMALL_ORDER, _unpack(slab, shapes)):
            dst[n] = val

    return (loss[0, 0], dx[None], *[grads[n] for n in WEIGHT_ORDER], *[deltas[n] for n in WEIGHT_ORDER],
            *[new_m[n] for n in WEIGHT_ORDER], *[new_v[n] for n in WEIGHT_ORDER])
```

```python
import functools

import jax
import jax.numpy as jnp
from jax import lax
from jax.experimental import pallas as pl
from jax.experimental.pallas import tpu as pltpu

F32 = jnp.float32
BF16 = jnp.bfloat16

D_MODEL = 1024
D_POOL = 512
D_ATTN = 512
POOL_WINDOWS = (2, 4, 8, 16)
POOL_GROUP_DIM = 128
POOL_HALO = 16
HEAD_DIM = 64
HEADS_PER_BLOCK = 2
D_FF = 4096
N_CHIPS = 4
N_DEV = 8
EPS = 1e-6
ATTN_SCALE = 0.125
ATTN_TILE = 256
ATTN_ROW_CHUNKS = 2
DEAD_LOG = -105.0
ROW_TILE = 512
LANES = 128

ADAM_LR = 0.001
ADAM_B1 = 0.9
ADAM_B2 = 0.999
ADAM_EPS = 1e-08
ADAM_WD = 0.01
ADAM_STEP = 10

MESH = pl.DeviceIdType.MESH
ANY = pl.BlockSpec(memory_space=pl.ANY)
VMEM_LIMIT = 56 * 1024 * 1024


def _nn(a, b):
    return jnp.dot(a, b, preferred_element_type=F32)


def _nt(a, b):
    return lax.dot_general(a, b, (((1,), (1,)), ((), ())), preferred_element_type=F32)


def _tn(a, b):
    return lax.dot_general(a, b, (((0,), (0,)), ((), ())), preferred_element_type=F32)


def _rms(x):
    return lax.rsqrt(jnp.mean(x * x, axis=-1, keepdims=True) + EPS)


def _rms_bwd(dy, n, r, g):
    dn = dy * g
    return r * (dn - n * jnp.mean(dn * n, axis=-1, keepdims=True))


def _params(**kw):
    return pltpu.CompilerParams(vmem_limit_bytes=VMEM_LIMIT, **kw)


def _run_at(first, middle, last, phases):
    pl.when(first)(phases[0])
    if len(phases) == 3:
        pl.when(middle)(phases[1])
    return lambda: pl.when(last)(phases[-1])


def _in_proj(x, g1, w_in, shards=()):
    S = x.shape[0]
    ts = ROW_TILE
    n = len(shards)

    def body(*refs):
        x_ref, g_ref, w_ref = refs[:3]
        hn_ref, u_ref, qkv_ref = refs[3 + n:6 + n]
        i = pl.program_id(0)
        j = pl.program_id(1)
        finish = lambda: None
        if n:
            steps = pl.num_programs(0)
            finish = _run_at(jnp.logical_and(i == 0, j == 0), jnp.logical_and(i == steps // 2, j == 0),
                             jnp.logical_and(i == steps - 1, j == 3),
                             _gather_phases(refs[3:3 + n], refs[6 + n:6 + 2 * n], *refs[6 + 2 * n:]))

        @pl.when(j == 0)
        def _():
            xf = x_ref[...]
            hn_ref[...] = (xf * _rms(xf) * g_ref[...]).astype(BF16)

        p = _nn(hn_ref[...], w_ref[0])

        @pl.when(j == 0)
        def _():
            u_ref[...] = p

        @pl.when(j > 0)
        def _():
            qkv_ref[0] = p.astype(BF16)

        finish()

    return pl.pallas_call(
        body,
        name="in_proj",
        grid=(S // ts, 4),
        in_specs=[
            pl.BlockSpec((ts, D_MODEL), lambda i, j: (i, 0)),
            pl.BlockSpec((1, D_MODEL), lambda i, j: (0, 0)),
            pl.BlockSpec((1, D_MODEL, 512), lambda i, j: (j, 0, 0)),
        ] + [ANY] * n,
        out_specs=[
            pl.BlockSpec((ts, D_MODEL), lambda i, j: (i, 0)),
            pl.BlockSpec((ts, D_POOL), lambda i, j: (i, 0)),
            pl.BlockSpec((1, ts, 512), lambda i, j: (jnp.maximum(j - 1, 0), i, 0)),
        ] + [ANY] * n,
        out_shape=[
            jax.ShapeDtypeStruct((S, D_MODEL), BF16),
            jax.ShapeDtypeStruct((S, D_POOL), F32),
            jax.ShapeDtypeStruct((3, S, 512), BF16),
        ] + _gather_shapes(shards),
        scratch_shapes=_gather_sems(n) if n else [],
        compiler_params=_params(),
    )(x, g1, w_in, *shards)


def _pool_counts(first_row, rows):
    t = first_row + lax.broadcasted_iota(jnp.int32, (rows, 1), 0)
    return [1.0 / jnp.minimum(t + 1, w).astype(F32) for w in POOL_WINDOWS]


def _pooled(u_tile, halo, first_row):
    ts = u_tile.shape[0]
    inv = _pool_counts(first_row, ts)
    outs = []
    for g, w in enumerate(POOL_WINDOWS):
        lanes = slice(g * POOL_GROUP_DIM, (g + 1) * POOL_GROUP_DIM)
        xg = u_tile[:, lanes]
        acc = jnp.concatenate([halo[:, lanes], xg], axis=0)
        shift = 1
        while shift < w:
            acc = acc + pltpu.roll(acc, shift, axis=0)
            shift *= 2
        outs.append(acc[POOL_HALO:, :] * inv[g] - xg)
    return outs


def _pool_fwd(u, pool_w, pool_scale):
    S = u.shape[0]
    ts = ROW_TILE
    hb = ts // POOL_HALO

    def body(u_ref, halo_ref, w_ref, s_ref, y_ref):
        i = pl.program_id(0)
        halo = jnp.where(i == 0, 0.0, halo_ref[...])
        pooled = _pooled(u_ref[...], halo, i * ts)
        for g in range(len(POOL_WINDOWS)):
            lanes = slice(g * POOL_GROUP_DIM, (g + 1) * POOL_GROUP_DIM)
            y_ref[:, lanes] = _nn(pooled[g].astype(BF16), w_ref[g]) * s_ref[:, lanes]

    return pl.pallas_call(
        body,
        name="pool_fwd",
        grid=(S // ts,),
        in_specs=[
            pl.BlockSpec((ts, D_POOL), lambda i: (i, 0)),
            pl.BlockSpec((POOL_HALO, D_POOL), lambda i: (jnp.maximum(i * hb - 1, 0), 0)),
            pl.BlockSpec((4, POOL_GROUP_DIM, POOL_GROUP_DIM), lambda i: (0, 0, 0)),
            pl.BlockSpec((1, D_POOL), lambda i: (0, 0)),
        ],
        out_specs=pl.BlockSpec((ts, D_POOL), lambda i: (i, 0)),
        out_shape=jax.ShapeDtypeStruct((S, D_POOL), F32),
        compiler_params=_params(),
    )(u, u, pool_w, pool_scale)


def _head_masks():
    lane = lax.broadcasted_iota(jnp.int32, (1, LANES), 1)
    first = lane < HEAD_DIM
    return [first, jnp.logical_not(first)]


def _tri_masks(t):
    row = lax.broadcasted_iota(jnp.int32, (t, t), 0)
    col = lax.broadcasted_iota(jnp.int32, (t, t), 1)
    return row, col


def _split_bf16(x):
    hi = x.astype(BF16)
    lo = (x - hi.astype(F32)).astype(BF16)
    return hi, lo


def _log_sigmoids(z):
    sp = jnp.log(1.0 + jnp.exp(-jnp.abs(z)))
    ls = jnp.minimum(z, 0.0) - sp
    return ls, ls - z


def _attn_fwd(qkv, shards=()):
    S = qkv.shape[1]
    t = ATTN_TILE
    n = len(shards)
    nblk = D_ATTN // LANES

    def body(*refs):
        q_ref, k_ref, v_ref = refs[:3]
        o_ref = refs[3 + n]
        vh_ref, acc_ref, z_ref = refs[4 + 2 * n:7 + 2 * n]
        hp = pl.program_id(0)
        i = pl.program_id(1)
        finish = lambda: None
        if n:
            finish = _run_at(jnp.logical_and(hp == 0, i == 0), jnp.logical_and(hp == nblk // 2, i == 0),
                             jnp.logical_and(hp == nblk - 1, i == pl.num_programs(1) - 1),
                             _gather_phases(refs[3:3 + n], refs[4 + n:4 + 2 * n], *refs[7 + 2 * n:]))
        masks = _head_masks()

        @pl.when(i == 0)
        def _():
            vv = v_ref[0]
            for h in range(HEADS_PER_BLOCK):
                vh_ref[h] = jnp.where(masks[h], vv, jnp.zeros_like(vv))

        row, col = _tri_masks(t)
        later = (row > col).astype(BF16)
        causal = col < row
        qs = q_ref[0] * ATTN_SCALE
        units = [(h, r) for h in range(HEADS_PER_BLOCK) for r in range(ATTN_ROW_CHUNKS)]
        rc = t // ATTN_ROW_CHUNKS
        qu = [jnp.where(masks[h], qs, jnp.zeros_like(qs))[r * rc:(r + 1) * rc] for h, r in units]
        causal_u = [causal[r * rc:(r + 1) * rc] for _, r in units]

        def scores(j, slot):
            kj = k_ref[0, pl.ds(pl.multiple_of(j * t, t), t), :]
            for u in range(len(units)):
                z_ref[slot, u] = _nt(qu[u], kj)

        def tile(j, left, slot, carry, diag):
            cs = list(carry)
            keys = pl.ds(pl.multiple_of(j * t, t), t)
            ls, tail = [None] * len(units), [None] * len(units)
            for u in range(len(units)):
                ls[u], l1m = _log_sigmoids(z_ref[slot, u])
                if diag:
                    l1m = jnp.where(causal_u[u], l1m, 0.0)
                hi, lo = _split_bf16(l1m)
                tail[u] = _nn(hi, later) + _nn(lo, later) + cs[u]
                cs[u] = cs[u] + jnp.sum(l1m, axis=1, keepdims=True)
            top = cs[0]
            for u in range(1, len(units)):
                top = jnp.maximum(top, cs[u])
            go = jnp.logical_and(left > 0, jnp.max(top) > DEAD_LOG)
            scores(jnp.maximum(j - 1, 0), 1 - slot)
            for u, (h, _) in enumerate(units):
                a = jnp.exp(ls[u] + tail[u])
                if diag:
                    a = jnp.where(causal_u[u], a, 0.0)
                pv = _nn(a.astype(BF16), vh_ref[h, keys, :])
                if diag:
                    acc_ref[u] = pv
                else:
                    acc_ref[u] += pv
            return (go, *cs)

        scores(i, 0)
        state = (jnp.int32(0), *tile(i, i, 0, [jnp.zeros((rc, 1), F32)] * len(units), True))

        def step(state):
            jj = state[0]
            return (jj + 1, *tile(i - 1 - jj, i - 1 - jj, (jj + 1) % 2, state[2:], False))

        lax.while_loop(lambda s: s[1], step, state)
        out = None
        for h in range(HEADS_PER_BLOCK):
            rows = jnp.concatenate([acc_ref[u] for u, (hh, _) in enumerate(units) if hh == h], axis=0)
            out = rows if out is None else out + rows
        o_ref[...] = out
        finish()

    return pl.pallas_call(
        body,
        name="attn_fwd",
        grid=(nblk, S // t),
        in_specs=[
            pl.BlockSpec((1, t, LANES), lambda hp, i: (0, i, hp)),
            pl.BlockSpec((1, S, LANES), lambda hp, i: (1, 0, hp)),
            pl.BlockSpec((1, S, LANES), lambda hp, i: (2, 0, hp)),
        ] + [ANY] * n,
        out_specs=[pl.BlockSpec((t, LANES), lambda hp, i: (i, hp))] + [ANY] * n,
        out_shape=[jax.ShapeDtypeStruct((S, D_ATTN), F32)] + _gather_shapes(shards),
        scratch_shapes=[pltpu.VMEM((HEADS_PER_BLOCK, S, LANES), BF16),
                        pltpu.VMEM((HEADS_PER_BLOCK * ATTN_ROW_CHUNKS, t // ATTN_ROW_CHUNKS, LANES), F32),
                        pltpu.VMEM((2, HEADS_PER_BLOCK * ATTN_ROW_CHUNKS, t // ATTN_ROW_CHUNKS, t), F32)]
        + (_gather_sems(n) if n else []),
        compiler_params=_params(),
    )(qkv, qkv, qkv, *shards)


def _out_proj(y_pool, y_attn, x, g_pool, g_attn, w_out, g2):
    S = x.shape[0]
    ts = ROW_TILE

    def body(yp_ref, ya_ref, x_ref, gp_ref, ga_ref, w_ref, g2_ref, mixed_ref, h1_ref, hn2_ref):
        yp = yp_ref[...]
        ya = ya_ref[...]
        mixed = jnp.concatenate([yp * _rms(yp) * gp_ref[...], ya * _rms(ya) * ga_ref[...]], axis=-1).astype(BF16)
        mixed_ref[...] = mixed
        h1 = x_ref[...] + _nn(mixed, w_ref[...])
        h1_ref[...] = h1
        hn2_ref[...] = (h1 * _rms(h1) * g2_ref[...]).astype(BF16)

    row = lambda w: pl.BlockSpec((ts, w), lambda i: (i, 0))
    vec = lambda w: pl.BlockSpec((1, w), lambda i: (0, 0))
    return pl.pallas_call(
        body,
        name="out_proj",
        grid=(S // ts,),
        in_specs=[row(D_POOL), row(D_ATTN), row(D_MODEL), vec(D_POOL), vec(D_ATTN),
                  pl.BlockSpec((D_MODEL, D_MODEL), lambda i: (0, 0)), vec(D_MODEL)],
        out_specs=[row(D_MODEL), row(D_MODEL), row(D_MODEL)],
        out_shape=[
            jax.ShapeDtypeStruct((S, D_MODEL), BF16),
            jax.ShapeDtypeStruct((S, D_MODEL), F32),
            jax.ShapeDtypeStruct((S, D_MODEL), BF16),
        ],
        compiler_params=_params(),
    )(y_pool, y_attn, x, g_pool, g_attn, w_out, g2)


def _mlp_fwd(hn2, h1, w_up, w_down, g_final, target):
    S = hn2.shape[0]
    ts = ROW_TILE
    nf = D_FF // 1024

    def body(hn2_ref, h1_ref, wu_ref, wd_ref, gf_ref, tg_ref, r_ref, dh2_ref, dh2b_ref, lsq_ref, dgf_ref, acc_ref):
        i = pl.program_id(0)
        c = pl.program_id(1)
        r = jnp.maximum(_nn(hn2_ref[...], wu_ref[0]), 0.0)
        r_ref[...] = r.astype(BF16)
        contrib = _nn((r * r).astype(BF16), wd_ref[0])

        @pl.when(c == 0)
        def _():
            acc_ref[...] = contrib

        @pl.when(c > 0)
        def _():
            acc_ref[...] += contrib

        @pl.when(jnp.logical_and(i == 0, c == 0))
        def _():
            lsq_ref[...] = jnp.zeros_like(lsq_ref)
            dgf_ref[...] = jnp.zeros_like(dgf_ref)

        @pl.when(c == nf - 1)
        def _():
            h2 = h1_ref[...] + acc_ref[...]
            rf = _rms(h2)
            n = h2 * rf
            gf = gf_ref[...]
            e = n * gf - tg_ref[...]
            lsq_ref[...] += jnp.sum(e * e, axis=0, keepdims=True)
            dy = e * (1.0 / D_MODEL)
            dgf_ref[...] += jnp.sum(dy * n, axis=0, keepdims=True)
            dh2 = _rms_bwd(dy, n, rf, gf)
            dh2_ref[...] = dh2
            dh2b_ref[...] = dh2.astype(BF16)

    row = lambda w: pl.BlockSpec((ts, w), lambda i, c: (i, 0))
    vec = lambda w: pl.BlockSpec((1, w), lambda i, c: (0, 0))
    return pl.pallas_call(
        body,
        name="mlp_fwd",
        grid=(S // ts, nf),
        in_specs=[row(D_MODEL), row(D_MODEL),
                  pl.BlockSpec((1, D_MODEL, 1024), lambda i, c: (c, 0, 0)),
                  pl.BlockSpec((1, 1024, D_MODEL), lambda i, c: (c, 0, 0)),
                  vec(D_MODEL), row(D_MODEL)],
        out_specs=[pl.BlockSpec((ts, 1024), lambda i, c: (i, c)), row(D_MODEL), row(D_MODEL),
                   vec(D_MODEL), vec(D_MODEL)],
        out_shape=[
            jax.ShapeDtypeStruct((S, D_FF), BF16),
            jax.ShapeDtypeStruct((S, D_MODEL), F32),
            jax.ShapeDtypeStruct((S, D_MODEL), BF16),
            jax.ShapeDtypeStruct((1, D_MODEL), F32),
            jax.ShapeDtypeStruct((1, D_MODEL), F32),
        ],
        scratch_shapes=[pltpu.VMEM((ts, D_MODEL), F32)],
        compiler_params=_params(),
    )(hn2, h1, w_up, w_down, g_final, target)


def _mlp_bwd_dx(dh2b, r_act, w_down, w_up, h1, dh2, g2):
    S = h1.shape[0]
    ts = ROW_TILE
    nf = D_FF // 1024

    def body(dh2b_ref, r_ref, wd_ref, wu_ref, h1_ref, dh2_ref, g2_ref, dup_ref, dh1_ref, dg2_ref, acc_ref):
        i = pl.program_id(0)
        c = pl.program_id(1)
        dup = (_nt(dh2b_ref[...], wd_ref[0]) * (2.0 * r_ref[...].astype(F32))).astype(BF16)
        dup_ref[...] = dup
        contrib = _nt(dup, wu_ref[0])

        @pl.when(c == 0)
        def _():
            acc_ref[...] = contrib

        @pl.when(c > 0)
        def _():
            acc_ref[...] += contrib

        @pl.when(jnp.logical_and(i == 0, c == 0))
        def _():
            dg2_ref[...] = jnp.zeros_like(dg2_ref)

        @pl.when(c == nf - 1)
        def _():
            dhn2 = acc_ref[...]
            h1v = h1_ref[...]
            r2 = _rms(h1v)
            n2 = h1v * r2
            dg2_ref[...] += jnp.sum(dhn2 * n2, axis=0, keepdims=True)
            dh1_ref[...] = dh2_ref[...] + _rms_bwd(dhn2, n2, r2, g2_ref[...])

    row = lambda w: pl.BlockSpec((ts, w), lambda i, c: (i, 0))
    vec = lambda w: pl.BlockSpec((1, w), lambda i, c: (0, 0))
    chunk = pl.BlockSpec((ts, 1024), lambda i, c: (i, c))
    return pl.pallas_call(
        body,
        name="mlp_bwd_dx",
        grid=(S // ts, nf),
        in_specs=[row(D_MODEL), chunk,
                  pl.BlockSpec((1, 1024, D_MODEL), lambda i, c: (c, 0, 0)),
                  pl.BlockSpec((1, D_MODEL, 1024), lambda i, c: (c, 0, 0)),
                  row(D_MODEL), row(D_MODEL), vec(D_MODEL)],
        out_specs=[chunk, row(D_MODEL), vec(D_MODEL)],
        out_shape=[
            jax.ShapeDtypeStruct((S, D_FF), BF16),
            jax.ShapeDtypeStruct((S, D_MODEL), F32),
            jax.ShapeDtypeStruct((1, D_MODEL), F32),
        ],
        scratch_shapes=[pltpu.VMEM((ts, D_MODEL), F32)],
        compiler_params=_params(),
    )(dh2b, r_act, w_down, w_up, h1, dh2, g2)


def _mlp_bwd_dw(r_act, dh2b, hn2, dup):
    S = hn2.shape[0]
    ts = ROW_TILE
    nf = D_FF // 1024

    def body(r_ref, dh2b_ref, hn2_ref, dup_ref, dwd_ref, dwu_ref, dwd_acc, dwu_acc):
        tt = pl.program_id(1)
        r = r_ref[...].astype(F32)
        dwd = _tn((r * r).astype(BF16), dh2b_ref[...])
        dwu = _tn(hn2_ref[...], dup_ref[...])

        @pl.when(tt == 0)
        def _():
            dwd_acc[...] = dwd
            dwu_acc[...] = dwu

        @pl.when(tt > 0)
        def _():
            dwd_acc[...] += dwd
            dwu_acc[...] += dwu

        @pl.when(tt == pl.num_programs(1) - 1)
        def _():
            dwd_ref[0] = dwd_acc[...].astype(BF16)
            dwu_ref[0] = dwu_acc[...].astype(BF16)

    row = pl.BlockSpec((ts, D_MODEL), lambda c, tt: (tt, 0))
    chunk = pl.BlockSpec((ts, 1024), lambda c, tt: (tt, c))
    return pl.pallas_call(
        body,
        name="mlp_bwd_dw",
        grid=(nf, S // ts),
        in_specs=[chunk, row, row, chunk],
        out_specs=[pl.BlockSpec((1, 1024, D_MODEL), lambda c, tt: (c, 0, 0)),
                   pl.BlockSpec((1, D_MODEL, 1024), lambda c, tt: (c, 0, 0))],
        out_shape=[
            jax.ShapeDtypeStruct((nf, 1024, D_MODEL), BF16),
            jax.ShapeDtypeStruct((nf, D_MODEL, 1024), BF16),
        ],
        scratch_shapes=[pltpu.VMEM((1024, D_MODEL), F32), pltpu.VMEM((D_MODEL, 1024), F32)],
        compiler_params=_params(),
    )(r_act, dh2b, hn2, dup)


def _out_bwd(dh1, w_out, mixed, y_pool, y_attn, g_pool, g_attn):
    S = dh1.shape[0]
    ts = ROW_TILE

    def body(dh1_ref, w_ref, mixed_ref, yp_ref, ya_ref, gp_ref, ga_ref,
             dyp_ref, dya_ref, dw_ref, dgp_ref, dga_ref, dw_acc):
        i = pl.program_id(0)
        dh1b = dh1_ref[...].astype(BF16)
        dmixed = _nt(dh1b, w_ref[...])
        dw = _tn(mixed_ref[...], dh1b)

        @pl.when(i == 0)
        def _():
            dw_acc[...] = dw
            dgp_ref[...] = jnp.zeros_like(dgp_ref)
            dga_ref[...] = jnp.zeros_like(dga_ref)

        @pl.when(i > 0)
        def _():
            dw_acc[...] += dw

        @pl.when(i == pl.num_programs(0) - 1)
        def _():
            dw_ref[...] = dw_acc[...].astype(BF16)

        for y_ref, g_ref, dy_ref, dg_ref, lanes in (
                (yp_ref, gp_ref, dyp_ref, dgp_ref, slice(0, D_POOL)),
                (ya_ref, ga_ref, dya_ref, dga_ref, slice(D_POOL, D_MODEL))):
            y = y_ref[...]
            r = _rms(y)
            n = y * r
            dm = dmixed[:, lanes]
            dg_ref[...] += jnp.sum(dm * n, axis=0, keepdims=True)
            dy_ref[...] = _rms_bwd(dm, n, r, g_ref[...])

    row = lambda w: pl.BlockSpec((ts, w), lambda i: (i, 0))
    vec = lambda w: pl.BlockSpec((1, w), lambda i: (0, 0))
    full = pl.BlockSpec((D_MODEL, D_MODEL), lambda i: (0, 0))
    return pl.pallas_call(
        body,
        name="out_bwd",
        grid=(S // ts,),
        in_specs=[row(D_MODEL), full, row(D_MODEL), row(D_POOL), row(D_ATTN), vec(D_POOL), vec(D_ATTN)],
        out_specs=[row(D_POOL), row(D_ATTN), full, vec(D_POOL), vec(D_ATTN)],
        out_shape=[
            jax.ShapeDtypeStruct((S, D_POOL), F32),
            jax.ShapeDtypeStruct((S, D_ATTN), F32),
            jax.ShapeDtypeStruct((D_MODEL, D_MODEL), BF16),
            jax.ShapeDtypeStruct((1, D_POOL), F32),
            jax.ShapeDtypeStruct((1, D_ATTN), F32),
        ],
        scratch_shapes=[pltpu.VMEM((D_MODEL, D_MODEL), F32)],
        compiler_params=_params(),
    )(dh1, w_out, mixed, y_pool, y_attn, g_pool, g_attn)


def _attn_bwd(qkv, o, do, partials=()):
    S = qkv.shape[1]
    t = ATTN_TILE
    n = len(partials)
    nblk = D_ATTN // LANES

    def body(*refs):
        q_ref, k_ref, v_ref, o_ref, do_ref = refs[:5]
        dq_ref, dk_ref, dv_ref = refs[5 + n:8 + n]
        kh_ref, dk_acc, dv_acc, dq_acc, z_ref, da_ref = refs[8 + 2 * n:14 + 2 * n]
        hp = pl.program_id(0)
        i = pl.program_id(1)
        finish = lambda: None
        if n:
            finish = _run_at(jnp.logical_and(hp == 0, i == 0), None,
                             jnp.logical_and(hp == nblk - 1, i == pl.num_programs(1) - 1),
                             _reduce_phases(refs[5:5 + n], refs[8 + n:8 + 2 * n], *refs[14 + 2 * n:]))
        masks = _head_masks()

        @pl.when(i == 0)
        def _():
            kk = k_ref[0]
            for h in range(HEADS_PER_BLOCK):
                kh_ref[h] = jnp.where(masks[h], kk, jnp.zeros_like(kk))
            dk_acc[...] = jnp.zeros_like(dk_acc)
            dv_acc[...] = jnp.zeros_like(dv_acc)

        row, col = _tri_masks(t)
        later = (row > col).astype(BF16)
        from_s = (row >= col).astype(BF16)
        causal = col < row
        qs = q_ref[0] * ATTN_SCALE
        dob = do_ref[...].astype(BF16)
        d_all = dob.astype(F32) * o_ref[...]
        qh = [jnp.where(masks[h], qs, jnp.zeros_like(qs)) for h in range(HEADS_PER_BLOCK)]
        doh = [jnp.where(masks[h], dob, jnp.zeros_like(dob)) for h in range(HEADS_PER_BLOCK)]
        d_row = [jnp.sum(jnp.where(masks[h], d_all, 0.0), axis=1, keepdims=True) for h in range(HEADS_PER_BLOCK)]

        heads = range(HEADS_PER_BLOCK)

        def scores(j, slot):
            keys = pl.ds(pl.multiple_of(j * t, t), t)
            kj = k_ref[0, keys, :]
            vj = v_ref[0, keys, :]
            for h in heads:
                z_ref[slot, h] = _nt(qh[h], kj)
                da_ref[slot, h] = _nt(doh[h], vj)

        def tile(j, left, slot, carry, diag):
            c_l, c_g = list(carry[0:2]), list(carry[2:4])
            keys = pl.ds(pl.multiple_of(j * t, t), t)
            ls, tail = [None, None], [None, None]
            for h in heads:
                ls[h], l1m = _log_sigmoids(z_ref[slot, h])
                if diag:
                    l1m = jnp.where(causal, l1m, 0.0)
                hi, lo = _split_bf16(l1m)
                tail[h] = _nn(hi, later) + _nn(lo, later) + c_l[h]
                c_l[h] = c_l[h] + jnp.sum(l1m, axis=1, keepdims=True)
            go = jnp.logical_and(left > 0, jnp.max(jnp.maximum(c_l[0], c_l[1])) > DEAD_LOG)
            g, before = [None, None], [None, None]
            dv = None
            for h in heads:
                a = jnp.exp(ls[h] + tail[h])
                if diag:
                    a = jnp.where(causal, a, 0.0)
                ab = a.astype(BF16)
                g[h] = ab.astype(F32) * da_ref[slot, h]
                ghi, glo = _split_bf16(g[h])
                before[h] = d_row[h] - (_nn(ghi, from_s) + _nn(glo, from_s) + c_g[h])
                c_g[h] = c_g[h] + jnp.sum(g[h], axis=1, keepdims=True)
                part = _tn(ab, doh[h])
                dv = part if dv is None else dv + part
            dv_acc[keys, :] += dv
            scores(jnp.maximum(j - 1, 0), 1 - slot)
            dk = None
            for h in heads:
                beta = jnp.exp(ls[h])
                dz = g[h] * (1.0 - beta) - before[h] * beta
                if diag:
                    dz = jnp.where(causal, dz, 0.0)
                dzb = dz.astype(BF16)
                dqh = _nn(dzb, kh_ref[h, keys, :])
                if diag:
                    dq_acc[h] = dqh
                else:
                    dq_acc[h] += dqh
                part = _tn(dzb, qh[h])
                dk = part if dk is None else dk + part
            dk_acc[keys, :] += dk
            return (go, *c_l, *c_g)

        scores(i, 0)
        state = (jnp.int32(0), *tile(i, i, 0, [jnp.zeros((t, 1), F32)] * 4, True))

        def step(state):
            jj = state[0]
            return (jj + 1, *tile(i - 1 - jj, i - 1 - jj, (jj + 1) % 2, state[2:], False))

        lax.while_loop(lambda s: s[1], step, state)
        dq_ref[...] = ((dq_acc[0] + dq_acc[1]) * ATTN_SCALE).astype(BF16)

        @pl.when(i == pl.num_programs(1) - 1)
        def _():
            dk_ref[...] = dk_acc[...].astype(BF16)
            dv_ref[...] = dv_acc[...].astype(BF16)

        finish()

    qtile = pl.BlockSpec((t, LANES), lambda hp, i: (i, hp))
    whole = pl.BlockSpec((S, LANES), lambda hp, i: (0, hp))
    return pl.pallas_call(
        body,
        name="attn_bwd",
        grid=(nblk, S // t),
        in_specs=[
            pl.BlockSpec((1, t, LANES), lambda hp, i: (0, i, hp)),
            pl.BlockSpec((1, S, LANES), lambda hp, i: (1, 0, hp)),
            pl.BlockSpec((1, S, LANES), lambda hp, i: (2, 0, hp)),
            qtile, qtile,
        ] + [ANY] * n,
        out_specs=[qtile, whole, whole] + [ANY] * n,
        out_shape=[jax.ShapeDtypeStruct((S, D_ATTN), BF16)] * 3 + _reduce_shapes(partials),
        scratch_shapes=[
            pltpu.VMEM((HEADS_PER_BLOCK, S, LANES), BF16),
            pltpu.VMEM((S, LANES), F32),
            pltpu.VMEM((S, LANES), F32),
            pltpu.VMEM((HEADS_PER_BLOCK, t, LANES), F32),
            pltpu.VMEM((2, HEADS_PER_BLOCK, t, t), F32),
            pltpu.VMEM((2, HEADS_PER_BLOCK, t, t), F32),
        ] + (_reduce_sems(n) if n else []),
        compiler_params=_params(),
    )(qkv, qkv, qkv, o, do, *partials)


def _pool_bwd_w(u, dyp, pool_w, pool_scale):
    S = u.shape[0]
    ts = ROW_TILE
    hb = ts // POOL_HALO

    def body(u_ref, halo_ref, dy_ref, w_ref, s_ref, dp_ref, dw_ref, ds_ref):
        i = pl.program_id(0)
        halo = jnp.where(i == 0, 0.0, halo_ref[...])
        pooled = _pooled(u_ref[...], halo, i * ts)

        @pl.when(i == 0)
        def _():
            dw_ref[...] = jnp.zeros_like(dw_ref)
            ds_ref[...] = jnp.zeros_like(ds_ref)

        for g in range(len(POOL_WINDOWS)):
            lanes = slice(g * POOL_GROUP_DIM, (g + 1) * POOL_GROUP_DIM)
            pg = pooled[g].astype(BF16)
            dy = dy_ref[:, lanes]
            ds_ref[:, lanes] += jnp.sum(dy * _nn(pg, w_ref[g]), axis=0, keepdims=True)
            dmapped = (dy * s_ref[:, lanes]).astype(BF16)
            dp_ref[:, lanes] = _nt(dmapped, w_ref[g])
            dw_ref[g] += _tn(pg, dmapped)

    row = pl.BlockSpec((ts, D_POOL), lambda i: (i, 0))
    vec = pl.BlockSpec((1, D_POOL), lambda i: (0, 0))
    wspec = pl.BlockSpec((4, POOL_GROUP_DIM, POOL_GROUP_DIM), lambda i: (0, 0, 0))
    return pl.pallas_call(
        body,
        name="pool_bwd_w",
        grid=(S // ts,),
        in_specs=[row, pl.BlockSpec((POOL_HALO, D_POOL), lambda i: (jnp.maximum(i * hb - 1, 0), 0)),
                  row, wspec, vec],
        out_specs=[row, wspec, vec],
        out_shape=[
            jax.ShapeDtypeStruct((S, D_POOL), F32),
            jax.ShapeDtypeStruct((4, POOL_GROUP_DIM, POOL_GROUP_DIM), F32),
            jax.ShapeDtypeStruct((1, D_POOL), F32),
        ],
        compiler_params=_params(),
    )(u, u, dyp, pool_w, pool_scale)


def _pool_bwd_u(dpooled):
    S = dpooled.shape[0]
    ts = ROW_TILE
    hb = ts // POOL_HALO
    last = S // ts - 1

    def body(dp_ref, halo_ref, du_ref):
        i = pl.program_id(0)
        dp = dp_ref[...]
        halo = jnp.where(i == last, 0.0, halo_ref[...])
        inv = _pool_counts(i * ts, ts)
        n = ts + POOL_HALO
        for g, w in enumerate(POOL_WINDOWS):
            lanes = slice(g * POOL_GROUP_DIM, (g + 1) * POOL_GROUP_DIM)
            dg = dp[:, lanes]
            acc = jnp.concatenate([dg * inv[g], halo[:, lanes] * (1.0 / w)], axis=0)
            shift = 1
            while shift < w:
                acc = acc + pltpu.roll(acc, n - shift, axis=0)
                shift *= 2
            du_ref[:, lanes] = (acc[:ts, :] - dg).astype(BF16)

    return pl.pallas_call(
        body,
        name="pool_bwd_u",
        grid=(S // ts,),
        in_specs=[pl.BlockSpec((ts, D_POOL), lambda i: (i, 0)),
                  pl.BlockSpec((POOL_HALO, D_POOL), lambda i: (jnp.minimum((i + 1) * hb, (last + 1) * hb - 1), 0))],
        out_specs=pl.BlockSpec((ts, D_POOL), lambda i: (i, 0)),
        out_shape=jax.ShapeDtypeStruct((S, D_POOL), BF16),
        compiler_params=_params(),
    )(dpooled, dpooled)


def _in_bwd(du, dq, dk, dv, hn, w_in, x, dh1, g1):
    S = x.shape[0]
    ts = ROW_TILE // 2

    def body(du_ref, dq_ref, dk_ref, dv_ref, hn_ref, w_ref, x_ref, dh1_ref, g_ref, dx_ref, dw_ref, dg_ref, dw_acc):
        i = pl.program_id(0)

        @pl.when(i == 0)
        def _():
            dw_acc[...] = jnp.zeros_like(dw_acc)
            dg_ref[...] = jnp.zeros_like(dg_ref)

        hn = hn_ref[...]
        dhn = jnp.zeros((ts, D_MODEL), F32)
        for j, dp_ref in enumerate((du_ref, dq_ref, dk_ref, dv_ref)):
            dp = dp_ref[...]
            dhn = dhn + _nt(dp, w_ref[j])
            dw_acc[j] += _tn(hn, dp)
        xv = x_ref[...]
        r1 = _rms(xv)
        n1 = xv * r1
        dg_ref[...] += jnp.sum(dhn * n1, axis=0, keepdims=True)
        dx_ref[...] = dh1_ref[...] + _rms_bwd(dhn, n1, r1, g_ref[...])

        @pl.when(i == pl.num_programs(0) - 1)
        def _():
            dw_ref[...] = dw_acc[...].astype(BF16)

    row = lambda w: pl.BlockSpec((ts, w), lambda i: (i, 0))
    vec = pl.BlockSpec((1, D_MODEL), lambda i: (0, 0))
    wspec = pl.BlockSpec((4, D_MODEL, 512), lambda i: (0, 0, 0))
    return pl.pallas_call(
        body,
        name="in_bwd",
        grid=(S // ts,),
        in_specs=[row(512), row(512), row(512), row(512), row(D_MODEL), wspec, row(D_MODEL), row(D_MODEL), vec],
        out_specs=[row(D_MODEL), wspec, vec],
        out_shape=[
            jax.ShapeDtypeStruct((S, D_MODEL), F32),
            jax.ShapeDtypeStruct((4, D_MODEL, 512), BF16),
            jax.ShapeDtypeStruct((1, D_MODEL), F32),
        ],
        scratch_shapes=[pltpu.VMEM((4, D_MODEL, 512), F32)],
        compiler_params=_params(),
    )(du, dq, dk, dv, hn, w_in, x, dh1, g1)


def _pieces(g):
    return g.reshape(N_CHIPS, 2, -1, g.shape[-1])


def _local_step(x, target, w_in, pool_w, small, full=None, shards=None):
    spread = shards is not None
    if spread:
        hn, u, qkv, w_out = _in_proj(x, small["norm1_g"], w_in, shards[:1])
        w_out = w_out.reshape(D_MODEL, D_MODEL)
    else:
        hn, u, qkv = _in_proj(x, small["norm1_g"], w_in)
        w_out, w_up, w_down = full
    y_pool = _pool_fwd(u, pool_w, small["pool_scale"])
    if spread:
        y_attn, w_up, w_down = _attn_fwd(qkv, shards[1:])
        w_up = w_up.reshape(N_CHIPS, D_MODEL, 1024)
        w_down = w_down.reshape(N_CHIPS, 1024, D_MODEL)
    else:
        (y_attn,) = _attn_fwd(qkv)
    mixed, h1, hn2 = _out_proj(y_pool, y_attn, x, small["pool_out_g"], small["attn_out_g"], w_out, small["norm2_g"])
    r_act, dh2, dh2b, lsq, dgf = _mlp_fwd(hn2, h1, w_up, w_down, small["final_g"], target)

    dup, dh1, dg2 = _mlp_bwd_dx(dh2b, r_act, w_down, w_up, h1, dh2, small["norm2_g"])
    dw_down, dw_up = _mlp_bwd_dw(r_act, dh2b, hn2, dup)
    dyp, dya, dw_out, dgp, dga = _out_bwd(dh1, w_out, mixed, y_pool, y_attn, small["pool_out_g"], small["attn_out_g"])
    if spread:
        dq, dk, dv, dw_out, dw_up, dw_down = _attn_bwd(
            qkv, y_attn, dya, [_pieces(dw_out), _pieces(dw_up), _pieces(dw_down)])
    else:
        dq, dk, dv = _attn_bwd(qkv, y_attn, dya)
    dpooled, dpool_w, dpool_scale = _pool_bwd_w(u, dyp, pool_w, small["pool_scale"])
    du = _pool_bwd_u(dpooled)
    dx, dw_in, dg1 = _in_bwd(du, dq, dk, dv, hn, w_in, x, dh1, small["norm1_g"])

    big = {"w_in": dw_in, "w_out": dw_out, "w_up": dw_up, "w_down": dw_down}
    little = {"norm1_g": dg1, "pool_w": dpool_w, "pool_scale": dpool_scale, "pool_out_g": dgp,
              "attn_out_g": dga, "norm2_g": dg2, "final_g": dgf, "loss_sq": lsq}
    return dx, big, little


def _place():
    x, y, c = lax.axis_index("x"), lax.axis_index("y"), lax.axis_index("c")
    other_chips = [(1 - x, y), (x, 1 - y), (1 - x, 1 - y)]
    return x, y, c, other_chips


def _chip_index(chip):
    return 2 * chip[0] + chip[1]


def _gather_shapes(shards):
    return [jax.ShapeDtypeStruct((N_CHIPS,) + s.shape, s.dtype) for s in shards]


def _gather_sems(n):
    return [pltpu.SemaphoreType.DMA((n, 7)), pltpu.SemaphoreType.DMA((n, 7)), pltpu.SemaphoreType.DMA((n,))]


def _gather_phases(ins, outs, send_sems, recv_sems, local_sems):
    n = len(ins)
    x, y, c, chips = _place()
    me = _chip_index((x, y))
    sibling = (x, y, 1 - c)

    def copy(a, k, chip_idx, half, to, src=None):
        dst = outs[a].at[chip_idx, half]
        return pltpu.make_async_remote_copy(
            src_ref=dst if src is None else src, dst_ref=dst,
            send_sem=send_sems.at[a, k], recv_sem=recv_sems.at[a, k],
            device_id=to, device_id_type=MESH)

    def mine(a):
        return pltpu.make_async_copy(ins[a].at[c], outs[a].at[me, c], local_sems.at[a])

    def first(a):
        return [copy(a, 0, me, c, sibling, src=ins[a].at[c])] + [
            copy(a, 1 + j, me, c, (*chip, c), src=ins[a].at[c]) for j, chip in enumerate(chips)]

    def passed(a, j):
        return copy(a, 4 + j, _chip_index(chips[j]), c, sibling)

    def start():
        for a in range(n):
            mine(a).start()
            for cp in first(a):
                cp.start()

    def forward():
        for a in range(n):
            for j, chip in enumerate(chips):
                copy(a, 1 + j, _chip_index(chip), c, (x, y, c)).wait_recv()
                passed(a, j).start()

    def finish():
        for a in range(n):
            copy(a, 0, me, 1 - c, (x, y, c)).wait_recv()
            for j, chip in enumerate(chips):
                copy(a, 4 + j, _chip_index(chip), 1 - c, (x, y, c)).wait_recv()
        for a in range(n):
            for cp in first(a):
                cp.wait_send()
            for j in range(len(chips)):
                passed(a, j).wait_send()
            mine(a).wait()

    return start, forward, finish


def _gather_weights(shards):
    n = len(shards)

    def body(*refs):
        start, forward, finish = _gather_phases(refs[:n], refs[n:2 * n], *refs[2 * n:])
        start()
        forward()
        finish()

    return pl.pallas_call(
        body,
        name="gather_weights",
        in_specs=[ANY] * n,
        out_specs=[ANY] * n,
        out_shape=_gather_shapes(shards),
        scratch_shapes=_gather_sems(n),
    )(*shards)


def _reduce_shapes(partials):
    return [jax.ShapeDtypeStruct((N_DEV,) + p.shape[2:], p.dtype) for p in partials]


def _reduce_sems(n):
    return [pltpu.SemaphoreType.DMA((n, N_DEV)), pltpu.SemaphoreType.DMA((n, N_DEV)), pltpu.SemaphoreType.DMA((n,))]


def _reduce_phases(ins, outs, send_sems, recv_sems, local_sems):
    n = len(ins)
    x, y, c, _ = _place()
    me = 4 * x + 2 * y + c

    def mine(a):
        return pltpu.make_async_copy(ins[a].at[2 * x + y, c], outs[a].at[me], local_sems.at[a])

    def to_peer(a, k):
        return pltpu.make_async_remote_copy(
            src_ref=ins[a].at[k // 2, k % 2], dst_ref=outs[a].at[me],
            send_sem=send_sems.at[a, k], recv_sem=recv_sems.at[a, me],
            device_id=(k // 4, (k // 2) % 2, k % 2), device_id_type=MESH)

    def from_peer(a, k):
        return pltpu.make_async_remote_copy(
            src_ref=ins[a].at[k // 2, k % 2], dst_ref=outs[a].at[k],
            send_sem=send_sems.at[a, k], recv_sem=recv_sems.at[a, k],
            device_id=(x, y, c), device_id_type=MESH)

    def start():
        for a in range(n):
            mine(a).start()
            for k in range(N_DEV):
                @pl.when(k != me)
                def _(a=a, k=k):
                    to_peer(a, k).start()

    def finish():
        for a in range(n):
            for k in range(N_DEV):
                @pl.when(k != me)
                def _(a=a, k=k):
                    from_peer(a, k).wait_recv()
                    to_peer(a, k).wait_send()
            mine(a).wait()

    return start, finish


def _sum_pieces(parts):
    _, h, cols = parts.shape
    hb = min(h, 256)

    def body(*refs):
        acc = refs[0][0].astype(F32)
        for p in refs[1:N_DEV]:
            acc = acc + p[0].astype(F32)
        refs[N_DEV][...] = acc

    return pl.pallas_call(
        body,
        name="sum_pieces",
        grid=(h // hb,),
        in_specs=[pl.BlockSpec((1, hb, cols), functools.partial(lambda r, k: (k, r, 0), k=k)) for k in range(N_DEV)],
        out_specs=pl.BlockSpec((hb, cols), lambda r: (r, 0)),
        out_shape=jax.ShapeDtypeStruct((h, cols), F32),
    )(*([parts] * N_DEV))


def _join_halves(halves):
    n = len(halves)

    def body(*refs):
        ins, outs = refs[:n], refs[n:2 * n]
        send_sems, recv_sems, local_sems = refs[2 * n:]
        x, y, c, _ = _place()
        local = []
        sends = []
        for a in range(n):
            mine = pltpu.make_async_copy(ins[a], outs[a].at[c], local_sems.at[a])
            mine.start()
            local.append(mine)
            cp = pltpu.make_async_remote_copy(
                src_ref=ins[a], dst_ref=outs[a].at[c],
                send_sem=send_sems.at[a], recv_sem=recv_sems.at[a],
                device_id=(x, y, 1 - c), device_id_type=MESH)
            cp.start()
            sends.append(cp)
        for a in range(n):
            pltpu.make_async_remote_copy(
                src_ref=ins[a], dst_ref=outs[a].at[1 - c],
                send_sem=send_sems.at[a], recv_sem=recv_sems.at[a],
                device_id=(x, y, c), device_id_type=MESH).wait_recv()
        for cp in sends:
            cp.wait_send()
        for cp in local:
            cp.wait()

    return pl.pallas_call(
        body,
        name="join_halves",
        in_specs=[ANY] * n,
        out_specs=[ANY] * n,
        out_shape=[jax.ShapeDtypeStruct((2,) + s.shape, s.dtype) for s in halves],
        scratch_shapes=[pltpu.SemaphoreType.DMA((n,)), pltpu.SemaphoreType.DMA((n,)),
                        pltpu.SemaphoreType.DMA((n,))],
    )(*halves)


def _adamw(w, g, m, v):
    m = ADAM_B1 * m + (1.0 - ADAM_B1) * g
    v = ADAM_B2 * v + (1.0 - ADAM_B2) * jnp.square(g)
    m_hat = m / (1.0 - ADAM_B1 ** ADAM_STEP)
    v_hat = v / (1.0 - ADAM_B2 ** ADAM_STEP)
    delta = -ADAM_LR * (m_hat / (jnp.sqrt(v_hat) + ADAM_EPS) + ADAM_WD * w)
    return delta, m, v


def _adamw_big(w, g, m, v):
    rows, cols = w.shape
    rb = min(rows, 256)

    def body(w_ref, g_ref, m_ref, v_ref, d_ref, mo_ref, vo_ref):
        d_ref[...], mo_ref[...], vo_ref[...] = _adamw(w_ref[...], g_ref[...], m_ref[...], v_ref[...])

    blk = pl.BlockSpec((rb, cols), lambda r: (r, 0))
    return pl.pallas_call(
        body,
        name="adamw_big",
        grid=(rows // rb,),
        in_specs=[blk] * 4,
        out_specs=[blk] * 3,
        out_shape=[jax.ShapeDtypeStruct(w.shape, F32)] * 3,
    )(w, g, m, v)


SMALL_ORDER = ("pool_w", "norm1_g", "pool_scale", "pool_out_g", "attn_out_g", "norm2_g", "final_g")
SUBLANES = 8


def _pack(parts):
    rows = []
    for p in parts:
        p = p.reshape(-1, LANES)
        pad = (-p.shape[0]) % SUBLANES
        if pad:
            p = jnp.pad(p, ((0, pad), (0, 0)))
        rows.append(p)
    return jnp.concatenate(rows, axis=0)


def _unpack(slab, shapes):
    out, r = [], 0
    for shp in shapes:
        size = 1
        for d in shp:
            size *= d
        nrow = size // LANES
        out.append(slab[r:r + nrow].reshape(shp))
        r += nrow + (-nrow) % SUBLANES
    return out


def _small_step(partials, w, m, v, loss_rows, big_partials):
    rows = partials.shape[0]
    nb = len(big_partials)

    def body(*refs):
        p_ref, w_ref, m_ref, v_ref = refs[:4]
        big_in = refs[4:4 + nb]
        g_ref, d_ref, mo_ref, vo_ref, loss_ref = refs[4 + nb:9 + nb]
        big_out = refs[9 + nb:9 + 2 * nb]
        buf, send_sems, recv_sems = refs[9 + 2 * nb:12 + 2 * nb]
        start_big, finish_big = _reduce_phases(big_in, big_out, *refs[12 + 2 * nb:])
        start_big()
        x, y, c, _ = _place()
        me = 4 * x + 2 * y + c
        for k in range(N_DEV):
            @pl.when(k != me)
            def _(k=k):
                pltpu.make_async_remote_copy(
                    src_ref=p_ref, dst_ref=buf.at[me],
                    send_sem=send_sems.at[k], recv_sem=recv_sems.at[me],
                    device_id=(k // 4, (k // 2) % 2, k % 2), device_id_type=MESH).start()
        buf[me] = p_ref[...]
        for k in range(N_DEV):
            @pl.when(k != me)
            def _(k=k):
                pltpu.make_async_remote_copy(
                    src_ref=p_ref, dst_ref=buf.at[k],
                    send_sem=send_sems.at[k], recv_sem=recv_sems.at[k],
                    device_id=(x, y, c), device_id_type=MESH).wait()
        g = buf[0]
        for k in range(1, N_DEV):
            g = g + buf[k]
        g_ref[...] = g
        d_ref[...], mo_ref[...], vo_ref[...] = _adamw(w_ref[...], g, m_ref[...], v_ref[...])
        loss = (0.5 / D_MODEL) * jnp.sum(g[rows - loss_rows:, :])
        loss_ref[...] = jnp.full(loss_ref.shape, loss, F32)
        finish_big()

    vm = pl.BlockSpec(memory_space=pltpu.VMEM)
    slab = jax.ShapeDtypeStruct((rows, LANES), F32)
    return pl.pallas_call(
        body,
        name="small_step",
        in_specs=[vm] * 4 + [ANY] * nb,
        out_specs=[vm] * 5 + [ANY] * nb,
        out_shape=[slab, slab, slab, slab, jax.ShapeDtypeStruct((SUBLANES, LANES), F32)] + _reduce_shapes(big_partials),
        scratch_shapes=[pltpu.VMEM((N_DEV, rows, LANES), F32),
                        pltpu.SemaphoreType.DMA((N_DEV,)), pltpu.SemaphoreType.DMA((N_DEV,))] + _reduce_sems(nb),
    )(partials, w, m, v, *big_partials)


BIG_ORDER = ("w_in", "w_out", "w_up", "w_down")
WEIGHT_ORDER = ("norm1_g", "w_in", "pool_w", "pool_scale", "pool_out_g", "attn_out_g", "w_out", "norm2_g",
                "w_up", "w_down", "final_g")


def _halves(a):
    return a.reshape(2, a.shape[0] // 2, a.shape[1])


def kernel(x, norm1_g, w_in, pool_w, pool_scale, pool_out_g, attn_out_g, w_out, norm2_g, w_up, w_down, final_g, loss_target, m_norm1_g, m_w_in, m_pool_w, m_pool_scale, m_pool_out_g, m_attn_out_g, m_w_out, m_norm2_g, m_w_up, m_w_down, m_final_g, v_norm1_g, v_w_in, v_pool_w, v_pool_scale, v_pool_out_g, v_attn_out_g, v_w_out, v_norm2_g, v_w_up, v_w_down, v_final_g):
    w = dict(norm1_g=norm1_g, w_in=w_in, pool_w=pool_w, pool_scale=pool_scale, pool_out_g=pool_out_g,
             attn_out_g=attn_out_g, w_out=w_out, norm2_g=norm2_g, w_up=w_up, w_down=w_down, final_g=final_g)
    m = dict(norm1_g=m_norm1_g, w_in=m_w_in, pool_w=m_pool_w, pool_scale=m_pool_scale, pool_out_g=m_pool_out_g,
             attn_out_g=m_attn_out_g, w_out=m_w_out, norm2_g=m_norm2_g, w_up=m_w_up, w_down=m_w_down,
             final_g=m_final_g)
    v = dict(norm1_g=v_norm1_g, w_in=v_w_in, pool_w=v_pool_w, pool_scale=v_pool_scale, pool_out_g=v_pool_out_g,
             attn_out_g=v_attn_out_g, w_out=v_w_out, norm2_g=v_norm2_g, w_up=v_w_up, w_down=v_w_down,
             final_g=v_final_g)

    shards = {n: _halves(w[n].astype(BF16)) for n in BIG_ORDER}
    (w_in_g,) = _gather_weights([shards["w_in"]])
    small = {n: w[n].reshape(1, -1) for n in ("norm1_g", "pool_scale", "pool_out_g", "attn_out_g", "norm2_g", "final_g")}
    dx, big, little = _local_step(
        x[0], loss_target[0], w_in_g.reshape(N_CHIPS, D_MODEL, 512), pool_w.astype(BF16), small,
        shards=[shards["w_out"], shards["w_up"], shards["w_down"]])

    grads, deltas, new_m, new_v = {}, {}, {}, {}
    loss_rows = D_MODEL // LANES
    slab_g = _pack([little[n] for n in SMALL_ORDER] + [little["loss_sq"]])
    zeros = jnp.zeros((loss_rows, LANES), F32)
    slab_w = _pack([w[n] for n in SMALL_ORDER] + [zeros])
    slab_m = _pack([m[n] for n in SMALL_ORDER] + [zeros])
    slab_v = _pack([v[n] for n in SMALL_ORDER] + [zeros])
    g_s, d_s, m_s, v_s, loss, big["w_in"] = _small_step(slab_g, slab_w, slab_m, slab_v, loss_rows, [_pieces(big["w_in"])])
    shapes = [w[n].shape for n in SMALL_ORDER]
    for slab, dst in ((g_s, grads), (d_s, deltas), (m_s, new_m), (v_s, new_v)):
        for n, val in zip(SMALL_ORDER, _unpack(slab, shapes)):
            dst[n] = val

    full = _join_halves([_sum_pieces(big[n]) for n in BIG_ORDER])
    for n, g in zip(BIG_ORDER, full):
        grads[n] = g.reshape(w[n].shape)
        deltas[n], new_m[n], new_v[n] = _adamw_big(w[n], grads[n], m[n], v[n])

    return (loss[0, 0], dx[None], *[grads[n] for n in WEIGHT_ORDER], *[deltas[n] for n in WEIGHT_ORDER],
            *[new_m[n] for n in WEIGHT_ORDER], *[new_v[n] for n in WEIGHT_ORDER])
```

```python
import functools

import jax
import jax.numpy as jnp
from jax import lax
from jax.experimental import pallas as pl
from jax.experimental.pallas import tpu as pltpu

F32 = jnp.float32
BF16 = jnp.bfloat16

D_MODEL = 1024
D_POOL = 512
D_ATTN = 512
POOL_WINDOWS = (2, 4, 8, 16)
POOL_GROUP_DIM = 128
POOL_HALO = 16
HEAD_DIM = 64
HEADS_PER_BLOCK = 2
D_FF = 4096
N_CHIPS = 4
N_DEV = 8
EPS = 1e-6
ATTN_SCALE = 0.125
ATTN_TILE = 256
ATTN_ROW_CHUNKS = 2
DEAD_LOG = -105.0
ROW_TILE = 512
LANES = 128

ADAM_LR = 0.001
ADAM_B1 = 0.9
ADAM_B2 = 0.999
ADAM_EPS = 1e-08
ADAM_WD = 0.01
ADAM_STEP = 10

MESH = pl.DeviceIdType.MESH
ANY = pl.BlockSpec(memory_space=pl.ANY)
VMEM_LIMIT = 56 * 1024 * 1024


def _nn(a, b):
    return jnp.dot(a, b, preferred_element_type=F32)


def _nt(a, b):
    return lax.dot_general(a, b, (((1,), (1,)), ((), ())), preferred_element_type=F32)


def _tn(a, b):
    return lax.dot_general(a, b, (((0,), (0,)), ((), ())), preferred_element_type=F32)


def _rms(x):
    return lax.rsqrt(jnp.mean(x * x, axis=-1, keepdims=True) + EPS)


def _rms_bwd(dy, n, r, g):
    dn = dy * g
    return r * (dn - n * jnp.mean(dn * n, axis=-1, keepdims=True))


def _params(**kw):
    return pltpu.CompilerParams(vmem_limit_bytes=VMEM_LIMIT, **kw)


def _run_at(first, middle, last, phases):
    pl.when(first)(phases[0])
    if len(phases) == 3:
        pl.when(middle)(phases[1])
    return lambda: pl.when(last)(phases[-1])


def _in_proj(x, g1, w_in, shards=()):
    S = x.shape[0]
    ts = ROW_TILE
    n = len(shards)

    def body(*refs):
        x_ref, g_ref, w_ref = refs[:3]
        hn_ref, u_ref, qkv_ref = refs[3 + n:6 + n]
        i = pl.program_id(0)
        j = pl.program_id(1)
        finish = lambda: None
        if n:
            steps = pl.num_programs(0)
            finish = _run_at(jnp.logical_and(i == 0, j == 0), jnp.logical_and(i == steps // 2, j == 0),
                             jnp.logical_and(i == steps - 1, j == 3),
                             _gather_phases(refs[3:3 + n], refs[6 + n:6 + 2 * n], *refs[6 + 2 * n:]))

        @pl.when(j == 0)
        def _():
            xf = x_ref[...]
            hn_ref[...] = (xf * _rms(xf) * g_ref[...]).astype(BF16)

        p = _nn(hn_ref[...], w_ref[0])

        @pl.when(j == 0)
        def _():
            u_ref[...] = p

        @pl.when(j > 0)
        def _():
            qkv_ref[0] = p.astype(BF16)

        finish()

    return pl.pallas_call(
        body,
        name="in_proj",
        grid=(S // ts, 4),
        in_specs=[
            pl.BlockSpec((ts, D_MODEL), lambda i, j: (i, 0)),
            pl.BlockSpec((1, D_MODEL), lambda i, j: (0, 0)),
            pl.BlockSpec((1, D_MODEL, 512), lambda i, j: (j, 0, 0)),
        ] + [ANY] * n,
        out_specs=[
            pl.BlockSpec((ts, D_MODEL), lambda i, j: (i, 0)),
            pl.BlockSpec((ts, D_POOL), lambda i, j: (i, 0)),
            pl.BlockSpec((1, ts, 512), lambda i, j: (jnp.maximum(j - 1, 0), i, 0)),
        ] + [ANY] * n,
        out_shape=[
            jax.ShapeDtypeStruct((S, D_MODEL), BF16),
            jax.ShapeDtypeStruct((S, D_POOL), F32),
            jax.ShapeDtypeStruct((3, S, 512), BF16),
        ] + _gather_shapes(shards),
        scratch_shapes=_gather_sems(n) if n else [],
        compiler_params=_params(),
    )(x, g1, w_in, *shards)


def _pool_counts(first_row, rows):
    t = first_row + lax.broadcasted_iota(jnp.int32, (rows, 1), 0)
    return [1.0 / jnp.minimum(t + 1, w).astype(F32) for w in POOL_WINDOWS]


def _pooled(u_tile, halo, first_row):
    ts = u_tile.shape[0]
    inv = _pool_counts(first_row, ts)
    outs = []
    for g, w in enumerate(POOL_WINDOWS):
        lanes = slice(g * POOL_GROUP_DIM, (g + 1) * POOL_GROUP_DIM)
        xg = u_tile[:, lanes]
        acc = jnp.concatenate([halo[:, lanes], xg], axis=0)
        shift = 1
        while shift < w:
            acc = acc + pltpu.roll(acc, shift, axis=0)
            shift *= 2
        outs.append(acc[POOL_HALO:, :] * inv[g] - xg)
    return outs


def _pool_fwd(u, pool_w, pool_scale):
    S = u.shape[0]
    ts = ROW_TILE
    hb = ts // POOL_HALO

    def body(u_ref, halo_ref, w_ref, s_ref, y_ref):
        i = pl.program_id(0)
        halo = jnp.where(i == 0, 0.0, halo_ref[...])
        pooled = _pooled(u_ref[...], halo, i * ts)
        for g in range(len(POOL_WINDOWS)):
            lanes = slice(g * POOL_GROUP_DIM, (g + 1) * POOL_GROUP_DIM)
            y_ref[:, lanes] = _nn(pooled[g].astype(BF16), w_ref[g]) * s_ref[:, lanes]

    return pl.pallas_call(
        body,
        name="pool_fwd",
        grid=(S // ts,),
        in_specs=[
            pl.BlockSpec((ts, D_POOL), lambda i: (i, 0)),
            pl.BlockSpec((POOL_HALO, D_POOL), lambda i: (jnp.maximum(i * hb - 1, 0), 0)),
            pl.BlockSpec((4, POOL_GROUP_DIM, POOL_GROUP_DIM), lambda i: (0, 0, 0)),
            pl.BlockSpec((1, D_POOL), lambda i: (0, 0)),
        ],
        out_specs=pl.BlockSpec((ts, D_POOL), lambda i: (i, 0)),
        out_shape=jax.ShapeDtypeStruct((S, D_POOL), F32),
        compiler_params=_params(),
    )(u, u, pool_w, pool_scale)


def _head_masks():
    lane = lax.broadcasted_iota(jnp.int32, (1, LANES), 1)
    first = lane < HEAD_DIM
    return [first, jnp.logical_not(first)]


def _tri_masks(t):
    row = lax.broadcasted_iota(jnp.int32, (t, t), 0)
    col = lax.broadcasted_iota(jnp.int32, (t, t), 1)
    return row, col


def _split_bf16(x):
    hi = x.astype(BF16)
    lo = (x - hi.astype(F32)).astype(BF16)
    return hi, lo


def _log_sigmoids(z):
    sp = jnp.log(1.0 + jnp.exp(-jnp.abs(z)))
    ls = jnp.minimum(z, 0.0) - sp
    return ls, ls - z


def _attn_fwd(qkv, shards=()):
    S = qkv.shape[1]
    t = ATTN_TILE
    n = len(shards)
    nblk = D_ATTN // LANES

    def body(*refs):
        q_ref, k_ref, v_ref = refs[:3]
        o_ref = refs[3 + n]
        vh_ref, acc_ref, z_ref = refs[4 + 2 * n:7 + 2 * n]
        hp = pl.program_id(0)
        i = pl.program_id(1)
        finish = lambda: None
        if n:
            finish = _run_at(jnp.logical_and(hp == 0, i == 0), jnp.logical_and(hp == nblk // 2, i == 0),
                             jnp.logical_and(hp == nblk - 1, i == pl.num_programs(1) - 1),
                             _gather_phases(refs[3:3 + n], refs[4 + n:4 + 2 * n], *refs[7 + 2 * n:]))
        masks = _head_masks()

        @pl.when(i == 0)
        def _():
            vv = v_ref[0]
            for h in range(HEADS_PER_BLOCK):
                vh_ref[h] = jnp.where(masks[h], vv, jnp.zeros_like(vv))

        row, col = _tri_masks(t)
        later = (row > col).astype(BF16)
        causal = col < row
        qs = q_ref[0] * ATTN_SCALE
        units = [(h, r) for h in range(HEADS_PER_BLOCK) for r in range(ATTN_ROW_CHUNKS)]
        rc = t // ATTN_ROW_CHUNKS
        qu = [jnp.where(masks[h], qs, jnp.zeros_like(qs))[r * rc:(r + 1) * rc] for h, r in units]
        causal_u = [causal[r * rc:(r + 1) * rc] for _, r in units]

        def scores(j, slot):
            kj = k_ref[0, pl.ds(pl.multiple_of(j * t, t), t), :]
            for u in range(len(units)):
                z_ref[slot, u] = _nt(qu[u], kj)

        def tile(j, left, slot, carry, diag):
            cs = list(carry)
            keys = pl.ds(pl.multiple_of(j * t, t), t)
            ls, tail = [None] * len(units), [None] * len(units)
            for u in range(len(units)):
                ls[u], l1m = _log_sigmoids(z_ref[slot, u])
                if diag:
                    l1m = jnp.where(causal_u[u], l1m, 0.0)
                hi, lo = _split_bf16(l1m)
                tail[u] = _nn(hi, later) + _nn(lo, later) + cs[u]
                cs[u] = cs[u] + jnp.sum(l1m, axis=1, keepdims=True)
            top = cs[0]
            for u in range(1, len(units)):
                top = jnp.maximum(top, cs[u])
            go = jnp.logical_and(left > 0, jnp.max(top) > DEAD_LOG)
            scores(jnp.maximum(j - 1, 0), 1 - slot)
            for u, (h, _) in enumerate(units):
                a = jnp.exp(ls[u] + tail[u])
                if diag:
                    a = jnp.where(causal_u[u], a, 0.0)
                pv = _nn(a.astype(BF16), vh_ref[h, keys, :])
                if diag:
                    acc_ref[u] = pv
                else:
                    acc_ref[u] += pv
            return (go, *cs)

        scores(i, 0)
        state = (jnp.int32(0), *tile(i, i, 0, [jnp.zeros((rc, 1), F32)] * len(units), True))

        def step(state):
            jj = state[0]
            return (jj + 1, *tile(i - 1 - jj, i - 1 - jj, (jj + 1) % 2, state[2:], False))

        lax.while_loop(lambda s: s[1], step, state)
        out = None
        for h in range(HEADS_PER_BLOCK):
            rows = jnp.concatenate([acc_ref[u] for u, (hh, _) in enumerate(units) if hh == h], axis=0)
            out = rows if out is None else out + rows
        o_ref[...] = out
        finish()

    return pl.pallas_call(
        body,
        name="attn_fwd",
        grid=(nblk, S // t),
        in_specs=[
            pl.BlockSpec((1, t, LANES), lambda hp, i: (0, i, hp)),
            pl.BlockSpec((1, S, LANES), lambda hp, i: (1, 0, hp)),
            pl.BlockSpec((1, S, LANES), lambda hp, i: (2, 0, hp)),
        ] + [ANY] * n,
        out_specs=[pl.BlockSpec((t, LANES), lambda hp, i: (i, hp))] + [ANY] * n,
        out_shape=[jax.ShapeDtypeStruct((S, D_ATTN), F32)] + _gather_shapes(shards),
        scratch_shapes=[pltpu.VMEM((HEADS_PER_BLOCK, S, LANES), BF16),
                        pltpu.VMEM((HEADS_PER_BLOCK * ATTN_ROW_CHUNKS, t // ATTN_ROW_CHUNKS, LANES), F32),
                        pltpu.VMEM((2, HEADS_PER_BLOCK * ATTN_ROW_CHUNKS, t // ATTN_ROW_CHUNKS, t), F32)]
        + (_gather_sems(n) if n else []),
        compiler_params=_params(),
    )(qkv, qkv, qkv, *shards)


def _out_proj(y_pool, y_attn, x, g_pool, g_attn, w_out, g2):
    S = x.shape[0]
    ts = ROW_TILE

    def body(yp_ref, ya_ref, x_ref, gp_ref, ga_ref, w_ref, g2_ref, mixed_ref, h1_ref, hn2_ref):
        yp = yp_ref[...]
        ya = ya_ref[...]
        mixed = jnp.concatenate([yp * _rms(yp) * gp_ref[...], ya * _rms(ya) * ga_ref[...]], axis=-1).astype(BF16)
        mixed_ref[...] = mixed
        h1 = x_ref[...] + _nn(mixed, w_ref[...])
        h1_ref[...] = h1
        hn2_ref[...] = (h1 * _rms(h1) * g2_ref[...]).astype(BF16)

    row = lambda w: pl.BlockSpec((ts, w), lambda i: (i, 0))
    vec = lambda w: pl.BlockSpec((1, w), lambda i: (0, 0))
    return pl.pallas_call(
        body,
        name="out_proj",
        grid=(S // ts,),
        in_specs=[row(D_POOL), row(D_ATTN), row(D_MODEL), vec(D_POOL), vec(D_ATTN),
                  pl.BlockSpec((D_MODEL, D_MODEL), lambda i: (0, 0)), vec(D_MODEL)],
        out_specs=[row(D_MODEL), row(D_MODEL), row(D_MODEL)],
        out_shape=[
            jax.ShapeDtypeStruct((S, D_MODEL), BF16),
            jax.ShapeDtypeStruct((S, D_MODEL), F32),
            jax.ShapeDtypeStruct((S, D_MODEL), BF16),
        ],
        compiler_params=_params(),
    )(y_pool, y_attn, x, g_pool, g_attn, w_out, g2)


def _mlp_fwd(hn2, h1, w_up, w_down, g_final, target):
    S = hn2.shape[0]
    ts = ROW_TILE
    nf = D_FF // 1024

    def body(hn2_ref, h1_ref, wu_ref, wd_ref, gf_ref, tg_ref, r_ref, dh2_ref, dh2b_ref, lsq_ref, dgf_ref, acc_ref):
        i = pl.program_id(0)
        c = pl.program_id(1)
        r = jnp.maximum(_nn(hn2_ref[...], wu_ref[0]), 0.0)
        r_ref[...] = r.astype(BF16)
        contrib = _nn((r * r).astype(BF16), wd_ref[0])

        @pl.when(c == 0)
        def _():
            acc_ref[...] = contrib

        @pl.when(c > 0)
        def _():
            acc_ref[...] += contrib

        @pl.when(jnp.logical_and(i == 0, c == 0))
        def _():
            lsq_ref[...] = jnp.zeros_like(lsq_ref)
            dgf_ref[...] = jnp.zeros_like(dgf_ref)

        @pl.when(c == nf - 1)
        def _():
            h2 = h1_ref[...] + acc_ref[...]
            rf = _rms(h2)
            n = h2 * rf
            gf = gf_ref[...]
            e = n * gf - tg_ref[...]
            lsq_ref[...] += jnp.sum(e * e, axis=0, keepdims=True)
            dy = e * (1.0 / D_MODEL)
            dgf_ref[...] += jnp.sum(dy * n, axis=0, keepdims=True)
            dh2 = _rms_bwd(dy, n, rf, gf)
            dh2_ref[...] = dh2
            dh2b_ref[...] = dh2.astype(BF16)

    row = lambda w: pl.BlockSpec((ts, w), lambda i, c: (i, 0))
    vec = lambda w: pl.BlockSpec((1, w), lambda i, c: (0, 0))
    return pl.pallas_call(
        body,
        name="mlp_fwd",
        grid=(S // ts, nf),
        in_specs=[row(D_MODEL), row(D_MODEL),
                  pl.BlockSpec((1, D_MODEL, 1024), lambda i, c: (c, 0, 0)),
                  pl.BlockSpec((1, 1024, D_MODEL), lambda i, c: (c, 0, 0)),
                  vec(D_MODEL), row(D_MODEL)],
        out_specs=[pl.BlockSpec((ts, 1024), lambda i, c: (i, c)), row(D_MODEL), row(D_MODEL),
                   vec(D_MODEL), vec(D_MODEL)],
        out_shape=[
            jax.ShapeDtypeStruct((S, D_FF), BF16),
            jax.ShapeDtypeStruct((S, D_MODEL), F32),
            jax.ShapeDtypeStruct((S, D_MODEL), BF16),
            jax.ShapeDtypeStruct((1, D_MODEL), F32),
            jax.ShapeDtypeStruct((1, D_MODEL), F32),
        ],
        scratch_shapes=[pltpu.VMEM((ts, D_MODEL), F32)],
        compiler_params=_params(),
    )(hn2, h1, w_up, w_down, g_final, target)


def _mlp_bwd_dx(dh2b, r_act, w_down, w_up, h1, dh2, g2):
    S = h1.shape[0]
    ts = ROW_TILE
    nf = D_FF // 1024

    def body(dh2b_ref, r_ref, wd_ref, wu_ref, h1_ref, dh2_ref, g2_ref, dup_ref, dh1_ref, dg2_ref, acc_ref):
        i = pl.program_id(0)
        c = pl.program_id(1)
        dup = (_nt(dh2b_ref[...], wd_ref[0]) * (2.0 * r_ref[...].astype(F32))).astype(BF16)
        dup_ref[...] = dup
        contrib = _nt(dup, wu_ref[0])

        @pl.when(c == 0)
        def _():
            acc_ref[...] = contrib

        @pl.when(c > 0)
        def _():
            acc_ref[...] += contrib

        @pl.when(jnp.logical_and(i == 0, c == 0))
        def _():
            dg2_ref[...] = jnp.zeros_like(dg2_ref)

        @pl.when(c == nf - 1)
        def _():
            dhn2 = acc_ref[...]
            h1v = h1_ref[...]
            r2 = _rms(h1v)
            n2 = h1v * r2
            dg2_ref[...] += jnp.sum(dhn2 * n2, axis=0, keepdims=True)
            dh1_ref[...] = dh2_ref[...] + _rms_bwd(dhn2, n2, r2, g2_ref[...])

    row = lambda w: pl.BlockSpec((ts, w), lambda i, c: (i, 0))
    vec = lambda w: pl.BlockSpec((1, w), lambda i, c: (0, 0))
    chunk = pl.BlockSpec((ts, 1024), lambda i, c: (i, c))
    return pl.pallas_call(
        body,
        name="mlp_bwd_dx",
        grid=(S // ts, nf),
        in_specs=[row(D_MODEL), chunk,
                  pl.BlockSpec((1, 1024, D_MODEL), lambda i, c: (c, 0, 0)),
                  pl.BlockSpec((1, D_MODEL, 1024), lambda i, c: (c, 0, 0)),
                  row(D_MODEL), row(D_MODEL), vec(D_MODEL)],
        out_specs=[chunk, row(D_MODEL), vec(D_MODEL)],
        out_shape=[
            jax.ShapeDtypeStruct((S, D_FF), BF16),
            jax.ShapeDtypeStruct((S, D_MODEL), F32),
            jax.ShapeDtypeStruct((1, D_MODEL), F32),
        ],
        scratch_shapes=[pltpu.VMEM((ts, D_MODEL), F32)],
        compiler_params=_params(),
    )(dh2b, r_act, w_down, w_up, h1, dh2, g2)


def _mlp_bwd_dw(r_act, dh2b, hn2, dup):
    S = hn2.shape[0]
    ts = ROW_TILE
    nf = D_FF // 1024

    def body(r_ref, dh2b_ref, hn2_ref, dup_ref, dwd_ref, dwu_ref, dwd_acc, dwu_acc):
        tt = pl.program_id(1)
        r = r_ref[...].astype(F32)
        dwd = _tn((r * r).astype(BF16), dh2b_ref[...])
        dwu = _tn(hn2_ref[...], dup_ref[...])

        @pl.when(tt == 0)
        def _():
            dwd_acc[...] = dwd
            dwu_acc[...] = dwu

        @pl.when(tt > 0)
        def _():
            dwd_acc[...] += dwd
            dwu_acc[...] += dwu

        @pl.when(tt == pl.num_programs(1) - 1)
        def _():
            dwd_ref[0] = dwd_acc[...].astype(BF16)
            dwu_ref[0] = dwu_acc[...].astype(BF16)

    row = pl.BlockSpec((ts, D_MODEL), lambda c, tt: (tt, 0))
    chunk = pl.BlockSpec((ts, 1024), lambda c, tt: (tt, c))
    return pl.pallas_call(
        body,
        name="mlp_bwd_dw",
        grid=(nf, S // ts),
        in_specs=[chunk, row, row, chunk],
        out_specs=[pl.BlockSpec((1, 1024, D_MODEL), lambda c, tt: (c, 0, 0)),
                   pl.BlockSpec((1, D_MODEL, 1024), lambda c, tt: (c, 0, 0))],
        out_shape=[
            jax.ShapeDtypeStruct((nf, 1024, D_MODEL), BF16),
            jax.ShapeDtypeStruct((nf, D_MODEL, 1024), BF16),
        ],
        scratch_shapes=[pltpu.VMEM((1024, D_MODEL), F32), pltpu.VMEM((D_MODEL, 1024), F32)],
        compiler_params=_params(),
    )(r_act, dh2b, hn2, dup)


def _out_bwd(dh1, w_out, mixed, y_pool, y_attn, g_pool, g_attn):
    S = dh1.shape[0]
    ts = ROW_TILE

    def body(dh1_ref, w_ref, mixed_ref, yp_ref, ya_ref, gp_ref, ga_ref,
             dyp_ref, dya_ref, dw_ref, dgp_ref, dga_ref, dw_acc):
        i = pl.program_id(0)
        dh1b = dh1_ref[...].astype(BF16)
        dmixed = _nt(dh1b, w_ref[...])
        dw = _tn(mixed_ref[...], dh1b)

        @pl.when(i == 0)
        def _():
            dw_acc[...] = dw
            dgp_ref[...] = jnp.zeros_like(dgp_ref)
            dga_ref[...] = jnp.zeros_like(dga_ref)

        @pl.when(i > 0)
        def _():
            dw_acc[...] += dw

        @pl.when(i == pl.num_programs(0) - 1)
        def _():
            dw_ref[...] = dw_acc[...].astype(BF16)

        for y_ref, g_ref, dy_ref, dg_ref, lanes in (
                (yp_ref, gp_ref, dyp_ref, dgp_ref, slice(0, D_POOL)),
                (ya_ref, ga_ref, dya_ref, dga_ref, slice(D_POOL, D_MODEL))):
            y = y_ref[...]
            r = _rms(y)
            n = y * r
            dm = dmixed[:, lanes]
            dg_ref[...] += jnp.sum(dm * n, axis=0, keepdims=True)
            dy_ref[...] = _rms_bwd(dm, n, r, g_ref[...])

    row = lambda w: pl.BlockSpec((ts, w), lambda i: (i, 0))
    vec = lambda w: pl.BlockSpec((1, w), lambda i: (0, 0))
    full = pl.BlockSpec((D_MODEL, D_MODEL), lambda i: (0, 0))
    return pl.pallas_call(
        body,
        name="out_bwd",
        grid=(S // ts,),
        in_specs=[row(D_MODEL), full, row(D_MODEL), row(D_POOL), row(D_ATTN), vec(D_POOL), vec(D_ATTN)],
        out_specs=[row(D_POOL), row(D_ATTN), full, vec(D_POOL), vec(D_ATTN)],
        out_shape=[
            jax.ShapeDtypeStruct((S, D_POOL), F32),
            jax.ShapeDtypeStruct((S, D_ATTN), F32),
            jax.ShapeDtypeStruct((D_MODEL, D_MODEL), BF16),
            jax.ShapeDtypeStruct((1, D_POOL), F32),
            jax.ShapeDtypeStruct((1, D_ATTN), F32),
        ],
        scratch_shapes=[pltpu.VMEM((D_MODEL, D_MODEL), F32)],
        compiler_params=_params(),
    )(dh1, w_out, mixed, y_pool, y_attn, g_pool, g_attn)


def _attn_bwd(qkv, o, do, partials=()):
    S = qkv.shape[1]
    t = ATTN_TILE
    n = len(partials)
    nblk = D_ATTN // LANES

    def body(*refs):
        q_ref, k_ref, v_ref, o_ref, do_ref = refs[:5]
        dq_ref, dk_ref, dv_ref = refs[5 + n:8 + n]
        kh_ref, dk_acc, dv_acc, dq_acc, z_ref, da_ref = refs[8 + 2 * n:14 + 2 * n]
        hp = pl.program_id(0)
        i = pl.program_id(1)
        finish = lambda: None
        if n:
            finish = _run_at(jnp.logical_and(hp == 0, i == 0), None,
                             jnp.logical_and(hp == nblk - 1, i == pl.num_programs(1) - 1),
                             _reduce_phases(refs[5:5 + n], refs[8 + n:8 + 2 * n], *refs[14 + 2 * n:]))
        masks = _head_masks()

        @pl.when(i == 0)
        def _():
            kk = k_ref[0]
            for h in range(HEADS_PER_BLOCK):
                kh_ref[h] = jnp.where(masks[h], kk, jnp.zeros_like(kk))
            dk_acc[...] = jnp.zeros_like(dk_acc)
            dv_acc[...] = jnp.zeros_like(dv_acc)

        row, col = _tri_masks(t)
        later = (row > col).astype(BF16)
        from_s = (row >= col).astype(BF16)
        causal = col < row
        qs = q_ref[0] * ATTN_SCALE
        dob = do_ref[...].astype(BF16)
        d_all = dob.astype(F32) * o_ref[...]
        qh = [jnp.where(masks[h], qs, jnp.zeros_like(qs)) for h in range(HEADS_PER_BLOCK)]
        doh = [jnp.where(masks[h], dob, jnp.zeros_like(dob)) for h in range(HEADS_PER_BLOCK)]
        d_row = [jnp.sum(jnp.where(masks[h], d_all, 0.0), axis=1, keepdims=True) for h in range(HEADS_PER_BLOCK)]

        heads = range(HEADS_PER_BLOCK)

        def scores(j, slot):
            keys = pl.ds(pl.multiple_of(j * t, t), t)
            kj = k_ref[0, keys, :]
            vj = v_ref[0, keys, :]
            for h in heads:
                z_ref[slot, h] = _nt(qh[h], kj)
                da_ref[slot, h] = _nt(doh[h], vj)

        def tile(j, left, slot, carry, diag):
            c_l, c_g = list(carry[0:2]), list(carry[2:4])
            keys = pl.ds(pl.multiple_of(j * t, t), t)
            ls, tail = [None, None], [None, None]
            for h in heads:
                ls[h], l1m = _log_sigmoids(z_ref[slot, h])
                if diag:
                    l1m = jnp.where(causal, l1m, 0.0)
                hi, lo = _split_bf16(l1m)
                tail[h] = _nn(hi, later) + _nn(lo, later) + c_l[h]
                c_l[h] = c_l[h] + jnp.sum(l1m, axis=1, keepdims=True)
            go = jnp.logical_and(left > 0, jnp.max(jnp.maximum(c_l[0], c_l[1])) > DEAD_LOG)
            g, before = [None, None], [None, None]
            dv = None
            for h in heads:
                a = jnp.exp(ls[h] + tail[h])
                if diag:
                    a = jnp.where(causal, a, 0.0)
                ab = a.astype(BF16)
                g[h] = ab.astype(F32) * da_ref[slot, h]
                ghi, glo = _split_bf16(g[h])
                before[h] = d_row[h] - (_nn(ghi, from_s) + _nn(glo, from_s) + c_g[h])
                c_g[h] = c_g[h] + jnp.sum(g[h], axis=1, keepdims=True)
                part = _tn(ab, doh[h])
                dv = part if dv is None else dv + part
            dv_acc[keys, :] += dv
            scores(jnp.maximum(j - 1, 0), 1 - slot)
            dk = None
            for h in heads:
                beta = jnp.exp(ls[h])
                dz = g[h] * (1.0 - beta) - before[h] * beta
                if diag:
                    dz = jnp.where(causal, dz, 0.0)
                dzb = dz.astype(BF16)
                dqh = _nn(dzb, kh_ref[h, keys, :])
                if diag:
                    dq_acc[h] = dqh
                else:
                    dq_acc[h] += dqh
                part = _tn(dzb, qh[h])
                dk = part if dk is None else dk + part
            dk_acc[keys, :] += dk
            return (go, *c_l, *c_g)

        scores(i, 0)
        state = (jnp.int32(0), *tile(i, i, 0, [jnp.zeros((t, 1), F32)] * 4, True))

        def step(state):
            jj = state[0]
            return (jj + 1, *tile(i - 1 - jj, i - 1 - jj, (jj + 1) % 2, state[2:], False))

        lax.while_loop(lambda s: s[1], step, state)
        dq_ref[...] = ((dq_acc[0] + dq_acc[1]) * ATTN_SCALE).astype(BF16)

        @pl.when(i == pl.num_programs(1) - 1)
        def _():
            dk_ref[...] = dk_acc[...].astype(BF16)
            dv_ref[...] = dv_acc[...].astype(BF16)

        finish()

    qtile = pl.BlockSpec((t, LANES), lambda hp, i: (i, hp))
    whole = pl.BlockSpec((S, LANES), lambda hp, i: (0, hp))
    return pl.pallas_call(
        body,
        name="attn_bwd",
        grid=(nblk, S // t),
        in_specs=[
            pl.BlockSpec((1, t, LANES), lambda hp, i: (0, i, hp)),
            pl.BlockSpec((1, S, LANES), lambda hp, i: (1, 0, hp)),
            pl.BlockSpec((1, S, LANES), lambda hp, i: (2, 0, hp)),
            qtile, qtile,
        ] + [ANY] * n,
        out_specs=[qtile, whole, whole] + [ANY] * n,
        out_shape=[jax.ShapeDtypeStruct((S, D_ATTN), BF16)] * 3 + _reduce_shapes(partials),
        scratch_shapes=[
            pltpu.VMEM((HEADS_PER_BLOCK, S, LANES), BF16),
            pltpu.VMEM((S, LANES), F32),
            pltpu.VMEM((S, LANES), F32),
            pltpu.VMEM((HEADS_PER_BLOCK, t, LANES), F32),
            pltpu.VMEM((2, HEADS_PER_BLOCK, t, t), F32),
            pltpu.VMEM((2, HEADS_PER_BLOCK, t, t), F32),
        ] + (_reduce_sems(n) if n else []),
        compiler_params=_params(),
    )(qkv, qkv, qkv, o, do, *partials)


def _pool_bwd_w(u, dyp, pool_w, pool_scale):
    S = u.shape[0]
    ts = ROW_TILE
    hb = ts // POOL_HALO

    def body(u_ref, halo_ref, dy_ref, w_ref, s_ref, dp_ref, dw_ref, ds_ref):
        i = pl.program_id(0)
        halo = jnp.where(i == 0, 0.0, halo_ref[...])
        pooled = _pooled(u_ref[...], halo, i * ts)

        @pl.when(i == 0)
        def _():
            dw_ref[...] = jnp.zeros_like(dw_ref)
            ds_ref[...] = jnp.zeros_like(ds_ref)

        for g in range(len(POOL_WINDOWS)):
            lanes = slice(g * POOL_GROUP_DIM, (g + 1) * POOL_GROUP_DIM)
            pg = pooled[g].astype(BF16)
            dy = dy_ref[:, lanes]
            ds_ref[:, lanes] += jnp.sum(dy * _nn(pg, w_ref[g]), axis=0, keepdims=True)
            dmapped = (dy * s_ref[:, lanes]).astype(BF16)
            dp_ref[:, lanes] = _nt(dmapped, w_ref[g])
            dw_ref[g] += _tn(pg, dmapped)

    row = pl.BlockSpec((ts, D_POOL), lambda i: (i, 0))
    vec = pl.BlockSpec((1, D_POOL), lambda i: (0, 0))
    wspec = pl.BlockSpec((4, POOL_GROUP_DIM, POOL_GROUP_DIM), lambda i: (0, 0, 0))
    return pl.pallas_call(
        body,
        name="pool_bwd_w",
        grid=(S // ts,),
        in_specs=[row, pl.BlockSpec((POOL_HALO, D_POOL), lambda i: (jnp.maximum(i * hb - 1, 0), 0)),
                  row, wspec, vec],
        out_specs=[row, wspec, vec],
        out_shape=[
            jax.ShapeDtypeStruct((S, D_POOL), F32),
            jax.ShapeDtypeStruct((4, POOL_GROUP_DIM, POOL_GROUP_DIM), F32),
            jax.ShapeDtypeStruct((1, D_POOL), F32),
        ],
        compiler_params=_params(),
    )(u, u, dyp, pool_w, pool_scale)


def _pool_bwd_u(dpooled):
    S = dpooled.shape[0]
    ts = ROW_TILE
    hb = ts // POOL_HALO
    last = S // ts - 1

    def body(dp_ref, halo_ref, du_ref):
        i = pl.program_id(0)
        dp = dp_ref[...]
        halo = jnp.where(i == last, 0.0, halo_ref[...])
        inv = _pool_counts(i * ts, ts)
        n = ts + POOL_HALO
        for g, w in enumerate(POOL_WINDOWS):
            lanes = slice(g * POOL_GROUP_DIM, (g + 1) * POOL_GROUP_DIM)
            dg = dp[:, lanes]
            acc = jnp.concatenate([dg * inv[g], halo[:, lanes] * (1.0 / w)], axis=0)
            shift = 1
            while shift < w:
                acc = acc + pltpu.roll(acc, n - shift, axis=0)
                shift *= 2
            du_ref[:, lanes] = (acc[:ts, :] - dg).astype(BF16)

    return pl.pallas_call(
        body,
        name="pool_bwd_u",
        grid=(S // ts,),
        in_specs=[pl.BlockSpec((ts, D_POOL), lambda i: (i, 0)),
                  pl.BlockSpec((POOL_HALO, D_POOL), lambda i: (jnp.minimum((i + 1) * hb, (last + 1) * hb - 1), 0))],
        out_specs=pl.BlockSpec((ts, D_POOL), lambda i: (i, 0)),
        out_shape=jax.ShapeDtypeStruct((S, D_POOL), BF16),
        compiler_params=_params(),
    )(dpooled, dpooled)


def _in_bwd(du, dq, dk, dv, hn, w_in, x, dh1, g1):
    S = x.shape[0]
    ts = ROW_TILE // 2

    def body(du_ref, dq_ref, dk_ref, dv_ref, hn_ref, w_ref, x_ref, dh1_ref, g_ref, dx_ref, dw_ref, dg_ref, dw_acc):
        i = pl.program_id(0)

        @pl.when(i == 0)
        def _():
            dw_acc[...] = jnp.zeros_like(dw_acc)
            dg_ref[...] = jnp.zeros_like(dg_ref)

        hn = hn_ref[...]
        dhn = jnp.zeros((ts, D_MODEL), F32)
        for j, dp_ref in enumerate((du_ref, dq_ref, dk_ref, dv_ref)):
            dp = dp_ref[...]
            dhn = dhn + _nt(dp, w_ref[j])
            dw_acc[j] += _tn(hn, dp)
        xv = x_ref[...]
        r1 = _rms(xv)
        n1 = xv * r1
        dg_ref[...] += jnp.sum(dhn * n1, axis=0, keepdims=True)
        dx_ref[...] = dh1_ref[...] + _rms_bwd(dhn, n1, r1, g_ref[...])

        @pl.when(i == pl.num_programs(0) - 1)
        def _():
            dw_ref[...] = dw_acc[...].astype(BF16)

    row = lambda w: pl.BlockSpec((ts, w), lambda i: (i, 0))
    vec = pl.BlockSpec((1, D_MODEL), lambda i: (0, 0))
    wspec = pl.BlockSpec((4, D_MODEL, 512), lambda i: (0, 0, 0))
    return pl.pallas_call(
        body,
        name="in_bwd",
        grid=(S // ts,),
        in_specs=[row(512), row(512), row(512), row(512), row(D_MODEL), wspec, row(D_MODEL), row(D_MODEL), vec],
        out_specs=[row(D_MODEL), wspec, vec],
        out_shape=[
            jax.ShapeDtypeStruct((S, D_MODEL), F32),
            jax.ShapeDtypeStruct((4, D_MODEL, 512), BF16),
            jax.ShapeDtypeStruct((1, D_MODEL), F32),
        ],
        scratch_shapes=[pltpu.VMEM((4, D_MODEL, 512), F32)],
        compiler_params=_params(),
    )(du, dq, dk, dv, hn, w_in, x, dh1, g1)


def _pieces(g):
    return g.reshape(N_CHIPS, 2, -1, g.shape[-1])


def _local_step(x, target, w_in, pool_w, small, full=None, shards=None):
    spread = shards is not None
    if spread:
        hn, u, qkv, w_out = _in_proj(x, small["norm1_g"], w_in, shards[:1])
        w_out = w_out.reshape(D_MODEL, D_MODEL)
    else:
        hn, u, qkv = _in_proj(x, small["norm1_g"], w_in)
        w_out, w_up, w_down = full
    y_pool = _pool_fwd(u, pool_w, small["pool_scale"])
    if spread:
        y_attn, w_up, w_down = _attn_fwd(qkv, shards[1:])
        w_up = w_up.reshape(N_CHIPS, D_MODEL, 1024)
        w_down = w_down.reshape(N_CHIPS, 1024, D_MODEL)
    else:
        (y_attn,) = _attn_fwd(qkv)
    mixed, h1, hn2 = _out_proj(y_pool, y_attn, x, small["pool_out_g"], small["attn_out_g"], w_out, small["norm2_g"])
    r_act, dh2, dh2b, lsq, dgf = _mlp_fwd(hn2, h1, w_up, w_down, small["final_g"], target)

    dup, dh1, dg2 = _mlp_bwd_dx(dh2b, r_act, w_down, w_up, h1, dh2, small["norm2_g"])
    dw_down, dw_up = _mlp_bwd_dw(r_act, dh2b, hn2, dup)
    dyp, dya, dw_out, dgp, dga = _out_bwd(dh1, w_out, mixed, y_pool, y_attn, small["pool_out_g"], small["attn_out_g"])
    if spread:
        dq, dk, dv, got_out, got_up, got_down = _attn_bwd(
            qkv, y_attn, dya, [_pieces(dw_out), _pieces(dw_up), _pieces(dw_down)])
    else:
        dq, dk, dv = _attn_bwd(qkv, y_attn, dya)
    dpooled, dpool_w, dpool_scale = _pool_bwd_w(u, dyp, pool_w, small["pool_scale"])
    du = _pool_bwd_u(dpooled)
    dx, dw_in, dg1 = _in_bwd(du, dq, dk, dv, hn, w_in, x, dh1, small["norm1_g"])

    big = {"w_in": dw_in, "w_out": dw_out, "w_up": dw_up, "w_down": dw_down}
    if spread:
        big["received"] = {"w_out": got_out, "w_up": got_up, "w_down": got_down}
    little = {"norm1_g": dg1, "pool_w": dpool_w, "pool_scale": dpool_scale, "pool_out_g": dgp,
              "attn_out_g": dga, "norm2_g": dg2, "final_g": dgf, "loss_sq": lsq}
    return dx, big, little


def _place():
    x, y, c = lax.axis_index("x"), lax.axis_index("y"), lax.axis_index("c")
    other_chips = [(1 - x, y), (x, 1 - y), (1 - x, 1 - y)]
    return x, y, c, other_chips


def _chip_index(chip):
    return 2 * chip[0] + chip[1]


def _gather_shapes(shards):
    return [jax.ShapeDtypeStruct((N_CHIPS,) + s.shape, s.dtype) for s in shards]


def _gather_sems(n):
    return [pltpu.SemaphoreType.DMA((n, 7)), pltpu.SemaphoreType.DMA((n, 7))]


def _gather_phases(ins, outs, send_sems, recv_sems):
    n = len(ins)
    x, y, c, chips = _place()
    me = _chip_index((x, y))
    sibling = (x, y, 1 - c)

    def copy(a, k, chip_idx, half, to, src=None):
        dst = outs[a].at[chip_idx, half]
        return pltpu.make_async_remote_copy(
            src_ref=dst if src is None else src, dst_ref=dst,
            send_sem=send_sems.at[a, k], recv_sem=recv_sems.at[a, k],
            device_id=to, device_id_type=MESH)

    def own_chip(a, to):
        return pltpu.make_async_remote_copy(
            src_ref=ins[a], dst_ref=outs[a].at[me],
            send_sem=send_sems.at[a, 0], recv_sem=recv_sems.at[a, 0],
            device_id=to, device_id_type=MESH)

    def first(a):
        return [own_chip(a, sibling)] + [
            copy(a, 1 + j, me, c, (*chip, c), src=ins[a].at[c]) for j, chip in enumerate(chips)]

    def passed(a, j):
        return copy(a, 4 + j, _chip_index(chips[j]), c, sibling)

    def start():
        for a in range(n):
            for cp in first(a):
                cp.start()

    def forward():
        for a in range(n):
            for j, chip in enumerate(chips):
                copy(a, 1 + j, _chip_index(chip), c, (x, y, c)).wait_recv()
                passed(a, j).start()

    def finish():
        for a in range(n):
            own_chip(a, (x, y, c)).wait_recv()
            for j, chip in enumerate(chips):
                copy(a, 4 + j, _chip_index(chip), 1 - c, (x, y, c)).wait_recv()
        for a in range(n):
            for cp in first(a):
                cp.wait_send()
            for j in range(len(chips)):
                passed(a, j).wait_send()

    return start, forward, finish


def _gather_weights(shards):
    n = len(shards)

    def body(*refs):
        start, forward, finish = _gather_phases(refs[:n], refs[n:2 * n], *refs[2 * n:])
        start()
        forward()
        finish()

    return pl.pallas_call(
        body,
        name="gather_weights",
        in_specs=[ANY] * n,
        out_specs=[ANY] * n,
        out_shape=_gather_shapes(shards),
        scratch_shapes=_gather_sems(n),
    )(*shards)


def _reduce_shapes(partials):
    return [jax.ShapeDtypeStruct((N_DEV,) + p.shape[2:], p.dtype) for p in partials]


def _reduce_sems(n):
    return [pltpu.SemaphoreType.DMA((n, N_DEV)), pltpu.SemaphoreType.DMA((n, N_DEV))]


def _reduce_phases(ins, outs, send_sems, recv_sems):
    n = len(ins)
    x, y, c, _ = _place()
    me = 4 * x + 2 * y + c

    def to_peer(a, k):
        return pltpu.make_async_remote_copy(
            src_ref=ins[a].at[k // 2, k % 2], dst_ref=outs[a].at[me],
            send_sem=send_sems.at[a, k], recv_sem=recv_sems.at[a, me],
            device_id=(k // 4, (k // 2) % 2, k % 2), device_id_type=MESH)

    def from_peer(a, k):
        return pltpu.make_async_remote_copy(
            src_ref=ins[a].at[k // 2, k % 2], dst_ref=outs[a].at[k],
            send_sem=send_sems.at[a, k], recv_sem=recv_sems.at[a, k],
            device_id=(x, y, c), device_id_type=MESH)

    def start():
        for a in range(n):
            for k in range(N_DEV):
                @pl.when(k != me)
                def _(a=a, k=k):
                    to_peer(a, k).start()

    def finish():
        for a in range(n):
            for k in range(N_DEV):
                @pl.when(k != me)
                def _(a=a, k=k):
                    from_peer(a, k).wait_recv()
                    to_peer(a, k).wait_send()

    return start, finish


def _sum_pieces(where, parts, own):
    _, h, cols = parts.shape
    hb = min(h, 256)

    def body(where_ref, *refs):
        me = where_ref[0]
        acc = None
        for k in range(N_DEV):
            piece = jnp.where(k == me, refs[N_DEV][0, 0], refs[k][0]).astype(F32)
            acc = piece if acc is None else acc + piece
        refs[N_DEV + 1][0] = acc

    def sent_by(k):
        return lambda r, w: (jnp.where(w[0] == k, (k + 1) % N_DEV, k), r, 0)

    return pl.pallas_call(
        body,
        name="sum_pieces",
        grid_spec=pltpu.PrefetchScalarGridSpec(
            num_scalar_prefetch=1,
            grid=(h // hb,),
            in_specs=[pl.BlockSpec((1, hb, cols), sent_by(k)) for k in range(N_DEV)]
            + [pl.BlockSpec((1, 1, hb, cols), lambda r, w: (w[1], w[2], r, 0))],
            out_specs=pl.BlockSpec((1, hb, cols), lambda r, w: (w[2], r, 0)),
        ),
        out_shape=jax.ShapeDtypeStruct((2, h, cols), F32),
    )(where, *([parts] * N_DEV), own)


def _join_halves(halves):
    n = len(halves)

    def body(*refs):
        outs = refs[n:2 * n]
        send_sems, recv_sems = refs[2 * n:]
        x, y, c, _ = _place()
        sends = [
            pltpu.make_async_remote_copy(
                src_ref=outs[a].at[c], dst_ref=outs[a].at[c],
                send_sem=send_sems.at[a], recv_sem=recv_sems.at[a],
                device_id=(x, y, 1 - c), device_id_type=MESH)
            for a in range(n)]
        for cp in sends:
            cp.start()
        for a in range(n):
            pltpu.make_async_remote_copy(
                src_ref=outs[a].at[c], dst_ref=outs[a].at[1 - c],
                send_sem=send_sems.at[a], recv_sem=recv_sems.at[a],
                device_id=(x, y, c), device_id_type=MESH).wait_recv()
        for cp in sends:
            cp.wait_send()

    return pl.pallas_call(
        body,
        name="join_halves",
        in_specs=[ANY] * n,
        out_specs=[ANY] * n,
        out_shape=[jax.ShapeDtypeStruct(s.shape, s.dtype) for s in halves],
        input_output_aliases={a: a for a in range(n)},
        scratch_shapes=[pltpu.SemaphoreType.DMA((n,)), pltpu.SemaphoreType.DMA((n,))],
    )(*halves)


def _adamw(w, g, m, v):
    m = ADAM_B1 * m + (1.0 - ADAM_B1) * g
    v = ADAM_B2 * v + (1.0 - ADAM_B2) * jnp.square(g)
    m_hat = m / (1.0 - ADAM_B1 ** ADAM_STEP)
    v_hat = v / (1.0 - ADAM_B2 ** ADAM_STEP)
    delta = -ADAM_LR * (m_hat / (jnp.sqrt(v_hat) + ADAM_EPS) + ADAM_WD * w)
    return delta, m, v


def _adamw_big(w, g, m, v):
    rows, cols = w.shape
    rb = min(rows, 256)

    def body(w_ref, g_ref, m_ref, v_ref, d_ref, mo_ref, vo_ref):
        d_ref[...], mo_ref[...], vo_ref[...] = _adamw(w_ref[...], g_ref[...], m_ref[...], v_ref[...])

    blk = pl.BlockSpec((rb, cols), lambda r: (r, 0))
    return pl.pallas_call(
        body,
        name="adamw_big",
        grid=(rows // rb,),
        in_specs=[blk] * 4,
        out_specs=[blk] * 3,
        out_shape=[jax.ShapeDtypeStruct(w.shape, F32)] * 3,
    )(w, g, m, v)


SMALL_ORDER = ("pool_w", "norm1_g", "pool_scale", "pool_out_g", "attn_out_g", "norm2_g", "final_g")
SUBLANES = 8


def _pack(parts):
    rows = []
    for p in parts:
        p = p.reshape(-1, LANES)
        pad = (-p.shape[0]) % SUBLANES
        if pad:
            p = jnp.pad(p, ((0, pad), (0, 0)))
        rows.append(p)
    return jnp.concatenate(rows, axis=0)


def _unpack(slab, shapes):
    out, r = [], 0
    for shp in shapes:
        size = 1
        for d in shp:
            size *= d
        nrow = size // LANES
        out.append(slab[r:r + nrow].reshape(shp))
        r += nrow + (-nrow) % SUBLANES
    return out


def _small_step(partials, w, m, v, loss_rows, big_partials):
    rows = partials.shape[0]
    nb = len(big_partials)

    def body(*refs):
        p_ref, w_ref, m_ref, v_ref = refs[:4]
        big_in = refs[4:4 + nb]
        g_ref, d_ref, mo_ref, vo_ref, loss_ref = refs[4 + nb:9 + nb]
        big_out = refs[9 + nb:9 + 2 * nb]
        buf, send_sems, recv_sems = refs[9 + 2 * nb:12 + 2 * nb]
        start_big, finish_big = _reduce_phases(big_in, big_out, *refs[12 + 2 * nb:])
        start_big()
        x, y, c, _ = _place()
        me = 4 * x + 2 * y + c
        for k in range(N_DEV):
            @pl.when(k != me)
            def _(k=k):
                pltpu.make_async_remote_copy(
                    src_ref=p_ref, dst_ref=buf.at[me],
                    send_sem=send_sems.at[k], recv_sem=recv_sems.at[me],
                    device_id=(k // 4, (k // 2) % 2, k % 2), device_id_type=MESH).start()
        buf[me] = p_ref[...]
        for k in range(N_DEV):
            @pl.when(k != me)
            def _(k=k):
                pltpu.make_async_remote_copy(
                    src_ref=p_ref, dst_ref=buf.at[k],
                    send_sem=send_sems.at[k], recv_sem=recv_sems.at[k],
                    device_id=(x, y, c), device_id_type=MESH).wait()
        g = buf[0]
        for k in range(1, N_DEV):
            g = g + buf[k]
        g_ref[...] = g
        d_ref[...], mo_ref[...], vo_ref[...] = _adamw(w_ref[...], g, m_ref[...], v_ref[...])
        loss = (0.5 / D_MODEL) * jnp.sum(g[rows - loss_rows:, :])
        loss_ref[...] = jnp.full(loss_ref.shape, loss, F32)
        finish_big()

    vm = pl.BlockSpec(memory_space=pltpu.VMEM)
    slab = jax.ShapeDtypeStruct((rows, LANES), F32)
    return pl.pallas_call(
        body,
        name="small_step",
        in_specs=[vm] * 4 + [ANY] * nb,
        out_specs=[vm] * 5 + [ANY] * nb,
        out_shape=[slab, slab, slab, slab, jax.ShapeDtypeStruct((SUBLANES, LANES), F32)] + _reduce_shapes(big_partials),
        scratch_shapes=[pltpu.VMEM((N_DEV, rows, LANES), F32),
                        pltpu.SemaphoreType.DMA((N_DEV,)), pltpu.SemaphoreType.DMA((N_DEV,))] + _reduce_sems(nb),
    )(partials, w, m, v, *big_partials)


BIG_ORDER = ("w_in", "w_out", "w_up", "w_down")
WEIGHT_ORDER = ("norm1_g", "w_in", "pool_w", "pool_scale", "pool_out_g", "attn_out_g", "w_out", "norm2_g",
                "w_up", "w_down", "final_g")


def _halves(a):
    return a.reshape(2, a.shape[0] // 2, a.shape[1])


def kernel(x, norm1_g, w_in, pool_w, pool_scale, pool_out_g, attn_out_g, w_out, norm2_g, w_up, w_down, final_g, loss_target, m_norm1_g, m_w_in, m_pool_w, m_pool_scale, m_pool_out_g, m_attn_out_g, m_w_out, m_norm2_g, m_w_up, m_w_down, m_final_g, v_norm1_g, v_w_in, v_pool_w, v_pool_scale, v_pool_out_g, v_attn_out_g, v_w_out, v_norm2_g, v_w_up, v_w_down, v_final_g):
    w = dict(norm1_g=norm1_g, w_in=w_in, pool_w=pool_w, pool_scale=pool_scale, pool_out_g=pool_out_g,
             attn_out_g=attn_out_g, w_out=w_out, norm2_g=norm2_g, w_up=w_up, w_down=w_down, final_g=final_g)
    m = dict(norm1_g=m_norm1_g, w_in=m_w_in, pool_w=m_pool_w, pool_scale=m_pool_scale, pool_out_g=m_pool_out_g,
             attn_out_g=m_attn_out_g, w_out=m_w_out, norm2_g=m_norm2_g, w_up=m_w_up, w_down=m_w_down,
             final_g=m_final_g)
    v = dict(norm1_g=v_norm1_g, w_in=v_w_in, pool_w=v_pool_w, pool_scale=v_pool_scale, pool_out_g=v_pool_out_g,
             attn_out_g=v_attn_out_g, w_out=v_w_out, norm2_g=v_norm2_g, w_up=v_w_up, w_down=v_w_down,
             final_g=v_final_g)

    shards = {n: _halves(w[n].astype(BF16)) for n in BIG_ORDER}
    (w_in_g,) = _gather_weights([shards["w_in"]])
    small = {n: w[n].reshape(1, -1) for n in ("norm1_g", "pool_scale", "pool_out_g", "attn_out_g", "norm2_g", "final_g")}
    dx, big, little = _local_step(
        x[0], loss_target[0], w_in_g.reshape(N_CHIPS, D_MODEL, 512), pool_w.astype(BF16), small,
        shards=[shards["w_out"], shards["w_up"], shards["w_down"]])

    grads, deltas, new_m, new_v = {}, {}, {}, {}
    loss_rows = D_MODEL // LANES
    slab_g = _pack([little[n] for n in SMALL_ORDER] + [little["loss_sq"]])
    zeros = jnp.zeros((loss_rows, LANES), F32)
    slab_w = _pack([w[n] for n in SMALL_ORDER] + [zeros])
    slab_m = _pack([m[n] for n in SMALL_ORDER] + [zeros])
    slab_v = _pack([v[n] for n in SMALL_ORDER] + [zeros])
    received = big.pop("received")
    g_s, d_s, m_s, v_s, loss, received["w_in"] = _small_step(
        slab_g, slab_w, slab_m, slab_v, loss_rows, [_pieces(big["w_in"])])
    shapes = [w[n].shape for n in SMALL_ORDER]
    for slab, dst in ((g_s, grads), (d_s, deltas), (m_s, new_m), (v_s, new_v)):
        for n, val in zip(SMALL_ORDER, _unpack(slab, shapes)):
            dst[n] = val

    xi, yi, ci = lax.axis_index("x"), lax.axis_index("y"), lax.axis_index("c")
    where = jnp.stack([4 * xi + 2 * yi + ci, 2 * xi + yi, ci]).astype(jnp.int32)
    full = _join_halves([_sum_pieces(where, received[n], _pieces(big[n])) for n in BIG_ORDER])
    for n, g in zip(BIG_ORDER, full):
        grads[n] = g.reshape(w[n].shape)
        deltas[n], new_m[n], new_v[n] = _adamw_big(w[n], grads[n], m[n], v[n])

    return (loss[0, 0], dx[None], *[grads[n] for n in WEIGHT_ORDER], *[deltas[n] for n in WEIGHT_ORDER],
            *[new_m[n] for n in WEIGHT_ORDER], *[new_v[n] for n in WEIGHT_ORDER])
```

```python
import functools

import jax
import jax.numpy as jnp
from jax import lax
from jax.experimental import pallas as pl
from jax.experimental.pallas import tpu as pltpu

F32 = jnp.float32
BF16 = jnp.bfloat16

D_MODEL = 1024
D_POOL = 512
D_ATTN = 512
POOL_WINDOWS = (2, 4, 8, 16)
POOL_GROUP_DIM = 128
POOL_HALO = 16
HEAD_DIM = 64
HEADS_PER_BLOCK = 4
ATTN_LANES = HEADS_PER_BLOCK * HEAD_DIM
D_FF = 4096
N_CHIPS = 4
N_DEV = 8
EPS = 1e-6
ATTN_SCALE = 0.125
ATTN_TILE = 256
ATTN_ROW_CHUNKS = 1
DEAD_LOG = -105.0
ROW_TILE = 512
MLP_TILE = 256
DW_TOKEN_TILE = 2048
LANES = 128

ADAM_LR = 0.001
ADAM_B1 = 0.9
ADAM_B2 = 0.999
ADAM_EPS = 1e-08
ADAM_WD = 0.01
ADAM_STEP = 10

MESH = pl.DeviceIdType.MESH
ANY = pl.BlockSpec(memory_space=pl.ANY)
VMEM_LIMIT = 56 * 1024 * 1024


def _nn(a, b):
    return jnp.dot(a, b, preferred_element_type=F32)


def _nt(a, b):
    return lax.dot_general(a, b, (((1,), (1,)), ((), ())), preferred_element_type=F32)


def _tn(a, b):
    return lax.dot_general(a, b, (((0,), (0,)), ((), ())), preferred_element_type=F32)


def _rms(x):
    return lax.rsqrt(jnp.mean(x * x, axis=-1, keepdims=True) + EPS)


def _rms_bwd(dy, n, r, g):
    dn = dy * g
    return r * (dn - n * jnp.mean(dn * n, axis=-1, keepdims=True))


def _params(**kw):
    return pltpu.CompilerParams(vmem_limit_bytes=VMEM_LIMIT, **kw)


def _run_at(first, middle, last, phases):
    pl.when(first)(phases[0])
    if len(phases) == 3:
        pl.when(middle)(phases[1])
    return lambda: pl.when(last)(phases[-1])


def _in_proj(x, g1, w_in, shards=()):
    S = x.shape[0]
    ts = ROW_TILE
    n = len(shards)

    def body(*refs):
        x_ref, g_ref, w_ref = refs[:3]
        hn_ref, u_ref, qkv_ref = refs[3 + n:6 + n]
        i = pl.program_id(0)
        j = pl.program_id(1)
        finish = lambda: None
        if n:
            steps = pl.num_programs(0)
            finish = _run_at(jnp.logical_and(i == 0, j == 0), jnp.logical_and(i == steps // 2, j == 0),
                             jnp.logical_and(i == steps - 1, j == 3),
                             _gather_phases(refs[3:3 + n], refs[6 + n:6 + 2 * n], *refs[6 + 2 * n:]))

        @pl.when(j == 0)
        def _():
            xf = x_ref[...]
            hn_ref[...] = (xf * _rms(xf) * g_ref[...]).astype(BF16)

        p = _nn(hn_ref[...], w_ref[0])

        @pl.when(j == 0)
        def _():
            u_ref[...] = p

        @pl.when(j > 0)
        def _():
            qkv_ref[0] = p.astype(BF16)

        finish()

    return pl.pallas_call(
        body,
        name="in_proj",
        grid=(S // ts, 4),
        in_specs=[
            pl.BlockSpec((ts, D_MODEL), lambda i, j: (i, 0)),
            pl.BlockSpec((1, D_MODEL), lambda i, j: (0, 0)),
            pl.BlockSpec((1, D_MODEL, 512), lambda i, j: (j, 0, 0)),
        ] + [ANY] * n,
        out_specs=[
            pl.BlockSpec((ts, D_MODEL), lambda i, j: (i, 0)),
            pl.BlockSpec((ts, D_POOL), lambda i, j: (i, 0)),
            pl.BlockSpec((1, ts, 512), lambda i, j: (jnp.maximum(j - 1, 0), i, 0)),
        ] + [ANY] * n,
        out_shape=[
            jax.ShapeDtypeStruct((S, D_MODEL), BF16),
            jax.ShapeDtypeStruct((S, D_POOL), F32),
            jax.ShapeDtypeStruct((3, S, 512), BF16),
        ] + _gather_shapes(shards),
        scratch_shapes=_gather_sems(n) if n else [],
        compiler_params=_params(),
    )(x, g1, w_in, *shards)


def _pool_counts(first_row, rows):
    t = first_row + lax.broadcasted_iota(jnp.int32, (rows, 1), 0)
    return [1.0 / jnp.minimum(t + 1, w).astype(F32) for w in POOL_WINDOWS]


def _pooled(u_tile, halo, first_row):
    ts = u_tile.shape[0]
    inv = _pool_counts(first_row, ts)
    outs = []
    for g, w in enumerate(POOL_WINDOWS):
        lanes = slice(g * POOL_GROUP_DIM, (g + 1) * POOL_GROUP_DIM)
        xg = u_tile[:, lanes]
        acc = jnp.concatenate([halo[:, lanes], xg], axis=0)
        shift = 1
        while shift < w:
            acc = acc + pltpu.roll(acc, shift, axis=0)
            shift *= 2
        outs.append(acc[POOL_HALO:, :] * inv[g] - xg)
    return outs


def _pool_fwd(u, pool_w, pool_scale):
    S = u.shape[0]
    ts = ROW_TILE
    hb = ts // POOL_HALO

    def body(u_ref, halo_ref, w_ref, s_ref, y_ref):
        i = pl.program_id(0)
        halo = jnp.where(i == 0, 0.0, halo_ref[...])
        pooled = _pooled(u_ref[...], halo, i * ts)
        for g in range(len(POOL_WINDOWS)):
            lanes = slice(g * POOL_GROUP_DIM, (g + 1) * POOL_GROUP_DIM)
            y_ref[:, lanes] = _nn(pooled[g].astype(BF16), w_ref[g]) * s_ref[:, lanes]

    return pl.pallas_call(
        body,
        name="pool_fwd",
        grid=(S // ts,),
        in_specs=[
            pl.BlockSpec((ts, D_POOL), lambda i: (i, 0)),
            pl.BlockSpec((POOL_HALO, D_POOL), lambda i: (jnp.maximum(i * hb - 1, 0), 0)),
            pl.BlockSpec((4, POOL_GROUP_DIM, POOL_GROUP_DIM), lambda i: (0, 0, 0)),
            pl.BlockSpec((1, D_POOL), lambda i: (0, 0)),
        ],
        out_specs=pl.BlockSpec((ts, D_POOL), lambda i: (i, 0)),
        out_shape=jax.ShapeDtypeStruct((S, D_POOL), F32),
        compiler_params=_params(),
    )(u, u, pool_w, pool_scale)


def _head_masks():
    lane = lax.broadcasted_iota(jnp.int32, (1, ATTN_LANES), 1)
    return [jnp.logical_and(lane >= h * HEAD_DIM, lane < (h + 1) * HEAD_DIM) for h in range(HEADS_PER_BLOCK)]


def _tri_masks(t):
    row = lax.broadcasted_iota(jnp.int32, (t, t), 0)
    col = lax.broadcasted_iota(jnp.int32, (t, t), 1)
    return row, col


def _split_bf16(x):
    hi = x.astype(BF16)
    lo = (x - hi.astype(F32)).astype(BF16)
    return hi, lo


def _log_sigmoids(z):
    sp = jnp.log(1.0 + jnp.exp(-jnp.abs(z)))
    ls = jnp.minimum(z, 0.0) - sp
    return ls, ls - z


def _attn_fwd(qkv, shards=()):
    S = qkv.shape[1]
    t = ATTN_TILE
    n = len(shards)
    nblk = D_ATTN // ATTN_LANES

    def body(*refs):
        q_ref, k_ref, v_ref = refs[:3]
        o_ref = refs[3 + n]
        vh_ref, acc_ref, z_ref = refs[4 + 2 * n:7 + 2 * n]
        hp = pl.program_id(0)
        i = pl.program_id(1)
        finish = lambda: None
        if n:
            finish = _run_at(jnp.logical_and(hp == 0, i == 0), jnp.logical_and(hp == nblk // 2, i == 0),
                             jnp.logical_and(hp == nblk - 1, i == pl.num_programs(1) - 1),
                             _gather_phases(refs[3:3 + n], refs[4 + n:4 + 2 * n], *refs[7 + 2 * n:]))
        masks = _head_masks()

        @pl.when(i == 0)
        def _():
            vv = v_ref[0]
            for h in range(HEADS_PER_BLOCK):
                vh_ref[h] = jnp.where(masks[h], vv, jnp.zeros_like(vv))

        row, col = _tri_masks(t)
        later = (row > col).astype(BF16)
        causal = col < row
        qs = q_ref[0] * ATTN_SCALE
        units = [(h, r) for h in range(HEADS_PER_BLOCK) for r in range(ATTN_ROW_CHUNKS)]
        rc = t // ATTN_ROW_CHUNKS
        qu = [jnp.where(masks[h], qs, jnp.zeros_like(qs))[r * rc:(r + 1) * rc] for h, r in units]
        causal_u = [causal[r * rc:(r + 1) * rc] for _, r in units]

        def scores(j, slot):
            kj = k_ref[0, pl.ds(pl.multiple_of(j * t, t), t), :]
            for u in range(len(units)):
                z_ref[slot, u] = _nt(qu[u], kj)

        def tile(j, left, slot, carry, diag):
            cs = list(carry)
            keys = pl.ds(pl.multiple_of(j * t, t), t)
            ls, tail = [None] * len(units), [None] * len(units)
            for u in range(len(units)):
                ls[u], l1m = _log_sigmoids(z_ref[slot, u])
                if diag:
                    l1m = jnp.where(causal_u[u], l1m, 0.0)
                hi, lo = _split_bf16(l1m)
                tail[u] = _nn(hi, later) + _nn(lo, later) + cs[u]
                cs[u] = cs[u] + jnp.sum(l1m, axis=1, keepdims=True)
            top = cs[0]
            for u in range(1, len(units)):
                top = jnp.maximum(top, cs[u])
            go = jnp.logical_and(left > 0, jnp.max(top) > DEAD_LOG)
            scores(jnp.maximum(j - 1, 0), 1 - slot)
            for u, (h, _) in enumerate(units):
                a = jnp.exp(ls[u] + tail[u])
                if diag:
                    a = jnp.where(causal_u[u], a, 0.0)
                pv = _nn(a.astype(BF16), vh_ref[h, keys, :])
                if diag:
                    acc_ref[u] = pv
                else:
                    acc_ref[u] += pv
            return (go, *cs)

        scores(i, 0)
        state = (jnp.int32(0), *tile(i, i, 0, [jnp.zeros((rc, 1), F32)] * len(units), True))

        def step(state):
            jj = state[0]
            return (jj + 1, *tile(i - 1 - jj, i - 1 - jj, (jj + 1) % 2, state[2:], False))

        lax.while_loop(lambda s: s[1], step, state)
        out = None
        for h in range(HEADS_PER_BLOCK):
            rows = jnp.concatenate([acc_ref[u] for u, (hh, _) in enumerate(units) if hh == h], axis=0)
            out = rows if out is None else out + rows
        o_ref[...] = out
        finish()

    return pl.pallas_call(
        body,
        name="attn_fwd",
        grid=(nblk, S // t),
        in_specs=[
            pl.BlockSpec((1, t, ATTN_LANES), lambda hp, i: (0, i, hp)),
            pl.BlockSpec((1, S, ATTN_LANES), lambda hp, i: (1, 0, hp)),
            pl.BlockSpec((1, S, ATTN_LANES), lambda hp, i: (2, 0, hp)),
        ] + [ANY] * n,
        out_specs=[pl.BlockSpec((t, ATTN_LANES), lambda hp, i: (i, hp))] + [ANY] * n,
        out_shape=[jax.ShapeDtypeStruct((S, D_ATTN), F32)] + _gather_shapes(shards),
        scratch_shapes=[pltpu.VMEM((HEADS_PER_BLOCK, S, ATTN_LANES), BF16),
                        pltpu.VMEM((HEADS_PER_BLOCK * ATTN_ROW_CHUNKS, t // ATTN_ROW_CHUNKS, ATTN_LANES), F32),
                        pltpu.VMEM((2, HEADS_PER_BLOCK * ATTN_ROW_CHUNKS, t // ATTN_ROW_CHUNKS, t), F32)]
        + (_gather_sems(n) if n else []),
        compiler_params=_params(),
    )(qkv, qkv, qkv, *shards)


def _out_proj(y_pool, y_attn, x, g_pool, g_attn, w_out, g2):
    S = x.shape[0]
    ts = ROW_TILE

    def body(yp_ref, ya_ref, x_ref, gp_ref, ga_ref, w_ref, g2_ref, mixed_ref, h1_ref, hn2_ref):
        yp = yp_ref[...]
        ya = ya_ref[...]
        mixed = jnp.concatenate([yp * _rms(yp) * gp_ref[...], ya * _rms(ya) * ga_ref[...]], axis=-1).astype(BF16)
        mixed_ref[...] = mixed
        h1 = x_ref[...] + _nn(mixed, w_ref[...])
        h1_ref[...] = h1
        hn2_ref[...] = (h1 * _rms(h1) * g2_ref[...]).astype(BF16)

    row = lambda w: pl.BlockSpec((ts, w), lambda i: (i, 0))
    vec = lambda w: pl.BlockSpec((1, w), lambda i: (0, 0))
    return pl.pallas_call(
        body,
        name="out_proj",
        grid=(S // ts,),
        in_specs=[row(D_POOL), row(D_ATTN), row(D_MODEL), vec(D_POOL), vec(D_ATTN),
                  pl.BlockSpec((D_MODEL, D_MODEL), lambda i: (0, 0)), vec(D_MODEL)],
        out_specs=[row(D_MODEL), row(D_MODEL), row(D_MODEL)],
        out_shape=[
            jax.ShapeDtypeStruct((S, D_MODEL), BF16),
            jax.ShapeDtypeStruct((S, D_MODEL), F32),
            jax.ShapeDtypeStruct((S, D_MODEL), BF16),
        ],
        compiler_params=_params(),
    )(y_pool, y_attn, x, g_pool, g_attn, w_out, g2)


def _mlp_fwd(hn2, h1, w_up, w_down, g_final, target):
    S = hn2.shape[0]
    ts = MLP_TILE
    nf = D_FF // 1024

    def body(hn2_ref, h1_ref, wu_ref, wd_ref, gf_ref, tg_ref, r_ref, dh2_ref, dh2b_ref, lsq_ref, dgf_ref):
        i = pl.program_id(0)
        hn2v = hn2_ref[...]
        acts = []
        for c in range(nf):
            r = jnp.maximum(_nn(hn2v, wu_ref[c]), 0.0)
            r_ref[:, c * 1024:(c + 1) * 1024] = r.astype(BF16)
            acts.append((r * r).astype(BF16))
        h2 = h1_ref[...] + _nn(jnp.concatenate(acts, axis=1), wd_ref[...])

        @pl.when(i == 0)
        def _():
            lsq_ref[...] = jnp.zeros_like(lsq_ref)
            dgf_ref[...] = jnp.zeros_like(dgf_ref)

        rf = _rms(h2)
        n = h2 * rf
        gf = gf_ref[...]
        e = n * gf - tg_ref[...]
        lsq_ref[...] += jnp.sum(e * e, axis=0, keepdims=True)
        dy = e * (1.0 / D_MODEL)
        dgf_ref[...] += jnp.sum(dy * n, axis=0, keepdims=True)
        dh2 = _rms_bwd(dy, n, rf, gf)
        dh2_ref[...] = dh2
        dh2b_ref[...] = dh2.astype(BF16)

    row = lambda w: pl.BlockSpec((ts, w), lambda i: (i, 0))
    vec = lambda w: pl.BlockSpec((1, w), lambda i: (0, 0))
    once = pl.Buffered(1)
    return pl.pallas_call(
        body,
        name="mlp_fwd",
        grid=(S // ts,),
        in_specs=[row(D_MODEL), row(D_MODEL),
                  pl.BlockSpec((nf, D_MODEL, 1024), lambda i: (0, 0, 0), pipeline_mode=once),
                  pl.BlockSpec((D_FF, D_MODEL), lambda i: (0, 0), pipeline_mode=once),
                  vec(D_MODEL), row(D_MODEL)],
        out_specs=[row(D_FF), row(D_MODEL), row(D_MODEL), vec(D_MODEL), vec(D_MODEL)],
        out_shape=[
            jax.ShapeDtypeStruct((S, D_FF), BF16),
            jax.ShapeDtypeStruct((S, D_MODEL), F32),
            jax.ShapeDtypeStruct((S, D_MODEL), BF16),
            jax.ShapeDtypeStruct((1, D_MODEL), F32),
            jax.ShapeDtypeStruct((1, D_MODEL), F32),
        ],
        compiler_params=_params(),
    )(hn2, h1, w_up, w_down.reshape(D_FF, D_MODEL), g_final, target)


def _mlp_bwd_dx(dh2b, r_act, w_down, w_up, h1, dh2, g2):
    S = h1.shape[0]
    ts = MLP_TILE
    nf = D_FF // 1024

    def body(dh2b_ref, r_ref, wd_ref, wu_ref, h1_ref, dh2_ref, g2_ref, dup_ref, dh1_ref, dg2_ref):
        i = pl.program_id(0)
        dh2b = dh2b_ref[...]
        dhn2 = None
        for c in range(nf):
            chunk = slice(c * 1024, (c + 1) * 1024)
            dup = (_nt(dh2b, wd_ref[c]) * (2.0 * r_ref[:, chunk].astype(F32))).astype(BF16)
            dup_ref[:, chunk] = dup
            part = _nt(dup, wu_ref[c])
            dhn2 = part if dhn2 is None else dhn2 + part

        @pl.when(i == 0)
        def _():
            dg2_ref[...] = jnp.zeros_like(dg2_ref)

        h1v = h1_ref[...]
        r2 = _rms(h1v)
        n2 = h1v * r2
        dg2_ref[...] += jnp.sum(dhn2 * n2, axis=0, keepdims=True)
        dh1_ref[...] = dh2_ref[...] + _rms_bwd(dhn2, n2, r2, g2_ref[...])

    row = lambda w: pl.BlockSpec((ts, w), lambda i: (i, 0))
    vec = lambda w: pl.BlockSpec((1, w), lambda i: (0, 0))
    once = pl.Buffered(1)
    return pl.pallas_call(
        body,
        name="mlp_bwd_dx",
        grid=(S // ts,),
        in_specs=[row(D_MODEL), row(D_FF),
                  pl.BlockSpec((nf, 1024, D_MODEL), lambda i: (0, 0, 0), pipeline_mode=once),
                  pl.BlockSpec((nf, D_MODEL, 1024), lambda i: (0, 0, 0), pipeline_mode=once),
                  row(D_MODEL), row(D_MODEL), vec(D_MODEL)],
        out_specs=[row(D_FF), row(D_MODEL), vec(D_MODEL)],
        out_shape=[
            jax.ShapeDtypeStruct((S, D_FF), BF16),
            jax.ShapeDtypeStruct((S, D_MODEL), F32),
            jax.ShapeDtypeStruct((1, D_MODEL), F32),
        ],
        compiler_params=_params(),
    )(dh2b, r_act, w_down, w_up, h1, dh2, g2)


def _tokens_tn(name, a, b, a_chunked, square_a):
    S = a.shape[0]
    ts = min(DW_TOKEN_TILE, S)
    nf = D_FF // 1024

    def body(a_ref, b_ref, o_ref, acc):
        tt = pl.program_id(1)
        av = a_ref[...]
        if square_a:
            af = av.astype(F32)
            av = (af * af).astype(BF16)
        part = _tn(av, b_ref[...])

        @pl.when(tt == 0)
        def _():
            acc[...] = part

        @pl.when(tt > 0)
        def _():
            acc[...] += part

        @pl.when(tt == pl.num_programs(1) - 1)
        def _():
            o_ref[0] = acc[...].astype(BF16)

    whole = pl.BlockSpec((ts, 1024), lambda c, tt: (tt, 0))
    chunk = pl.BlockSpec((ts, 1024), lambda c, tt: (tt, c))
    return pl.pallas_call(
        body,
        name=name,
        grid=(nf, S // ts),
        in_specs=[chunk, whole] if a_chunked else [whole, chunk],
        out_specs=pl.BlockSpec((1, 1024, 1024), lambda c, tt: (c, 0, 0)),
        out_shape=jax.ShapeDtypeStruct((nf, 1024, 1024), BF16),
        scratch_shapes=[pltpu.VMEM((1024, 1024), F32)],
        compiler_params=_params(),
    )(a, b)


def _mlp_bwd_dw(r_act, dh2b, hn2, dup):
    return (_tokens_tn("mlp_bwd_dw_down", r_act, dh2b, True, True),
            _tokens_tn("mlp_bwd_dw_up", hn2, dup, False, False))


def _out_bwd(dh1, w_out, mixed, y_pool, y_attn, g_pool, g_attn):
    S = dh1.shape[0]
    ts = ROW_TILE

    def body(dh1_ref, w_ref, mixed_ref, yp_ref, ya_ref, gp_ref, ga_ref,
             dyp_ref, dya_ref, dw_ref, dgp_ref, dga_ref, dw_acc):
        i = pl.program_id(0)
        dh1b = dh1_ref[...].astype(BF16)
        dmixed = _nt(dh1b, w_ref[...])
        dw = _tn(mixed_ref[...], dh1b)

        @pl.when(i == 0)
        def _():
            dw_acc[...] = dw
            dgp_ref[...] = jnp.zeros_like(dgp_ref)
            dga_ref[...] = jnp.zeros_like(dga_ref)

        @pl.when(i > 0)
        def _():
            dw_acc[...] += dw

        @pl.when(i == pl.num_programs(0) - 1)
        def _():
            dw_ref[...] = dw_acc[...].astype(BF16)

        for y_ref, g_ref, dy_ref, dg_ref, lanes in (
                (yp_ref, gp_ref, dyp_ref, dgp_ref, slice(0, D_POOL)),
                (ya_ref, ga_ref, dya_ref, dga_ref, slice(D_POOL, D_MODEL))):
            y = y_ref[...]
            r = _rms(y)
            n = y * r
            dm = dmixed[:, lanes]
            dg_ref[...] += jnp.sum(dm * n, axis=0, keepdims=True)
            dy_ref[...] = _rms_bwd(dm, n, r, g_ref[...])

    row = lambda w: pl.BlockSpec((ts, w), lambda i: (i, 0))
    vec = lambda w: pl.BlockSpec((1, w), lambda i: (0, 0))
    full = pl.BlockSpec((D_MODEL, D_MODEL), lambda i: (0, 0))
    return pl.pallas_call(
        body,
        name="out_bwd",
        grid=(S // ts,),
        in_specs=[row(D_MODEL), full, row(D_MODEL), row(D_POOL), row(D_ATTN), vec(D_POOL), vec(D_ATTN)],
        out_specs=[row(D_POOL), row(D_ATTN), full, vec(D_POOL), vec(D_ATTN)],
        out_shape=[
            jax.ShapeDtypeStruct((S, D_POOL), F32),
            jax.ShapeDtypeStruct((S, D_ATTN), F32),
            jax.ShapeDtypeStruct((D_MODEL, D_MODEL), BF16),
            jax.ShapeDtypeStruct((1, D_POOL), F32),
            jax.ShapeDtypeStruct((1, D_ATTN), F32),
        ],
        scratch_shapes=[pltpu.VMEM((D_MODEL, D_MODEL), F32)],
        compiler_params=_params(),
    )(dh1, w_out, mixed, y_pool, y_attn, g_pool, g_attn)


def _attn_bwd(qkv, o, do, partials=()):
    S = qkv.shape[1]
    t = ATTN_TILE
    n = len(partials)
    nblk = D_ATTN // ATTN_LANES

    def body(*refs):
        q_ref, k_ref, v_ref, o_ref, do_ref = refs[:5]
        dq_ref, dk_ref, dv_ref = refs[5 + n:8 + n]
        kh_ref, dk_acc, dv_acc, dq_acc, z_ref, da_ref = refs[8 + 2 * n:14 + 2 * n]
        hp = pl.program_id(0)
        i = pl.program_id(1)
        finish = lambda: None
        if n:
            finish = _run_at(jnp.logical_and(hp == 0, i == 0), None,
                             jnp.logical_and(hp == nblk - 1, i == pl.num_programs(1) - 1),
                             _reduce_phases(refs[5:5 + n], refs[8 + n:8 + 2 * n], *refs[14 + 2 * n:]))
        masks = _head_masks()

        @pl.when(i == 0)
        def _():
            kk = k_ref[0]
            for h in range(HEADS_PER_BLOCK):
                kh_ref[h] = jnp.where(masks[h], kk, jnp.zeros_like(kk))
            dk_acc[...] = jnp.zeros_like(dk_acc)
            dv_acc[...] = jnp.zeros_like(dv_acc)

        row, col = _tri_masks(t)
        later = (row > col).astype(BF16)
        from_s = (row >= col).astype(BF16)
        causal = col < row
        qs = q_ref[0] * ATTN_SCALE
        dob = do_ref[...].astype(BF16)
        d_all = dob.astype(F32) * o_ref[...]
        qh = [jnp.where(masks[h], qs, jnp.zeros_like(qs)) for h in range(HEADS_PER_BLOCK)]
        doh = [jnp.where(masks[h], dob, jnp.zeros_like(dob)) for h in range(HEADS_PER_BLOCK)]
        d_row = [jnp.sum(jnp.where(masks[h], d_all, 0.0), axis=1, keepdims=True) for h in range(HEADS_PER_BLOCK)]

        heads = range(HEADS_PER_BLOCK)

        def scores(j, slot):
            keys = pl.ds(pl.multiple_of(j * t, t), t)
            kj = k_ref[0, keys, :]
            vj = v_ref[0, keys, :]
            for h in heads:
                z_ref[slot, h] = _nt(qh[h], kj)
                da_ref[slot, h] = _nt(doh[h], vj)

        def tile(j, left, slot, carry, diag):
            nh = HEADS_PER_BLOCK
            c_l, c_g = list(carry[:nh]), list(carry[nh:])
            keys = pl.ds(pl.multiple_of(j * t, t), t)
            ls, tail = [None] * nh, [None] * nh
            for h in heads:
                ls[h], l1m = _log_sigmoids(z_ref[slot, h])
                if diag:
                    l1m = jnp.where(causal, l1m, 0.0)
                hi, lo = _split_bf16(l1m)
                tail[h] = _nn(hi, later) + _nn(lo, later) + c_l[h]
                c_l[h] = c_l[h] + jnp.sum(l1m, axis=1, keepdims=True)
            top = c_l[0]
            for h in range(1, nh):
                top = jnp.maximum(top, c_l[h])
            go = jnp.logical_and(left > 0, jnp.max(top) > DEAD_LOG)
            g, before = [None] * nh, [None] * nh
            dv = None
            for h in heads:
                a = jnp.exp(ls[h] + tail[h])
                if diag:
                    a = jnp.where(causal, a, 0.0)
                ab = a.astype(BF16)
                g[h] = ab.astype(F32) * da_ref[slot, h]
                ghi, glo = _split_bf16(g[h])
                before[h] = d_row[h] - (_nn(ghi, from_s) + _nn(glo, from_s) + c_g[h])
                c_g[h] = c_g[h] + jnp.sum(g[h], axis=1, keepdims=True)
                part = _tn(ab, doh[h])
                dv = part if dv is None else dv + part
            dv_acc[keys, :] += dv
            scores(jnp.maximum(j - 1, 0), 1 - slot)
            dk = None
            for h in heads:
                beta = jnp.exp(ls[h])
                dz = g[h] * (1.0 - beta) - before[h] * beta
                if diag:
                    dz = jnp.where(causal, dz, 0.0)
                dzb = dz.astype(BF16)
                dqh = _nn(dzb, kh_ref[h, keys, :])
                if diag:
                    dq_acc[h] = dqh
                else:
                    dq_acc[h] += dqh
                part = _tn(dzb, qh[h])
                dk = part if dk is None else dk + part
            dk_acc[keys, :] += dk
            return (go, *c_l, *c_g)

        scores(i, 0)
        state = (jnp.int32(0), *tile(i, i, 0, [jnp.zeros((t, 1), F32)] * (2 * HEADS_PER_BLOCK), True))

        def step(state):
            jj = state[0]
            return (jj + 1, *tile(i - 1 - jj, i - 1 - jj, (jj + 1) % 2, state[2:], False))

        lax.while_loop(lambda s: s[1], step, state)
        dq = dq_acc[0]
        for h in range(1, HEADS_PER_BLOCK):
            dq = dq + dq_acc[h]
        dq_ref[...] = (dq * ATTN_SCALE).astype(BF16)

        @pl.when(i == pl.num_programs(1) - 1)
        def _():
            dk_ref[...] = dk_acc[...].astype(BF16)
            dv_ref[...] = dv_acc[...].astype(BF16)

        finish()

    qtile = pl.BlockSpec((t, ATTN_LANES), lambda hp, i: (i, hp))
    whole = pl.BlockSpec((S, ATTN_LANES), lambda hp, i: (0, hp))
    return pl.pallas_call(
        body,
        name="attn_bwd",
        grid=(nblk, S // t),
        in_specs=[
            pl.BlockSpec((1, t, ATTN_LANES), lambda hp, i: (0, i, hp)),
            pl.BlockSpec((1, S, ATTN_LANES), lambda hp, i: (1, 0, hp)),
            pl.BlockSpec((1, S, ATTN_LANES), lambda hp, i: (2, 0, hp)),
            qtile, qtile,
        ] + [ANY] * n,
        out_specs=[qtile, whole, whole] + [ANY] * n,
        out_shape=[jax.ShapeDtypeStruct((S, D_ATTN), BF16)] * 3 + _reduce_shapes(partials),
        scratch_shapes=[
            pltpu.VMEM((HEADS_PER_BLOCK, S, ATTN_LANES), BF16),
            pltpu.VMEM((S, ATTN_LANES), F32),
            pltpu.VMEM((S, ATTN_LANES), F32),
            pltpu.VMEM((HEADS_PER_BLOCK, t, ATTN_LANES), F32),
            pltpu.VMEM((2, HEADS_PER_BLOCK, t, t), F32),
            pltpu.VMEM((2, HEADS_PER_BLOCK, t, t), F32),
        ] + (_reduce_sems(n) if n else []),
        compiler_params=_params(),
    )(qkv, qkv, qkv, o, do, *partials)


def _pool_bwd_w(u, dyp, pool_w, pool_scale):
    S = u.shape[0]
    ts = ROW_TILE
    hb = ts // POOL_HALO

    def body(u_ref, halo_ref, dy_ref, w_ref, s_ref, dp_ref, dw_ref, ds_ref):
        i = pl.program_id(0)
        halo = jnp.where(i == 0, 0.0, halo_ref[...])
        pooled = _pooled(u_ref[...], halo, i * ts)

        @pl.when(i == 0)
        def _():
            dw_ref[...] = jnp.zeros_like(dw_ref)
            ds_ref[...] = jnp.zeros_like(ds_ref)

        for g in range(len(POOL_WINDOWS)):
            lanes = slice(g * POOL_GROUP_DIM, (g + 1) * POOL_GROUP_DIM)
            pg = pooled[g].astype(BF16)
            dy = dy_ref[:, lanes]
            ds_ref[:, lanes] += jnp.sum(dy * _nn(pg, w_ref[g]), axis=0, keepdims=True)
            dmapped = (dy * s_ref[:, lanes]).astype(BF16)
            dp_ref[:, lanes] = _nt(dmapped, w_ref[g])
            dw_ref[g] += _tn(pg, dmapped)

    row = pl.BlockSpec((ts, D_POOL), lambda i: (i, 0))
    vec = pl.BlockSpec((1, D_POOL), lambda i: (0, 0))
    wspec = pl.BlockSpec((4, POOL_GROUP_DIM, POOL_GROUP_DIM), lambda i: (0, 0, 0))
    return pl.pallas_call(
        body,
        name="pool_bwd_w",
        grid=(S // ts,),
        in_specs=[row, pl.BlockSpec((POOL_HALO, D_POOL), lambda i: (jnp.maximum(i * hb - 1, 0), 0)),
                  row, wspec, vec],
        out_specs=[row, wspec, vec],
        out_shape=[
            jax.ShapeDtypeStruct((S, D_POOL), F32),
            jax.ShapeDtypeStruct((4, POOL_GROUP_DIM, POOL_GROUP_DIM), F32),
            jax.ShapeDtypeStruct((1, D_POOL), F32),
        ],
        compiler_params=_params(),
    )(u, u, dyp, pool_w, pool_scale)


def _pool_bwd_u(dpooled):
    S = dpooled.shape[0]
    ts = ROW_TILE
    hb = ts // POOL_HALO
    last = S // ts - 1

    def body(dp_ref, halo_ref, du_ref):
        i = pl.program_id(0)
        dp = dp_ref[...]
        halo = jnp.where(i == last, 0.0, halo_ref[...])
        inv = _pool_counts(i * ts, ts)
        n = ts + POOL_HALO
        for g, w in enumerate(POOL_WINDOWS):
            lanes = slice(g * POOL_GROUP_DIM, (g + 1) * POOL_GROUP_DIM)
            dg = dp[:, lanes]
            acc = jnp.concatenate([dg * inv[g], halo[:, lanes] * (1.0 / w)], axis=0)
            shift = 1
            while shift < w:
                acc = acc + pltpu.roll(acc, n - shift, axis=0)
                shift *= 2
            du_ref[:, lanes] = (acc[:ts, :] - dg).astype(BF16)

    return pl.pallas_call(
        body,
        name="pool_bwd_u",
        grid=(S // ts,),
        in_specs=[pl.BlockSpec((ts, D_POOL), lambda i: (i, 0)),
                  pl.BlockSpec((POOL_HALO, D_POOL), lambda i: (jnp.minimum((i + 1) * hb, (last + 1) * hb - 1), 0))],
        out_specs=pl.BlockSpec((ts, D_POOL), lambda i: (i, 0)),
        out_shape=jax.ShapeDtypeStruct((S, D_POOL), BF16),
        compiler_params=_params(),
    )(dpooled, dpooled)


def _in_bwd(du, dq, dk, dv, hn, w_in, x, dh1, g1):
    S = x.shape[0]
    ts = ROW_TILE // 2

    def body(du_ref, dq_ref, dk_ref, dv_ref, hn_ref, w_ref, x_ref, dh1_ref, g_ref, dx_ref, dw_ref, dg_ref, dw_acc):
        i = pl.program_id(0)

        @pl.when(i == 0)
        def _():
            dw_acc[...] = jnp.zeros_like(dw_acc)
            dg_ref[...] = jnp.zeros_like(dg_ref)

        hn = hn_ref[...]
        dhn = jnp.zeros((ts, D_MODEL), F32)
        for j, dp_ref in enumerate((du_ref, dq_ref, dk_ref, dv_ref)):
            dp = dp_ref[...]
            dhn = dhn + _nt(dp, w_ref[j])
            dw_acc[j] += _tn(hn, dp)
        xv = x_ref[...]
        r1 = _rms(xv)
        n1 = xv * r1
        dg_ref[...] += jnp.sum(dhn * n1, axis=0, keepdims=True)
        dx_ref[...] = dh1_ref[...] + _rms_bwd(dhn, n1, r1, g_ref[...])

        @pl.when(i == pl.num_programs(0) - 1)
        def _():
            dw_ref[...] = dw_acc[...].astype(BF16)

    row = lambda w: pl.BlockSpec((ts, w), lambda i: (i, 0))
    vec = pl.BlockSpec((1, D_MODEL), lambda i: (0, 0))
    wspec = pl.BlockSpec((4, D_MODEL, 512), lambda i: (0, 0, 0))
    return pl.pallas_call(
        body,
        name="in_bwd",
        grid=(S // ts,),
        in_specs=[row(512), row(512), row(512), row(512), row(D_MODEL), wspec, row(D_MODEL), row(D_MODEL), vec],
        out_specs=[row(D_MODEL), wspec, vec],
        out_shape=[
            jax.ShapeDtypeStruct((S, D_MODEL), F32),
            jax.ShapeDtypeStruct((4, D_MODEL, 512), BF16),
            jax.ShapeDtypeStruct((1, D_MODEL), F32),
        ],
        scratch_shapes=[pltpu.VMEM((4, D_MODEL, 512), F32)],
        compiler_params=_params(),
    )(du, dq, dk, dv, hn, w_in, x, dh1, g1)


def _pieces(g):
    return g.reshape(N_CHIPS, 2, -1, g.shape[-1])


def _local_step(x, target, w_in, pool_w, small, full=None, shards=None):
    spread = shards is not None
    if spread:
        hn, u, qkv, w_out = _in_proj(x, small["norm1_g"], w_in, shards[:1])
        w_out = w_out.reshape(D_MODEL, D_MODEL)
    else:
        hn, u, qkv = _in_proj(x, small["norm1_g"], w_in)
        w_out, w_up, w_down = full
    y_pool = _pool_fwd(u, pool_w, small["pool_scale"])
    if spread:
        y_attn, w_up, w_down = _attn_fwd(qkv, shards[1:])
        w_up = w_up.reshape(N_CHIPS, D_MODEL, 1024)
        w_down = w_down.reshape(N_CHIPS, 1024, D_MODEL)
    else:
        (y_attn,) = _attn_fwd(qkv)
    mixed, h1, hn2 = _out_proj(y_pool, y_attn, x, small["pool_out_g"], small["attn_out_g"], w_out, small["norm2_g"])
    r_act, dh2, dh2b, lsq, dgf = _mlp_fwd(hn2, h1, w_up, w_down, small["final_g"], target)

    dup, dh1, dg2 = _mlp_bwd_dx(dh2b, r_act, w_down, w_up, h1, dh2, small["norm2_g"])
    dw_down, dw_up = _mlp_bwd_dw(r_act, dh2b, hn2, dup)
    dyp, dya, dw_out, dgp, dga = _out_bwd(dh1, w_out, mixed, y_pool, y_attn, small["pool_out_g"], small["attn_out_g"])
    if spread:
        dq, dk, dv, got_out, got_up, got_down = _attn_bwd(
            qkv, y_attn, dya, [_pieces(dw_out), _pieces(dw_up), _pieces(dw_down)])
    else:
        dq, dk, dv = _attn_bwd(qkv, y_attn, dya)
    dpooled, dpool_w, dpool_scale = _pool_bwd_w(u, dyp, pool_w, small["pool_scale"])
    du = _pool_bwd_u(dpooled)
    dx, dw_in, dg1 = _in_bwd(du, dq, dk, dv, hn, w_in, x, dh1, small["norm1_g"])

    big = {"w_in": dw_in, "w_out": dw_out, "w_up": dw_up, "w_down": dw_down}
    if spread:
        big["received"] = {"w_out": got_out, "w_up": got_up, "w_down": got_down}
    little = {"norm1_g": dg1, "pool_w": dpool_w, "pool_scale": dpool_scale, "pool_out_g": dgp,
              "attn_out_g": dga, "norm2_g": dg2, "final_g": dgf, "loss_sq": lsq}
    return dx, big, little


def _place():
    x, y, c = lax.axis_index("x"), lax.axis_index("y"), lax.axis_index("c")
    other_chips = [(1 - x, y), (x, 1 - y), (1 - x, 1 - y)]
    return x, y, c, other_chips


def _chip_index(chip):
    return 2 * chip[0] + chip[1]


def _gather_shapes(shards):
    return [jax.ShapeDtypeStruct((N_CHIPS,) + s.shape, s.dtype) for s in shards]


def _gather_sems(n):
    return [pltpu.SemaphoreType.DMA((n, 7)), pltpu.SemaphoreType.DMA((n, 7))]


def _gather_phases(ins, outs, send_sems, recv_sems):
    n = len(ins)
    x, y, c, chips = _place()
    me = _chip_index((x, y))
    sibling = (x, y, 1 - c)

    def copy(a, k, chip_idx, half, to, src=None):
        dst = outs[a].at[chip_idx, half]
        return pltpu.make_async_remote_copy(
            src_ref=dst if src is None else src, dst_ref=dst,
            send_sem=send_sems.at[a, k], recv_sem=recv_sems.at[a, k],
            device_id=to, device_id_type=MESH)

    def own_chip(a, to):
        return pltpu.make_async_remote_copy(
            src_ref=ins[a], dst_ref=outs[a].at[me],
            send_sem=send_sems.at[a, 0], recv_sem=recv_sems.at[a, 0],
            device_id=to, device_id_type=MESH)

    def first(a):
        return [own_chip(a, sibling)] + [
            copy(a, 1 + j, me, c, (*chip, c), src=ins[a].at[c]) for j, chip in enumerate(chips)]

    def passed(a, j):
        return copy(a, 4 + j, _chip_index(chips[j]), c, sibling)

    def start():
        for a in range(n):
            for cp in first(a):
                cp.start()

    def forward():
        for a in range(n):
            for j, chip in enumerate(chips):
                copy(a, 1 + j, _chip_index(chip), c, (x, y, c)).wait_recv()
                passed(a, j).start()

    def finish():
        for a in range(n):
            own_chip(a, (x, y, c)).wait_recv()
            for j, chip in enumerate(chips):
                copy(a, 4 + j, _chip_index(chip), 1 - c, (x, y, c)).wait_recv()
        for a in range(n):
            for cp in first(a):
                cp.wait_send()
            for j in range(len(chips)):
                passed(a, j).wait_send()

    return start, forward, finish


def _gather_weights(shards):
    n = len(shards)

    def body(*refs):
        start, forward, finish = _gather_phases(refs[:n], refs[n:2 * n], *refs[2 * n:])
        start()
        forward()
        finish()

    return pl.pallas_call(
        body,
        name="gather_weights",
        in_specs=[ANY] * n,
        out_specs=[ANY] * n,
        out_shape=_gather_shapes(shards),
        scratch_shapes=_gather_sems(n),
    )(*shards)


def _reduce_shapes(partials):
    return [jax.ShapeDtypeStruct((N_DEV,) + p.shape[2:], p.dtype) for p in partials]


def _reduce_sems(n):
    return [pltpu.SemaphoreType.DMA((n, N_DEV)), pltpu.SemaphoreType.DMA((n, N_DEV))]


def _reduce_phases(ins, outs, send_sems, recv_sems):
    n = len(ins)
    x, y, c, _ = _place()
    me = 4 * x + 2 * y + c

    def to_peer(a, k):
        return pltpu.make_async_remote_copy(
            src_ref=ins[a].at[k // 2, k % 2], dst_ref=outs[a].at[me],
            send_sem=send_sems.at[a, k], recv_sem=recv_sems.at[a, me],
            device_id=(k // 4, (k // 2) % 2, k % 2), device_id_type=MESH)

    def from_peer(a, k):
        return pltpu.make_async_remote_copy(
            src_ref=ins[a].at[k // 2, k % 2], dst_ref=outs[a].at[k],
            send_sem=send_sems.at[a, k], recv_sem=recv_sems.at[a, k],
            device_id=(x, y, c), device_id_type=MESH)

    def start():
        for a in range(n):
            for k in range(N_DEV):
                @pl.when(k != me)
                def _(a=a, k=k):
                    to_peer(a, k).start()

    def finish():
        for a in range(n):
            for k in range(N_DEV):
                @pl.when(k != me)
                def _(a=a, k=k):
                    from_peer(a, k).wait_recv()
                    to_peer(a, k).wait_send()

    return start, finish


def _sum_pieces(where, parts, own):
    _, h, cols = parts.shape
    hb = min(h, 256)

    def body(where_ref, *refs):
        me = where_ref[0]
        acc = None
        for k in range(N_DEV):
            piece = jnp.where(k == me, refs[N_DEV][0, 0], refs[k][0]).astype(F32)
            acc = piece if acc is None else acc + piece
        refs[N_DEV + 1][0] = acc

    def sent_by(k):
        return lambda r, w: (jnp.where(w[0] == k, (k + 1) % N_DEV, k), r, 0)

    return pl.pallas_call(
        body,
        name="sum_pieces",
        grid_spec=pltpu.PrefetchScalarGridSpec(
            num_scalar_prefetch=1,
            grid=(h // hb,),
            in_specs=[pl.BlockSpec((1, hb, cols), sent_by(k)) for k in range(N_DEV)]
            + [pl.BlockSpec((1, 1, hb, cols), lambda r, w: (w[1], w[2], r, 0))],
            out_specs=pl.BlockSpec((1, hb, cols), lambda r, w: (w[2], r, 0)),
        ),
        out_shape=jax.ShapeDtypeStruct((2, h, cols), F32),
    )(where, *([parts] * N_DEV), own)


def _join_halves(halves):
    n = len(halves)

    def body(*refs):
        outs = refs[n:2 * n]
        send_sems, recv_sems = refs[2 * n:]
        x, y, c, _ = _place()
        sends = [
            pltpu.make_async_remote_copy(
                src_ref=outs[a].at[c], dst_ref=outs[a].at[c],
                send_sem=send_sems.at[a], recv_sem=recv_sems.at[a],
                device_id=(x, y, 1 - c), device_id_type=MESH)
            for a in range(n)]
        for cp in sends:
            cp.start()
        for a in range(n):
            pltpu.make_async_remote_copy(
                src_ref=outs[a].at[c], dst_ref=outs[a].at[1 - c],
                send_sem=send_sems.at[a], recv_sem=recv_sems.at[a],
                device_id=(x, y, c), device_id_type=MESH).wait_recv()
        for cp in sends:
            cp.wait_send()

    return pl.pallas_call(
        body,
        name="join_halves",
        in_specs=[ANY] * n,
        out_specs=[ANY] * n,
        out_shape=[jax.ShapeDtypeStruct(s.shape, s.dtype) for s in halves],
        input_output_aliases={a: a for a in range(n)},
        scratch_shapes=[pltpu.SemaphoreType.DMA((n,)), pltpu.SemaphoreType.DMA((n,))],
    )(*halves)


def _adamw(w, g, m, v):
    m = ADAM_B1 * m + (1.0 - ADAM_B1) * g
    v = ADAM_B2 * v + (1.0 - ADAM_B2) * jnp.square(g)
    m_hat = m / (1.0 - ADAM_B1 ** ADAM_STEP)
    v_hat = v / (1.0 - ADAM_B2 ** ADAM_STEP)
    delta = -ADAM_LR * (m_hat / (jnp.sqrt(v_hat) + ADAM_EPS) + ADAM_WD * w)
    return delta, m, v


def _adamw_big(w, g, m, v):
    rows, cols = w.shape
    rb = min(rows, 256)

    def body(w_ref, g_ref, m_ref, v_ref, d_ref, mo_ref, vo_ref):
        d_ref[...], mo_ref[...], vo_ref[...] = _adamw(w_ref[...], g_ref[...], m_ref[...], v_ref[...])

    blk = pl.BlockSpec((rb, cols), lambda r: (r, 0))
    return pl.pallas_call(
        body,
        name="adamw_big",
        grid=(rows // rb,),
        in_specs=[blk] * 4,
        out_specs=[blk] * 3,
        out_shape=[jax.ShapeDtypeStruct(w.shape, F32)] * 3,
    )(w, g, m, v)


SMALL_ORDER = ("pool_w", "norm1_g", "pool_scale", "pool_out_g", "attn_out_g", "norm2_g", "final_g")
SUBLANES = 8


def _pack(parts):
    rows = []
    for p in parts:
        p = p.reshape(-1, LANES)
        pad = (-p.shape[0]) % SUBLANES
        if pad:
            p = jnp.pad(p, ((0, pad), (0, 0)))
        rows.append(p)
    return jnp.concatenate(rows, axis=0)


def _unpack(slab, shapes):
    out, r = [], 0
    for shp in shapes:
        size = 1
        for d in shp:
            size *= d
        nrow = size // LANES
        out.append(slab[r:r + nrow].reshape(shp))
        r += nrow + (-nrow) % SUBLANES
    return out


def _small_step(partials, w, m, v, loss_rows, big_partials):
    rows = partials.shape[0]
    nb = len(big_partials)

    def body(*refs):
        p_ref, w_ref, m_ref, v_ref = refs[:4]
        big_in = refs[4:4 + nb]
        g_ref, d_ref, mo_ref, vo_ref, loss_ref = refs[4 + nb:9 + nb]
        big_out = refs[9 + nb:9 + 2 * nb]
        buf, send_sems, recv_sems = refs[9 + 2 * nb:12 + 2 * nb]
        start_big, finish_big = _reduce_phases(big_in, big_out, *refs[12 + 2 * nb:])
        start_big()
        x, y, c, _ = _place()
        me = 4 * x + 2 * y + c
        for k in range(N_DEV):
            @pl.when(k != me)
            def _(k=k):
                pltpu.make_async_remote_copy(
                    src_ref=p_ref, dst_ref=buf.at[me],
                    send_sem=send_sems.at[k], recv_sem=recv_sems.at[me],
                    device_id=(k // 4, (k // 2) % 2, k % 2), device_id_type=MESH).start()
        buf[me] = p_ref[...]
        for k in range(N_DEV):
            @pl.when(k != me)
            def _(k=k):
                pltpu.make_async_remote_copy(
                    src_ref=p_ref, dst_ref=buf.at[k],
                    send_sem=send_sems.at[k], recv_sem=recv_sems.at[k],
                    device_id=(x, y, c), device_id_type=MESH).wait()
        g = buf[0]
        for k in range(1, N_DEV):
            g = g + buf[k]
        g_ref[...] = g
        d_ref[...], mo_ref[...], vo_ref[...] = _adamw(w_ref[...], g, m_ref[...], v_ref[...])
        loss = (0.5 / D_MODEL) * jnp.sum(g[rows - loss_rows:, :])
        loss_ref[...] = jnp.full(loss_ref.shape, loss, F32)
        finish_big()

    vm = pl.BlockSpec(memory_space=pltpu.VMEM)
    slab = jax.ShapeDtypeStruct((rows, LANES), F32)
    return pl.pallas_call(
        body,
        name="small_step",
        in_specs=[vm] * 4 + [ANY] * nb,
        out_specs=[vm] * 5 + [ANY] * nb,
        out_shape=[slab, slab, slab, slab, jax.ShapeDtypeStruct((SUBLANES, LANES), F32)] + _reduce_shapes(big_partials),
        scratch_shapes=[pltpu.VMEM((N_DEV, rows, LANES), F32),
                        pltpu.SemaphoreType.DMA((N_DEV,)), pltpu.SemaphoreType.DMA((N_DEV,))] + _reduce_sems(nb),
    )(partials, w, m, v, *big_partials)


BIG_ORDER = ("w_in", "w_out", "w_up", "w_down")
WEIGHT_ORDER = ("norm1_g", "w_in", "pool_w", "pool_scale", "pool_out_g", "attn_out_g", "w_out", "norm2_g",
                "w_up", "w_down", "final_g")


def _halves(a):
    return a.reshape(2, a.shape[0] // 2, a.shape[1])


def kernel(x, norm1_g, w_in, pool_w, pool_scale, pool_out_g, attn_out_g, w_out, norm2_g, w_up, w_down, final_g, loss_target, m_norm1_g, m_w_in, m_pool_w, m_pool_scale, m_pool_out_g, m_attn_out_g, m_w_out, m_norm2_g, m_w_up, m_w_down, m_final_g, v_norm1_g, v_w_in, v_pool_w, v_pool_scale, v_pool_out_g, v_attn_out_g, v_w_out, v_norm2_g, v_w_up, v_w_down, v_final_g):
    w = dict(norm1_g=norm1_g, w_in=w_in, pool_w=pool_w, pool_scale=pool_scale, pool_out_g=pool_out_g,
             attn_out_g=attn_out_g, w_out=w_out, norm2_g=norm2_g, w_up=w_up, w_down=w_down, final_g=final_g)
    m = dict(norm1_g=m_norm1_g, w_in=m_w_in, pool_w=m_pool_w, pool_scale=m_pool_scale, pool_out_g=m_pool_out_g,
             attn_out_g=m_attn_out_g, w_out=m_w_out, norm2_g=m_norm2_g, w_up=m_w_up, w_down=m_w_down,
             final_g=m_final_g)
    v = dict(norm1_g=v_norm1_g, w_in=v_w_in, pool_w=v_pool_w, pool_scale=v_pool_scale, pool_out_g=v_pool_out_g,
             attn_out_g=v_attn_out_g, w_out=v_w_out, norm2_g=v_norm2_g, w_up=v_w_up, w_down=v_w_down,
             final_g=v_final_g)

    shards = {n: _halves(w[n].astype(BF16)) for n in BIG_ORDER}
    (w_in_g,) = _gather_weights([shards["w_in"]])
    small = {n: w[n].reshape(1, -1) for n in ("norm1_g", "pool_scale", "pool_out_g", "attn_out_g", "norm2_g", "final_g")}
    dx, big, little = _local_step(
        x[0], loss_target[0], w_in_g.reshape(N_CHIPS, D_MODEL, 512), pool_w.astype(BF16), small,
        shards=[shards["w_out"], shards["w_up"], shards["w_down"]])

    grads, deltas, new_m, new_v = {}, {}, {}, {}
    loss_rows = D_MODEL // LANES
    slab_g = _pack([little[n] for n in SMALL_ORDER] + [little["loss_sq"]])
    zeros = jnp.zeros((loss_rows, LANES), F32)
    slab_w = _pack([w[n] for n in SMALL_ORDER] + [zeros])
    slab_m = _pack([m[n] for n in SMALL_ORDER] + [zeros])
    slab_v = _pack([v[n] for n in SMALL_ORDER] + [zeros])
    received = big.pop("received")
    g_s, d_s, m_s, v_s, loss, received["w_in"] = _small_step(
        slab_g, slab_w, slab_m, slab_v, loss_rows, [_pieces(big["w_in"])])
    shapes = [w[n].shape for n in SMALL_ORDER]
    for slab, dst in ((g_s, grads), (d_s, deltas), (m_s, new_m), (v_s, new_v)):
        for n, val in zip(SMALL_ORDER, _unpack(slab, shapes)):
            dst[n] = val

    xi, yi, ci = lax.axis_index("x"), lax.axis_index("y"), lax.axis_index("c")
    where = jnp.stack([4 * xi + 2 * yi + ci, 2 * xi + yi, ci]).astype(jnp.int32)
    full = _join_halves([_sum_pieces(where, received[n], _pieces(big[n])) for n in BIG_ORDER])
    for n, g in zip(BIG_ORDER, full):
        grads[n] = g.reshape(w[n].shape)
        deltas[n], new_m[n], new_v[n] = _adamw_big(w[n], grads[n], m[n], v[n])

    return (loss[0, 0], dx[None], *[grads[n] for n in WEIGHT_ORDER], *[deltas[n] for n in WEIGHT_ORDER],
            *[new_m[n] for n in WEIGHT_ORDER], *[new_v[n] for n in WEIGHT_ORDER])
```

```python
import functools

import jax
import jax.numpy as jnp
from jax import lax
from jax.experimental import pallas as pl
from jax.experimental.pallas import tpu as pltpu

F32 = jnp.float32
BF16 = jnp.bfloat16

D_MODEL = 1024
D_POOL = 512
D_ATTN = 512
POOL_WINDOWS = (2, 4, 8, 16)
POOL_GROUP_DIM = 128
POOL_HALO = 16
HEAD_DIM = 64
HEADS_PER_BLOCK = 4
ATTN_LANES = HEADS_PER_BLOCK * HEAD_DIM
D_FF = 4096
N_CHIPS = 4
N_DEV = 8
EPS = 1e-6
ATTN_SCALE = 0.125
ATTN_TILE = 256
ATTN_ROW_CHUNKS = 1
DEAD_LOG = -105.0
ROW_TILE = 512
MLP_TILE = 256
DW_TOKEN_TILE = 2048
LANES = 128

ADAM_LR = 0.001
ADAM_B1 = 0.9
ADAM_B2 = 0.999
ADAM_EPS = 1e-08
ADAM_WD = 0.01
ADAM_STEP = 10

MESH = pl.DeviceIdType.MESH
ANY = pl.BlockSpec(memory_space=pl.ANY)
VMEM_LIMIT = 56 * 1024 * 1024


def _nn(a, b):
    return jnp.dot(a, b, preferred_element_type=F32)


def _nt(a, b):
    return lax.dot_general(a, b, (((1,), (1,)), ((), ())), preferred_element_type=F32)


def _tn(a, b):
    return lax.dot_general(a, b, (((0,), (0,)), ((), ())), preferred_element_type=F32)


def _rms(x):
    return lax.rsqrt(jnp.mean(x * x, axis=-1, keepdims=True) + EPS)


def _rms_bwd(dy, n, r, g):
    dn = dy * g
    return r * (dn - n * jnp.mean(dn * n, axis=-1, keepdims=True))


def _params(**kw):
    return pltpu.CompilerParams(vmem_limit_bytes=VMEM_LIMIT, **kw)


def _run_at(first, middle, last, phases):
    pl.when(first)(phases[0])
    if len(phases) == 3:
        pl.when(middle)(phases[1])
    return lambda: pl.when(last)(phases[-1])


def _in_proj(x, g1, w_in, shards=()):
    S = x.shape[0]
    ts = ROW_TILE
    n = len(shards)

    def body(*refs):
        x_ref, g_ref, w_ref = refs[:3]
        hn_ref, u_ref, qkv_ref = refs[3 + n:6 + n]
        i = pl.program_id(0)
        finish = lambda: None
        if n:
            steps = pl.num_programs(0)
            finish = _run_at(i == 0, i == steps // 2, i == steps - 1,
                             _gather_phases(refs[3:3 + n], refs[6 + n:6 + 2 * n], *refs[6 + 2 * n:]))

        xf = x_ref[...]
        hn = (xf * _rms(xf) * g_ref[...]).astype(BF16)
        hn_ref[...] = hn
        u_ref[...] = _nn(hn, w_ref[0])
        for j in range(1, 4):
            qkv_ref[j - 1] = _nn(hn, w_ref[j]).astype(BF16)

        finish()

    return pl.pallas_call(
        body,
        name="in_proj",
        grid=(S // ts,),
        in_specs=[
            pl.BlockSpec((ts, D_MODEL), lambda i: (i, 0)),
            pl.BlockSpec((1, D_MODEL), lambda i: (0, 0)),
            pl.BlockSpec((4, D_MODEL, 512), lambda i: (0, 0, 0), pipeline_mode=pl.Buffered(1)),
        ] + [ANY] * n,
        out_specs=[
            pl.BlockSpec((ts, D_MODEL), lambda i: (i, 0)),
            pl.BlockSpec((ts, D_POOL), lambda i: (i, 0)),
            pl.BlockSpec((3, ts, 512), lambda i: (0, i, 0)),
        ] + [ANY] * n,
        out_shape=[
            jax.ShapeDtypeStruct((S, D_MODEL), BF16),
            jax.ShapeDtypeStruct((S, D_POOL), F32),
            jax.ShapeDtypeStruct((3, S, 512), BF16),
        ] + _gather_shapes(shards),
        scratch_shapes=_gather_sems(n) if n else [],
        compiler_params=_params(),
    )(x, g1, w_in, *shards)


def _pool_counts(first_row, rows):
    t = first_row + lax.broadcasted_iota(jnp.int32, (rows, 1), 0)
    return [1.0 / jnp.minimum(t + 1, w).astype(F32) for w in POOL_WINDOWS]


def _pooled(u_tile, halo, first_row):
    ts = u_tile.shape[0]
    inv = _pool_counts(first_row, ts)
    outs = []
    for g, w in enumerate(POOL_WINDOWS):
        lanes = slice(g * POOL_GROUP_DIM, (g + 1) * POOL_GROUP_DIM)
        xg = u_tile[:, lanes]
        acc = jnp.concatenate([halo[:, lanes], xg], axis=0)
        shift = 1
        while shift < w:
            acc = acc + pltpu.roll(acc, shift, axis=0)
            shift *= 2
        outs.append(acc[POOL_HALO:, :] * inv[g] - xg)
    return outs


def _pool_fwd(u, pool_w, pool_scale):
    S = u.shape[0]
    ts = ROW_TILE
    hb = ts // POOL_HALO

    def body(u_ref, halo_ref, w_ref, s_ref, y_ref):
        i = pl.program_id(0)
        halo = jnp.where(i == 0, 0.0, halo_ref[...])
        pooled = _pooled(u_ref[...], halo, i * ts)
        for g in range(len(POOL_WINDOWS)):
            lanes = slice(g * POOL_GROUP_DIM, (g + 1) * POOL_GROUP_DIM)
            y_ref[:, lanes] = _nn(pooled[g].astype(BF16), w_ref[g]) * s_ref[:, lanes]

    return pl.pallas_call(
        body,
        name="pool_fwd",
        grid=(S // ts,),
        in_specs=[
            pl.BlockSpec((ts, D_POOL), lambda i: (i, 0)),
            pl.BlockSpec((POOL_HALO, D_POOL), lambda i: (jnp.maximum(i * hb - 1, 0), 0)),
            pl.BlockSpec((4, POOL_GROUP_DIM, POOL_GROUP_DIM), lambda i: (0, 0, 0)),
            pl.BlockSpec((1, D_POOL), lambda i: (0, 0)),
        ],
        out_specs=pl.BlockSpec((ts, D_POOL), lambda i: (i, 0)),
        out_shape=jax.ShapeDtypeStruct((S, D_POOL), F32),
        compiler_params=_params(),
    )(u, u, pool_w, pool_scale)


def _head_masks():
    lane = lax.broadcasted_iota(jnp.int32, (1, ATTN_LANES), 1)
    return [jnp.logical_and(lane >= h * HEAD_DIM, lane < (h + 1) * HEAD_DIM) for h in range(HEADS_PER_BLOCK)]


def _tri_masks(t):
    row = lax.broadcasted_iota(jnp.int32, (t, t), 0)
    col = lax.broadcasted_iota(jnp.int32, (t, t), 1)
    return row, col


def _split_bf16(x):
    hi = x.astype(BF16)
    lo = (x - hi.astype(F32)).astype(BF16)
    return hi, lo


def _log_sigmoids(z):
    sp = jnp.log(1.0 + jnp.exp(-jnp.abs(z)))
    ls = jnp.minimum(z, 0.0) - sp
    return ls, ls - z


def _attn_fwd(qkv, shards=()):
    S = qkv.shape[1]
    t = ATTN_TILE
    n = len(shards)
    nblk = D_ATTN // ATTN_LANES

    def body(*refs):
        q_ref, k_ref, v_ref = refs[:3]
        o_ref = refs[3 + n]
        vh_ref, acc_ref, z_ref = refs[4 + 2 * n:7 + 2 * n]
        hp = pl.program_id(0)
        i = pl.program_id(1)
        finish = lambda: None
        if n:
            finish = _run_at(jnp.logical_and(hp == 0, i == 0),
                             jnp.logical_and(hp == nblk - 1, i == (3 * pl.num_programs(1)) // 4),
                             jnp.logical_and(hp == nblk - 1, i == pl.num_programs(1) - 1),
                             _gather_phases(refs[3:3 + n], refs[4 + n:4 + 2 * n], *refs[7 + 2 * n:]))
        masks = _head_masks()

        @pl.when(i == 0)
        def _():
            vv = v_ref[0]
            for h in range(HEADS_PER_BLOCK):
                vh_ref[h] = jnp.where(masks[h], vv, jnp.zeros_like(vv))

        row, col = _tri_masks(t)
        later = (row > col).astype(BF16)
        causal = col < row
        qs = q_ref[0] * ATTN_SCALE
        units = [(h, r) for h in range(HEADS_PER_BLOCK) for r in range(ATTN_ROW_CHUNKS)]
        rc = t // ATTN_ROW_CHUNKS
        qu = [jnp.where(masks[h], qs, jnp.zeros_like(qs))[r * rc:(r + 1) * rc] for h, r in units]
        causal_u = [causal[r * rc:(r + 1) * rc] for _, r in units]

        def scores(j, slot):
            kj = k_ref[0, pl.ds(pl.multiple_of(j * t, t), t), :]
            for u in range(len(units)):
                z_ref[slot, u] = _nt(qu[u], kj)

        def tile(j, left, slot, carry, diag):
            cs = list(carry)
            keys = pl.ds(pl.multiple_of(j * t, t), t)
            ls, tail = [None] * len(units), [None] * len(units)
            for u in range(len(units)):
                ls[u], l1m = _log_sigmoids(z_ref[slot, u])
                if diag:
                    l1m = jnp.where(causal_u[u], l1m, 0.0)
                hi, lo = _split_bf16(l1m)
                tail[u] = _nn(hi, later) + _nn(lo, later) + cs[u]
                cs[u] = cs[u] + jnp.sum(l1m, axis=1, keepdims=True)
            top = cs[0]
            for u in range(1, len(units)):
                top = jnp.maximum(top, cs[u])
            go = jnp.logical_and(left > 0, jnp.max(top) > DEAD_LOG)
            scores(jnp.maximum(j - 1, 0), 1 - slot)
            for u, (h, _) in enumerate(units):
                a = jnp.exp(ls[u] + tail[u])
                if diag:
                    a = jnp.where(causal_u[u], a, 0.0)
                pv = _nn(a.astype(BF16), vh_ref[h, keys, :])
                if diag:
                    acc_ref[u] = pv
                else:
                    acc_ref[u] += pv
            return (go, *cs)

        scores(i, 0)
        state = (jnp.int32(0), *tile(i, i, 0, [jnp.zeros((rc, 1), F32)] * len(units), True))

        def step(state):
            jj = state[0]
            return (jj + 1, *tile(i - 1 - jj, i - 1 - jj, (jj + 1) % 2, state[2:], False))

        lax.while_loop(lambda s: s[1], step, state)
        out = None
        for h in range(HEADS_PER_BLOCK):
            rows = jnp.concatenate([acc_ref[u] for u, (hh, _) in enumerate(units) if hh == h], axis=0)
            out = rows if out is None else out + rows
        o_ref[...] = out
        finish()

    return pl.pallas_call(
        body,
        name="attn_fwd",
        grid=(nblk, S // t),
        in_specs=[
            pl.BlockSpec((1, t, ATTN_LANES), lambda hp, i: (0, i, hp)),
            pl.BlockSpec((1, S, ATTN_LANES), lambda hp, i: (1, 0, hp)),
            pl.BlockSpec((1, S, ATTN_LANES), lambda hp, i: (2, 0, hp)),
        ] + [ANY] * n,
        out_specs=[pl.BlockSpec((t, ATTN_LANES), lambda hp, i: (i, hp))] + [ANY] * n,
        out_shape=[jax.ShapeDtypeStruct((S, D_ATTN), F32)] + _gather_shapes(shards),
        scratch_shapes=[pltpu.VMEM((HEADS_PER_BLOCK, S, ATTN_LANES), BF16),
                        pltpu.VMEM((HEADS_PER_BLOCK * ATTN_ROW_CHUNKS, t // ATTN_ROW_CHUNKS, ATTN_LANES), F32),
                        pltpu.VMEM((2, HEADS_PER_BLOCK * ATTN_ROW_CHUNKS, t // ATTN_ROW_CHUNKS, t), F32)]
        + (_gather_sems(n) if n else []),
        compiler_params=_params(),
    )(qkv, qkv, qkv, *shards)


def _out_proj(y_pool, y_attn, x, g_pool, g_attn, w_out, g2):
    S = x.shape[0]
    ts = ROW_TILE

    def body(yp_ref, ya_ref, x_ref, gp_ref, ga_ref, w_ref, g2_ref, mixed_ref, h1_ref, hn2_ref):
        yp = yp_ref[...]
        ya = ya_ref[...]
        mixed = jnp.concatenate([yp * _rms(yp) * gp_ref[...], ya * _rms(ya) * ga_ref[...]], axis=-1).astype(BF16)
        mixed_ref[...] = mixed
        h1 = x_ref[...] + _nn(mixed, w_ref[...])
        h1_ref[...] = h1
        hn2_ref[...] = (h1 * _rms(h1) * g2_ref[...]).astype(BF16)

    row = lambda w: pl.BlockSpec((ts, w), lambda i: (i, 0))
    vec = lambda w: pl.BlockSpec((1, w), lambda i: (0, 0))
    return pl.pallas_call(
        body,
        name="out_proj",
        grid=(S // ts,),
        in_specs=[row(D_POOL), row(D_ATTN), row(D_MODEL), vec(D_POOL), vec(D_ATTN),
                  pl.BlockSpec((D_MODEL, D_MODEL), lambda i: (0, 0)), vec(D_MODEL)],
        out_specs=[row(D_MODEL), row(D_MODEL), row(D_MODEL)],
        out_shape=[
            jax.ShapeDtypeStruct((S, D_MODEL), BF16),
            jax.ShapeDtypeStruct((S, D_MODEL), F32),
            jax.ShapeDtypeStruct((S, D_MODEL), BF16),
        ],
        compiler_params=_params(),
    )(y_pool, y_attn, x, g_pool, g_attn, w_out, g2)


def _mlp_fwd(hn2, h1, w_up, w_down, g_final, target):
    S = hn2.shape[0]
    ts = MLP_TILE
    nf = D_FF // 1024

    def body(hn2_ref, h1_ref, wu_ref, wd_ref, gf_ref, tg_ref, r_ref, dh2_ref, dh2b_ref, lsq_ref, dgf_ref):
        i = pl.program_id(0)
        hn2v = hn2_ref[...]
        acts = []
        for c in range(nf):
            r = jnp.maximum(_nn(hn2v, wu_ref[c]), 0.0)
            r_ref[:, c * 1024:(c + 1) * 1024] = r.astype(BF16)
            acts.append((r * r).astype(BF16))
        h2 = h1_ref[...] + _nn(jnp.concatenate(acts, axis=1), wd_ref[...])

        @pl.when(i == 0)
        def _():
            lsq_ref[...] = jnp.zeros_like(lsq_ref)
            dgf_ref[...] = jnp.zeros_like(dgf_ref)

        rf = _rms(h2)
        n = h2 * rf
        gf = gf_ref[...]
        e = n * gf - tg_ref[...]
        lsq_ref[...] += jnp.sum(e * e, axis=0, keepdims=True)
        dy = e * (1.0 / D_MODEL)
        dgf_ref[...] += jnp.sum(dy * n, axis=0, keepdims=True)
        dh2 = _rms_bwd(dy, n, rf, gf)
        dh2_ref[...] = dh2
        dh2b_ref[...] = dh2.astype(BF16)

    row = lambda w: pl.BlockSpec((ts, w), lambda i: (i, 0))
    vec = lambda w: pl.BlockSpec((1, w), lambda i: (0, 0))
    once = pl.Buffered(1)
    return pl.pallas_call(
        body,
        name="mlp_fwd",
        grid=(S // ts,),
        in_specs=[row(D_MODEL), row(D_MODEL),
                  pl.BlockSpec((nf, D_MODEL, 1024), lambda i: (0, 0, 0), pipeline_mode=once),
                  pl.BlockSpec((D_FF, D_MODEL), lambda i: (0, 0), pipeline_mode=once),
                  vec(D_MODEL), row(D_MODEL)],
        out_specs=[row(D_FF), row(D_MODEL), row(D_MODEL), vec(D_MODEL), vec(D_MODEL)],
        out_shape=[
            jax.ShapeDtypeStruct((S, D_FF), BF16),
            jax.ShapeDtypeStruct((S, D_MODEL), F32),
            jax.ShapeDtypeStruct((S, D_MODEL), BF16),
            jax.ShapeDtypeStruct((1, D_MODEL), F32),
            jax.ShapeDtypeStruct((1, D_MODEL), F32),
        ],
        compiler_params=_params(),
    )(hn2, h1, w_up, w_down.reshape(D_FF, D_MODEL), g_final, target)


def _mlp_bwd_dx(dh2b, r_act, w_down, w_up, h1, dh2, g2):
    S = h1.shape[0]
    ts = MLP_TILE
    nf = D_FF // 1024

    def body(dh2b_ref, r_ref, wd_ref, wu_ref, h1_ref, dh2_ref, g2_ref, dup_ref, dh1_ref, dg2_ref):
        i = pl.program_id(0)
        dh2b = dh2b_ref[...]
        dhn2 = None
        for c in range(nf):
            chunk = slice(c * 1024, (c + 1) * 1024)
            dup = (_nt(dh2b, wd_ref[c]) * (2.0 * r_ref[:, chunk].astype(F32))).astype(BF16)
            dup_ref[:, chunk] = dup
            part = _nt(dup, wu_ref[c])
            dhn2 = part if dhn2 is None else dhn2 + part

        @pl.when(i == 0)
        def _():
            dg2_ref[...] = jnp.zeros_like(dg2_ref)

        h1v = h1_ref[...]
        r2 = _rms(h1v)
        n2 = h1v * r2
        dg2_ref[...] += jnp.sum(dhn2 * n2, axis=0, keepdims=True)
        dh1_ref[...] = dh2_ref[...] + _rms_bwd(dhn2, n2, r2, g2_ref[...])

    row = lambda w: pl.BlockSpec((ts, w), lambda i: (i, 0))
    vec = lambda w: pl.BlockSpec((1, w), lambda i: (0, 0))
    once = pl.Buffered(1)
    return pl.pallas_call(
        body,
        name="mlp_bwd_dx",
        grid=(S // ts,),
        in_specs=[row(D_MODEL), row(D_FF),
                  pl.BlockSpec((nf, 1024, D_MODEL), lambda i: (0, 0, 0), pipeline_mode=once),
                  pl.BlockSpec((nf, D_MODEL, 1024), lambda i: (0, 0, 0), pipeline_mode=once),
                  row(D_MODEL), row(D_MODEL), vec(D_MODEL)],
        out_specs=[row(D_FF), row(D_MODEL), vec(D_MODEL)],
        out_shape=[
            jax.ShapeDtypeStruct((S, D_FF), BF16),
            jax.ShapeDtypeStruct((S, D_MODEL), F32),
            jax.ShapeDtypeStruct((1, D_MODEL), F32),
        ],
        compiler_params=_params(),
    )(dh2b, r_act, w_down, w_up, h1, dh2, g2)


def _tokens_tn(name, a, b, a_chunked, square_a):
    S = a.shape[0]
    ts = min(DW_TOKEN_TILE, S)
    nf = D_FF // 1024

    def body(a_ref, b_ref, o_ref, acc):
        tt = pl.program_id(1)
        av = a_ref[...]
        if square_a:
            af = av.astype(F32)
            av = (af * af).astype(BF16)
        part = _tn(av, b_ref[...])

        @pl.when(tt == 0)
        def _():
            acc[...] = part

        @pl.when(tt > 0)
        def _():
            acc[...] += part

        @pl.when(tt == pl.num_programs(1) - 1)
        def _():
            o_ref[0] = acc[...].astype(BF16)

    whole = pl.BlockSpec((ts, 1024), lambda c, tt: (tt, 0))
    chunk = pl.BlockSpec((ts, 1024), lambda c, tt: (tt, c))
    return pl.pallas_call(
        body,
        name=name,
        grid=(nf, S // ts),
        in_specs=[chunk, whole] if a_chunked else [whole, chunk],
        out_specs=pl.BlockSpec((1, 1024, 1024), lambda c, tt: (c, 0, 0)),
        out_shape=jax.ShapeDtypeStruct((nf, 1024, 1024), BF16),
        scratch_shapes=[pltpu.VMEM((1024, 1024), F32)],
        compiler_params=_params(),
    )(a, b)


def _mlp_bwd_dw(r_act, dh2b, hn2, dup):
    return (_tokens_tn("mlp_bwd_dw_down", r_act, dh2b, True, True),
            _tokens_tn("mlp_bwd_dw_up", hn2, dup, False, False))


def _out_bwd(dh1, w_out, mixed, y_pool, y_attn, g_pool, g_attn):
    S = dh1.shape[0]
    ts = ROW_TILE

    def body(dh1_ref, w_ref, mixed_ref, yp_ref, ya_ref, gp_ref, ga_ref,
             dyp_ref, dya_ref, dw_ref, dgp_ref, dga_ref, dw_acc):
        i = pl.program_id(0)
        dh1b = dh1_ref[...].astype(BF16)
        dmixed = _nt(dh1b, w_ref[...])
        dw = _tn(mixed_ref[...], dh1b)

        @pl.when(i == 0)
        def _():
            dw_acc[...] = dw
            dgp_ref[...] = jnp.zeros_like(dgp_ref)
            dga_ref[...] = jnp.zeros_like(dga_ref)

        @pl.when(i > 0)
        def _():
            dw_acc[...] += dw

        @pl.when(i == pl.num_programs(0) - 1)
        def _():
            dw_ref[...] = dw_acc[...].astype(BF16)

        for y_ref, g_ref, dy_ref, dg_ref, lanes in (
                (yp_ref, gp_ref, dyp_ref, dgp_ref, slice(0, D_POOL)),
                (ya_ref, ga_ref, dya_ref, dga_ref, slice(D_POOL, D_MODEL))):
            y = y_ref[...]
            r = _rms(y)
            n = y * r
            dm = dmixed[:, lanes]
            dg_ref[...] += jnp.sum(dm * n, axis=0, keepdims=True)
            dy_ref[...] = _rms_bwd(dm, n, r, g_ref[...])

    row = lambda w: pl.BlockSpec((ts, w), lambda i: (i, 0))
    vec = lambda w: pl.BlockSpec((1, w), lambda i: (0, 0))
    full = pl.BlockSpec((D_MODEL, D_MODEL), lambda i: (0, 0))
    return pl.pallas_call(
        body,
        name="out_bwd",
        grid=(S // ts,),
        in_specs=[row(D_MODEL), full, row(D_MODEL), row(D_POOL), row(D_ATTN), vec(D_POOL), vec(D_ATTN)],
        out_specs=[row(D_POOL), row(D_ATTN), full, vec(D_POOL), vec(D_ATTN)],
        out_shape=[
            jax.ShapeDtypeStruct((S, D_POOL), F32),
            jax.ShapeDtypeStruct((S, D_ATTN), F32),
            jax.ShapeDtypeStruct((D_MODEL, D_MODEL), BF16),
            jax.ShapeDtypeStruct((1, D_POOL), F32),
            jax.ShapeDtypeStruct((1, D_ATTN), F32),
        ],
        scratch_shapes=[pltpu.VMEM((D_MODEL, D_MODEL), F32)],
        compiler_params=_params(),
    )(dh1, w_out, mixed, y_pool, y_attn, g_pool, g_attn)


def _attn_bwd(qkv, o, do, partials=()):
    S = qkv.shape[1]
    t = ATTN_TILE
    n = len(partials)
    nblk = D_ATTN // ATTN_LANES

    def body(*refs):
        q_ref, k_ref, v_ref, o_ref, do_ref = refs[:5]
        dq_ref, dk_ref, dv_ref = refs[5 + n:8 + n]
        kh_ref, dk_acc, dv_acc, dq_acc, z_ref, da_ref = refs[8 + 2 * n:14 + 2 * n]
        hp = pl.program_id(0)
        i = pl.program_id(1)
        finish = lambda: None
        if n:
            finish = _run_at(jnp.logical_and(hp == 0, i == 0), None,
                             jnp.logical_and(hp == nblk - 1, i == pl.num_programs(1) - 1),
                             _reduce_phases(refs[5:5 + n], refs[8 + n:8 + 2 * n], *refs[14 + 2 * n:]))
        masks = _head_masks()

        @pl.when(i == 0)
        def _():
            kk = k_ref[0]
            for h in range(HEADS_PER_BLOCK):
                kh_ref[h] = jnp.where(masks[h], kk, jnp.zeros_like(kk))
            dk_acc[...] = jnp.zeros_like(dk_acc)
            dv_acc[...] = jnp.zeros_like(dv_acc)

        row, col = _tri_masks(t)
        later = (row > col).astype(BF16)
        from_s = (row >= col).astype(BF16)
        causal = col < row
        qs = q_ref[0] * ATTN_SCALE
        dob = do_ref[...].astype(BF16)
        d_all = dob.astype(F32) * o_ref[...]
        qh = [jnp.where(masks[h], qs, jnp.zeros_like(qs)) for h in range(HEADS_PER_BLOCK)]
        doh = [jnp.where(masks[h], dob, jnp.zeros_like(dob)) for h in range(HEADS_PER_BLOCK)]
        d_row = [jnp.sum(jnp.where(masks[h], d_all, 0.0), axis=1, keepdims=True) for h in range(HEADS_PER_BLOCK)]

        heads = range(HEADS_PER_BLOCK)

        def scores(j, slot):
            keys = pl.ds(pl.multiple_of(j * t, t), t)
            kj = k_ref[0, keys, :]
            vj = v_ref[0, keys, :]
            for h in heads:
                z_ref[slot, h] = _nt(qh[h], kj)
                da_ref[slot, h] = _nt(doh[h], vj)

        def tile(j, left, slot, carry, diag):
            nh = HEADS_PER_BLOCK
            c_l, c_g = list(carry[:nh]), list(carry[nh:])
            keys = pl.ds(pl.multiple_of(j * t, t), t)
            ls, tail = [None] * nh, [None] * nh
            for h in heads:
                ls[h], l1m = _log_sigmoids(z_ref[slot, h])
                if diag:
                    l1m = jnp.where(causal, l1m, 0.0)
                hi, lo = _split_bf16(l1m)
                tail[h] = _nn(hi, later) + _nn(lo, later) + c_l[h]
                c_l[h] = c_l[h] + jnp.sum(l1m, axis=1, keepdims=True)
            top = c_l[0]
            for h in range(1, nh):
                top = jnp.maximum(top, c_l[h])
            go = jnp.logical_and(left > 0, jnp.max(top) > DEAD_LOG)
            g, before = [None] * nh, [None] * nh
            dv = None
            for h in heads:
                a = jnp.exp(ls[h] + tail[h])
                if diag:
                    a = jnp.where(causal, a, 0.0)
                ab = a.astype(BF16)
                g[h] = ab.astype(F32) * da_ref[slot, h]
                ghi, glo = _split_bf16(g[h])
                before[h] = d_row[h] - (_nn(ghi, from_s) + _nn(glo, from_s) + c_g[h])
                c_g[h] = c_g[h] + jnp.sum(g[h], axis=1, keepdims=True)
                part = _tn(ab, doh[h])
                dv = part if dv is None else dv + part
            dv_acc[keys, :] += dv
            scores(jnp.maximum(j - 1, 0), 1 - slot)
            dk = None
            for h in heads:
                beta = jnp.exp(ls[h])
                dz = g[h] * (1.0 - beta) - before[h] * beta
                if diag:
                    dz = jnp.where(causal, dz, 0.0)
                dzb = dz.astype(BF16)
                dqh = _nn(dzb, kh_ref[h, keys, :])
                if diag:
                    dq_acc[h] = dqh
                else:
                    dq_acc[h] += dqh
                part = _tn(dzb, qh[h])
                dk = part if dk is None else dk + part
            dk_acc[keys, :] += dk
            return (go, *c_l, *c_g)

        scores(i, 0)
        state = (jnp.int32(0), *tile(i, i, 0, [jnp.zeros((t, 1), F32)] * (2 * HEADS_PER_BLOCK), True))

        def step(state):
            jj = state[0]
            return (jj + 1, *tile(i - 1 - jj, i - 1 - jj, (jj + 1) % 2, state[2:], False))

        lax.while_loop(lambda s: s[1], step, state)
        dq = dq_acc[0]
        for h in range(1, HEADS_PER_BLOCK):
            dq = dq + dq_acc[h]
        dq_ref[...] = (dq * ATTN_SCALE).astype(BF16)

        @pl.when(i == pl.num_programs(1) - 1)
        def _():
            dk_ref[...] = dk_acc[...].astype(BF16)
            dv_ref[...] = dv_acc[...].astype(BF16)

        finish()

    qtile = pl.BlockSpec((t, ATTN_LANES), lambda hp, i: (i, hp))
    whole = pl.BlockSpec((S, ATTN_LANES), lambda hp, i: (0, hp))
    return pl.pallas_call(
        body,
        name="attn_bwd",
        grid=(nblk, S // t),
        in_specs=[
            pl.BlockSpec((1, t, ATTN_LANES), lambda hp, i: (0, i, hp)),
            pl.BlockSpec((1, S, ATTN_LANES), lambda hp, i: (1, 0, hp)),
            pl.BlockSpec((1, S, ATTN_LANES), lambda hp, i: (2, 0, hp)),
            qtile, qtile,
        ] + [ANY] * n,
        out_specs=[qtile, whole, whole] + [ANY] * n,
        out_shape=[jax.ShapeDtypeStruct((S, D_ATTN), BF16)] * 3 + _reduce_shapes(partials),
        scratch_shapes=[
            pltpu.VMEM((HEADS_PER_BLOCK, S, ATTN_LANES), BF16),
            pltpu.VMEM((S, ATTN_LANES), F32),
            pltpu.VMEM((S, ATTN_LANES), F32),
            pltpu.VMEM((HEADS_PER_BLOCK, t, ATTN_LANES), F32),
            pltpu.VMEM((2, HEADS_PER_BLOCK, t, t), F32),
            pltpu.VMEM((2, HEADS_PER_BLOCK, t, t), F32),
        ] + (_reduce_sems(n) if n else []),
        compiler_params=_params(),
    )(qkv, qkv, qkv, o, do, *partials)


def _pool_bwd_w(u, dyp, pool_w, pool_scale):
    S = u.shape[0]
    ts = ROW_TILE
    hb = ts // POOL_HALO

    def body(u_ref, halo_ref, dy_ref, w_ref, s_ref, dp_ref, dw_ref, ds_ref):
        i = pl.program_id(0)
        halo = jnp.where(i == 0, 0.0, halo_ref[...])
        pooled = _pooled(u_ref[...], halo, i * ts)

        @pl.when(i == 0)
        def _():
            dw_ref[...] = jnp.zeros_like(dw_ref)
            ds_ref[...] = jnp.zeros_like(ds_ref)

        for g in range(len(POOL_WINDOWS)):
            lanes = slice(g * POOL_GROUP_DIM, (g + 1) * POOL_GROUP_DIM)
            pg = pooled[g].astype(BF16)
            dy = dy_ref[:, lanes]
            ds_ref[:, lanes] += jnp.sum(dy * _nn(pg, w_ref[g]), axis=0, keepdims=True)
            dmapped = (dy * s_ref[:, lanes]).astype(BF16)
            dp_ref[:, lanes] = _nt(dmapped, w_ref[g])
            dw_ref[g] += _tn(pg, dmapped)

    row = pl.BlockSpec((ts, D_POOL), lambda i: (i, 0))
    vec = pl.BlockSpec((1, D_POOL), lambda i: (0, 0))
    wspec = pl.BlockSpec((4, POOL_GROUP_DIM, POOL_GROUP_DIM), lambda i: (0, 0, 0))
    return pl.pallas_call(
        body,
        name="pool_bwd_w",
        grid=(S // ts,),
        in_specs=[row, pl.BlockSpec((POOL_HALO, D_POOL), lambda i: (jnp.maximum(i * hb - 1, 0), 0)),
                  row, wspec, vec],
        out_specs=[row, wspec, vec],
        out_shape=[
            jax.ShapeDtypeStruct((S, D_POOL), F32),
            jax.ShapeDtypeStruct((4, POOL_GROUP_DIM, POOL_GROUP_DIM), F32),
            jax.ShapeDtypeStruct((1, D_POOL), F32),
        ],
        compiler_params=_params(),
    )(u, u, dyp, pool_w, pool_scale)


def _pool_bwd_u(dpooled):
    S = dpooled.shape[0]
    ts = ROW_TILE
    hb = ts // POOL_HALO
    last = S // ts - 1

    def body(dp_ref, halo_ref, du_ref):
        i = pl.program_id(0)
        dp = dp_ref[...]
        halo = jnp.where(i == last, 0.0, halo_ref[...])
        inv = _pool_counts(i * ts, ts)
        n = ts + POOL_HALO
        for g, w in enumerate(POOL_WINDOWS):
            lanes = slice(g * POOL_GROUP_DIM, (g + 1) * POOL_GROUP_DIM)
            dg = dp[:, lanes]
            acc = jnp.concatenate([dg * inv[g], halo[:, lanes] * (1.0 / w)], axis=0)
            shift = 1
            while shift < w:
                acc = acc + pltpu.roll(acc, n - shift, axis=0)
                shift *= 2
            du_ref[:, lanes] = (acc[:ts, :] - dg).astype(BF16)

    return pl.pallas_call(
        body,
        name="pool_bwd_u",
        grid=(S // ts,),
        in_specs=[pl.BlockSpec((ts, D_POOL), lambda i: (i, 0)),
                  pl.BlockSpec((POOL_HALO, D_POOL), lambda i: (jnp.minimum((i + 1) * hb, (last + 1) * hb - 1), 0))],
        out_specs=pl.BlockSpec((ts, D_POOL), lambda i: (i, 0)),
        out_shape=jax.ShapeDtypeStruct((S, D_POOL), BF16),
        compiler_params=_params(),
    )(dpooled, dpooled)


def _in_bwd_dw(hn, dprojs):
    S = hn.shape[0]
    ts = min(DW_TOKEN_TILE, S)

    def body(hn_ref, du_ref, dq_ref, dk_ref, dv_ref, o_ref, acc):
        j = pl.program_id(0)
        tt = pl.program_id(1)
        for k, dp_ref in enumerate((du_ref, dq_ref, dk_ref, dv_ref)):
            @pl.when(j == k)
            def _(dp_ref=dp_ref):
                part = _tn(hn_ref[...], dp_ref[...])

                @pl.when(tt == 0)
                def _():
                    acc[...] = part

                @pl.when(tt > 0)
                def _():
                    acc[...] += part

        @pl.when(tt == pl.num_programs(1) - 1)
        def _():
            o_ref[0] = acc[...].astype(BF16)

    def taken_at(k):
        return lambda j, tt: (jnp.where(j == k, tt, 0), 0)

    return pl.pallas_call(
        body,
        name="in_bwd_dw",
        grid=(4, S // ts),
        in_specs=[pl.BlockSpec((ts, D_MODEL), lambda j, tt: (tt, 0))]
        + [pl.BlockSpec((ts, 512), taken_at(k)) for k in range(4)],
        out_specs=pl.BlockSpec((1, D_MODEL, 512), lambda j, tt: (j, 0, 0)),
        out_shape=jax.ShapeDtypeStruct((4, D_MODEL, 512), BF16),
        scratch_shapes=[pltpu.VMEM((D_MODEL, 512), F32)],
        compiler_params=_params(),
    )(hn, *dprojs)


def _in_bwd_dx(du, dq, dk, dv, w_in, x, dh1, g1, partials=()):
    S = x.shape[0]
    ts = ROW_TILE
    n = len(partials)

    def body(*refs):
        du_ref, dq_ref, dk_ref, dv_ref, w_ref, x_ref, dh1_ref, g_ref = refs[:8]
        dx_ref, dg_ref = refs[8 + n:10 + n]
        i = pl.program_id(0)
        finish = lambda: None
        if n:
            finish = _run_at(i == 0, None, i == pl.num_programs(0) - 1,
                             _reduce_phases(refs[8:8 + n], refs[10 + n:10 + 2 * n], *refs[10 + 2 * n:]))

        @pl.when(i == 0)
        def _():
            dg_ref[...] = jnp.zeros_like(dg_ref)

        dhn = None
        for j, dp_ref in enumerate((du_ref, dq_ref, dk_ref, dv_ref)):
            part = _nt(dp_ref[...], w_ref[j])
            dhn = part if dhn is None else dhn + part
        xv = x_ref[...]
        r1 = _rms(xv)
        n1 = xv * r1
        dg_ref[...] += jnp.sum(dhn * n1, axis=0, keepdims=True)
        dx_ref[...] = dh1_ref[...] + _rms_bwd(dhn, n1, r1, g_ref[...])
        finish()

    row = lambda w: pl.BlockSpec((ts, w), lambda i: (i, 0))
    vec = pl.BlockSpec((1, D_MODEL), lambda i: (0, 0))
    wspec = pl.BlockSpec((4, D_MODEL, 512), lambda i: (0, 0, 0), pipeline_mode=pl.Buffered(1))
    return pl.pallas_call(
        body,
        name="in_bwd_dx",
        grid=(S // ts,),
        in_specs=[row(512), row(512), row(512), row(512), wspec, row(D_MODEL), row(D_MODEL), vec] + [ANY] * n,
        out_specs=[row(D_MODEL), vec] + [ANY] * n,
        out_shape=[
            jax.ShapeDtypeStruct((S, D_MODEL), F32),
            jax.ShapeDtypeStruct((1, D_MODEL), F32),
        ] + _reduce_shapes(partials),
        scratch_shapes=_reduce_sems(n) if n else [],
        compiler_params=_params(),
    )(du, dq, dk, dv, w_in, x, dh1, g1, *partials)


def _pieces(g):
    return g.reshape(N_CHIPS, 2, -1, g.shape[-1])


def _local_step(x, target, w_in, pool_w, small, full=None, shards=None):
    spread = shards is not None
    if spread:
        hn, u, qkv, w_out = _in_proj(x, small["norm1_g"], w_in, shards[:1])
        w_out = w_out.reshape(D_MODEL, D_MODEL)
    else:
        hn, u, qkv = _in_proj(x, small["norm1_g"], w_in)
        w_out, w_up, w_down = full
    y_pool = _pool_fwd(u, pool_w, small["pool_scale"])
    if spread:
        y_attn, w_up, w_down = _attn_fwd(qkv, shards[1:])
        w_up = w_up.reshape(N_CHIPS, D_MODEL, 1024)
        w_down = w_down.reshape(N_CHIPS, 1024, D_MODEL)
    else:
        (y_attn,) = _attn_fwd(qkv)
    mixed, h1, hn2 = _out_proj(y_pool, y_attn, x, small["pool_out_g"], small["attn_out_g"], w_out, small["norm2_g"])
    r_act, dh2, dh2b, lsq, dgf = _mlp_fwd(hn2, h1, w_up, w_down, small["final_g"], target)

    dup, dh1, dg2 = _mlp_bwd_dx(dh2b, r_act, w_down, w_up, h1, dh2, small["norm2_g"])
    dw_down, dw_up = _mlp_bwd_dw(r_act, dh2b, hn2, dup)
    dyp, dya, dw_out, dgp, dga = _out_bwd(dh1, w_out, mixed, y_pool, y_attn, small["pool_out_g"], small["attn_out_g"])
    if spread:
        dq, dk, dv, got_out, got_up, got_down = _attn_bwd(
            qkv, y_attn, dya, [_pieces(dw_out), _pieces(dw_up), _pieces(dw_down)])
    else:
        dq, dk, dv = _attn_bwd(qkv, y_attn, dya)
    dpooled, dpool_w, dpool_scale = _pool_bwd_w(u, dyp, pool_w, small["pool_scale"])
    du = _pool_bwd_u(dpooled)
    dw_in = _in_bwd_dw(hn, (du, dq, dk, dv))
    if spread:
        dx, dg1, got_in = _in_bwd_dx(du, dq, dk, dv, w_in, x, dh1, small["norm1_g"], [_pieces(dw_in)])
    else:
        dx, dg1 = _in_bwd_dx(du, dq, dk, dv, w_in, x, dh1, small["norm1_g"])

    big = {"w_in": dw_in, "w_out": dw_out, "w_up": dw_up, "w_down": dw_down}
    if spread:
        big["received"] = {"w_in": got_in, "w_out": got_out, "w_up": got_up, "w_down": got_down}
    little = {"norm1_g": dg1, "pool_w": dpool_w, "pool_scale": dpool_scale, "pool_out_g": dgp,
              "attn_out_g": dga, "norm2_g": dg2, "final_g": dgf, "loss_sq": lsq}
    return dx, big, little


def _place():
    x, y, c = lax.axis_index("x"), lax.axis_index("y"), lax.axis_index("c")
    other_chips = [(1 - x, y), (x, 1 - y), (1 - x, 1 - y)]
    return x, y, c, other_chips


def _chip_index(chip):
    return 2 * chip[0] + chip[1]


def _gather_shapes(shards):
    return [jax.ShapeDtypeStruct((N_CHIPS,) + s.shape, s.dtype) for s in shards]


def _gather_sems(n):
    return [pltpu.SemaphoreType.DMA((n, 7)), pltpu.SemaphoreType.DMA((n, 7))]


def _gather_phases(ins, outs, send_sems, recv_sems):
    n = len(ins)
    x, y, c, chips = _place()
    me = _chip_index((x, y))
    sibling = (x, y, 1 - c)

    def copy(a, k, chip_idx, half, to, src=None):
        dst = outs[a].at[chip_idx, half]
        return pltpu.make_async_remote_copy(
            src_ref=dst if src is None else src, dst_ref=dst,
            send_sem=send_sems.at[a, k], recv_sem=recv_sems.at[a, k],
            device_id=to, device_id_type=MESH)

    def own_chip(a, to):
        return pltpu.make_async_remote_copy(
            src_ref=ins[a], dst_ref=outs[a].at[me],
            send_sem=send_sems.at[a, 0], recv_sem=recv_sems.at[a, 0],
            device_id=to, device_id_type=MESH)

    def first(a):
        return [own_chip(a, sibling)] + [
            copy(a, 1 + j, me, c, (*chip, c), src=ins[a].at[c]) for j, chip in enumerate(chips)]

    def passed(a, j):
        return copy(a, 4 + j, _chip_index(chips[j]), c, sibling)

    def start():
        for a in range(n):
            for cp in first(a):
                cp.start()

    def forward():
        for a in range(n):
            for j, chip in enumerate(chips):
                copy(a, 1 + j, _chip_index(chip), c, (x, y, c)).wait_recv()
                passed(a, j).start()

    def finish():
        for a in range(n):
            own_chip(a, (x, y, c)).wait_recv()
            for j, chip in enumerate(chips):
                copy(a, 4 + j, _chip_index(chip), 1 - c, (x, y, c)).wait_recv()
        for a in range(n):
            for cp in first(a):
                cp.wait_send()
            for j in range(len(chips)):
                passed(a, j).wait_send()

    return start, forward, finish


def _gather_weights(shards):
    n = len(shards)

    def body(*refs):
        start, forward, finish = _gather_phases(refs[:n], refs[n:2 * n], *refs[2 * n:])
        start()
        forward()
        finish()

    return pl.pallas_call(
        body,
        name="gather_weights",
        in_specs=[ANY] * n,
        out_specs=[ANY] * n,
        out_shape=_gather_shapes(shards),
        scratch_shapes=_gather_sems(n),
    )(*shards)


def _reduce_shapes(partials):
    return [jax.ShapeDtypeStruct((N_DEV,) + p.shape[2:], p.dtype) for p in partials]


def _reduce_sems(n):
    return [pltpu.SemaphoreType.DMA((n, N_DEV)), pltpu.SemaphoreType.DMA((n, N_DEV))]


def _reduce_phases(ins, outs, send_sems, recv_sems):
    n = len(ins)
    x, y, c, _ = _place()
    me = 4 * x + 2 * y + c

    def to_peer(a, k):
        return pltpu.make_async_remote_copy(
            src_ref=ins[a].at[k // 2, k % 2], dst_ref=outs[a].at[me],
            send_sem=send_sems.at[a, k], recv_sem=recv_sems.at[a, me],
            device_id=(k // 4, (k // 2) % 2, k % 2), device_id_type=MESH)

    def from_peer(a, k):
        return pltpu.make_async_remote_copy(
            src_ref=ins[a].at[k // 2, k % 2], dst_ref=outs[a].at[k],
            send_sem=send_sems.at[a, k], recv_sem=recv_sems.at[a, k],
            device_id=(x, y, c), device_id_type=MESH)

    def start():
        for a in range(n):
            for k in range(N_DEV):
                @pl.when(k != me)
                def _(a=a, k=k):
                    to_peer(a, k).start()

    def finish():
        for a in range(n):
            for k in range(N_DEV):
                @pl.when(k != me)
                def _(a=a, k=k):
                    from_peer(a, k).wait_recv()
                    to_peer(a, k).wait_send()

    return start, finish


def _sum_pieces(where, parts, own):
    _, h, cols = parts.shape
    hb = min(h, 256)

    def body(where_ref, *refs):
        me = where_ref[0]
        acc = None
        for k in range(N_DEV):
            piece = jnp.where(k == me, refs[N_DEV][0, 0], refs[k][0]).astype(F32)
            acc = piece if acc is None else acc + piece
        refs[N_DEV + 1][0] = acc

    def sent_by(k):
        return lambda r, w: (jnp.where(w[0] == k, (k + 1) % N_DEV, k), r, 0)

    return pl.pallas_call(
        body,
        name="sum_pieces",
        grid_spec=pltpu.PrefetchScalarGridSpec(
            num_scalar_prefetch=1,
            grid=(h // hb,),
            in_specs=[pl.BlockSpec((1, hb, cols), sent_by(k)) for k in range(N_DEV)]
            + [pl.BlockSpec((1, 1, hb, cols), lambda r, w: (w[1], w[2], r, 0))],
            out_specs=pl.BlockSpec((1, hb, cols), lambda r, w: (w[2], r, 0)),
        ),
        out_shape=jax.ShapeDtypeStruct((2, h, cols), F32),
    )(where, *([parts] * N_DEV), own)


def _join_halves(halves):
    n = len(halves)

    def body(*refs):
        outs = refs[n:2 * n]
        send_sems, recv_sems = refs[2 * n:]
        x, y, c, _ = _place()
        sends = [
            pltpu.make_async_remote_copy(
                src_ref=outs[a].at[c], dst_ref=outs[a].at[c],
                send_sem=send_sems.at[a], recv_sem=recv_sems.at[a],
                device_id=(x, y, 1 - c), device_id_type=MESH)
            for a in range(n)]
        for cp in sends:
            cp.start()
        for a in range(n):
            pltpu.make_async_remote_copy(
                src_ref=outs[a].at[c], dst_ref=outs[a].at[1 - c],
                send_sem=send_sems.at[a], recv_sem=recv_sems.at[a],
                device_id=(x, y, c), device_id_type=MESH).wait_recv()
        for cp in sends:
            cp.wait_send()

    return pl.pallas_call(
        body,
        name="join_halves",
        in_specs=[ANY] * n,
        out_specs=[ANY] * n,
        out_shape=[jax.ShapeDtypeStruct(s.shape, s.dtype) for s in halves],
        input_output_aliases={a: a for a in range(n)},
        scratch_shapes=[pltpu.SemaphoreType.DMA((n,)), pltpu.SemaphoreType.DMA((n,))],
    )(*halves)


def _adamw(w, g, m, v):
    m = ADAM_B1 * m + (1.0 - ADAM_B1) * g
    v = ADAM_B2 * v + (1.0 - ADAM_B2) * jnp.square(g)
    m_hat = m / (1.0 - ADAM_B1 ** ADAM_STEP)
    v_hat = v / (1.0 - ADAM_B2 ** ADAM_STEP)
    delta = -ADAM_LR * (m_hat / (jnp.sqrt(v_hat) + ADAM_EPS) + ADAM_WD * w)
    return delta, m, v


def _adamw_big(w, g, m, v):
    rows, cols = w.shape
    rb = min(rows, 256)

    def body(w_ref, g_ref, m_ref, v_ref, d_ref, mo_ref, vo_ref):
        d_ref[...], mo_ref[...], vo_ref[...] = _adamw(w_ref[...], g_ref[...], m_ref[...], v_ref[...])

    blk = pl.BlockSpec((rb, cols), lambda r: (r, 0))
    return pl.pallas_call(
        body,
        name="adamw_big",
        grid=(rows // rb,),
        in_specs=[blk] * 4,
        out_specs=[blk] * 3,
        out_shape=[jax.ShapeDtypeStruct(w.shape, F32)] * 3,
    )(w, g, m, v)


SMALL_ORDER = ("pool_w", "norm1_g", "pool_scale", "pool_out_g", "attn_out_g", "norm2_g", "final_g")
SUBLANES = 8


def _pack(parts):
    rows = []
    for p in parts:
        p = p.reshape(-1, LANES)
        pad = (-p.shape[0]) % SUBLANES
        if pad:
            p = jnp.pad(p, ((0, pad), (0, 0)))
        rows.append(p)
    return jnp.concatenate(rows, axis=0)


def _unpack(slab, shapes):
    out, r = [], 0
    for shp in shapes:
        size = 1
        for d in shp:
            size *= d
        nrow = size // LANES
        out.append(slab[r:r + nrow].reshape(shp))
        r += nrow + (-nrow) % SUBLANES
    return out


def _small_step(partials, w, m, v, loss_rows):
    rows = partials.shape[0]

    def body(p_ref, w_ref, m_ref, v_ref, g_ref, d_ref, mo_ref, vo_ref, loss_ref, buf, send_sems, recv_sems):
        x, y, c, _ = _place()
        me = 4 * x + 2 * y + c
        for k in range(N_DEV):
            @pl.when(k != me)
            def _(k=k):
                pltpu.make_async_remote_copy(
                    src_ref=p_ref, dst_ref=buf.at[me],
                    send_sem=send_sems.at[k], recv_sem=recv_sems.at[me],
                    device_id=(k // 4, (k // 2) % 2, k % 2), device_id_type=MESH).start()
        buf[me] = p_ref[...]
        for k in range(N_DEV):
            @pl.when(k != me)
            def _(k=k):
                pltpu.make_async_remote_copy(
                    src_ref=p_ref, dst_ref=buf.at[k],
                    send_sem=send_sems.at[k], recv_sem=recv_sems.at[k],
                    device_id=(x, y, c), device_id_type=MESH).wait()
        g = buf[0]
        for k in range(1, N_DEV):
            g = g + buf[k]
        g_ref[...] = g
        d_ref[...], mo_ref[...], vo_ref[...] = _adamw(w_ref[...], g, m_ref[...], v_ref[...])
        loss = (0.5 / D_MODEL) * jnp.sum(g[rows - loss_rows:, :])
        loss_ref[...] = jnp.full(loss_ref.shape, loss, F32)

    vm = pl.BlockSpec(memory_space=pltpu.VMEM)
    slab = jax.ShapeDtypeStruct((rows, LANES), F32)
    return pl.pallas_call(
        body,
        name="small_step",
        in_specs=[vm] * 4,
        out_specs=[vm] * 5,
        out_shape=[slab, slab, slab, slab, jax.ShapeDtypeStruct((SUBLANES, LANES), F32)],
        scratch_shapes=[pltpu.VMEM((N_DEV, rows, LANES), F32),
                        pltpu.SemaphoreType.DMA((N_DEV,)), pltpu.SemaphoreType.DMA((N_DEV,))],
    )(partials, w, m, v)


BIG_ORDER = ("w_in", "w_out", "w_up", "w_down")
WEIGHT_ORDER = ("norm1_g", "w_in", "pool_w", "pool_scale", "pool_out_g", "attn_out_g", "w_out", "norm2_g",
                "w_up", "w_down", "final_g")


def _halves(a):
    return a.reshape(2, a.shape[0] // 2, a.shape[1])


def kernel(x, norm1_g, w_in, pool_w, pool_scale, pool_out_g, attn_out_g, w_out, norm2_g, w_up, w_down, final_g, loss_target, m_norm1_g, m_w_in, m_pool_w, m_pool_scale, m_pool_out_g, m_attn_out_g, m_w_out, m_norm2_g, m_w_up, m_w_down, m_final_g, v_norm1_g, v_w_in, v_pool_w, v_pool_scale, v_pool_out_g, v_attn_out_g, v_w_out, v_norm2_g, v_w_up, v_w_down, v_final_g):
    w = dict(norm1_g=norm1_g, w_in=w_in, pool_w=pool_w, pool_scale=pool_scale, pool_out_g=pool_out_g,
             attn_out_g=attn_out_g, w_out=w_out, norm2_g=norm2_g, w_up=w_up, w_down=w_down, final_g=final_g)
    m = dict(norm1_g=m_norm1_g, w_in=m_w_in, pool_w=m_pool_w, pool_scale=m_pool_scale, pool_out_g=m_pool_out_g,
             attn_out_g=m_attn_out_g, w_out=m_w_out, norm2_g=m_norm2_g, w_up=m_w_up, w_down=m_w_down,
             final_g=m_final_g)
    v = dict(norm1_g=v_norm1_g, w_in=v_w_in, pool_w=v_pool_w, pool_scale=v_pool_scale, pool_out_g=v_pool_out_g,
             attn_out_g=v_attn_out_g, w_out=v_w_out, norm2_g=v_norm2_g, w_up=v_w_up, w_down=v_w_down,
             final_g=v_final_g)

    shards = {n: _halves(w[n].astype(BF16)) for n in BIG_ORDER}
    (w_in_g,) = _gather_weights([shards["w_in"]])
    small = {n: w[n].reshape(1, -1) for n in ("norm1_g", "pool_scale", "pool_out_g", "attn_out_g", "norm2_g", "final_g")}
    dx, big, little = _local_step(
        x[0], loss_target[0], w_in_g.reshape(N_CHIPS, D_MODEL, 512), pool_w.astype(BF16), small,
        shards=[shards["w_out"], shards["w_up"], shards["w_down"]])

    grads, deltas, new_m, new_v = {}, {}, {}, {}
    loss_rows = D_MODEL // LANES
    slab_g = _pack([little[n] for n in SMALL_ORDER] + [little["loss_sq"]])
    zeros = jnp.zeros((loss_rows, LANES), F32)
    slab_w = _pack([w[n] for n in SMALL_ORDER] + [zeros])
    slab_m = _pack([m[n] for n in SMALL_ORDER] + [zeros])
    slab_v = _pack([v[n] for n in SMALL_ORDER] + [zeros])
    received = big.pop("received")
    g_s, d_s, m_s, v_s, loss = _small_step(slab_g, slab_w, slab_m, slab_v, loss_rows)
    shapes = [w[n].shape for n in SMALL_ORDER]
    for slab, dst in ((g_s, grads), (d_s, deltas), (m_s, new_m), (v_s, new_v)):
        for n, val in zip(SMALL_ORDER, _unpack(slab, shapes)):
            dst[n] = val

    xi, yi, ci = lax.axis_index("x"), lax.axis_index("y"), lax.axis_index("c")
    where = jnp.stack([4 * xi + 2 * yi + ci, 2 * xi + yi, ci]).astype(jnp.int32)
    full = _join_halves([_sum_pieces(where, received[n], _pieces(big[n])) for n in BIG_ORDER])
    for n, g in zip(BIG_ORDER, full):
        grads[n] = g.reshape(w[n].shape)
        deltas[n], new_m[n], new_v[n] = _adamw_big(w[n], grads[n], m[n], v[n])

    return (loss[0, 0], dx[None], *[grads[n] for n in WEIGHT_ORDER], *[deltas[n] for n in WEIGHT_ORDER],
            *[new_m[n] for n in WEIGHT_ORDER], *[new_v[n] for n in WEIGHT_ORDER])
```

```python
import functools

import jax
import jax.numpy as jnp
from jax import lax
from jax.experimental import pallas as pl
from jax.experimental.pallas import tpu as pltpu

F32 = jnp.float32
BF16 = jnp.bfloat16

D_MODEL = 1024
D_POOL = 512
D_ATTN = 512
POOL_WINDOWS = (2, 4, 8, 16)
POOL_GROUP_DIM = 128
POOL_HALO = 16
HEAD_DIM = 64
HEADS_PER_BLOCK = 4
ATTN_LANES = HEADS_PER_BLOCK * HEAD_DIM
D_FF = 4096
N_CHIPS = 4
N_DEV = 8
EPS = 1e-6
ATTN_SCALE = 0.125
ATTN_TILE = 256
ATTN_ROW_CHUNKS = 1
DEAD_LOG = -105.0
ROW_TILE = 512
MLP_TILE = 256
DW_TOKEN_TILE = 2048
LANES = 128

ADAM_LR = 0.001
ADAM_B1 = 0.9
ADAM_B2 = 0.999
ADAM_EPS = 1e-08
ADAM_WD = 0.01
ADAM_STEP = 10

MESH = pl.DeviceIdType.MESH
ANY = pl.BlockSpec(memory_space=pl.ANY)
VMEM_LIMIT = 56 * 1024 * 1024


def _nn(a, b):
    return jnp.dot(a, b, preferred_element_type=F32)


def _nt(a, b):
    return lax.dot_general(a, b, (((1,), (1,)), ((), ())), preferred_element_type=F32)


def _tn(a, b):
    return lax.dot_general(a, b, (((0,), (0,)), ((), ())), preferred_element_type=F32)


def _rms(x):
    return lax.rsqrt(jnp.mean(x * x, axis=-1, keepdims=True) + EPS)


def _rms_bwd(dy, n, r, g):
    dn = dy * g
    return r * (dn - n * jnp.mean(dn * n, axis=-1, keepdims=True))


def _params(**kw):
    return pltpu.CompilerParams(vmem_limit_bytes=VMEM_LIMIT, **kw)


def _run_at(first, middle, last, phases):
    pl.when(first)(phases[0])
    if len(phases) == 3:
        pl.when(middle)(phases[1])
    return lambda: pl.when(last)(phases[-1])


def _in_proj(x, g1, w_in, shards=()):
    S = x.shape[0]
    ts = ROW_TILE
    n = len(shards)

    def body(*refs):
        x_ref, g_ref, w_ref = refs[:3]
        hn_ref, u_ref, qkv_ref = refs[3 + n:6 + n]
        i = pl.program_id(0)
        finish = lambda: None
        if n:
            steps = pl.num_programs(0)
            finish = _run_at(i == 0, i == steps // 2, i == steps - 1,
                             _gather_phases(refs[3:3 + n], refs[6 + n:6 + 2 * n], *refs[6 + 2 * n:]))

        xf = x_ref[...]
        hn = (xf * _rms(xf) * g_ref[...]).astype(BF16)
        hn_ref[...] = hn
        u_ref[...] = _nn(hn, w_ref[0])
        for j in range(1, 4):
            qkv_ref[j - 1] = _nn(hn, w_ref[j]).astype(BF16)

        finish()

    return pl.pallas_call(
        body,
        name="in_proj",
        grid=(S // ts,),
        in_specs=[
            pl.BlockSpec((ts, D_MODEL), lambda i: (i, 0)),
            pl.BlockSpec((1, D_MODEL), lambda i: (0, 0)),
            pl.BlockSpec((4, D_MODEL, 512), lambda i: (0, 0, 0), pipeline_mode=pl.Buffered(1)),
        ] + [ANY] * n,
        out_specs=[
            pl.BlockSpec((ts, D_MODEL), lambda i: (i, 0)),
            pl.BlockSpec((ts, D_POOL), lambda i: (i, 0)),
            pl.BlockSpec((3, ts, 512), lambda i: (0, i, 0)),
        ] + [ANY] * n,
        out_shape=[
            jax.ShapeDtypeStruct((S, D_MODEL), BF16),
            jax.ShapeDtypeStruct((S, D_POOL), F32),
            jax.ShapeDtypeStruct((3, S, 512), BF16),
        ] + _gather_shapes(shards),
        scratch_shapes=_gather_sems(n) if n else [],
        compiler_params=_params(),
    )(x, g1, w_in, *shards)


def _pool_counts(first_row, rows):
    t = first_row + lax.broadcasted_iota(jnp.int32, (rows, 1), 0)
    return [1.0 / jnp.minimum(t + 1, w).astype(F32) for w in POOL_WINDOWS]


def _pooled(u_tile, halo, first_row):
    ts = u_tile.shape[0]
    inv = _pool_counts(first_row, ts)
    outs = []
    for g, w in enumerate(POOL_WINDOWS):
        lanes = slice(g * POOL_GROUP_DIM, (g + 1) * POOL_GROUP_DIM)
        xg = u_tile[:, lanes]
        acc = jnp.concatenate([halo[:, lanes], xg], axis=0)
        shift = 1
        while shift < w:
            acc = acc + pltpu.roll(acc, shift, axis=0)
            shift *= 2
        outs.append(acc[POOL_HALO:, :] * inv[g] - xg)
    return outs


def _pool_fwd(u, pool_w, pool_scale):
    S = u.shape[0]
    ts = ROW_TILE
    hb = ts // POOL_HALO

    def body(u_ref, halo_ref, w_ref, s_ref, y_ref):
        i = pl.program_id(0)
        halo = jnp.where(i == 0, 0.0, halo_ref[...])
        pooled = _pooled(u_ref[...], halo, i * ts)
        for g in range(len(POOL_WINDOWS)):
            lanes = slice(g * POOL_GROUP_DIM, (g + 1) * POOL_GROUP_DIM)
            y_ref[:, lanes] = _nn(pooled[g].astype(BF16), w_ref[g]) * s_ref[:, lanes]

    return pl.pallas_call(
        body,
        name="pool_fwd",
        grid=(S // ts,),
        in_specs=[
            pl.BlockSpec((ts, D_POOL), lambda i: (i, 0)),
            pl.BlockSpec((POOL_HALO, D_POOL), lambda i: (jnp.maximum(i * hb - 1, 0), 0)),
            pl.BlockSpec((4, POOL_GROUP_DIM, POOL_GROUP_DIM), lambda i: (0, 0, 0)),
            pl.BlockSpec((1, D_POOL), lambda i: (0, 0)),
        ],
        out_specs=pl.BlockSpec((ts, D_POOL), lambda i: (i, 0)),
        out_shape=jax.ShapeDtypeStruct((S, D_POOL), F32),
        compiler_params=_params(),
    )(u, u, pool_w, pool_scale)


def _head_masks():
    lane = lax.broadcasted_iota(jnp.int32, (1, ATTN_LANES), 1)
    return [jnp.logical_and(lane >= h * HEAD_DIM, lane < (h + 1) * HEAD_DIM) for h in range(HEADS_PER_BLOCK)]


def _tri_masks(t):
    row = lax.broadcasted_iota(jnp.int32, (t, t), 0)
    col = lax.broadcasted_iota(jnp.int32, (t, t), 1)
    return row, col


def _split_bf16(x):
    hi = x.astype(BF16)
    lo = (x - hi.astype(F32)).astype(BF16)
    return hi, lo


def _log_sigmoids(z):
    sp = jnp.log(1.0 + jnp.exp(-jnp.abs(z)))
    ls = jnp.minimum(z, 0.0) - sp
    return ls, ls - z


def _attn_fwd(qkv, shards=()):
    S = qkv.shape[1]
    t = ATTN_TILE
    n = len(shards)
    nblk = D_ATTN // ATTN_LANES

    def body(*refs):
        q_ref, k_ref, v_ref = refs[:3]
        o_ref = refs[3 + n]
        vh_ref, acc_ref, z_ref = refs[4 + 2 * n:7 + 2 * n]
        hp = pl.program_id(0)
        i = pl.program_id(1)
        finish = lambda: None
        if n:
            finish = _run_at(jnp.logical_and(hp == 0, i == 0),
                             jnp.logical_and(hp == nblk - 1, i == (3 * pl.num_programs(1)) // 4),
                             jnp.logical_and(hp == nblk - 1, i == pl.num_programs(1) - 1),
                             _gather_phases(refs[3:3 + n], refs[4 + n:4 + 2 * n], *refs[7 + 2 * n:]))
        masks = _head_masks()

        @pl.when(i == 0)
        def _():
            vv = v_ref[0]
            for h in range(HEADS_PER_BLOCK):
                vh_ref[h] = jnp.where(masks[h], vv, jnp.zeros_like(vv))

        row, col = _tri_masks(t)
        later = (row > col).astype(BF16)
        causal = col < row
        qs = q_ref[0] * ATTN_SCALE
        heads = range(HEADS_PER_BLOCK)
        qh = [jnp.where(masks[h], qs, jnp.zeros_like(qs)) for h in heads]

        def scores(j, slot):
            kj = k_ref[0, pl.ds(pl.multiple_of(j * t, t), t), :]
            for h in heads:
                z_ref[slot, h] = _nt(qh[h], kj)

        def tiles(walk, left, after, carry):
            cs = list(carry)
            ls, tail = {}, {}
            for w, (j, slot, diag, counts) in enumerate(walk):
                for h in heads:
                    ls[w, h], l1m = _log_sigmoids(z_ref[slot, h])
                    if diag:
                        l1m = jnp.where(causal, l1m, 0.0)
                    hi, lo = _split_bf16(l1m)
                    tail[w, h] = _nn(hi, later) + _nn(lo, later) + cs[h]
                    cs[h] = cs[h] + jnp.sum(l1m, axis=1, keepdims=True)
            top = cs[0]
            for h in heads[1:]:
                top = jnp.maximum(top, cs[h])
            go = jnp.logical_and(left > 0, jnp.max(top) > DEAD_LOG)
            scores(*after)
            for w, (j, slot, diag, counts) in enumerate(walk):
                keys = pl.ds(pl.multiple_of(j * t, t), t)
                for h in heads:
                    a = jnp.exp(ls[w, h] + tail[w, h])
                    if diag:
                        a = jnp.where(causal, a, 0.0)
                    if counts is not None:
                        a = jnp.where(counts, a, 0.0)
                    pv = _nn(a.astype(BF16), vh_ref[h, keys, :])
                    if diag:
                        acc_ref[h] = pv
                    else:
                        acc_ref[h] += pv
            return (go, *cs)

        before = jnp.maximum(i - 1, 0)
        scores(i, 0)
        scores(before, 1)
        state = (jnp.int32(0), *tiles([(i, 0, True, None), (before, 1, False, i >= 1)], i - 1,
                                      (jnp.maximum(i - 2, 0), 0), [jnp.zeros((t, 1), F32)] * HEADS_PER_BLOCK))

        def step(state):
            jj = state[0]
            j = i - 2 - jj
            return (jj + 1, *tiles([(j, jj % 2, False, None)], j, (jnp.maximum(j - 1, 0), 1 - jj % 2), state[2:]))

        lax.while_loop(lambda s: s[1], step, state)
        out = acc_ref[0]
        for h in heads[1:]:
            out = out + acc_ref[h]
        o_ref[...] = out
        finish()

    return pl.pallas_call(
        body,
        name="attn_fwd",
        grid=(nblk, S // t),
        in_specs=[
            pl.BlockSpec((1, t, ATTN_LANES), lambda hp, i: (0, i, hp)),
            pl.BlockSpec((1, S, ATTN_LANES), lambda hp, i: (1, 0, hp)),
            pl.BlockSpec((1, S, ATTN_LANES), lambda hp, i: (2, 0, hp)),
        ] + [ANY] * n,
        out_specs=[pl.BlockSpec((t, ATTN_LANES), lambda hp, i: (i, hp))] + [ANY] * n,
        out_shape=[jax.ShapeDtypeStruct((S, D_ATTN), F32)] + _gather_shapes(shards),
        scratch_shapes=[pltpu.VMEM((HEADS_PER_BLOCK, S, ATTN_LANES), BF16),
                        pltpu.VMEM((HEADS_PER_BLOCK * ATTN_ROW_CHUNKS, t // ATTN_ROW_CHUNKS, ATTN_LANES), F32),
                        pltpu.VMEM((2, HEADS_PER_BLOCK * ATTN_ROW_CHUNKS, t // ATTN_ROW_CHUNKS, t), F32)]
        + (_gather_sems(n) if n else []),
        compiler_params=_params(),
    )(qkv, qkv, qkv, *shards)


def _out_proj(y_pool, y_attn, x, g_pool, g_attn, w_out, g2):
    S = x.shape[0]
    ts = ROW_TILE

    def body(yp_ref, ya_ref, x_ref, gp_ref, ga_ref, w_ref, g2_ref, mixed_ref, h1_ref, hn2_ref):
        yp = yp_ref[...]
        ya = ya_ref[...]
        mixed = jnp.concatenate([yp * _rms(yp) * gp_ref[...], ya * _rms(ya) * ga_ref[...]], axis=-1).astype(BF16)
        mixed_ref[...] = mixed
        h1 = x_ref[...] + _nn(mixed, w_ref[...])
        h1_ref[...] = h1
        hn2_ref[...] = (h1 * _rms(h1) * g2_ref[...]).astype(BF16)

    row = lambda w: pl.BlockSpec((ts, w), lambda i: (i, 0))
    vec = lambda w: pl.BlockSpec((1, w), lambda i: (0, 0))
    return pl.pallas_call(
        body,
        name="out_proj",
        grid=(S // ts,),
        in_specs=[row(D_POOL), row(D_ATTN), row(D_MODEL), vec(D_POOL), vec(D_ATTN),
                  pl.BlockSpec((D_MODEL, D_MODEL), lambda i: (0, 0)), vec(D_MODEL)],
        out_specs=[row(D_MODEL), row(D_MODEL), row(D_MODEL)],
        out_shape=[
            jax.ShapeDtypeStruct((S, D_MODEL), BF16),
            jax.ShapeDtypeStruct((S, D_MODEL), F32),
            jax.ShapeDtypeStruct((S, D_MODEL), BF16),
        ],
        compiler_params=_params(),
    )(y_pool, y_attn, x, g_pool, g_attn, w_out, g2)


def _mlp_fwd(hn2, h1, w_up, w_down, g_final, target):
    S = hn2.shape[0]
    ts = MLP_TILE
    nf = D_FF // 1024

    def body(hn2_ref, h1_ref, wu_ref, wd_ref, gf_ref, tg_ref, r_ref, dh2_ref, dh2b_ref, lsq_ref, dgf_ref):
        i = pl.program_id(0)
        hn2v = hn2_ref[...]
        acts = []
        for c in range(nf):
            r = jnp.maximum(_nn(hn2v, wu_ref[c]), 0.0)
            r_ref[:, c * 1024:(c + 1) * 1024] = r.astype(BF16)
            acts.append((r * r).astype(BF16))
        h2 = h1_ref[...] + _nn(jnp.concatenate(acts, axis=1), wd_ref[...])

        @pl.when(i == 0)
        def _():
            lsq_ref[...] = jnp.zeros_like(lsq_ref)
            dgf_ref[...] = jnp.zeros_like(dgf_ref)

        rf = _rms(h2)
        n = h2 * rf
        gf = gf_ref[...]
        e = n * gf - tg_ref[...]
        lsq_ref[...] += jnp.sum(e * e, axis=0, keepdims=True)
        dy = e * (1.0 / D_MODEL)
        dgf_ref[...] += jnp.sum(dy * n, axis=0, keepdims=True)
        dh2 = _rms_bwd(dy, n, rf, gf)
        dh2_ref[...] = dh2
        dh2b_ref[...] = dh2.astype(BF16)

    row = lambda w: pl.BlockSpec((ts, w), lambda i: (i, 0))
    vec = lambda w: pl.BlockSpec((1, w), lambda i: (0, 0))
    once = pl.Buffered(1)
    return pl.pallas_call(
        body,
        name="mlp_fwd",
        grid=(S // ts,),
        in_specs=[row(D_MODEL), row(D_MODEL),
                  pl.BlockSpec((nf, D_MODEL, 1024), lambda i: (0, 0, 0), pipeline_mode=once),
                  pl.BlockSpec((D_FF, D_MODEL), lambda i: (0, 0), pipeline_mode=once),
                  vec(D_MODEL), row(D_MODEL)],
        out_specs=[row(D_FF), row(D_MODEL), row(D_MODEL), vec(D_MODEL), vec(D_MODEL)],
        out_shape=[
            jax.ShapeDtypeStruct((S, D_FF), BF16),
            jax.ShapeDtypeStruct((S, D_MODEL), F32),
            jax.ShapeDtypeStruct((S, D_MODEL), BF16),
            jax.ShapeDtypeStruct((1, D_MODEL), F32),
            jax.ShapeDtypeStruct((1, D_MODEL), F32),
        ],
        compiler_params=_params(),
    )(hn2, h1, w_up, w_down.reshape(D_FF, D_MODEL), g_final, target)


def _mlp_bwd_dx(dh2b, r_act, w_down, w_up, h1, dh2, g2):
    S = h1.shape[0]
    ts = MLP_TILE
    nf = D_FF // 1024

    def body(dh2b_ref, r_ref, wd_ref, wu_ref, h1_ref, dh2_ref, g2_ref, dup_ref, dh1_ref, dg2_ref):
        i = pl.program_id(0)
        dh2b = dh2b_ref[...]
        dhn2 = None
        for c in range(nf):
            chunk = slice(c * 1024, (c + 1) * 1024)
            dup = (_nt(dh2b, wd_ref[c]) * (2.0 * r_ref[:, chunk].astype(F32))).astype(BF16)
            dup_ref[:, chunk] = dup
            part = _nt(dup, wu_ref[c])
            dhn2 = part if dhn2 is None else dhn2 + part

        @pl.when(i == 0)
        def _():
            dg2_ref[...] = jnp.zeros_like(dg2_ref)

        h1v = h1_ref[...]
        r2 = _rms(h1v)
        n2 = h1v * r2
        dg2_ref[...] += jnp.sum(dhn2 * n2, axis=0, keepdims=True)
        dh1_ref[...] = dh2_ref[...] + _rms_bwd(dhn2, n2, r2, g2_ref[...])

    row = lambda w: pl.BlockSpec((ts, w), lambda i: (i, 0))
    vec = lambda w: pl.BlockSpec((1, w), lambda i: (0, 0))
    once = pl.Buffered(1)
    return pl.pallas_call(
        body,
        name="mlp_bwd_dx",
        grid=(S // ts,),
        in_specs=[row(D_MODEL), row(D_FF),
                  pl.BlockSpec((nf, 1024, D_MODEL), lambda i: (0, 0, 0), pipeline_mode=once),
                  pl.BlockSpec((nf, D_MODEL, 1024), lambda i: (0, 0, 0), pipeline_mode=once),
                  row(D_MODEL), row(D_MODEL), vec(D_MODEL)],
        out_specs=[row(D_FF), row(D_MODEL), vec(D_MODEL)],
        out_shape=[
            jax.ShapeDtypeStruct((S, D_FF), BF16),
            jax.ShapeDtypeStruct((S, D_MODEL), F32),
            jax.ShapeDtypeStruct((1, D_MODEL), F32),
        ],
        compiler_params=_params(),
    )(dh2b, r_act, w_down, w_up, h1, dh2, g2)


def _tokens_tn(name, a, b, a_chunked, square_a, partials=()):
    S = a.shape[0]
    ts = min(DW_TOKEN_TILE, S)
    nf = D_FF // 1024
    n = len(partials)

    def body(*refs):
        a_ref, b_ref = refs[:2]
        o_ref = refs[2 + n]
        acc = refs[3 + 2 * n]
        tt = pl.program_id(1)
        finish = lambda: None
        if n:
            blk = pl.program_id(0)
            finish = _run_at(jnp.logical_and(blk == 0, tt == 0), None,
                             jnp.logical_and(blk == nf - 1, tt == pl.num_programs(1) - 1),
                             _reduce_phases(refs[2:2 + n], refs[3 + n:3 + 2 * n], *refs[4 + 2 * n:], same_core=True))
        av = a_ref[...]
        if square_a:
            af = av.astype(F32)
            av = (af * af).astype(BF16)
        part = _tn(av, b_ref[...])

        @pl.when(tt == 0)
        def _():
            acc[...] = part

        @pl.when(tt > 0)
        def _():
            acc[...] += part

        @pl.when(tt == pl.num_programs(1) - 1)
        def _():
            o_ref[0] = acc[...].astype(BF16)

        finish()

    whole = pl.BlockSpec((ts, 1024), lambda c, tt: (tt, 0))
    chunk = pl.BlockSpec((ts, 1024), lambda c, tt: (tt, c))
    return pl.pallas_call(
        body,
        name=name,
        grid=(nf, S // ts),
        in_specs=([chunk, whole] if a_chunked else [whole, chunk]) + [ANY] * n,
        out_specs=[pl.BlockSpec((1, 1024, 1024), lambda c, tt: (c, 0, 0))] + [ANY] * n,
        out_shape=[jax.ShapeDtypeStruct((nf, 1024, 1024), BF16)] + _reduce_shapes(partials),
        scratch_shapes=[pltpu.VMEM((1024, 1024), F32)] + (_reduce_sems(n) if n else []),
        compiler_params=_params(),
    )(a, b, *partials)


def _out_bwd(dh1, w_out, mixed, y_pool, y_attn, g_pool, g_attn, partials=(), received=()):
    S = dh1.shape[0]
    ts = ROW_TILE
    n = len(partials)

    def body(*refs):
        dh1_ref, w_ref, mixed_ref, yp_ref, ya_ref, gp_ref, ga_ref = refs[:7]
        dyp_ref, dya_ref, dw_ref, dgp_ref, dga_ref = refs[7 + 2 * n:12 + 2 * n]
        dw_acc = refs[12 + 3 * n]
        i = pl.program_id(0)
        finish = lambda: None
        if n:
            finish = _run_at(i == 0, None, i == pl.num_programs(0) - 1,
                             _reduce_phases(refs[7:7 + n], refs[12 + 2 * n:12 + 3 * n], *refs[13 + 3 * n:],
                                            same_core=False))
        dh1b = dh1_ref[...].astype(BF16)
        dmixed = _nt(dh1b, w_ref[...])
        dw = _tn(mixed_ref[...], dh1b)

        @pl.when(i == 0)
        def _():
            dw_acc[...] = dw
            dgp_ref[...] = jnp.zeros_like(dgp_ref)
            dga_ref[...] = jnp.zeros_like(dga_ref)

        @pl.when(i > 0)
        def _():
            dw_acc[...] += dw

        @pl.when(i == pl.num_programs(0) - 1)
        def _():
            dw_ref[...] = dw_acc[...].astype(BF16)

        for y_ref, g_ref, dy_ref, dg_ref, lanes in (
                (yp_ref, gp_ref, dyp_ref, dgp_ref, slice(0, D_POOL)),
                (ya_ref, ga_ref, dya_ref, dga_ref, slice(D_POOL, D_MODEL))):
            y = y_ref[...]
            r = _rms(y)
            nrm = y * r
            dm = dmixed[:, lanes]
            dg_ref[...] += jnp.sum(dm * nrm, axis=0, keepdims=True)
            dy_ref[...] = _rms_bwd(dm, nrm, r, g_ref[...])

        finish()

    row = lambda w: pl.BlockSpec((ts, w), lambda i: (i, 0))
    vec = lambda w: pl.BlockSpec((1, w), lambda i: (0, 0))
    full = pl.BlockSpec((D_MODEL, D_MODEL), lambda i: (0, 0))
    return pl.pallas_call(
        body,
        name="out_bwd",
        grid=(S // ts,),
        in_specs=[row(D_MODEL), full, row(D_MODEL), row(D_POOL), row(D_ATTN), vec(D_POOL), vec(D_ATTN)]
        + [ANY] * (2 * n),
        out_specs=[row(D_POOL), row(D_ATTN), full, vec(D_POOL), vec(D_ATTN)] + [ANY] * n,
        out_shape=[
            jax.ShapeDtypeStruct((S, D_POOL), F32),
            jax.ShapeDtypeStruct((S, D_ATTN), F32),
            jax.ShapeDtypeStruct((D_MODEL, D_MODEL), BF16),
            jax.ShapeDtypeStruct((1, D_POOL), F32),
            jax.ShapeDtypeStruct((1, D_ATTN), F32),
        ] + [jax.ShapeDtypeStruct(r.shape, r.dtype) for r in received],
        input_output_aliases={7 + n + a: 5 + a for a in range(n)},
        scratch_shapes=[pltpu.VMEM((D_MODEL, D_MODEL), F32)] + (_reduce_sems(n) if n else []),
        compiler_params=_params(),
    )(dh1, w_out, mixed, y_pool, y_attn, g_pool, g_attn, *partials, *received)


def _attn_bwd(qkv, o, do, partials=()):
    S = qkv.shape[1]
    t = ATTN_TILE
    n = len(partials)
    nblk = D_ATTN // ATTN_LANES

    def body(*refs):
        q_ref, k_ref, v_ref, o_ref, do_ref = refs[:5]
        dq_ref, dk_ref, dv_ref = refs[5 + n:8 + n]
        kh_ref, dk_acc, dv_acc, dq_acc, z_ref, da_ref = refs[8 + 2 * n:14 + 2 * n]
        hp = pl.program_id(0)
        i = pl.program_id(1)
        finish = lambda: None
        if n:
            finish = _run_at(jnp.logical_and(hp == 0, i == 0), None,
                             jnp.logical_and(hp == nblk - 1, i == pl.num_programs(1) - 1),
                             _reduce_phases(refs[5:5 + n], refs[8 + n:8 + 2 * n], *refs[14 + 2 * n:]))
        masks = _head_masks()

        @pl.when(i == 0)
        def _():
            kk = k_ref[0]
            for h in range(HEADS_PER_BLOCK):
                kh_ref[h] = jnp.where(masks[h], kk, jnp.zeros_like(kk))
            dk_acc[...] = jnp.zeros_like(dk_acc)
            dv_acc[...] = jnp.zeros_like(dv_acc)

        row, col = _tri_masks(t)
        later = (row > col).astype(BF16)
        from_s = (row >= col).astype(BF16)
        causal = col < row
        qs = q_ref[0] * ATTN_SCALE
        dob = do_ref[...].astype(BF16)
        d_all = dob.astype(F32) * o_ref[...]
        qh = [jnp.where(masks[h], qs, jnp.zeros_like(qs)) for h in range(HEADS_PER_BLOCK)]
        doh = [jnp.where(masks[h], dob, jnp.zeros_like(dob)) for h in range(HEADS_PER_BLOCK)]
        d_row = [jnp.sum(jnp.where(masks[h], d_all, 0.0), axis=1, keepdims=True) for h in range(HEADS_PER_BLOCK)]

        heads = range(HEADS_PER_BLOCK)

        def scores(j, slot):
            keys = pl.ds(pl.multiple_of(j * t, t), t)
            kj = k_ref[0, keys, :]
            vj = v_ref[0, keys, :]
            for h in heads:
                z_ref[slot, h] = _nt(qh[h], kj)
                da_ref[slot, h] = _nt(doh[h], vj)

        def tiles(walk, left, after, carry):
            nh = HEADS_PER_BLOCK
            c_l, c_g = list(carry[:nh]), list(carry[nh:])
            ls, tail, g, before = {}, {}, {}, {}
            for w, (j, slot, diag, counts) in enumerate(walk):
                for h in heads:
                    ls[w, h], l1m = _log_sigmoids(z_ref[slot, h])
                    if diag:
                        l1m = jnp.where(causal, l1m, 0.0)
                    hi, lo = _split_bf16(l1m)
                    tail[w, h] = _nn(hi, later) + _nn(lo, later) + c_l[h]
                    c_l[h] = c_l[h] + jnp.sum(l1m, axis=1, keepdims=True)
            top = c_l[0]
            for h in range(1, nh):
                top = jnp.maximum(top, c_l[h])
            go = jnp.logical_and(left > 0, jnp.max(top) > DEAD_LOG)
            for w, (j, slot, diag, counts) in enumerate(walk):
                keys = pl.ds(pl.multiple_of(j * t, t), t)
                dv = None
                for h in heads:
                    a = jnp.exp(ls[w, h] + tail[w, h])
                    if diag:
                        a = jnp.where(causal, a, 0.0)
                    if counts is not None:
                        a = jnp.where(counts, a, 0.0)
                    ab = a.astype(BF16)
                    g[w, h] = ab.astype(F32) * da_ref[slot, h]
                    ghi, glo = _split_bf16(g[w, h])
                    before[w, h] = d_row[h] - (_nn(ghi, from_s) + _nn(glo, from_s) + c_g[h])
                    c_g[h] = c_g[h] + jnp.sum(g[w, h], axis=1, keepdims=True)
                    part = _tn(ab, doh[h])
                    dv = part if dv is None else dv + part
                dv_acc[keys, :] += dv
            scores(*after)
            for w, (j, slot, diag, counts) in enumerate(walk):
                keys = pl.ds(pl.multiple_of(j * t, t), t)
                dk = None
                for h in heads:
                    beta = jnp.exp(ls[w, h])
                    dz = g[w, h] * (1.0 - beta) - before[w, h] * beta
                    if diag:
                        dz = jnp.where(causal, dz, 0.0)
                    if counts is not None:
                        dz = jnp.where(counts, dz, 0.0)
                    dzb = dz.astype(BF16)
                    dqh = _nn(dzb, kh_ref[h, keys, :])
                    if diag:
                        dq_acc[h] = dqh
                    else:
                        dq_acc[h] += dqh
                    part = _tn(dzb, qh[h])
                    dk = part if dk is None else dk + part
                dk_acc[keys, :] += dk
            return (go, *c_l, *c_g)

        prev = jnp.maximum(i - 1, 0)
        scores(i, 0)
        scores(prev, 1)
        state = (jnp.int32(0), *tiles([(i, 0, True, None), (prev, 1, False, i >= 1)], i - 1,
                                      (jnp.maximum(i - 2, 0), 0),
                                      [jnp.zeros((t, 1), F32)] * (2 * HEADS_PER_BLOCK)))

        def step(state):
            jj = state[0]
            j = i - 2 - jj
            return (jj + 1, *tiles([(j, jj % 2, False, None)], j, (jnp.maximum(j - 1, 0), 1 - jj % 2), state[2:]))

        lax.while_loop(lambda s: s[1], step, state)
        dq = dq_acc[0]
        for h in range(1, HEADS_PER_BLOCK):
            dq = dq + dq_acc[h]
        dq_ref[...] = (dq * ATTN_SCALE).astype(BF16)

        @pl.when(i == pl.num_programs(1) - 1)
        def _():
            dk_ref[...] = dk_acc[...].astype(BF16)
            dv_ref[...] = dv_acc[...].astype(BF16)

        finish()

    qtile = pl.BlockSpec((t, ATTN_LANES), lambda hp, i: (i, hp))
    whole = pl.BlockSpec((S, ATTN_LANES), lambda hp, i: (0, hp))
    return pl.pallas_call(
        body,
        name="attn_bwd",
        grid=(nblk, S // t),
        in_specs=[
            pl.BlockSpec((1, t, ATTN_LANES), lambda hp, i: (0, i, hp)),
            pl.BlockSpec((1, S, ATTN_LANES), lambda hp, i: (1, 0, hp)),
            pl.BlockSpec((1, S, ATTN_LANES), lambda hp, i: (2, 0, hp)),
            qtile, qtile,
        ] + [ANY] * n,
        out_specs=[qtile, whole, whole] + [ANY] * n,
        out_shape=[jax.ShapeDtypeStruct((S, D_ATTN), BF16)] * 3 + _reduce_shapes(partials),
        scratch_shapes=[
            pltpu.VMEM((HEADS_PER_BLOCK, S, ATTN_LANES), BF16),
            pltpu.VMEM((S, ATTN_LANES), F32),
            pltpu.VMEM((S, ATTN_LANES), F32),
            pltpu.VMEM((HEADS_PER_BLOCK, t, ATTN_LANES), F32),
            pltpu.VMEM((2, HEADS_PER_BLOCK, t, t), F32),
            pltpu.VMEM((2, HEADS_PER_BLOCK, t, t), F32),
        ] + (_reduce_sems(n) if n else []),
        compiler_params=_params(),
    )(qkv, qkv, qkv, o, do, *partials)


def _pool_bwd_w(u, dyp, pool_w, pool_scale):
    S = u.shape[0]
    ts = ROW_TILE
    hb = ts // POOL_HALO

    def body(u_ref, halo_ref, dy_ref, w_ref, s_ref, dp_ref, dw_ref, ds_ref):
        i = pl.program_id(0)
        halo = jnp.where(i == 0, 0.0, halo_ref[...])
        pooled = _pooled(u_ref[...], halo, i * ts)

        @pl.when(i == 0)
        def _():
            dw_ref[...] = jnp.zeros_like(dw_ref)
            ds_ref[...] = jnp.zeros_like(ds_ref)

        for g in range(len(POOL_WINDOWS)):
            lanes = slice(g * POOL_GROUP_DIM, (g + 1) * POOL_GROUP_DIM)
            pg = pooled[g].astype(BF16)
            dy = dy_ref[:, lanes]
            ds_ref[:, lanes] += jnp.sum(dy * _nn(pg, w_ref[g]), axis=0, keepdims=True)
            dmapped = (dy * s_ref[:, lanes]).astype(BF16)
            dp_ref[:, lanes] = _nt(dmapped, w_ref[g])
            dw_ref[g] += _tn(pg, dmapped)

    row = pl.BlockSpec((ts, D_POOL), lambda i: (i, 0))
    vec = pl.BlockSpec((1, D_POOL), lambda i: (0, 0))
    wspec = pl.BlockSpec((4, POOL_GROUP_DIM, POOL_GROUP_DIM), lambda i: (0, 0, 0))
    return pl.pallas_call(
        body,
        name="pool_bwd_w",
        grid=(S // ts,),
        in_specs=[row, pl.BlockSpec((POOL_HALO, D_POOL), lambda i: (jnp.maximum(i * hb - 1, 0), 0)),
                  row, wspec, vec],
        out_specs=[row, wspec, vec],
        out_shape=[
            jax.ShapeDtypeStruct((S, D_POOL), F32),
            jax.ShapeDtypeStruct((4, POOL_GROUP_DIM, POOL_GROUP_DIM), F32),
            jax.ShapeDtypeStruct((1, D_POOL), F32),
        ],
        compiler_params=_params(),
    )(u, u, dyp, pool_w, pool_scale)


def _pool_bwd_u(dpooled):
    S = dpooled.shape[0]
    ts = ROW_TILE
    hb = ts // POOL_HALO
    last = S // ts - 1

    def body(dp_ref, halo_ref, du_ref):
        i = pl.program_id(0)
        dp = dp_ref[...]
        halo = jnp.where(i == last, 0.0, halo_ref[...])
        inv = _pool_counts(i * ts, ts)
        n = ts + POOL_HALO
        for g, w in enumerate(POOL_WINDOWS):
            lanes = slice(g * POOL_GROUP_DIM, (g + 1) * POOL_GROUP_DIM)
            dg = dp[:, lanes]
            acc = jnp.concatenate([dg * inv[g], halo[:, lanes] * (1.0 / w)], axis=0)
            shift = 1
            while shift < w:
                acc = acc + pltpu.roll(acc, n - shift, axis=0)
                shift *= 2
            du_ref[:, lanes] = (acc[:ts, :] - dg).astype(BF16)

    return pl.pallas_call(
        body,
        name="pool_bwd_u",
        grid=(S // ts,),
        in_specs=[pl.BlockSpec((ts, D_POOL), lambda i: (i, 0)),
                  pl.BlockSpec((POOL_HALO, D_POOL), lambda i: (jnp.minimum((i + 1) * hb, (last + 1) * hb - 1), 0))],
        out_specs=pl.BlockSpec((ts, D_POOL), lambda i: (i, 0)),
        out_shape=jax.ShapeDtypeStruct((S, D_POOL), BF16),
        compiler_params=_params(),
    )(dpooled, dpooled)


def _in_bwd_dw(hn, dprojs):
    S = hn.shape[0]
    ts = min(DW_TOKEN_TILE, S)

    def body(hn_ref, du_ref, dq_ref, dk_ref, dv_ref, o_ref, acc):
        j = pl.program_id(0)
        tt = pl.program_id(1)
        for k, dp_ref in enumerate((du_ref, dq_ref, dk_ref, dv_ref)):
            @pl.when(j == k)
            def _(dp_ref=dp_ref):
                part = _tn(hn_ref[...], dp_ref[...])

                @pl.when(tt == 0)
                def _():
                    acc[...] = part

                @pl.when(tt > 0)
                def _():
                    acc[...] += part

        @pl.when(tt == pl.num_programs(1) - 1)
        def _():
            o_ref[0] = acc[...].astype(BF16)

    def taken_at(k):
        return lambda j, tt: (jnp.where(j == k, tt, 0), 0)

    return pl.pallas_call(
        body,
        name="in_bwd_dw",
        grid=(4, S // ts),
        in_specs=[pl.BlockSpec((ts, D_MODEL), lambda j, tt: (tt, 0))]
        + [pl.BlockSpec((ts, 512), taken_at(k)) for k in range(4)],
        out_specs=pl.BlockSpec((1, D_MODEL, 512), lambda j, tt: (j, 0, 0)),
        out_shape=jax.ShapeDtypeStruct((4, D_MODEL, 512), BF16),
        scratch_shapes=[pltpu.VMEM((D_MODEL, 512), F32)],
        compiler_params=_params(),
    )(hn, *dprojs)


def _in_bwd_dx(du, dq, dk, dv, w_in, x, dh1, g1, partials=()):
    S = x.shape[0]
    ts = ROW_TILE
    n = len(partials)

    def body(*refs):
        du_ref, dq_ref, dk_ref, dv_ref, w_ref, x_ref, dh1_ref, g_ref = refs[:8]
        dx_ref, dg_ref = refs[8 + n:10 + n]
        i = pl.program_id(0)
        finish = lambda: None
        if n:
            finish = _run_at(i == 0, None, i == pl.num_programs(0) - 1,
                             _reduce_phases(refs[8:8 + n], refs[10 + n:10 + 2 * n], *refs[10 + 2 * n:]))

        @pl.when(i == 0)
        def _():
            dg_ref[...] = jnp.zeros_like(dg_ref)

        dhn = None
        for j, dp_ref in enumerate((du_ref, dq_ref, dk_ref, dv_ref)):
            part = _nt(dp_ref[...], w_ref[j])
            dhn = part if dhn is None else dhn + part
        xv = x_ref[...]
        r1 = _rms(xv)
        n1 = xv * r1
        dg_ref[...] += jnp.sum(dhn * n1, axis=0, keepdims=True)
        dx_ref[...] = dh1_ref[...] + _rms_bwd(dhn, n1, r1, g_ref[...])
        finish()

    row = lambda w: pl.BlockSpec((ts, w), lambda i: (i, 0))
    vec = pl.BlockSpec((1, D_MODEL), lambda i: (0, 0))
    wspec = pl.BlockSpec((4, D_MODEL, 512), lambda i: (0, 0, 0), pipeline_mode=pl.Buffered(1))
    return pl.pallas_call(
        body,
        name="in_bwd_dx",
        grid=(S // ts,),
        in_specs=[row(512), row(512), row(512), row(512), wspec, row(D_MODEL), row(D_MODEL), vec] + [ANY] * n,
        out_specs=[row(D_MODEL), vec] + [ANY] * n,
        out_shape=[
            jax.ShapeDtypeStruct((S, D_MODEL), F32),
            jax.ShapeDtypeStruct((1, D_MODEL), F32),
        ] + _reduce_shapes(partials),
        scratch_shapes=_reduce_sems(n) if n else [],
        compiler_params=_params(),
    )(du, dq, dk, dv, w_in, x, dh1, g1, *partials)


def _pieces(g):
    return g.reshape(N_CHIPS, 2, -1, g.shape[-1])


def _local_step(x, target, w_in, pool_w, small, full=None, shards=None):
    spread = shards is not None
    if spread:
        hn, u, qkv, w_out = _in_proj(x, small["norm1_g"], w_in, shards[:1])
        w_out = w_out.reshape(D_MODEL, D_MODEL)
    else:
        hn, u, qkv = _in_proj(x, small["norm1_g"], w_in)
        w_out, w_up, w_down = full
    y_pool = _pool_fwd(u, pool_w, small["pool_scale"])
    if spread:
        y_attn, w_up, w_down = _attn_fwd(qkv, shards[1:])
        w_up = w_up.reshape(N_CHIPS, D_MODEL, 1024)
        w_down = w_down.reshape(N_CHIPS, 1024, D_MODEL)
    else:
        (y_attn,) = _attn_fwd(qkv)
    mixed, h1, hn2 = _out_proj(y_pool, y_attn, x, small["pool_out_g"], small["attn_out_g"], w_out, small["norm2_g"])
    r_act, dh2, dh2b, lsq, dgf = _mlp_fwd(hn2, h1, w_up, w_down, small["final_g"], target)

    dup, dh1, dg2 = _mlp_bwd_dx(dh2b, r_act, w_down, w_up, h1, dh2, small["norm2_g"])
    (dw_down,) = _tokens_tn("mlp_bwd_dw_down", r_act, dh2b, True, True)
    out_args = (dh1, w_out, mixed, y_pool, y_attn, small["pool_out_g"], small["attn_out_g"])
    if spread:
        dw_up, got_down = _tokens_tn("mlp_bwd_dw_up", hn2, dup, False, False, [_pieces(dw_down)])
        dyp, dya, dw_out, dgp, dga, got_down = _out_bwd(*out_args, [_pieces(dw_down)], [got_down])
        dq, dk, dv, got_out, got_up = _attn_bwd(qkv, y_attn, dya, [_pieces(dw_out), _pieces(dw_up)])
    else:
        (dw_up,) = _tokens_tn("mlp_bwd_dw_up", hn2, dup, False, False)
        dyp, dya, dw_out, dgp, dga = _out_bwd(*out_args)
        dq, dk, dv = _attn_bwd(qkv, y_attn, dya)
    dpooled, dpool_w, dpool_scale = _pool_bwd_w(u, dyp, pool_w, small["pool_scale"])
    du = _pool_bwd_u(dpooled)
    dw_in = _in_bwd_dw(hn, (du, dq, dk, dv))
    if spread:
        dx, dg1, got_in = _in_bwd_dx(du, dq, dk, dv, w_in, x, dh1, small["norm1_g"], [_pieces(dw_in)])
    else:
        dx, dg1 = _in_bwd_dx(du, dq, dk, dv, w_in, x, dh1, small["norm1_g"])

    big = {"w_in": dw_in, "w_out": dw_out, "w_up": dw_up, "w_down": dw_down}
    if spread:
        big["received"] = {"w_in": got_in, "w_out": got_out, "w_up": got_up, "w_down": got_down}
    little = {"norm1_g": dg1, "pool_w": dpool_w, "pool_scale": dpool_scale, "pool_out_g": dgp,
              "attn_out_g": dga, "norm2_g": dg2, "final_g": dgf, "loss_sq": lsq}
    return dx, big, little


def _place():
    x, y, c = lax.axis_index("x"), lax.axis_index("y"), lax.axis_index("c")
    other_chips = [(1 - x, y), (x, 1 - y), (1 - x, 1 - y)]
    return x, y, c, other_chips


def _chip_index(chip):
    return 2 * chip[0] + chip[1]


def _gather_shapes(shards):
    return [jax.ShapeDtypeStruct((N_CHIPS,) + s.shape, s.dtype) for s in shards]


def _gather_sems(n):
    return [pltpu.SemaphoreType.DMA((n, 7)), pltpu.SemaphoreType.DMA((n, 7))]


def _gather_phases(ins, outs, send_sems, recv_sems):
    n = len(ins)
    x, y, c, chips = _place()
    me = _chip_index((x, y))
    sibling = (x, y, 1 - c)

    def copy(a, k, chip_idx, half, to, src=None):
        dst = outs[a].at[chip_idx, half]
        return pltpu.make_async_remote_copy(
            src_ref=dst if src is None else src, dst_ref=dst,
            send_sem=send_sems.at[a, k], recv_sem=recv_sems.at[a, k],
            device_id=to, device_id_type=MESH)

    def own_chip(a, to):
        return pltpu.make_async_remote_copy(
            src_ref=ins[a], dst_ref=outs[a].at[me],
            send_sem=send_sems.at[a, 0], recv_sem=recv_sems.at[a, 0],
            device_id=to, device_id_type=MESH)

    def first(a):
        return [own_chip(a, sibling)] + [
            copy(a, 1 + j, me, c, (*chip, c), src=ins[a].at[c]) for j, chip in enumerate(chips)]

    def passed(a, j):
        return copy(a, 4 + j, _chip_index(chips[j]), c, sibling)

    def start():
        for a in range(n):
            for cp in first(a):
                cp.start()

    def forward():
        for a in range(n):
            for j, chip in enumerate(chips):
                copy(a, 1 + j, _chip_index(chip), c, (x, y, c)).wait_recv()
                passed(a, j).start()

    def finish():
        for a in range(n):
            own_chip(a, (x, y, c)).wait_recv()
            for j, chip in enumerate(chips):
                copy(a, 4 + j, _chip_index(chip), 1 - c, (x, y, c)).wait_recv()
        for a in range(n):
            for cp in first(a):
                cp.wait_send()
            for j in range(len(chips)):
                passed(a, j).wait_send()

    return start, forward, finish


def _gather_weights(shards):
    n = len(shards)

    def body(*refs):
        start, forward, finish = _gather_phases(refs[:n], refs[n:2 * n], *refs[2 * n:])
        start()
        forward()
        finish()

    return pl.pallas_call(
        body,
        name="gather_weights",
        in_specs=[ANY] * n,
        out_specs=[ANY] * n,
        out_shape=_gather_shapes(shards),
        scratch_shapes=_gather_sems(n),
    )(*shards)


def _reduce_shapes(partials):
    return [jax.ShapeDtypeStruct((N_DEV,) + p.shape[2:], p.dtype) for p in partials]


def _reduce_sems(n):
    return [pltpu.SemaphoreType.DMA((n, N_DEV)), pltpu.SemaphoreType.DMA((n, N_DEV))]


def _reduce_phases(ins, outs, send_sems, recv_sems, same_core=None):
    n = len(ins)
    x, y, c, _ = _place()
    me = 4 * x + 2 * y + c

    def to_peer(a, k):
        return pltpu.make_async_remote_copy(
            src_ref=ins[a].at[k // 2, k % 2], dst_ref=outs[a].at[me],
            send_sem=send_sems.at[a, k], recv_sem=recv_sems.at[a, me],
            device_id=(k // 4, (k // 2) % 2, k % 2), device_id_type=MESH)

    def from_peer(a, k):
        return pltpu.make_async_remote_copy(
            src_ref=ins[a].at[k // 2, k % 2], dst_ref=outs[a].at[k],
            send_sem=send_sems.at[a, k], recv_sem=recv_sems.at[a, k],
            device_id=(x, y, c), device_id_type=MESH)

    def taken(k):
        if same_core is None:
            return k != me
        if same_core:
            return jnp.logical_and(k != me, k % 2 == c)
        return k % 2 != c

    def start():
        for a in range(n):
            for k in range(N_DEV):
                @pl.when(taken(k))
                def _(a=a, k=k):
                    to_peer(a, k).start()

    def finish():
        for a in range(n):
            for k in range(N_DEV):
                @pl.when(taken(k))
                def _(a=a, k=k):
                    from_peer(a, k).wait_recv()
                    to_peer(a, k).wait_send()

    return start, finish


def _sum_pieces(where, parts, own):
    _, h, cols = parts.shape
    hb = min(h, 256)

    def body(where_ref, *refs):
        me = where_ref[0]
        acc = None
        for k in range(N_DEV):
            piece = jnp.where(k == me, refs[N_DEV][0, 0], refs[k][0]).astype(F32)
            acc = piece if acc is None else acc + piece
        refs[N_DEV + 1][0] = acc

    def sent_by(k):
        return lambda r, w: (jnp.where(w[0] == k, (k + 1) % N_DEV, k), r, 0)

    return pl.pallas_call(
        body,
        name="sum_pieces",
        grid_spec=pltpu.PrefetchScalarGridSpec(
            num_scalar_prefetch=1,
            grid=(h // hb,),
            in_specs=[pl.BlockSpec((1, hb, cols), sent_by(k)) for k in range(N_DEV)]
            + [pl.BlockSpec((1, 1, hb, cols), lambda r, w: (w[1], w[2], r, 0))],
            out_specs=pl.BlockSpec((1, hb, cols), lambda r, w: (w[2], r, 0)),
        ),
        out_shape=jax.ShapeDtypeStruct((2, h, cols), F32),
    )(where, *([parts] * N_DEV), own)


def _join_halves(halves):
    n = len(halves)

    def body(*refs):
        outs = refs[n:2 * n]
        send_sems, recv_sems = refs[2 * n:]
        x, y, c, _ = _place()
        sends = [
            pltpu.make_async_remote_copy(
                src_ref=outs[a].at[c], dst_ref=outs[a].at[c],
                send_sem=send_sems.at[a], recv_sem=recv_sems.at[a],
                device_id=(x, y, 1 - c), device_id_type=MESH)
            for a in range(n)]
        for cp in sends:
            cp.start()
        for a in range(n):
            pltpu.make_async_remote_copy(
                src_ref=outs[a].at[c], dst_ref=outs[a].at[1 - c],
                send_sem=send_sems.at[a], recv_sem=recv_sems.at[a],
                device_id=(x, y, c), device_id_type=MESH).wait_recv()
        for cp in sends:
            cp.wait_send()

    return pl.pallas_call(
        body,
        name="join_halves",
        in_specs=[ANY] * n,
        out_specs=[ANY] * n,
        out_shape=[jax.ShapeDtypeStruct(s.shape, s.dtype) for s in halves],
        input_output_aliases={a: a for a in range(n)},
        scratch_shapes=[pltpu.SemaphoreType.DMA((n,)), pltpu.SemaphoreType.DMA((n,))],
    )(*halves)


def _adamw(w, g, m, v):
    m = ADAM_B1 * m + (1.0 - ADAM_B1) * g
    v = ADAM_B2 * v + (1.0 - ADAM_B2) * jnp.square(g)
    m_hat = m / (1.0 - ADAM_B1 ** ADAM_STEP)
    v_hat = v / (1.0 - ADAM_B2 ** ADAM_STEP)
    delta = -ADAM_LR * (m_hat / (jnp.sqrt(v_hat) + ADAM_EPS) + ADAM_WD * w)
    return delta, m, v


def _adamw_big(w, g, m, v):
    rows, cols = w.shape
    rb = min(rows, 256)

    def body(w_ref, g_ref, m_ref, v_ref, d_ref, mo_ref, vo_ref):
        d_ref[...], mo_ref[...], vo_ref[...] = _adamw(w_ref[...], g_ref[...], m_ref[...], v_ref[...])

    blk = pl.BlockSpec((rb, cols), lambda r: (r, 0))
    return pl.pallas_call(
        body,
        name="adamw_big",
        grid=(rows // rb,),
        in_specs=[blk] * 4,
        out_specs=[blk] * 3,
        out_shape=[jax.ShapeDtypeStruct(w.shape, F32)] * 3,
    )(w, g, m, v)


SMALL_ORDER = ("pool_w", "norm1_g", "pool_scale", "pool_out_g", "attn_out_g", "norm2_g", "final_g")
SUBLANES = 8


def _pack(parts):
    rows = []
    for p in parts:
        p = p.reshape(-1, LANES)
        pad = (-p.shape[0]) % SUBLANES
        if pad:
            p = jnp.pad(p, ((0, pad), (0, 0)))
        rows.append(p)
    return jnp.concatenate(rows, axis=0)


def _unpack(slab, shapes):
    out, r = [], 0
    for shp in shapes:
        size = 1
        for d in shp:
            size *= d
        nrow = size // LANES
        out.append(slab[r:r + nrow].reshape(shp))
        r += nrow + (-nrow) % SUBLANES
    return out


def _small_step(partials, w, m, v, loss_rows):
    rows = partials.shape[0]

    def body(p_ref, w_ref, m_ref, v_ref, g_ref, d_ref, mo_ref, vo_ref, loss_ref, buf, send_sems, recv_sems):
        x, y, c, _ = _place()
        me = 4 * x + 2 * y + c
        for k in range(N_DEV):
            @pl.when(k != me)
            def _(k=k):
                pltpu.make_async_remote_copy(
                    src_ref=p_ref, dst_ref=buf.at[me],
                    send_sem=send_sems.at[k], recv_sem=recv_sems.at[me],
                    device_id=(k // 4, (k // 2) % 2, k % 2), device_id_type=MESH).start()
        buf[me] = p_ref[...]
        for k in range(N_DEV):
            @pl.when(k != me)
            def _(k=k):
                pltpu.make_async_remote_copy(
                    src_ref=p_ref, dst_ref=buf.at[k],
                    send_sem=send_sems.at[k], recv_sem=recv_sems.at[k],
                    device_id=(x, y, c), device_id_type=MESH).wait()
        g = buf[0]
        for k in range(1, N_DEV):
            g = g + buf[k]
        g_ref[...] = g
        d_ref[...], mo_ref[...], vo_ref[...] = _adamw(w_ref[...], g, m_ref[...], v_ref[...])
        loss = (0.5 / D_MODEL) * jnp.sum(g[rows - loss_rows:, :])
        loss_ref[...] = jnp.full(loss_ref.shape, loss, F32)

    vm = pl.BlockSpec(memory_space=pltpu.VMEM)
    slab = jax.ShapeDtypeStruct((rows, LANES), F32)
    return pl.pallas_call(
        body,
        name="small_step",
        in_specs=[vm] * 4,
        out_specs=[vm] * 5,
        out_shape=[slab, slab, slab, slab, jax.ShapeDtypeStruct((SUBLANES, LANES), F32)],
        scratch_shapes=[pltpu.VMEM((N_DEV, rows, LANES), F32),
                        pltpu.SemaphoreType.DMA((N_DEV,)), pltpu.SemaphoreType.DMA((N_DEV,))],
    )(partials, w, m, v)


BIG_ORDER = ("w_in", "w_out", "w_up", "w_down")
WEIGHT_ORDER = ("norm1_g", "w_in", "pool_w", "pool_scale", "pool_out_g", "attn_out_g", "w_out", "norm2_g",
                "w_up", "w_down", "final_g")


def _halves(a):
    return a.reshape(2, a.shape[0] // 2, a.shape[1])


def kernel(x, norm1_g, w_in, pool_w, pool_scale, pool_out_g, attn_out_g, w_out, norm2_g, w_up, w_down, final_g, loss_target, m_norm1_g, m_w_in, m_pool_w, m_pool_scale, m_pool_out_g, m_attn_out_g, m_w_out, m_norm2_g, m_w_up, m_w_down, m_final_g, v_norm1_g, v_w_in, v_pool_w, v_pool_scale, v_pool_out_g, v_attn_out_g, v_w_out, v_norm2_g, v_w_up, v_w_down, v_final_g):
    w = dict(norm1_g=norm1_g, w_in=w_in, pool_w=pool_w, pool_scale=pool_scale, pool_out_g=pool_out_g,
             attn_out_g=attn_out_g, w_out=w_out, norm2_g=norm2_g, w_up=w_up, w_down=w_down, final_g=final_g)
    m = dict(norm1_g=m_norm1_g, w_in=m_w_in, pool_w=m_pool_w, pool_scale=m_pool_scale, pool_out_g=m_pool_out_g,
             attn_out_g=m_attn_out_g, w_out=m_w_out, norm2_g=m_norm2_g, w_up=m_w_up, w_down=m_w_down,
             final_g=m_final_g)
    v = dict(norm1_g=v_norm1_g, w_in=v_w_in, pool_w=v_pool_w, pool_scale=v_pool_scale, pool_out_g=v_pool_out_g,
             attn_out_g=v_attn_out_g, w_out=v_w_out, norm2_g=v_norm2_g, w_up=v_w_up, w_down=v_w_down,
             final_g=v_final_g)

    shards = {n: _halves(w[n].astype(BF16)) for n in BIG_ORDER}
    (w_in_g,) = _gather_weights([shards["w_in"]])
    small = {n: w[n].reshape(1, -1) for n in ("norm1_g", "pool_scale", "pool_out_g", "attn_out_g", "norm2_g", "final_g")}
    dx, big, little = _local_step(
        x[0], loss_target[0], w_in_g.reshape(N_CHIPS, D_MODEL, 512), pool_w.astype(BF16), small,
        shards=[shards["w_out"], shards["w_up"], shards["w_down"]])

    grads, deltas, new_m, new_v = {}, {}, {}, {}
    loss_rows = D_MODEL // LANES
    slab_g = _pack([little[n] for n in SMALL_ORDER] + [little["loss_sq"]])
    zeros = jnp.zeros((loss_rows, LANES), F32)
    slab_w = _pack([w[n] for n in SMALL_ORDER] + [zeros])
    slab_m = _pack([m[n] for n in SMALL_ORDER] + [zeros])
    slab_v = _pack([v[n] for n in SMALL_ORDER] + [zeros])
    received = big.pop("received")
    g_s, d_s, m_s, v_s, loss = _small_step(slab_g, slab_w, slab_m, slab_v, loss_rows)
    shapes = [w[n].shape for n in SMALL_ORDER]
    for slab, dst in ((g_s, grads), (d_s, deltas), (m_s, new_m), (v_s, new_v)):
        for n, val in zip(SMALL_ORDER, _unpack(slab, shapes)):
            dst[n] = val

    xi, yi, ci = lax.axis_index("x"), lax.axis_index("y"), lax.axis_index("c")
    where = jnp.stack([4 * xi + 2 * yi + ci, 2 * xi + yi, ci]).astype(jnp.int32)
    full = _join_halves([_sum_pieces(where, received[n], _pieces(big[n])) for n in BIG_ORDER])
    for n, g in zip(BIG_ORDER, full):
        grads[n] = g.reshape(w[n].shape)
        deltas[n], new_m[n], new_v[n] = _adamw_big(w[n], grads[n], m[n], v[n])

    return (loss[0, 0], dx[None], *[grads[n] for n in WEIGHT_ORDER], *[deltas[n] for n in WEIGHT_ORDER],
            *[new_m[n] for n in WEIGHT_ORDER], *[new_v[n] for n in WEIGHT_ORDER])
```

```python
import functools

import jax
import jax.numpy as jnp
from jax import lax
from jax.experimental import pallas as pl
from jax.experimental.pallas import tpu as pltpu

F32 = jnp.float32
BF16 = jnp.bfloat16

D_MODEL = 1024
D_POOL = 512
D_ATTN = 512
POOL_WINDOWS = (2, 4, 8, 16)
POOL_GROUP_DIM = 128
POOL_HALO = 16
HEAD_DIM = 64
HEADS_PER_BLOCK = 4
ATTN_LANES = HEADS_PER_BLOCK * HEAD_DIM
D_FF = 4096
N_CHIPS = 4
N_DEV = 8
EPS = 1e-6
ATTN_SCALE = 0.125
ATTN_TILE = 256
ATTN_ROW_CHUNKS = 1
DEAD_LOG = -105.0
ROW_TILE = 512
MLP_TILE = 512
DW_TOKEN_TILE = 2048
LANES = 128

ADAM_LR = 0.001
ADAM_B1 = 0.9
ADAM_B2 = 0.999
ADAM_EPS = 1e-08
ADAM_WD = 0.01
ADAM_STEP = 10

HOPS_DIAGONAL = (6, 7)
HOPS_SAME_CORE = (2, 4)
HOPS_REST = (1, 3, 5)

MESH = pl.DeviceIdType.MESH
ANY = pl.BlockSpec(memory_space=pl.ANY)
VMEM_LIMIT = 56 * 1024 * 1024


def _nn(a, b):
    return jnp.dot(a, b, preferred_element_type=F32)


def _nt(a, b):
    return lax.dot_general(a, b, (((1,), (1,)), ((), ())), preferred_element_type=F32)


def _tn(a, b):
    return lax.dot_general(a, b, (((0,), (0,)), ((), ())), preferred_element_type=F32)


def _rms(x):
    return lax.rsqrt(jnp.mean(x * x, axis=-1, keepdims=True) + EPS)


def _rms_bwd(dy, n, r, g):
    dn = dy * g
    return r * (dn - n * jnp.mean(dn * n, axis=-1, keepdims=True))


def _params(**kw):
    return pltpu.CompilerParams(vmem_limit_bytes=VMEM_LIMIT, **kw)


def _run_at(first, middle, last, phases):
    pl.when(first)(phases[0])
    if len(phases) == 3:
        pl.when(middle)(phases[1])
    return lambda: pl.when(last)(phases[-1])


def _in_proj(x, g1, w_in, shards=()):
    S = x.shape[0]
    ts = ROW_TILE
    n = len(shards)

    def body(*refs):
        x_ref, g_ref, w_ref = refs[:3]
        hn_ref, u_ref, qkv_ref = refs[3 + n:6 + n]
        i = pl.program_id(0)
        finish = lambda: None
        if n:
            steps = pl.num_programs(0)
            finish = _run_at(i == 0, i == steps // 2, i == steps - 1,
                             _gather_phases(refs[3:3 + n], refs[6 + n:6 + 2 * n], *refs[6 + 2 * n:]))

        xf = x_ref[...]
        hn = (xf * _rms(xf) * g_ref[...]).astype(BF16)
        hn_ref[...] = hn
        u_ref[...] = _nn(hn, w_ref[0])
        for j in range(1, 4):
            qkv_ref[j - 1] = _nn(hn, w_ref[j]).astype(BF16)

        finish()

    return pl.pallas_call(
        body,
        name="in_proj",
        grid=(S // ts,),
        in_specs=[
            pl.BlockSpec((ts, D_MODEL), lambda i: (i, 0)),
            pl.BlockSpec((1, D_MODEL), lambda i: (0, 0)),
            pl.BlockSpec((4, D_MODEL, 512), lambda i: (0, 0, 0), pipeline_mode=pl.Buffered(1)),
        ] + [ANY] * n,
        out_specs=[
            pl.BlockSpec((ts, D_MODEL), lambda i: (i, 0)),
            pl.BlockSpec((ts, D_POOL), lambda i: (i, 0)),
            pl.BlockSpec((3, ts, 512), lambda i: (0, i, 0)),
        ] + [ANY] * n,
        out_shape=[
            jax.ShapeDtypeStruct((S, D_MODEL), BF16),
            jax.ShapeDtypeStruct((S, D_POOL), F32),
            jax.ShapeDtypeStruct((3, S, 512), BF16),
        ] + _gather_shapes(shards),
        scratch_shapes=_gather_sems(n) if n else [],
        compiler_params=_params(),
    )(x, g1, w_in, *shards)


def _pool_counts(first_row, rows):
    t = first_row + lax.broadcasted_iota(jnp.int32, (rows, 1), 0)
    return [1.0 / jnp.minimum(t + 1, w).astype(F32) for w in POOL_WINDOWS]


def _pooled(u_tile, halo, first_row):
    ts = u_tile.shape[0]
    inv = _pool_counts(first_row, ts)
    outs = []
    for g, w in enumerate(POOL_WINDOWS):
        lanes = slice(g * POOL_GROUP_DIM, (g + 1) * POOL_GROUP_DIM)
        xg = u_tile[:, lanes]
        acc = jnp.concatenate([halo[:, lanes], xg], axis=0)
        shift = 1
        while shift < w:
            acc = acc + pltpu.roll(acc, shift, axis=0)
            shift *= 2
        outs.append(acc[POOL_HALO:, :] * inv[g] - xg)
    return outs


def _pool_fwd(u, pool_w, pool_scale):
    S = u.shape[0]
    ts = ROW_TILE
    hb = ts // POOL_HALO

    def body(u_ref, halo_ref, w_ref, s_ref, y_ref):
        i = pl.program_id(0)
        halo = jnp.where(i == 0, 0.0, halo_ref[...])
        pooled = _pooled(u_ref[...], halo, i * ts)
        for g in range(len(POOL_WINDOWS)):
            lanes = slice(g * POOL_GROUP_DIM, (g + 1) * POOL_GROUP_DIM)
            y_ref[:, lanes] = _nn(pooled[g].astype(BF16), w_ref[g]) * s_ref[:, lanes]

    return pl.pallas_call(
        body,
        name="pool_fwd",
        grid=(S // ts,),
        in_specs=[
            pl.BlockSpec((ts, D_POOL), lambda i: (i, 0)),
            pl.BlockSpec((POOL_HALO, D_POOL), lambda i: (jnp.maximum(i * hb - 1, 0), 0)),
            pl.BlockSpec((4, POOL_GROUP_DIM, POOL_GROUP_DIM), lambda i: (0, 0, 0)),
            pl.BlockSpec((1, D_POOL), lambda i: (0, 0)),
        ],
        out_specs=pl.BlockSpec((ts, D_POOL), lambda i: (i, 0)),
        out_shape=jax.ShapeDtypeStruct((S, D_POOL), F32),
        compiler_params=_params(),
    )(u, u, pool_w, pool_scale)


def _head_masks():
    lane = lax.broadcasted_iota(jnp.int32, (1, ATTN_LANES), 1)
    return [jnp.logical_and(lane >= h * HEAD_DIM, lane < (h + 1) * HEAD_DIM) for h in range(HEADS_PER_BLOCK)]


def _tri_masks(t):
    row = lax.broadcasted_iota(jnp.int32, (t, t), 0)
    col = lax.broadcasted_iota(jnp.int32, (t, t), 1)
    return row, col


def _split_bf16(x):
    hi = x.astype(BF16)
    lo = (x - hi.astype(F32)).astype(BF16)
    return hi, lo


def _log_sigmoids(z):
    sp = jnp.log(1.0 + jnp.exp(-jnp.abs(z)))
    ls = jnp.minimum(z, 0.0) - sp
    return ls, ls - z


def _attn_fwd(qkv, shards=()):
    S = qkv.shape[1]
    t = ATTN_TILE
    n = len(shards)
    nblk = D_ATTN // ATTN_LANES

    def body(*refs):
        q_ref, k_ref, v_ref = refs[:3]
        o_ref = refs[3 + n]
        vh_ref, acc_ref, z_ref = refs[4 + 2 * n:7 + 2 * n]
        hp = pl.program_id(0)
        i = pl.program_id(1)
        finish = lambda: None
        if n:
            finish = _run_at(jnp.logical_and(hp == 0, i == 0),
                             jnp.logical_and(hp == nblk - 1, i == (3 * pl.num_programs(1)) // 4),
                             jnp.logical_and(hp == nblk - 1, i == pl.num_programs(1) - 1),
                             _gather_phases(refs[3:3 + n], refs[4 + n:4 + 2 * n], *refs[7 + 2 * n:]))
        masks = _head_masks()

        @pl.when(i == 0)
        def _():
            vv = v_ref[0]
            for h in range(HEADS_PER_BLOCK):
                vh_ref[h] = jnp.where(masks[h], vv, jnp.zeros_like(vv))

        row, col = _tri_masks(t)
        later = (row > col).astype(BF16)
        causal = col < row
        qs = q_ref[0] * ATTN_SCALE
        heads = range(HEADS_PER_BLOCK)
        qh = [jnp.where(masks[h], qs, jnp.zeros_like(qs)) for h in heads]

        def scores(j, slot):
            kj = k_ref[0, pl.ds(pl.multiple_of(j * t, t), t), :]
            for h in heads:
                z_ref[slot, h] = _nt(qh[h], kj)

        def tiles(walk, left, after, carry):
            cs = list(carry)
            ls, tail = {}, {}
            for w, (j, slot, diag, counts) in enumerate(walk):
                for h in heads:
                    ls[w, h], l1m = _log_sigmoids(z_ref[slot, h])
                    if diag:
                        l1m = jnp.where(causal, l1m, 0.0)
                    hi, lo = _split_bf16(l1m)
                    tail[w, h] = _nn(hi, later) + _nn(lo, later) + cs[h]
                    cs[h] = cs[h] + jnp.sum(l1m, axis=1, keepdims=True)
            top = cs[0]
            for h in heads[1:]:
                top = jnp.maximum(top, cs[h])
            go = jnp.logical_and(left > 0, jnp.max(top) > DEAD_LOG)
            scores(*after)
            for w, (j, slot, diag, counts) in enumerate(walk):
                keys = pl.ds(pl.multiple_of(j * t, t), t)
                for h in heads:
                    a = jnp.exp(ls[w, h] + tail[w, h])
                    if diag:
                        a = jnp.where(causal, a, 0.0)
                    if counts is not None:
                        a = jnp.where(counts, a, 0.0)
                    pv = _nn(a.astype(BF16), vh_ref[h, keys, :])
                    if diag:
                        acc_ref[h] = pv
                    else:
                        acc_ref[h] += pv
            return (go, *cs)

        before = jnp.maximum(i - 1, 0)
        scores(i, 0)
        scores(before, 1)
        state = (jnp.int32(0), *tiles([(i, 0, True, None), (before, 1, False, i >= 1)], i - 1,
                                      (jnp.maximum(i - 2, 0), 0), [jnp.zeros((t, 1), F32)] * HEADS_PER_BLOCK))

        def step(state):
            jj = state[0]
            j = i - 2 - jj
            return (jj + 1, *tiles([(j, jj % 2, False, None)], j, (jnp.maximum(j - 1, 0), 1 - jj % 2), state[2:]))

        lax.while_loop(lambda s: s[1], step, state)
        out = acc_ref[0]
        for h in heads[1:]:
            out = out + acc_ref[h]
        o_ref[...] = out
        finish()

    return pl.pallas_call(
        body,
        name="attn_fwd",
        grid=(nblk, S // t),
        in_specs=[
            pl.BlockSpec((1, t, ATTN_LANES), lambda hp, i: (0, i, hp)),
            pl.BlockSpec((1, S, ATTN_LANES), lambda hp, i: (1, 0, hp)),
            pl.BlockSpec((1, S, ATTN_LANES), lambda hp, i: (2, 0, hp)),
        ] + [ANY] * n,
        out_specs=[pl.BlockSpec((t, ATTN_LANES), lambda hp, i: (i, hp))] + [ANY] * n,
        out_shape=[jax.ShapeDtypeStruct((S, D_ATTN), F32)] + _gather_shapes(shards),
        scratch_shapes=[pltpu.VMEM((HEADS_PER_BLOCK, S, ATTN_LANES), BF16),
                        pltpu.VMEM((HEADS_PER_BLOCK * ATTN_ROW_CHUNKS, t // ATTN_ROW_CHUNKS, ATTN_LANES), F32),
                        pltpu.VMEM((2, HEADS_PER_BLOCK * ATTN_ROW_CHUNKS, t // ATTN_ROW_CHUNKS, t), F32)]
        + (_gather_sems(n) if n else []),
        compiler_params=_params(),
    )(qkv, qkv, qkv, *shards)


def _out_proj(y_pool, y_attn, x, g_pool, g_attn, w_out, g2):
    S = x.shape[0]
    ts = ROW_TILE

    def body(yp_ref, ya_ref, x_ref, gp_ref, ga_ref, w_ref, g2_ref, mixed_ref, h1_ref, hn2_ref):
        yp = yp_ref[...]
        ya = ya_ref[...]
        mixed = jnp.concatenate([yp * _rms(yp) * gp_ref[...], ya * _rms(ya) * ga_ref[...]], axis=-1).astype(BF16)
        mixed_ref[...] = mixed
        h1 = x_ref[...] + _nn(mixed, w_ref[...])
        h1_ref[...] = h1
        hn2_ref[...] = (h1 * _rms(h1) * g2_ref[...]).astype(BF16)

    row = lambda w: pl.BlockSpec((ts, w), lambda i: (i, 0))
    vec = lambda w: pl.BlockSpec((1, w), lambda i: (0, 0))
    return pl.pallas_call(
        body,
        name="out_proj",
        grid=(S // ts,),
        in_specs=[row(D_POOL), row(D_ATTN), row(D_MODEL), vec(D_POOL), vec(D_ATTN),
                  pl.BlockSpec((D_MODEL, D_MODEL), lambda i: (0, 0)), vec(D_MODEL)],
        out_specs=[row(D_MODEL), row(D_MODEL), row(D_MODEL)],
        out_shape=[
            jax.ShapeDtypeStruct((S, D_MODEL), BF16),
            jax.ShapeDtypeStruct((S, D_MODEL), F32),
            jax.ShapeDtypeStruct((S, D_MODEL), BF16),
        ],
        compiler_params=_params(),
    )(y_pool, y_attn, x, g_pool, g_attn, w_out, g2)


def _mlp_fwd(hn2, h1, w_up, w_down, g_final, target):
    S = hn2.shape[0]
    ts = MLP_TILE
    nf = D_FF // 1024

    def body(hn2_ref, h1_ref, wu_ref, wd_ref, gf_ref, tg_ref, r_ref, dh2_ref, dh2b_ref, lsq_ref, dgf_ref):
        i = pl.program_id(0)
        hn2v = hn2_ref[...]
        acts = []
        for c in range(nf):
            r = jnp.maximum(_nn(hn2v, wu_ref[c]), 0.0)
            r_ref[:, c * 1024:(c + 1) * 1024] = r.astype(BF16)
            acts.append((r * r).astype(BF16))
        h2 = h1_ref[...] + _nn(jnp.concatenate(acts, axis=1), wd_ref[...])

        @pl.when(i == 0)
        def _():
            lsq_ref[...] = jnp.zeros_like(lsq_ref)
            dgf_ref[...] = jnp.zeros_like(dgf_ref)

        rf = _rms(h2)
        n = h2 * rf
        gf = gf_ref[...]
        e = n * gf - tg_ref[...]
        lsq_ref[...] += jnp.sum(e * e, axis=0, keepdims=True)
        dy = e * (1.0 / D_MODEL)
        dgf_ref[...] += jnp.sum(dy * n, axis=0, keepdims=True)
        dh2 = _rms_bwd(dy, n, rf, gf)
        dh2_ref[...] = dh2
        dh2b_ref[...] = dh2.astype(BF16)

    row = lambda w: pl.BlockSpec((ts, w), lambda i: (i, 0))
    vec = lambda w: pl.BlockSpec((1, w), lambda i: (0, 0))
    once = pl.Buffered(1)
    return pl.pallas_call(
        body,
        name="mlp_fwd",
        grid=(S // ts,),
        in_specs=[row(D_MODEL), row(D_MODEL),
                  pl.BlockSpec((nf, D_MODEL, 1024), lambda i: (0, 0, 0), pipeline_mode=once),
                  pl.BlockSpec((D_FF, D_MODEL), lambda i: (0, 0), pipeline_mode=once),
                  vec(D_MODEL), row(D_MODEL)],
        out_specs=[row(D_FF), row(D_MODEL), row(D_MODEL), vec(D_MODEL), vec(D_MODEL)],
        out_shape=[
            jax.ShapeDtypeStruct((S, D_FF), BF16),
            jax.ShapeDtypeStruct((S, D_MODEL), F32),
            jax.ShapeDtypeStruct((S, D_MODEL), BF16),
            jax.ShapeDtypeStruct((1, D_MODEL), F32),
            jax.ShapeDtypeStruct((1, D_MODEL), F32),
        ],
        compiler_params=_params(),
    )(hn2, h1, w_up, w_down.reshape(D_FF, D_MODEL), g_final, target)


def _mlp_bwd_dx(dh2b, r_act, w_down, w_up, h1, dh2, g2):
    S = h1.shape[0]
    ts = MLP_TILE
    nf = D_FF // 1024

    def body(dh2b_ref, r_ref, wd_ref, wu_ref, h1_ref, dh2_ref, g2_ref, dup_ref, dh1_ref, dg2_ref):
        i = pl.program_id(0)
        dh2b = dh2b_ref[...]
        dhn2 = None
        for c in range(nf):
            chunk = slice(c * 1024, (c + 1) * 1024)
            dup = (_nt(dh2b, wd_ref[c]) * (2.0 * r_ref[:, chunk].astype(F32))).astype(BF16)
            dup_ref[:, chunk] = dup
            part = _nt(dup, wu_ref[c])
            dhn2 = part if dhn2 is None else dhn2 + part

        @pl.when(i == 0)
        def _():
            dg2_ref[...] = jnp.zeros_like(dg2_ref)

        h1v = h1_ref[...]
        r2 = _rms(h1v)
        n2 = h1v * r2
        dg2_ref[...] += jnp.sum(dhn2 * n2, axis=0, keepdims=True)
        dh1_ref[...] = dh2_ref[...] + _rms_bwd(dhn2, n2, r2, g2_ref[...])

    row = lambda w: pl.BlockSpec((ts, w), lambda i: (i, 0))
    vec = lambda w: pl.BlockSpec((1, w), lambda i: (0, 0))
    once = pl.Buffered(1)
    return pl.pallas_call(
        body,
        name="mlp_bwd_dx",
        grid=(S // ts,),
        in_specs=[row(D_MODEL), row(D_FF),
                  pl.BlockSpec((nf, 1024, D_MODEL), lambda i: (0, 0, 0), pipeline_mode=once),
                  pl.BlockSpec((nf, D_MODEL, 1024), lambda i: (0, 0, 0), pipeline_mode=once),
                  row(D_MODEL), row(D_MODEL), vec(D_MODEL)],
        out_specs=[row(D_FF), row(D_MODEL), vec(D_MODEL)],
        out_shape=[
            jax.ShapeDtypeStruct((S, D_FF), BF16),
            jax.ShapeDtypeStruct((S, D_MODEL), F32),
            jax.ShapeDtypeStruct((1, D_MODEL), F32),
        ],
        compiler_params=_params(),
    )(dh2b, r_act, w_down, w_up, h1, dh2, g2)


def _tokens_tn(name, a, b, a_chunked, square_a, partials=()):
    S = a.shape[0]
    ts = min(DW_TOKEN_TILE, S)
    nf = D_FF // 1024
    n = len(partials)

    def body(*refs):
        a_ref, b_ref = refs[:2]
        o_ref = refs[2 + n]
        acc = refs[3 + 2 * n]
        tt = pl.program_id(1)
        finish = lambda: None
        if n:
            blk = pl.program_id(0)
            finish = _run_at(jnp.logical_and(blk == 0, tt == 0), None,
                             jnp.logical_and(blk == nf - 1, tt == pl.num_programs(1) - 1),
                             _reduce_phases(refs[2:2 + n], refs[3 + n:3 + 2 * n], *refs[4 + 2 * n:], hops=HOPS_DIAGONAL))
        av = a_ref[...]
        if square_a:
            af = av.astype(F32)
            av = (af * af).astype(BF16)
        part = _tn(av, b_ref[...])

        @pl.when(tt == 0)
        def _():
            acc[...] = part

        @pl.when(tt > 0)
        def _():
            acc[...] += part

        @pl.when(tt == pl.num_programs(1) - 1)
        def _():
            o_ref[0] = acc[...].astype(BF16)

        finish()

    whole = pl.BlockSpec((ts, 1024), lambda c, tt: (tt, 0))
    chunk = pl.BlockSpec((ts, 1024), lambda c, tt: (tt, c))
    return pl.pallas_call(
        body,
        name=name,
        grid=(nf, S // ts),
        in_specs=([chunk, whole] if a_chunked else [whole, chunk]) + [ANY] * n,
        out_specs=[pl.BlockSpec((1, 1024, 1024), lambda c, tt: (c, 0, 0))] + [ANY] * n,
        out_shape=[jax.ShapeDtypeStruct((nf, 1024, 1024), BF16)] + _reduce_shapes(partials),
        scratch_shapes=[pltpu.VMEM((1024, 1024), F32)] + (_reduce_sems(n) if n else []),
        compiler_params=_params(),
    )(a, b, *partials)


def _out_bwd(dh1, w_out, mixed, y_pool, y_attn, g_pool, g_attn, partials=(), received=()):
    S = dh1.shape[0]
    ts = ROW_TILE
    n = len(partials)

    def body(*refs):
        dh1_ref, w_ref, mixed_ref, yp_ref, ya_ref, gp_ref, ga_ref = refs[:7]
        dyp_ref, dya_ref, dw_ref, dgp_ref, dga_ref = refs[7 + 2 * n:12 + 2 * n]
        dw_acc = refs[12 + 3 * n]
        i = pl.program_id(0)
        finish = lambda: None
        if n:
            finish = _run_at(i == 0, None, i == pl.num_programs(0) - 1,
                             _reduce_phases(refs[7:7 + n], refs[12 + 2 * n:12 + 3 * n], *refs[13 + 3 * n:],
                                            hops=HOPS_SAME_CORE))
        dh1b = dh1_ref[...].astype(BF16)
        dmixed = _nt(dh1b, w_ref[...])
        dw = _tn(mixed_ref[...], dh1b)

        @pl.when(i == 0)
        def _():
            dw_acc[...] = dw
            dgp_ref[...] = jnp.zeros_like(dgp_ref)
            dga_ref[...] = jnp.zeros_like(dga_ref)

        @pl.when(i > 0)
        def _():
            dw_acc[...] += dw

        @pl.when(i == pl.num_programs(0) - 1)
        def _():
            dw_ref[...] = dw_acc[...].astype(BF16)

        for y_ref, g_ref, dy_ref, dg_ref, lanes in (
                (yp_ref, gp_ref, dyp_ref, dgp_ref, slice(0, D_POOL)),
                (ya_ref, ga_ref, dya_ref, dga_ref, slice(D_POOL, D_MODEL))):
            y = y_ref[...]
            r = _rms(y)
            nrm = y * r
            dm = dmixed[:, lanes]
            dg_ref[...] += jnp.sum(dm * nrm, axis=0, keepdims=True)
            dy_ref[...] = _rms_bwd(dm, nrm, r, g_ref[...])

        finish()

    row = lambda w: pl.BlockSpec((ts, w), lambda i: (i, 0))
    vec = lambda w: pl.BlockSpec((1, w), lambda i: (0, 0))
    full = pl.BlockSpec((D_MODEL, D_MODEL), lambda i: (0, 0))
    return pl.pallas_call(
        body,
        name="out_bwd",
        grid=(S // ts,),
        in_specs=[row(D_MODEL), full, row(D_MODEL), row(D_POOL), row(D_ATTN), vec(D_POOL), vec(D_ATTN)]
        + [ANY] * (2 * n),
        out_specs=[row(D_POOL), row(D_ATTN), full, vec(D_POOL), vec(D_ATTN)] + [ANY] * n,
        out_shape=[
            jax.ShapeDtypeStruct((S, D_POOL), F32),
            jax.ShapeDtypeStruct((S, D_ATTN), F32),
            jax.ShapeDtypeStruct((D_MODEL, D_MODEL), BF16),
            jax.ShapeDtypeStruct((1, D_POOL), F32),
            jax.ShapeDtypeStruct((1, D_ATTN), F32),
        ] + [jax.ShapeDtypeStruct(r.shape, r.dtype) for r in received],
        input_output_aliases={7 + n + a: 5 + a for a in range(n)},
        scratch_shapes=[pltpu.VMEM((D_MODEL, D_MODEL), F32)] + (_reduce_sems(n) if n else []),
        compiler_params=_params(),
    )(dh1, w_out, mixed, y_pool, y_attn, g_pool, g_attn, *partials, *received)


def _attn_bwd(qkv, o, do, partials=(), late=(), late_received=()):
    S = qkv.shape[1]
    t = ATTN_TILE
    n = len(partials)
    m = len(late)
    nblk = D_ATTN // ATTN_LANES
    outs_at = 5 + n + 2 * m
    scratch_at = outs_at + 3 + n + m

    def body(*refs):
        q_ref, k_ref, v_ref, o_ref, do_ref = refs[:5]
        dq_ref, dk_ref, dv_ref = refs[outs_at:outs_at + 3]
        kh_ref, dk_acc, dv_acc, dq_acc, z_ref, da_ref = refs[scratch_at:scratch_at + 6]
        hp = pl.program_id(0)
        i = pl.program_id(1)
        first = jnp.logical_and(hp == 0, i == 0)
        last = jnp.logical_and(hp == nblk - 1, i == pl.num_programs(1) - 1)
        finishes = []
        if n:
            finishes.append(_run_at(first, None, last, _reduce_phases(
                refs[5:5 + n], refs[outs_at + 3:outs_at + 3 + n], *refs[scratch_at + 6:scratch_at + 8])))
        if m:
            finishes.append(_run_at(first, None, last, _reduce_phases(
                refs[5 + n:5 + n + m], refs[outs_at + 3 + n:outs_at + 3 + n + m],
                *refs[scratch_at + 6 + (2 if n else 0):], hops=HOPS_REST)))

        def finish():
            for f in finishes:
                f()

        masks = _head_masks()

        @pl.when(i == 0)
        def _():
            kk = k_ref[0]
            for h in range(HEADS_PER_BLOCK):
                kh_ref[h] = jnp.where(masks[h], kk, jnp.zeros_like(kk))
            dk_acc[...] = jnp.zeros_like(dk_acc)
            dv_acc[...] = jnp.zeros_like(dv_acc)

        row, col = _tri_masks(t)
        later = (row > col).astype(BF16)
        from_s = (row >= col).astype(BF16)
        causal = col < row
        qs = q_ref[0] * ATTN_SCALE
        dob = do_ref[...].astype(BF16)
        d_all = dob.astype(F32) * o_ref[...]
        qh = [jnp.where(masks[h], qs, jnp.zeros_like(qs)) for h in range(HEADS_PER_BLOCK)]
        doh = [jnp.where(masks[h], dob, jnp.zeros_like(dob)) for h in range(HEADS_PER_BLOCK)]
        d_row = [jnp.sum(jnp.where(masks[h], d_all, 0.0), axis=1, keepdims=True) for h in range(HEADS_PER_BLOCK)]

        heads = range(HEADS_PER_BLOCK)

        def scores(j, slot):
            keys = pl.ds(pl.multiple_of(j * t, t), t)
            kj = k_ref[0, keys, :]
            vj = v_ref[0, keys, :]
            for h in heads:
                z_ref[slot, h] = _nt(qh[h], kj)
                da_ref[slot, h] = _nt(doh[h], vj)

        def tiles(walk, left, after, carry):
            nh = HEADS_PER_BLOCK
            c_l, c_g = list(carry[:nh]), list(carry[nh:])
            ls, tail, g, before = {}, {}, {}, {}
            for w, (j, slot, diag, counts) in enumerate(walk):
                for h in heads:
                    ls[w, h], l1m = _log_sigmoids(z_ref[slot, h])
                    if diag:
                        l1m = jnp.where(causal, l1m, 0.0)
                    hi, lo = _split_bf16(l1m)
                    tail[w, h] = _nn(hi, later) + _nn(lo, later) + c_l[h]
                    c_l[h] = c_l[h] + jnp.sum(l1m, axis=1, keepdims=True)
            top = c_l[0]
            for h in range(1, nh):
                top = jnp.maximum(top, c_l[h])
            go = jnp.logical_and(left > 0, jnp.max(top) > DEAD_LOG)
            for w, (j, slot, diag, counts) in enumerate(walk):
                keys = pl.ds(pl.multiple_of(j * t, t), t)
                dv = None
                for h in heads:
                    a = jnp.exp(ls[w, h] + tail[w, h])
                    if diag:
                        a = jnp.where(causal, a, 0.0)
                    if counts is not None:
                        a = jnp.where(counts, a, 0.0)
                    ab = a.astype(BF16)
                    g[w, h] = ab.astype(F32) * da_ref[slot, h]
                    ghi, glo = _split_bf16(g[w, h])
                    before[w, h] = d_row[h] - (_nn(ghi, from_s) + _nn(glo, from_s) + c_g[h])
                    c_g[h] = c_g[h] + jnp.sum(g[w, h], axis=1, keepdims=True)
                    part = _tn(ab, doh[h])
                    dv = part if dv is None else dv + part
                dv_acc[keys, :] += dv
            scores(*after)
            for w, (j, slot, diag, counts) in enumerate(walk):
                keys = pl.ds(pl.multiple_of(j * t, t), t)
                dk = None
                for h in heads:
                    beta = jnp.exp(ls[w, h])
                    dz = g[w, h] * (1.0 - beta) - before[w, h] * beta
                    if diag:
                        dz = jnp.where(causal, dz, 0.0)
                    if counts is not None:
                        dz = jnp.where(counts, dz, 0.0)
                    dzb = dz.astype(BF16)
                    dqh = _nn(dzb, kh_ref[h, keys, :])
                    if diag:
                        dq_acc[h] = dqh
                    else:
                        dq_acc[h] += dqh
                    part = _tn(dzb, qh[h])
                    dk = part if dk is None else dk + part
                dk_acc[keys, :] += dk
            return (go, *c_l, *c_g)

        prev = jnp.maximum(i - 1, 0)
        scores(i, 0)
        scores(prev, 1)
        state = (jnp.int32(0), *tiles([(i, 0, True, None), (prev, 1, False, i >= 1)], i - 1,
                                      (jnp.maximum(i - 2, 0), 0),
                                      [jnp.zeros((t, 1), F32)] * (2 * HEADS_PER_BLOCK)))

        def step(state):
            jj = state[0]
            j = i - 2 - jj
            return (jj + 1, *tiles([(j, jj % 2, False, None)], j, (jnp.maximum(j - 1, 0), 1 - jj % 2), state[2:]))

        lax.while_loop(lambda s: s[1], step, state)
        dq = dq_acc[0]
        for h in range(1, HEADS_PER_BLOCK):
            dq = dq + dq_acc[h]
        dq_ref[...] = (dq * ATTN_SCALE).astype(BF16)

        @pl.when(i == pl.num_programs(1) - 1)
        def _():
            dk_ref[...] = dk_acc[...].astype(BF16)
            dv_ref[...] = dv_acc[...].astype(BF16)

        finish()

    qtile = pl.BlockSpec((t, ATTN_LANES), lambda hp, i: (i, hp))
    whole = pl.BlockSpec((S, ATTN_LANES), lambda hp, i: (0, hp))
    return pl.pallas_call(
        body,
        name="attn_bwd",
        grid=(nblk, S // t),
        in_specs=[
            pl.BlockSpec((1, t, ATTN_LANES), lambda hp, i: (0, i, hp)),
            pl.BlockSpec((1, S, ATTN_LANES), lambda hp, i: (1, 0, hp)),
            pl.BlockSpec((1, S, ATTN_LANES), lambda hp, i: (2, 0, hp)),
            qtile, qtile,
        ] + [ANY] * (n + 2 * m),
        out_specs=[qtile, whole, whole] + [ANY] * (n + m),
        out_shape=[jax.ShapeDtypeStruct((S, D_ATTN), BF16)] * 3 + _reduce_shapes(partials)
        + [jax.ShapeDtypeStruct(r.shape, r.dtype) for r in late_received],
        input_output_aliases={5 + n + m + a: 3 + n + a for a in range(m)},
        scratch_shapes=[
            pltpu.VMEM((HEADS_PER_BLOCK, S, ATTN_LANES), BF16),
            pltpu.VMEM((S, ATTN_LANES), F32),
            pltpu.VMEM((S, ATTN_LANES), F32),
            pltpu.VMEM((HEADS_PER_BLOCK, t, ATTN_LANES), F32),
            pltpu.VMEM((2, HEADS_PER_BLOCK, t, t), F32),
            pltpu.VMEM((2, HEADS_PER_BLOCK, t, t), F32),
        ] + (_reduce_sems(n) if n else []) + (_reduce_sems(m) if m else []),
        compiler_params=_params(),
    )(qkv, qkv, qkv, o, do, *partials, *late, *late_received)


def _pool_bwd_w(u, dyp, pool_w, pool_scale):
    S = u.shape[0]
    ts = ROW_TILE
    hb = ts // POOL_HALO

    def body(u_ref, halo_ref, dy_ref, w_ref, s_ref, dp_ref, dw_ref, ds_ref):
        i = pl.program_id(0)
        halo = jnp.where(i == 0, 0.0, halo_ref[...])
        pooled = _pooled(u_ref[...], halo, i * ts)

        @pl.when(i == 0)
        def _():
            dw_ref[...] = jnp.zeros_like(dw_ref)
            ds_ref[...] = jnp.zeros_like(ds_ref)

        for g in range(len(POOL_WINDOWS)):
            lanes = slice(g * POOL_GROUP_DIM, (g + 1) * POOL_GROUP_DIM)
            pg = pooled[g].astype(BF16)
            dy = dy_ref[:, lanes]
            ds_ref[:, lanes] += jnp.sum(dy * _nn(pg, w_ref[g]), axis=0, keepdims=True)
            dmapped = (dy * s_ref[:, lanes]).astype(BF16)
            dp_ref[:, lanes] = _nt(dmapped, w_ref[g])
            dw_ref[g] += _tn(pg, dmapped)

    row = pl.BlockSpec((ts, D_POOL), lambda i: (i, 0))
    vec = pl.BlockSpec((1, D_POOL), lambda i: (0, 0))
    wspec = pl.BlockSpec((4, POOL_GROUP_DIM, POOL_GROUP_DIM), lambda i: (0, 0, 0))
    return pl.pallas_call(
        body,
        name="pool_bwd_w",
        grid=(S // ts,),
        in_specs=[row, pl.BlockSpec((POOL_HALO, D_POOL), lambda i: (jnp.maximum(i * hb - 1, 0), 0)),
                  row, wspec, vec],
        out_specs=[row, wspec, vec],
        out_shape=[
            jax.ShapeDtypeStruct((S, D_POOL), F32),
            jax.ShapeDtypeStruct((4, POOL_GROUP_DIM, POOL_GROUP_DIM), F32),
            jax.ShapeDtypeStruct((1, D_POOL), F32),
        ],
        compiler_params=_params(),
    )(u, u, dyp, pool_w, pool_scale)


def _pool_bwd_u(dpooled):
    S = dpooled.shape[0]
    ts = ROW_TILE
    hb = ts // POOL_HALO
    last = S // ts - 1

    def body(dp_ref, halo_ref, du_ref):
        i = pl.program_id(0)
        dp = dp_ref[...]
        halo = jnp.where(i == last, 0.0, halo_ref[...])
        inv = _pool_counts(i * ts, ts)
        n = ts + POOL_HALO
        for g, w in enumerate(POOL_WINDOWS):
            lanes = slice(g * POOL_GROUP_DIM, (g + 1) * POOL_GROUP_DIM)
            dg = dp[:, lanes]
            acc = jnp.concatenate([dg * inv[g], halo[:, lanes] * (1.0 / w)], axis=0)
            shift = 1
            while shift < w:
                acc = acc + pltpu.roll(acc, n - shift, axis=0)
                shift *= 2
            du_ref[:, lanes] = (acc[:ts, :] - dg).astype(BF16)

    return pl.pallas_call(
        body,
        name="pool_bwd_u",
        grid=(S // ts,),
        in_specs=[pl.BlockSpec((ts, D_POOL), lambda i: (i, 0)),
                  pl.BlockSpec((POOL_HALO, D_POOL), lambda i: (jnp.minimum((i + 1) * hb, (last + 1) * hb - 1), 0))],
        out_specs=pl.BlockSpec((ts, D_POOL), lambda i: (i, 0)),
        out_shape=jax.ShapeDtypeStruct((S, D_POOL), BF16),
        compiler_params=_params(),
    )(dpooled, dpooled)


def _in_bwd_dw(hn, dprojs):
    S = hn.shape[0]
    ts = min(DW_TOKEN_TILE, S)

    def body(hn_ref, du_ref, dq_ref, dk_ref, dv_ref, o_ref, acc):
        j = pl.program_id(0)
        tt = pl.program_id(1)
        for k, dp_ref in enumerate((du_ref, dq_ref, dk_ref, dv_ref)):
            @pl.when(j == k)
            def _(dp_ref=dp_ref):
                part = _tn(hn_ref[...], dp_ref[...])

                @pl.when(tt == 0)
                def _():
                    acc[...] = part

                @pl.when(tt > 0)
                def _():
                    acc[...] += part

        @pl.when(tt == pl.num_programs(1) - 1)
        def _():
            o_ref[0] = acc[...].astype(BF16)

    def taken_at(k):
        return lambda j, tt: (jnp.where(j == k, tt, 0), 0)

    return pl.pallas_call(
        body,
        name="in_bwd_dw",
        grid=(4, S // ts),
        in_specs=[pl.BlockSpec((ts, D_MODEL), lambda j, tt: (tt, 0))]
        + [pl.BlockSpec((ts, 512), taken_at(k)) for k in range(4)],
        out_specs=pl.BlockSpec((1, D_MODEL, 512), lambda j, tt: (j, 0, 0)),
        out_shape=jax.ShapeDtypeStruct((4, D_MODEL, 512), BF16),
        scratch_shapes=[pltpu.VMEM((D_MODEL, 512), F32)],
        compiler_params=_params(),
    )(hn, *dprojs)


def _in_bwd_dx(du, dq, dk, dv, w_in, x, dh1, g1, partials=()):
    S = x.shape[0]
    ts = ROW_TILE
    n = len(partials)

    def body(*refs):
        du_ref, dq_ref, dk_ref, dv_ref, w_ref, x_ref, dh1_ref, g_ref = refs[:8]
        dx_ref, dg_ref = refs[8 + n:10 + n]
        i = pl.program_id(0)
        finish = lambda: None
        if n:
            finish = _run_at(i == 0, None, i == pl.num_programs(0) - 1,
                             _reduce_phases(refs[8:8 + n], refs[10 + n:10 + 2 * n], *refs[10 + 2 * n:]))

        @pl.when(i == 0)
        def _():
            dg_ref[...] = jnp.zeros_like(dg_ref)

        dhn = None
        for j, dp_ref in enumerate((du_ref, dq_ref, dk_ref, dv_ref)):
            part = _nt(dp_ref[...], w_ref[j])
            dhn = part if dhn is None else dhn + part
        xv = x_ref[...]
        r1 = _rms(xv)
        n1 = xv * r1
        dg_ref[...] += jnp.sum(dhn * n1, axis=0, keepdims=True)
        dx_ref[...] = dh1_ref[...] + _rms_bwd(dhn, n1, r1, g_ref[...])
        finish()

    row = lambda w: pl.BlockSpec((ts, w), lambda i: (i, 0))
    vec = pl.BlockSpec((1, D_MODEL), lambda i: (0, 0))
    wspec = pl.BlockSpec((4, D_MODEL, 512), lambda i: (0, 0, 0), pipeline_mode=pl.Buffered(1))
    return pl.pallas_call(
        body,
        name="in_bwd_dx",
        grid=(S // ts,),
        in_specs=[row(512), row(512), row(512), row(512), wspec, row(D_MODEL), row(D_MODEL), vec] + [ANY] * n,
        out_specs=[row(D_MODEL), vec] + [ANY] * n,
        out_shape=[
            jax.ShapeDtypeStruct((S, D_MODEL), F32),
            jax.ShapeDtypeStruct((1, D_MODEL), F32),
        ] + _reduce_shapes(partials),
        scratch_shapes=_reduce_sems(n) if n else [],
        compiler_params=_params(),
    )(du, dq, dk, dv, w_in, x, dh1, g1, *partials)


def _pieces(g):
    return g.reshape(N_CHIPS, 2, -1, g.shape[-1])


def _local_step(x, target, w_in, pool_w, small, full=None, shards=None):
    spread = shards is not None
    if spread:
        hn, u, qkv, w_out = _in_proj(x, small["norm1_g"], w_in, shards[:1])
        w_out = w_out.reshape(D_MODEL, D_MODEL)
    else:
        hn, u, qkv = _in_proj(x, small["norm1_g"], w_in)
        w_out, w_up, w_down = full
    y_pool = _pool_fwd(u, pool_w, small["pool_scale"])
    if spread:
        y_attn, w_up, w_down = _attn_fwd(qkv, shards[1:])
        w_up = w_up.reshape(N_CHIPS, D_MODEL, 1024)
        w_down = w_down.reshape(N_CHIPS, 1024, D_MODEL)
    else:
        (y_attn,) = _attn_fwd(qkv)
    mixed, h1, hn2 = _out_proj(y_pool, y_attn, x, small["pool_out_g"], small["attn_out_g"], w_out, small["norm2_g"])
    r_act, dh2, dh2b, lsq, dgf = _mlp_fwd(hn2, h1, w_up, w_down, small["final_g"], target)

    dup, dh1, dg2 = _mlp_bwd_dx(dh2b, r_act, w_down, w_up, h1, dh2, small["norm2_g"])
    (dw_down,) = _tokens_tn("mlp_bwd_dw_down", r_act, dh2b, True, True)
    out_args = (dh1, w_out, mixed, y_pool, y_attn, small["pool_out_g"], small["attn_out_g"])
    if spread:
        dw_up, got_down = _tokens_tn("mlp_bwd_dw_up", hn2, dup, False, False, [_pieces(dw_down)])
        dyp, dya, dw_out, dgp, dga, got_down = _out_bwd(*out_args, [_pieces(dw_down)], [got_down])
        dq, dk, dv, got_out, got_up, got_down = _attn_bwd(
            qkv, y_attn, dya, [_pieces(dw_out), _pieces(dw_up)], [_pieces(dw_down)], [got_down])
    else:
        (dw_up,) = _tokens_tn("mlp_bwd_dw_up", hn2, dup, False, False)
        dyp, dya, dw_out, dgp, dga = _out_bwd(*out_args)
        dq, dk, dv = _attn_bwd(qkv, y_attn, dya)
    dpooled, dpool_w, dpool_scale = _pool_bwd_w(u, dyp, pool_w, small["pool_scale"])
    du = _pool_bwd_u(dpooled)
    dw_in = _in_bwd_dw(hn, (du, dq, dk, dv))
    if spread:
        dx, dg1, got_in = _in_bwd_dx(du, dq, dk, dv, w_in, x, dh1, small["norm1_g"], [_pieces(dw_in)])
    else:
        dx, dg1 = _in_bwd_dx(du, dq, dk, dv, w_in, x, dh1, small["norm1_g"])

    big = {"w_in": dw_in, "w_out": dw_out, "w_up": dw_up, "w_down": dw_down}
    if spread:
        big["received"] = {"w_in": got_in, "w_out": got_out, "w_up": got_up, "w_down": got_down}
    little = {"norm1_g": dg1, "pool_w": dpool_w, "pool_scale": dpool_scale, "pool_out_g": dgp,
              "attn_out_g": dga, "norm2_g": dg2, "final_g": dgf, "loss_sq": lsq}
    return dx, big, little


def _place():
    x, y, c = lax.axis_index("x"), lax.axis_index("y"), lax.axis_index("c")
    other_chips = [(1 - x, y), (x, 1 - y), (1 - x, 1 - y)]
    return x, y, c, other_chips


def _chip_index(chip):
    return 2 * chip[0] + chip[1]


def _gather_shapes(shards):
    return [jax.ShapeDtypeStruct((N_CHIPS,) + s.shape, s.dtype) for s in shards]


def _gather_sems(n):
    return [pltpu.SemaphoreType.DMA((n, 7)), pltpu.SemaphoreType.DMA((n, 7))]


def _gather_phases(ins, outs, send_sems, recv_sems):
    n = len(ins)
    x, y, c, chips = _place()
    me = _chip_index((x, y))
    sibling = (x, y, 1 - c)

    def copy(a, k, chip_idx, half, to, src=None):
        dst = outs[a].at[chip_idx, half]
        return pltpu.make_async_remote_copy(
            src_ref=dst if src is None else src, dst_ref=dst,
            send_sem=send_sems.at[a, k], recv_sem=recv_sems.at[a, k],
            device_id=to, device_id_type=MESH)

    def own_chip(a, to):
        return pltpu.make_async_remote_copy(
            src_ref=ins[a], dst_ref=outs[a].at[me],
            send_sem=send_sems.at[a, 0], recv_sem=recv_sems.at[a, 0],
            device_id=to, device_id_type=MESH)

    def first(a):
        return [own_chip(a, sibling)] + [
            copy(a, 1 + j, me, c, (*chip, c), src=ins[a].at[c]) for j, chip in enumerate(chips)]

    def passed(a, j):
        return copy(a, 4 + j, _chip_index(chips[j]), c, sibling)

    def start():
        for a in range(n):
            for cp in first(a):
                cp.start()

    def forward():
        for a in range(n):
            for j, chip in enumerate(chips):
                copy(a, 1 + j, _chip_index(chip), c, (x, y, c)).wait_recv()
                passed(a, j).start()

    def finish():
        for a in range(n):
            own_chip(a, (x, y, c)).wait_recv()
            for j, chip in enumerate(chips):
                copy(a, 4 + j, _chip_index(chip), 1 - c, (x, y, c)).wait_recv()
        for a in range(n):
            for cp in first(a):
                cp.wait_send()
            for j in range(len(chips)):
                passed(a, j).wait_send()

    return start, forward, finish


def _gather_weights(shards):
    n = len(shards)

    def body(*refs):
        start, forward, finish = _gather_phases(refs[:n], refs[n:2 * n], *refs[2 * n:])
        start()
        forward()
        finish()

    return pl.pallas_call(
        body,
        name="gather_weights",
        in_specs=[ANY] * n,
        out_specs=[ANY] * n,
        out_shape=_gather_shapes(shards),
        scratch_shapes=_gather_sems(n),
    )(*shards)


def _reduce_shapes(partials):
    return [jax.ShapeDtypeStruct((N_DEV,) + p.shape[2:], p.dtype) for p in partials]


def _reduce_sems(n):
    return [pltpu.SemaphoreType.DMA((n, N_DEV)), pltpu.SemaphoreType.DMA((n, N_DEV))]


def _reduce_phases(ins, outs, send_sems, recv_sems, hops=None):
    n = len(ins)
    x, y, c, _ = _place()
    me = 4 * x + 2 * y + c

    def to_peer(a, k):
        return pltpu.make_async_remote_copy(
            src_ref=ins[a].at[k // 2, k % 2], dst_ref=outs[a].at[me],
            send_sem=send_sems.at[a, k], recv_sem=recv_sems.at[a, me],
            device_id=(k // 4, (k // 2) % 2, k % 2), device_id_type=MESH)

    def from_peer(a, k):
        return pltpu.make_async_remote_copy(
            src_ref=ins[a].at[k // 2, k % 2], dst_ref=outs[a].at[k],
            send_sem=send_sems.at[a, k], recv_sem=recv_sems.at[a, k],
            device_id=(x, y, c), device_id_type=MESH)

    def taken(k):
        if hops is None:
            return k != me
        hit = (k ^ hops[0]) == me
        for d in hops[1:]:
            hit = jnp.logical_or(hit, (k ^ d) == me)
        return hit

    def start():
        for a in range(n):
            for k in range(N_DEV):
                @pl.when(taken(k))
                def _(a=a, k=k):
                    to_peer(a, k).start()

    def finish():
        for a in range(n):
            for k in range(N_DEV):
                @pl.when(taken(k))
                def _(a=a, k=k):
                    from_peer(a, k).wait_recv()
                    to_peer(a, k).wait_send()

    return start, finish


def _sum_pieces(where, parts, own):
    _, h, cols = parts.shape
    hb = min(h, 256)

    def body(where_ref, *refs):
        me = where_ref[0]
        acc = None
        for k in range(N_DEV):
            piece = jnp.where(k == me, refs[N_DEV][0, 0], refs[k][0]).astype(F32)
            acc = piece if acc is None else acc + piece
        refs[N_DEV + 1][0] = acc

    def sent_by(k):
        return lambda r, w: (jnp.where(w[0] == k, (k + 1) % N_DEV, k), r, 0)

    return pl.pallas_call(
        body,
        name="sum_pieces",
        grid_spec=pltpu.PrefetchScalarGridSpec(
            num_scalar_prefetch=1,
            grid=(h // hb,),
            in_specs=[pl.BlockSpec((1, hb, cols), sent_by(k)) for k in range(N_DEV)]
            + [pl.BlockSpec((1, 1, hb, cols), lambda r, w: (w[1], w[2], r, 0))],
            out_specs=pl.BlockSpec((1, hb, cols), lambda r, w: (w[2], r, 0)),
        ),
        out_shape=jax.ShapeDtypeStruct((2, h, cols), F32),
    )(where, *([parts] * N_DEV), own)


def _join_halves(halves):
    n = len(halves)

    def body(*refs):
        outs = refs[n:2 * n]
        send_sems, recv_sems = refs[2 * n:]
        x, y, c, _ = _place()
        sends = [
            pltpu.make_async_remote_copy(
                src_ref=outs[a].at[c], dst_ref=outs[a].at[c],
                send_sem=send_sems.at[a], recv_sem=recv_sems.at[a],
                device_id=(x, y, 1 - c), device_id_type=MESH)
            for a in range(n)]
        for cp in sends:
            cp.start()
        for a in range(n):
            pltpu.make_async_remote_copy(
                src_ref=outs[a].at[c], dst_ref=outs[a].at[1 - c],
                send_sem=send_sems.at[a], recv_sem=recv_sems.at[a],
                device_id=(x, y, c), device_id_type=MESH).wait_recv()
        for cp in sends:
            cp.wait_send()

    return pl.pallas_call(
        body,
        name="join_halves",
        in_specs=[ANY] * n,
        out_specs=[ANY] * n,
        out_shape=[jax.ShapeDtypeStruct(s.shape, s.dtype) for s in halves],
        input_output_aliases={a: a for a in range(n)},
        scratch_shapes=[pltpu.SemaphoreType.DMA((n,)), pltpu.SemaphoreType.DMA((n,))],
    )(*halves)


def _adamw(w, g, m, v):
    m = ADAM_B1 * m + (1.0 - ADAM_B1) * g
    v = ADAM_B2 * v + (1.0 - ADAM_B2) * jnp.square(g)
    m_hat = m / (1.0 - ADAM_B1 ** ADAM_STEP)
    v_hat = v / (1.0 - ADAM_B2 ** ADAM_STEP)
    delta = -ADAM_LR * (m_hat / (jnp.sqrt(v_hat) + ADAM_EPS) + ADAM_WD * w)
    return delta, m, v


def _adamw_big(w, g, m, v):
    rows, cols = w.shape
    rb = min(rows, 256)

    def body(w_ref, g_ref, m_ref, v_ref, d_ref, mo_ref, vo_ref):
        d_ref[...], mo_ref[...], vo_ref[...] = _adamw(w_ref[...], g_ref[...], m_ref[...], v_ref[...])

    blk = pl.BlockSpec((rb, cols), lambda r: (r, 0))
    return pl.pallas_call(
        body,
        name="adamw_big",
        grid=(rows // rb,),
        in_specs=[blk] * 4,
        out_specs=[blk] * 3,
        out_shape=[jax.ShapeDtypeStruct(w.shape, F32)] * 3,
    )(w, g, m, v)


SMALL_ORDER = ("pool_w", "norm1_g", "pool_scale", "pool_out_g", "attn_out_g", "norm2_g", "final_g")
SUBLANES = 8


def _pack(parts):
    rows = []
    for p in parts:
        p = p.reshape(-1, LANES)
        pad = (-p.shape[0]) % SUBLANES
        if pad:
            p = jnp.pad(p, ((0, pad), (0, 0)))
        rows.append(p)
    return jnp.concatenate(rows, axis=0)


def _unpack(slab, shapes):
    out, r = [], 0
    for shp in shapes:
        size = 1
        for d in shp:
            size *= d
        nrow = size // LANES
        out.append(slab[r:r + nrow].reshape(shp))
        r += nrow + (-nrow) % SUBLANES
    return out


def _small_step(partials, w, m, v, loss_rows):
    rows = partials.shape[0]

    def body(p_ref, w_ref, m_ref, v_ref, g_ref, d_ref, mo_ref, vo_ref, loss_ref, buf, send_sems, recv_sems):
        x, y, c, _ = _place()
        me = 4 * x + 2 * y + c
        for k in range(N_DEV):
            @pl.when(k != me)
            def _(k=k):
                pltpu.make_async_remote_copy(
                    src_ref=p_ref, dst_ref=buf.at[me],
                    send_sem=send_sems.at[k], recv_sem=recv_sems.at[me],
                    device_id=(k // 4, (k // 2) % 2, k % 2), device_id_type=MESH).start()
        buf[me] = p_ref[...]
        for k in range(N_DEV):
            @pl.when(k != me)
            def _(k=k):
                pltpu.make_async_remote_copy(
                    src_ref=p_ref, dst_ref=buf.at[k],
                    send_sem=send_sems.at[k], recv_sem=recv_sems.at[k],
                    device_id=(x, y, c), device_id_type=MESH).wait()
        g = buf[0]
        for k in range(1, N_DEV):
            g = g + buf[k]
        g_ref[...] = g
        d_ref[...], mo_ref[...], vo_ref[...] = _adamw(w_ref[...], g, m_ref[...], v_ref[...])
        loss = (0.5 / D_MODEL) * jnp.sum(g[rows - loss_rows:, :])
        loss_ref[...] = jnp.full(loss_ref.shape, loss, F32)

    vm = pl.BlockSpec(memory_space=pltpu.VMEM)
    slab = jax.ShapeDtypeStruct((rows, LANES), F32)
    return pl.pallas_call(
        body,
        name="small_step",
        in_specs=[vm] * 4,
        out_specs=[vm] * 5,
        out_shape=[slab, slab, slab, slab, jax.ShapeDtypeStruct((SUBLANES, LANES), F32)],
        scratch_shapes=[pltpu.VMEM((N_DEV, rows, LANES), F32),
                        pltpu.SemaphoreType.DMA((N_DEV,)), pltpu.SemaphoreType.DMA((N_DEV,))],
    )(partials, w, m, v)


BIG_ORDER = ("w_in", "w_out", "w_up", "w_down")
WEIGHT_ORDER = ("norm1_g", "w_in", "pool_w", "pool_scale", "pool_out_g", "attn_out_g", "w_out", "norm2_g",
                "w_up", "w_down", "final_g")


def _halves(a):
    return a.reshape(2, a.shape[0] // 2, a.shape[1])


def kernel(x, norm1_g, w_in, pool_w, pool_scale, pool_out_g, attn_out_g, w_out, norm2_g, w_up, w_down, final_g, loss_target, m_norm1_g, m_w_in, m_pool_w, m_pool_scale, m_pool_out_g, m_attn_out_g, m_w_out, m_norm2_g, m_w_up, m_w_down, m_final_g, v_norm1_g, v_w_in, v_pool_w, v_pool_scale, v_pool_out_g, v_attn_out_g, v_w_out, v_norm2_g, v_w_up, v_w_down, v_final_g):
    w = dict(norm1_g=norm1_g, w_in=w_in, pool_w=pool_w, pool_scale=pool_scale, pool_out_g=pool_out_g,
             attn_out_g=attn_out_g, w_out=w_out, norm2_g=norm2_g, w_up=w_up, w_down=w_down, final_g=final_g)
    m = dict(norm1_g=m_norm1_g, w_in=m_w_in, pool_w=m_pool_w, pool_scale=m_pool_scale, pool_out_g=m_pool_out_g,
             attn_out_g=m_attn_out_g, w_out=m_w_out, norm2_g=m_norm2_g, w_up=m_w_up, w_down=m_w_down,
             final_g=m_final_g)
    v = dict(norm1_g=v_norm1_g, w_in=v_w_in, pool_w=v_pool_w, pool_scale=v_pool_scale, pool_out_g=v_pool_out_g,
             attn_out_g=v_attn_out_g, w_out=v_w_out, norm2_g=v_norm2_g, w_up=v_w_up, w_down=v_w_down,
             final_g=v_final_g)

    shards = {n: _halves(w[n].astype(BF16)) for n in BIG_ORDER}
    (w_in_g,) = _gather_weights([shards["w_in"]])
    small = {n: w[n].reshape(1, -1) for n in ("norm1_g", "pool_scale", "pool_out_g", "attn_out_g", "norm2_g", "final_g")}
    dx, big, little = _local_step(
        x[0], loss_target[0], w_in_g.reshape(N_CHIPS, D_MODEL, 512), pool_w.astype(BF16), small,
        shards=[shards["w_out"], shards["w_up"], shards["w_down"]])

    grads, deltas, new_m, new_v = {}, {}, {}, {}
    loss_rows = D_MODEL // LANES
    slab_g = _pack([little[n] for n in SMALL_ORDER] + [little["loss_sq"]])
    zeros = jnp.zeros((loss_rows, LANES), F32)
    slab_w = _pack([w[n] for n in SMALL_ORDER] + [zeros])
    slab_m = _pack([m[n] for n in SMALL_ORDER] + [zeros])
    slab_v = _pack([v[n] for n in SMALL_ORDER] + [zeros])
    received = big.pop("received")
    g_s, d_s, m_s, v_s, loss = _small_step(slab_g, slab_w, slab_m, slab_v, loss_rows)
    shapes = [w[n].shape for n in SMALL_ORDER]
    for slab, dst in ((g_s, grads), (d_s, deltas), (m_s, new_m), (v_s, new_v)):
        for n, val in zip(SMALL_ORDER, _unpack(slab, shapes)):
            dst[n] = val

    xi, yi, ci = lax.axis_index("x"), lax.axis_index("y"), lax.axis_index("c")
    where = jnp.stack([4 * xi + 2 * yi + ci, 2 * xi + yi, ci]).astype(jnp.int32)
    full = _join_halves([_sum_pieces(where, received[n], _pieces(big[n])) for n in BIG_ORDER])
    for n, g in zip(BIG_ORDER, full):
        grads[n] = g.reshape(w[n].shape)
        deltas[n], new_m[n], new_v[n] = _adamw_big(w[n], grads[n], m[n], v[n])

    return (loss[0, 0], dx[None], *[grads[n] for n in WEIGHT_ORDER], *[deltas[n] for n in WEIGHT_ORDER],
            *[new_m[n] for n in WEIGHT_ORDER], *[new_v[n] for n in WEIGHT_ORDER])
```

```python
import functools

import jax
import jax.numpy as jnp
from jax import lax
from jax.experimental import pallas as pl
from jax.experimental.pallas import tpu as pltpu

F32 = jnp.float32
BF16 = jnp.bfloat16

D_MODEL = 1024
D_POOL = 512
D_ATTN = 512
POOL_WINDOWS = (2, 4, 8, 16)
POOL_GROUP_DIM = 128
POOL_HALO = 16
HEAD_DIM = 64
HEADS_PER_BLOCK = 4
ATTN_LANES = HEADS_PER_BLOCK * HEAD_DIM
D_FF = 4096
N_CHIPS = 4
N_DEV = 8
EPS = 1e-6
ATTN_SCALE = 0.125
ATTN_TILE = 256
ATTN_ROW_CHUNKS = 1
DEAD_LOG = -105.0
ROW_TILE = 512
MLP_TILE = 512
DW_TOKEN_TILE = 2048
LANES = 128

ADAM_LR = 0.001
ADAM_B1 = 0.9
ADAM_B2 = 0.999
ADAM_EPS = 1e-08
ADAM_WD = 0.01
ADAM_STEP = 10

HOPS_DIAGONAL = (6, 7)
HOPS_SAME_CORE = (2, 4)
HOPS_REST = (1, 3, 5)

MESH = pl.DeviceIdType.MESH
ANY = pl.BlockSpec(memory_space=pl.ANY)
VMEM_LIMIT = 56 * 1024 * 1024


def _nn(a, b):
    return jnp.dot(a, b, preferred_element_type=F32)


def _nt(a, b):
    return lax.dot_general(a, b, (((1,), (1,)), ((), ())), preferred_element_type=F32)


def _tn(a, b):
    return lax.dot_general(a, b, (((0,), (0,)), ((), ())), preferred_element_type=F32)


def _rms(x):
    return lax.rsqrt(jnp.mean(x * x, axis=-1, keepdims=True) + EPS)


def _rms_bwd(dy, n, r, g):
    dn = dy * g
    return r * (dn - n * jnp.mean(dn * n, axis=-1, keepdims=True))


def _params(**kw):
    return pltpu.CompilerParams(vmem_limit_bytes=VMEM_LIMIT, **kw)


def _run_at(first, middle, last, phases):
    pl.when(first)(phases[0])
    if len(phases) == 3:
        pl.when(middle)(phases[1])
    return lambda: pl.when(last)(phases[-1])


def _in_proj(x, g1, w_in, shards=()):
    S = x.shape[0]
    ts = ROW_TILE
    n = len(shards)

    def body(*refs):
        x_ref, g_ref, w_ref = refs[:3]
        hn_ref, u_ref, qkv_ref = refs[3 + n:6 + n]
        i = pl.program_id(0)
        finish = lambda: None
        if n:
            steps = pl.num_programs(0)
            finish = _run_at(i == 0, i == jnp.maximum(steps - 2, 0), i == steps - 1,
                             _gather_phases(refs[3:3 + n], refs[6 + n:6 + 2 * n], *refs[6 + 2 * n:]))

        xf = x_ref[...]
        hn = (xf * _rms(xf) * g_ref[...]).astype(BF16)
        hn_ref[...] = hn
        u_ref[...] = _nn(hn, w_ref[0])
        for j in range(1, 4):
            qkv_ref[j - 1] = _nn(hn, w_ref[j]).astype(BF16)

        finish()

    return pl.pallas_call(
        body,
        name="in_proj",
        grid=(S // ts,),
        in_specs=[
            pl.BlockSpec((ts, D_MODEL), lambda i: (i, 0)),
            pl.BlockSpec((1, D_MODEL), lambda i: (0, 0)),
            pl.BlockSpec((4, D_MODEL, 512), lambda i: (0, 0, 0), pipeline_mode=pl.Buffered(1)),
        ] + [ANY] * n,
        out_specs=[
            pl.BlockSpec((ts, D_MODEL), lambda i: (i, 0)),
            pl.BlockSpec((ts, D_POOL), lambda i: (i, 0)),
            pl.BlockSpec((3, ts, 512), lambda i: (0, i, 0)),
        ] + [ANY] * n,
        out_shape=[
            jax.ShapeDtypeStruct((S, D_MODEL), BF16),
            jax.ShapeDtypeStruct((S, D_POOL), F32),
            jax.ShapeDtypeStruct((3, S, 512), BF16),
        ] + _gather_shapes(shards),
        scratch_shapes=_gather_sems(n) if n else [],
        compiler_params=_params(),
    )(x, g1, w_in, *shards)


def _pool_counts(first_row, rows):
    t = first_row + lax.broadcasted_iota(jnp.int32, (rows, 1), 0)
    return [1.0 / jnp.minimum(t + 1, w).astype(F32) for w in POOL_WINDOWS]


def _pooled(u_tile, halo, first_row):
    ts = u_tile.shape[0]
    inv = _pool_counts(first_row, ts)
    outs = []
    for g, w in enumerate(POOL_WINDOWS):
        lanes = slice(g * POOL_GROUP_DIM, (g + 1) * POOL_GROUP_DIM)
        xg = u_tile[:, lanes]
        acc = jnp.concatenate([halo[:, lanes], xg], axis=0)
        shift = 1
        while shift < w:
            acc = acc + pltpu.roll(acc, shift, axis=0)
            shift *= 2
        outs.append(acc[POOL_HALO:, :] * inv[g] - xg)
    return outs


def _pool_fwd(u, pool_w, pool_scale):
    S = u.shape[0]
    ts = ROW_TILE
    hb = ts // POOL_HALO

    def body(u_ref, halo_ref, w_ref, s_ref, y_ref):
        i = pl.program_id(0)
        halo = jnp.where(i == 0, 0.0, halo_ref[...])
        pooled = _pooled(u_ref[...], halo, i * ts)
        for g in range(len(POOL_WINDOWS)):
            lanes = slice(g * POOL_GROUP_DIM, (g + 1) * POOL_GROUP_DIM)
            y_ref[:, lanes] = _nn(pooled[g].astype(BF16), w_ref[g]) * s_ref[:, lanes]

    return pl.pallas_call(
        body,
        name="pool_fwd",
        grid=(S // ts,),
        in_specs=[
            pl.BlockSpec((ts, D_POOL), lambda i: (i, 0)),
            pl.BlockSpec((POOL_HALO, D_POOL), lambda i: (jnp.maximum(i * hb - 1, 0), 0)),
            pl.BlockSpec((4, POOL_GROUP_DIM, POOL_GROUP_DIM), lambda i: (0, 0, 0)),
            pl.BlockSpec((1, D_POOL), lambda i: (0, 0)),
        ],
        out_specs=pl.BlockSpec((ts, D_POOL), lambda i: (i, 0)),
        out_shape=jax.ShapeDtypeStruct((S, D_POOL), F32),
        compiler_params=_params(),
    )(u, u, pool_w, pool_scale)


def _head_masks():
    lane = lax.broadcasted_iota(jnp.int32, (1, ATTN_LANES), 1)
    return [jnp.logical_and(lane >= h * HEAD_DIM, lane < (h + 1) * HEAD_DIM) for h in range(HEADS_PER_BLOCK)]


def _tri_masks(t):
    row = lax.broadcasted_iota(jnp.int32, (t, t), 0)
    col = lax.broadcasted_iota(jnp.int32, (t, t), 1)
    return row, col


def _split_bf16(x):
    hi = x.astype(BF16)
    lo = (x - hi.astype(F32)).astype(BF16)
    return hi, lo


def _log_sigmoids(z):
    sp = jnp.log(1.0 + jnp.exp(-jnp.abs(z)))
    ls = jnp.minimum(z, 0.0) - sp
    return ls, ls - z


def _attn_fwd(qkv, shards=()):
    S = qkv.shape[1]
    t = ATTN_TILE
    n = len(shards)
    nblk = D_ATTN // ATTN_LANES

    def body(*refs):
        q_ref, k_ref, v_ref = refs[:3]
        o_ref = refs[3 + n]
        vh_ref, acc_ref, z_ref = refs[4 + 2 * n:7 + 2 * n]
        hp = pl.program_id(0)
        i = pl.program_id(1)
        finish = lambda: None
        if n:
            finish = _run_at(jnp.logical_and(hp == 0, i == 0),
                             jnp.logical_and(hp == nblk - 1, i == (3 * pl.num_programs(1)) // 4),
                             jnp.logical_and(hp == nblk - 1, i == pl.num_programs(1) - 1),
                             _gather_phases(refs[3:3 + n], refs[4 + n:4 + 2 * n], *refs[7 + 2 * n:]))
        masks = _head_masks()

        @pl.when(i == 0)
        def _():
            vv = v_ref[0]
            for h in range(HEADS_PER_BLOCK):
                vh_ref[h] = jnp.where(masks[h], vv, jnp.zeros_like(vv))

        row, col = _tri_masks(t)
        later = (row > col).astype(BF16)
        causal = col < row
        qs = q_ref[0] * ATTN_SCALE
        heads = range(HEADS_PER_BLOCK)
        qh = [jnp.where(masks[h], qs, jnp.zeros_like(qs)) for h in heads]

        def scores(j, slot):
            kj = k_ref[0, pl.ds(pl.multiple_of(j * t, t), t), :]
            for h in heads:
                z_ref[slot, h] = _nt(qh[h], kj)

        def tiles(walk, left, after, carry):
            cs = list(carry)
            ls, tail = {}, {}
            for w, (j, slot, diag, counts) in enumerate(walk):
                for h in heads:
                    ls[w, h], l1m = _log_sigmoids(z_ref[slot, h])
                    if diag:
                        l1m = jnp.where(causal, l1m, 0.0)
                    hi, lo = _split_bf16(l1m)
                    tail[w, h] = _nn(hi, later) + _nn(lo, later) + cs[h]
                    cs[h] = cs[h] + jnp.sum(l1m, axis=1, keepdims=True)
            top = cs[0]
            for h in heads[1:]:
                top = jnp.maximum(top, cs[h])
            go = jnp.logical_and(left > 0, jnp.max(top) > DEAD_LOG)
            for w, (j, slot, diag, counts) in enumerate(walk):
                keys = pl.ds(pl.multiple_of(j * t, t), t)
                for h in heads:
                    a = jnp.exp(ls[w, h] + tail[w, h])
                    if diag:
                        a = jnp.where(causal, a, 0.0)
                    if counts is not None:
                        a = jnp.where(counts, a, 0.0)
                    pv = _nn(a.astype(BF16), vh_ref[h, keys, :])
                    if diag:
                        acc_ref[h] = pv
                    else:
                        acc_ref[h] += pv
            pl.when(go)(lambda: scores(*after))
            return (go, *cs)

        before = jnp.maximum(i - 1, 0)
        scores(i, 0)
        scores(before, 1)
        state = (jnp.int32(0), *tiles([(i, 0, True, None), (before, 1, False, i >= 1)], i - 1,
                                      (jnp.maximum(i - 2, 0), 0), [jnp.zeros((t, 1), F32)] * HEADS_PER_BLOCK))

        def step(state):
            jj = state[0]
            j = i - 2 - jj
            return (jj + 1, *tiles([(j, jj % 2, False, None)], j, (jnp.maximum(j - 1, 0), 1 - jj % 2), state[2:]))

        lax.while_loop(lambda s: s[1], step, state)
        out = acc_ref[0]
        for h in heads[1:]:
            out = out + acc_ref[h]
        o_ref[...] = out
        finish()

    return pl.pallas_call(
        body,
        name="attn_fwd",
        grid=(nblk, S // t),
        in_specs=[
            pl.BlockSpec((1, t, ATTN_LANES), lambda hp, i: (0, i, hp)),
            pl.BlockSpec((1, S, ATTN_LANES), lambda hp, i: (1, 0, hp)),
            pl.BlockSpec((1, S, ATTN_LANES), lambda hp, i: (2, 0, hp)),
        ] + [ANY] * n,
        out_specs=[pl.BlockSpec((t, ATTN_LANES), lambda hp, i: (i, hp))] + [ANY] * n,
        out_shape=[jax.ShapeDtypeStruct((S, D_ATTN), F32)] + _gather_shapes(shards),
        scratch_shapes=[pltpu.VMEM((HEADS_PER_BLOCK, S, ATTN_LANES), BF16),
                        pltpu.VMEM((HEADS_PER_BLOCK * ATTN_ROW_CHUNKS, t // ATTN_ROW_CHUNKS, ATTN_LANES), F32),
                        pltpu.VMEM((2, HEADS_PER_BLOCK * ATTN_ROW_CHUNKS, t // ATTN_ROW_CHUNKS, t), F32)]
        + (_gather_sems(n) if n else []),
        compiler_params=_params(),
    )(qkv, qkv, qkv, *shards)


def _out_proj(y_pool, y_attn, x, g_pool, g_attn, w_out, g2):
    S = x.shape[0]
    ts = ROW_TILE

    def body(yp_ref, ya_ref, x_ref, gp_ref, ga_ref, w_ref, g2_ref, mixed_ref, h1_ref, hn2_ref):
        yp = yp_ref[...]
        ya = ya_ref[...]
        mixed = jnp.concatenate([yp * _rms(yp) * gp_ref[...], ya * _rms(ya) * ga_ref[...]], axis=-1).astype(BF16)
        mixed_ref[...] = mixed
        h1 = x_ref[...] + _nn(mixed, w_ref[...])
        h1_ref[...] = h1
        hn2_ref[...] = (h1 * _rms(h1) * g2_ref[...]).astype(BF16)

    row = lambda w: pl.BlockSpec((ts, w), lambda i: (i, 0))
    vec = lambda w: pl.BlockSpec((1, w), lambda i: (0, 0))
    return pl.pallas_call(
        body,
        name="out_proj",
        grid=(S // ts,),
        in_specs=[row(D_POOL), row(D_ATTN), row(D_MODEL), vec(D_POOL), vec(D_ATTN),
                  pl.BlockSpec((D_MODEL, D_MODEL), lambda i: (0, 0)), vec(D_MODEL)],
        out_specs=[row(D_MODEL), row(D_MODEL), row(D_MODEL)],
        out_shape=[
            jax.ShapeDtypeStruct((S, D_MODEL), BF16),
            jax.ShapeDtypeStruct((S, D_MODEL), F32),
            jax.ShapeDtypeStruct((S, D_MODEL), BF16),
        ],
        compiler_params=_params(),
    )(y_pool, y_attn, x, g_pool, g_attn, w_out, g2)


def _mlp_fwd(hn2, h1, w_up, w_down, g_final, target):
    S = hn2.shape[0]
    ts = MLP_TILE
    nf = D_FF // 1024

    def body(hn2_ref, h1_ref, wu_ref, wd_ref, gf_ref, tg_ref, r_ref, dh2_ref, dh2b_ref, lsq_ref, dgf_ref):
        i = pl.program_id(0)
        hn2v = hn2_ref[...]
        acts = []
        for c in range(nf):
            r = jnp.maximum(_nn(hn2v, wu_ref[c]), 0.0)
            r_ref[:, c * 1024:(c + 1) * 1024] = r.astype(BF16)
            acts.append((r * r).astype(BF16))
        h2 = h1_ref[...] + _nn(jnp.concatenate(acts, axis=1), wd_ref[...])

        @pl.when(i == 0)
        def _():
            lsq_ref[...] = jnp.zeros_like(lsq_ref)
            dgf_ref[...] = jnp.zeros_like(dgf_ref)

        rf = _rms(h2)
        n = h2 * rf
        gf = gf_ref[...]
        e = n * gf - tg_ref[...]
        lsq_ref[...] += jnp.sum(e * e, axis=0, keepdims=True)
        dy = e * (1.0 / D_MODEL)
        dgf_ref[...] += jnp.sum(dy * n, axis=0, keepdims=True)
        dh2 = _rms_bwd(dy, n, rf, gf)
        dh2_ref[...] = dh2
        dh2b_ref[...] = dh2.astype(BF16)

    row = lambda w: pl.BlockSpec((ts, w), lambda i: (i, 0))
    vec = lambda w: pl.BlockSpec((1, w), lambda i: (0, 0))
    once = pl.Buffered(1)
    return pl.pallas_call(
        body,
        name="mlp_fwd",
        grid=(S // ts,),
        in_specs=[row(D_MODEL), row(D_MODEL),
                  pl.BlockSpec((nf, D_MODEL, 1024), lambda i: (0, 0, 0), pipeline_mode=once),
                  pl.BlockSpec((D_FF, D_MODEL), lambda i: (0, 0), pipeline_mode=once),
                  vec(D_MODEL), row(D_MODEL)],
        out_specs=[row(D_FF), row(D_MODEL), row(D_MODEL), vec(D_MODEL), vec(D_MODEL)],
        out_shape=[
            jax.ShapeDtypeStruct((S, D_FF), BF16),
            jax.ShapeDtypeStruct((S, D_MODEL), F32),
            jax.ShapeDtypeStruct((S, D_MODEL), BF16),
            jax.ShapeDtypeStruct((1, D_MODEL), F32),
            jax.ShapeDtypeStruct((1, D_MODEL), F32),
        ],
        compiler_params=_params(),
    )(hn2, h1, w_up, w_down.reshape(D_FF, D_MODEL), g_final, target)


def _mlp_bwd_dx(dh2b, r_act, w_down, w_up, h1, dh2, g2):
    S = h1.shape[0]
    ts = MLP_TILE
    nf = D_FF // 1024

    def body(dh2b_ref, r_ref, wd_ref, wu_ref, h1_ref, dh2_ref, g2_ref, dup_ref, dh1_ref, dg2_ref):
        i = pl.program_id(0)
        dh2b = dh2b_ref[...]
        dhn2 = None
        for c in range(nf):
            chunk = slice(c * 1024, (c + 1) * 1024)
            dup = (_nt(dh2b, wd_ref[c]) * (2.0 * r_ref[:, chunk].astype(F32))).astype(BF16)
            dup_ref[:, chunk] = dup
            part = _nt(dup, wu_ref[c])
            dhn2 = part if dhn2 is None else dhn2 + part

        @pl.when(i == 0)
        def _():
            dg2_ref[...] = jnp.zeros_like(dg2_ref)

        h1v = h1_ref[...]
        r2 = _rms(h1v)
        n2 = h1v * r2
        dg2_ref[...] += jnp.sum(dhn2 * n2, axis=0, keepdims=True)
        dh1_ref[...] = dh2_ref[...] + _rms_bwd(dhn2, n2, r2, g2_ref[...])

    row = lambda w: pl.BlockSpec((ts, w), lambda i: (i, 0))
    vec = lambda w: pl.BlockSpec((1, w), lambda i: (0, 0))
    once = pl.Buffered(1)
    return pl.pallas_call(
        body,
        name="mlp_bwd_dx",
        grid=(S // ts,),
        in_specs=[row(D_MODEL), row(D_FF),
                  pl.BlockSpec((nf, 1024, D_MODEL), lambda i: (0, 0, 0), pipeline_mode=once),
                  pl.BlockSpec((nf, D_MODEL, 1024), lambda i: (0, 0, 0), pipeline_mode=once),
                  row(D_MODEL), row(D_MODEL), vec(D_MODEL)],
        out_specs=[row(D_FF), row(D_MODEL), vec(D_MODEL)],
        out_shape=[
            jax.ShapeDtypeStruct((S, D_FF), BF16),
            jax.ShapeDtypeStruct((S, D_MODEL), F32),
            jax.ShapeDtypeStruct((1, D_MODEL), F32),
        ],
        compiler_params=_params(),
    )(dh2b, r_act, w_down, w_up, h1, dh2, g2)


def _tokens_tn(name, a, b, a_chunked, square_a, partials=()):
    S = a.shape[0]
    ts = min(DW_TOKEN_TILE, S)
    nf = D_FF // 1024
    n = len(partials)

    def body(*refs):
        a_ref, b_ref = refs[:2]
        o_ref = refs[2 + n]
        acc = refs[3 + 2 * n]
        tt = pl.program_id(1)
        finish = lambda: None
        if n:
            blk = pl.program_id(0)
            finish = _run_at(jnp.logical_and(blk == 0, tt == 0), None,
                             jnp.logical_and(blk == nf - 1, tt == pl.num_programs(1) - 1),
                             _reduce_phases(refs[2:2 + n], refs[3 + n:3 + 2 * n], *refs[4 + 2 * n:], hops=HOPS_DIAGONAL))
        av = a_ref[...]
        if square_a:
            af = av.astype(F32)
            av = (af * af).astype(BF16)
        part = _tn(av, b_ref[...])

        @pl.when(tt == 0)
        def _():
            acc[...] = part

        @pl.when(tt > 0)
        def _():
            acc[...] += part

        @pl.when(tt == pl.num_programs(1) - 1)
        def _():
            o_ref[0] = acc[...].astype(BF16)

        finish()

    whole = pl.BlockSpec((ts, 1024), lambda c, tt: (tt, 0))
    chunk = pl.BlockSpec((ts, 1024), lambda c, tt: (tt, c))
    return pl.pallas_call(
        body,
        name=name,
        grid=(nf, S // ts),
        in_specs=([chunk, whole] if a_chunked else [whole, chunk]) + [ANY] * n,
        out_specs=[pl.BlockSpec((1, 1024, 1024), lambda c, tt: (c, 0, 0))] + [ANY] * n,
        out_shape=[jax.ShapeDtypeStruct((nf, 1024, 1024), BF16)] + _reduce_shapes(partials),
        scratch_shapes=[pltpu.VMEM((1024, 1024), F32)] + (_reduce_sems(n) if n else []),
        compiler_params=_params(),
    )(a, b, *partials)


def _out_bwd(dh1, w_out, mixed, y_pool, y_attn, g_pool, g_attn, partials=(), received=()):
    S = dh1.shape[0]
    ts = ROW_TILE
    n = len(partials)

    def body(*refs):
        dh1_ref, w_ref, mixed_ref, yp_ref, ya_ref, gp_ref, ga_ref = refs[:7]
        dyp_ref, dya_ref, dw_ref, dgp_ref, dga_ref = refs[7 + 2 * n:12 + 2 * n]
        dw_acc = refs[12 + 3 * n]
        i = pl.program_id(0)
        finish = lambda: None
        if n:
            finish = _run_at(i == 0, None, i == pl.num_programs(0) - 1,
                             _reduce_phases(refs[7:7 + n], refs[12 + 2 * n:12 + 3 * n], *refs[13 + 3 * n:],
                                            hops=HOPS_SAME_CORE))
        dh1b = dh1_ref[...].astype(BF16)
        dmixed = _nt(dh1b, w_ref[...])
        dw = _tn(mixed_ref[...], dh1b)

        @pl.when(i == 0)
        def _():
            dw_acc[...] = dw
            dgp_ref[...] = jnp.zeros_like(dgp_ref)
            dga_ref[...] = jnp.zeros_like(dga_ref)

        @pl.when(i > 0)
        def _():
            dw_acc[...] += dw

        @pl.when(i == pl.num_programs(0) - 1)
        def _():
            dw_ref[...] = dw_acc[...].astype(BF16)

        for y_ref, g_ref, dy_ref, dg_ref, lanes in (
                (yp_ref, gp_ref, dyp_ref, dgp_ref, slice(0, D_POOL)),
                (ya_ref, ga_ref, dya_ref, dga_ref, slice(D_POOL, D_MODEL))):
            y = y_ref[...]
            r = _rms(y)
            nrm = y * r
            dm = dmixed[:, lanes]
            dg_ref[...] += jnp.sum(dm * nrm, axis=0, keepdims=True)
            dy_ref[...] = _rms_bwd(dm, nrm, r, g_ref[...])

        finish()

    row = lambda w: pl.BlockSpec((ts, w), lambda i: (i, 0))
    vec = lambda w: pl.BlockSpec((1, w), lambda i: (0, 0))
    full = pl.BlockSpec((D_MODEL, D_MODEL), lambda i: (0, 0))
    return pl.pallas_call(
        body,
        name="out_bwd",
        grid=(S // ts,),
        in_specs=[row(D_MODEL), full, row(D_MODEL), row(D_POOL), row(D_ATTN), vec(D_POOL), vec(D_ATTN)]
        + [ANY] * (2 * n),
        out_specs=[row(D_POOL), row(D_ATTN), full, vec(D_POOL), vec(D_ATTN)] + [ANY] * n,
        out_shape=[
            jax.ShapeDtypeStruct((S, D_POOL), F32),
            jax.ShapeDtypeStruct((S, D_ATTN), F32),
            jax.ShapeDtypeStruct((D_MODEL, D_MODEL), BF16),
            jax.ShapeDtypeStruct((1, D_POOL), F32),
            jax.ShapeDtypeStruct((1, D_ATTN), F32),
        ] + [jax.ShapeDtypeStruct(r.shape, r.dtype) for r in received],
        input_output_aliases={7 + n + a: 5 + a for a in range(n)},
        scratch_shapes=[pltpu.VMEM((D_MODEL, D_MODEL), F32)] + (_reduce_sems(n) if n else []),
        compiler_params=_params(),
    )(dh1, w_out, mixed, y_pool, y_attn, g_pool, g_attn, *partials, *received)


def _attn_bwd(qkv, o, do, partials=(), late=(), late_received=()):
    S = qkv.shape[1]
    t = ATTN_TILE
    n = len(partials)
    m = len(late)
    nblk = D_ATTN // ATTN_LANES
    outs_at = 5 + n + 2 * m
    scratch_at = outs_at + 3 + n + m

    def body(*refs):
        q_ref, k_ref, v_ref, o_ref, do_ref = refs[:5]
        dq_ref, dk_ref, dv_ref = refs[outs_at:outs_at + 3]
        kh_ref, dk_acc, dv_acc, dq_acc, z_ref, da_ref = refs[scratch_at:scratch_at + 6]
        hp = pl.program_id(0)
        i = pl.program_id(1)
        first = jnp.logical_and(hp == 0, i == 0)
        last = jnp.logical_and(hp == nblk - 1, i == pl.num_programs(1) - 1)
        finishes = []
        if n:
            finishes.append(_run_at(first, None, last, _reduce_phases(
                refs[5:5 + n], refs[outs_at + 3:outs_at + 3 + n], *refs[scratch_at + 6:scratch_at + 8])))
        if m:
            finishes.append(_run_at(first, None, last, _reduce_phases(
                refs[5 + n:5 + n + m], refs[outs_at + 3 + n:outs_at + 3 + n + m],
                *refs[scratch_at + 6 + (2 if n else 0):], hops=HOPS_REST)))

        def finish():
            for f in finishes:
                f()

        masks = _head_masks()

        @pl.when(i == 0)
        def _():
            kk = k_ref[0]
            for h in range(HEADS_PER_BLOCK):
                kh_ref[h] = jnp.where(masks[h], kk, jnp.zeros_like(kk))
            dk_acc[...] = jnp.zeros_like(dk_acc)
            dv_acc[...] = jnp.zeros_like(dv_acc)

        row, col = _tri_masks(t)
        later = (row > col).astype(BF16)
        from_s = (row >= col).astype(BF16)
        causal = col < row
        qs = q_ref[0] * ATTN_SCALE
        dob = do_ref[...].astype(BF16)
        d_all = dob.astype(F32) * o_ref[...]
        qh = [jnp.where(masks[h], qs, jnp.zeros_like(qs)) for h in range(HEADS_PER_BLOCK)]
        doh = [jnp.where(masks[h], dob, jnp.zeros_like(dob)) for h in range(HEADS_PER_BLOCK)]
        d_row = [jnp.sum(jnp.where(masks[h], d_all, 0.0), axis=1, keepdims=True) for h in range(HEADS_PER_BLOCK)]

        heads = range(HEADS_PER_BLOCK)

        def scores(j, slot):
            keys = pl.ds(pl.multiple_of(j * t, t), t)
            kj = k_ref[0, keys, :]
            vj = v_ref[0, keys, :]
            for h in heads:
                z_ref[slot, h] = _nt(qh[h], kj)
                da_ref[slot, h] = _nt(doh[h], vj)

        def tiles(walk, left, after, carry):
            nh = HEADS_PER_BLOCK
            c_l, c_g = list(carry[:nh]), list(carry[nh:])
            ls, tail, g, before = {}, {}, {}, {}
            for w, (j, slot, diag, counts) in enumerate(walk):
                for h in heads:
                    ls[w, h], l1m = _log_sigmoids(z_ref[slot, h])
                    if diag:
                        l1m = jnp.where(causal, l1m, 0.0)
                    hi, lo = _split_bf16(l1m)
                    tail[w, h] = _nn(hi, later) + _nn(lo, later) + c_l[h]
                    c_l[h] = c_l[h] + jnp.sum(l1m, axis=1, keepdims=True)
            top = c_l[0]
            for h in range(1, nh):
                top = jnp.maximum(top, c_l[h])
            go = jnp.logical_and(left > 0, jnp.max(top) > DEAD_LOG)
            for w, (j, slot, diag, counts) in enumerate(walk):
                keys = pl.ds(pl.multiple_of(j * t, t), t)
                dv = None
                for h in heads:
                    a = jnp.exp(ls[w, h] + tail[w, h])
                    if diag:
                        a = jnp.where(causal, a, 0.0)
                    if counts is not None:
                        a = jnp.where(counts, a, 0.0)
                    ab = a.astype(BF16)
                    g[w, h] = ab.astype(F32) * da_ref[slot, h]
                    ghi, glo = _split_bf16(g[w, h])
                    before[w, h] = d_row[h] - (_nn(ghi, from_s) + _nn(glo, from_s) + c_g[h])
                    c_g[h] = c_g[h] + jnp.sum(g[w, h], axis=1, keepdims=True)
                    part = _tn(ab, doh[h])
                    dv = part if dv is None else dv + part
                dv_acc[keys, :] += dv
            for w, (j, slot, diag, counts) in enumerate(walk):
                keys = pl.ds(pl.multiple_of(j * t, t), t)
                dk = None
                for h in heads:
                    beta = jnp.exp(ls[w, h])
                    dz = g[w, h] * (1.0 - beta) - before[w, h] * beta
                    if diag:
                        dz = jnp.where(causal, dz, 0.0)
                    if counts is not None:
                        dz = jnp.where(counts, dz, 0.0)
                    dzb = dz.astype(BF16)
                    dqh = _nn(dzb, kh_ref[h, keys, :])
                    if diag:
                        dq_acc[h] = dqh
                    else:
                        dq_acc[h] += dqh
                    part = _tn(dzb, qh[h])
                    dk = part if dk is None else dk + part
                dk_acc[keys, :] += dk
            pl.when(go)(lambda: scores(*after))
            return (go, *c_l, *c_g)

        prev = jnp.maximum(i - 1, 0)
        scores(i, 0)
        scores(prev, 1)
        state = (jnp.int32(0), *tiles([(i, 0, True, None), (prev, 1, False, i >= 1)], i - 1,
                                      (jnp.maximum(i - 2, 0), 0),
                                      [jnp.zeros((t, 1), F32)] * (2 * HEADS_PER_BLOCK)))

        def step(state):
            jj = state[0]
            j = i - 2 - jj
            return (jj + 1, *tiles([(j, jj % 2, False, None)], j, (jnp.maximum(j - 1, 0), 1 - jj % 2), state[2:]))

        lax.while_loop(lambda s: s[1], step, state)
        dq = dq_acc[0]
        for h in range(1, HEADS_PER_BLOCK):
            dq = dq + dq_acc[h]
        dq_ref[...] = (dq * ATTN_SCALE).astype(BF16)

        @pl.when(i == pl.num_programs(1) - 1)
        def _():
            dk_ref[...] = dk_acc[...].astype(BF16)
            dv_ref[...] = dv_acc[...].astype(BF16)

        finish()

    qtile = pl.BlockSpec((t, ATTN_LANES), lambda hp, i: (i, hp))
    whole = pl.BlockSpec((S, ATTN_LANES), lambda hp, i: (0, hp))
    return pl.pallas_call(
        body,
        name="attn_bwd",
        grid=(nblk, S // t),
        in_specs=[
            pl.BlockSpec((1, t, ATTN_LANES), lambda hp, i: (0, i, hp)),
            pl.BlockSpec((1, S, ATTN_LANES), lambda hp, i: (1, 0, hp)),
            pl.BlockSpec((1, S, ATTN_LANES), lambda hp, i: (2, 0, hp)),
            qtile, qtile,
        ] + [ANY] * (n + 2 * m),
        out_specs=[qtile, whole, whole] + [ANY] * (n + m),
        out_shape=[jax.ShapeDtypeStruct((S, D_ATTN), BF16)] * 3 + _reduce_shapes(partials)
        + [jax.ShapeDtypeStruct(r.shape, r.dtype) for r in late_received],
        input_output_aliases={5 + n + m + a: 3 + n + a for a in range(m)},
        scratch_shapes=[
            pltpu.VMEM((HEADS_PER_BLOCK, S, ATTN_LANES), BF16),
            pltpu.VMEM((S, ATTN_LANES), F32),
            pltpu.VMEM((S, ATTN_LANES), F32),
            pltpu.VMEM((HEADS_PER_BLOCK, t, ATTN_LANES), F32),
            pltpu.VMEM((2, HEADS_PER_BLOCK, t, t), F32),
            pltpu.VMEM((2, HEADS_PER_BLOCK, t, t), F32),
        ] + (_reduce_sems(n) if n else []) + (_reduce_sems(m) if m else []),
        compiler_params=_params(),
    )(qkv, qkv, qkv, o, do, *partials, *late, *late_received)


def _pool_bwd_w(u, dyp, pool_w, pool_scale):
    S = u.shape[0]
    ts = ROW_TILE
    hb = ts // POOL_HALO

    def body(u_ref, halo_ref, dy_ref, w_ref, s_ref, dp_ref, dw_ref, ds_ref):
        i = pl.program_id(0)
        halo = jnp.where(i == 0, 0.0, halo_ref[...])
        pooled = _pooled(u_ref[...], halo, i * ts)

        @pl.when(i == 0)
        def _():
            dw_ref[...] = jnp.zeros_like(dw_ref)
            ds_ref[...] = jnp.zeros_like(ds_ref)

        for g in range(len(POOL_WINDOWS)):
            lanes = slice(g * POOL_GROUP_DIM, (g + 1) * POOL_GROUP_DIM)
            pg = pooled[g].astype(BF16)
            dy = dy_ref[:, lanes]
            ds_ref[:, lanes] += jnp.sum(dy * _nn(pg, w_ref[g]), axis=0, keepdims=True)
            dmapped = (dy * s_ref[:, lanes]).astype(BF16)
            dp_ref[:, lanes] = _nt(dmapped, w_ref[g])
            dw_ref[g] += _tn(pg, dmapped)

    row = pl.BlockSpec((ts, D_POOL), lambda i: (i, 0))
    vec = pl.BlockSpec((1, D_POOL), lambda i: (0, 0))
    wspec = pl.BlockSpec((4, POOL_GROUP_DIM, POOL_GROUP_DIM), lambda i: (0, 0, 0))
    return pl.pallas_call(
        body,
        name="pool_bwd_w",
        grid=(S // ts,),
        in_specs=[row, pl.BlockSpec((POOL_HALO, D_POOL), lambda i: (jnp.maximum(i * hb - 1, 0), 0)),
                  row, wspec, vec],
        out_specs=[row, wspec, vec],
        out_shape=[
            jax.ShapeDtypeStruct((S, D_POOL), F32),
            jax.ShapeDtypeStruct((4, POOL_GROUP_DIM, POOL_GROUP_DIM), F32),
            jax.ShapeDtypeStruct((1, D_POOL), F32),
        ],
        compiler_params=_params(),
    )(u, u, dyp, pool_w, pool_scale)


def _pool_bwd_u(dpooled):
    S = dpooled.shape[0]
    ts = ROW_TILE
    hb = ts // POOL_HALO
    last = S // ts - 1

    def body(dp_ref, halo_ref, du_ref):
        i = pl.program_id(0)
        dp = dp_ref[...]
        halo = jnp.where(i == last, 0.0, halo_ref[...])
        inv = _pool_counts(i * ts, ts)
        n = ts + POOL_HALO
        for g, w in enumerate(POOL_WINDOWS):
            lanes = slice(g * POOL_GROUP_DIM, (g + 1) * POOL_GROUP_DIM)
            dg = dp[:, lanes]
            acc = jnp.concatenate([dg * inv[g], halo[:, lanes] * (1.0 / w)], axis=0)
            shift = 1
            while shift < w:
                acc = acc + pltpu.roll(acc, n - shift, axis=0)
                shift *= 2
            du_ref[:, lanes] = (acc[:ts, :] - dg).astype(BF16)

    return pl.pallas_call(
        body,
        name="pool_bwd_u",
        grid=(S // ts,),
        in_specs=[pl.BlockSpec((ts, D_POOL), lambda i: (i, 0)),
                  pl.BlockSpec((POOL_HALO, D_POOL), lambda i: (jnp.minimum((i + 1) * hb, (last + 1) * hb - 1), 0))],
        out_specs=pl.BlockSpec((ts, D_POOL), lambda i: (i, 0)),
        out_shape=jax.ShapeDtypeStruct((S, D_POOL), BF16),
        compiler_params=_params(),
    )(dpooled, dpooled)


def _in_bwd_dw(hn, dprojs):
    S = hn.shape[0]
    ts = min(DW_TOKEN_TILE, S)

    def body(hn_ref, du_ref, dq_ref, dk_ref, dv_ref, o_ref, acc):
        j = pl.program_id(0)
        tt = pl.program_id(1)
        for k, dp_ref in enumerate((du_ref, dq_ref, dk_ref, dv_ref)):
            @pl.when(j == k)
            def _(dp_ref=dp_ref):
                part = _tn(hn_ref[...], dp_ref[...])

                @pl.when(tt == 0)
                def _():
                    acc[...] = part

                @pl.when(tt > 0)
                def _():
                    acc[...] += part

        @pl.when(tt == pl.num_programs(1) - 1)
        def _():
            o_ref[0] = acc[...].astype(BF16)

    def taken_at(k):
        return lambda j, tt: (jnp.where(j == k, tt, 0), 0)

    return pl.pallas_call(
        body,
        name="in_bwd_dw",
        grid=(4, S // ts),
        in_specs=[pl.BlockSpec((ts, D_MODEL), lambda j, tt: (tt, 0))]
        + [pl.BlockSpec((ts, 512), taken_at(k)) for k in range(4)],
        out_specs=pl.BlockSpec((1, D_MODEL, 512), lambda j, tt: (j, 0, 0)),
        out_shape=jax.ShapeDtypeStruct((4, D_MODEL, 512), BF16),
        scratch_shapes=[pltpu.VMEM((D_MODEL, 512), F32)],
        compiler_params=_params(),
    )(hn, *dprojs)


def _in_bwd_dx(du, dq, dk, dv, w_in, x, dh1, g1, partials=()):
    S = x.shape[0]
    ts = ROW_TILE
    n = len(partials)

    def body(*refs):
        du_ref, dq_ref, dk_ref, dv_ref, w_ref, x_ref, dh1_ref, g_ref = refs[:8]
        dx_ref, dg_ref = refs[8 + n:10 + n]
        i = pl.program_id(0)
        finish = lambda: None
        if n:
            finish = _run_at(i == 0, None, i == pl.num_programs(0) - 1,
                             _reduce_phases(refs[8:8 + n], refs[10 + n:10 + 2 * n], *refs[10 + 2 * n:]))

        @pl.when(i == 0)
        def _():
            dg_ref[...] = jnp.zeros_like(dg_ref)

        dhn = None
        for j, dp_ref in enumerate((du_ref, dq_ref, dk_ref, dv_ref)):
            part = _nt(dp_ref[...], w_ref[j])
            dhn = part if dhn is None else dhn + part
        xv = x_ref[...]
        r1 = _rms(xv)
        n1 = xv * r1
        dg_ref[...] += jnp.sum(dhn * n1, axis=0, keepdims=True)
        dx_ref[...] = dh1_ref[...] + _rms_bwd(dhn, n1, r1, g_ref[...])
        finish()

    row = lambda w: pl.BlockSpec((ts, w), lambda i: (i, 0))
    vec = pl.BlockSpec((1, D_MODEL), lambda i: (0, 0))
    wspec = pl.BlockSpec((4, D_MODEL, 512), lambda i: (0, 0, 0), pipeline_mode=pl.Buffered(1))
    return pl.pallas_call(
        body,
        name="in_bwd_dx",
        grid=(S // ts,),
        in_specs=[row(512), row(512), row(512), row(512), wspec, row(D_MODEL), row(D_MODEL), vec] + [ANY] * n,
        out_specs=[row(D_MODEL), vec] + [ANY] * n,
        out_shape=[
            jax.ShapeDtypeStruct((S, D_MODEL), F32),
            jax.ShapeDtypeStruct((1, D_MODEL), F32),
        ] + _reduce_shapes(partials),
        scratch_shapes=_reduce_sems(n) if n else [],
        compiler_params=_params(),
    )(du, dq, dk, dv, w_in, x, dh1, g1, *partials)


def _pieces(g):
    return g.reshape(N_CHIPS, 2, -1, g.shape[-1])


def _local_step(x, target, w_in, pool_w, small, full=None, shards=None):
    spread = shards is not None
    if spread:
        hn, u, qkv, w_down = _in_proj(x, small["norm1_g"], w_in, shards[2:])
        w_down = w_down.reshape(N_CHIPS, 1024, D_MODEL)
    else:
        hn, u, qkv = _in_proj(x, small["norm1_g"], w_in)
        w_out, w_up, w_down = full
    y_pool = _pool_fwd(u, pool_w, small["pool_scale"])
    if spread:
        y_attn, w_out, w_up = _attn_fwd(qkv, shards[:2])
        w_out = w_out.reshape(D_MODEL, D_MODEL)
        w_up = w_up.reshape(N_CHIPS, D_MODEL, 1024)
    else:
        (y_attn,) = _attn_fwd(qkv)
    mixed, h1, hn2 = _out_proj(y_pool, y_attn, x, small["pool_out_g"], small["attn_out_g"], w_out, small["norm2_g"])
    r_act, dh2, dh2b, lsq, dgf = _mlp_fwd(hn2, h1, w_up, w_down, small["final_g"], target)

    dup, dh1, dg2 = _mlp_bwd_dx(dh2b, r_act, w_down, w_up, h1, dh2, small["norm2_g"])
    (dw_down,) = _tokens_tn("mlp_bwd_dw_down", r_act, dh2b, True, True)
    out_args = (dh1, w_out, mixed, y_pool, y_attn, small["pool_out_g"], small["attn_out_g"])
    if spread:
        dw_up, got_down = _tokens_tn("mlp_bwd_dw_up", hn2, dup, False, False, [_pieces(dw_down)])
        dyp, dya, dw_out, dgp, dga, got_down = _out_bwd(*out_args, [_pieces(dw_down)], [got_down])
        dq, dk, dv, got_out, got_up, got_down = _attn_bwd(
            qkv, y_attn, dya, [_pieces(dw_out), _pieces(dw_up)], [_pieces(dw_down)], [got_down])
    else:
        (dw_up,) = _tokens_tn("mlp_bwd_dw_up", hn2, dup, False, False)
        dyp, dya, dw_out, dgp, dga = _out_bwd(*out_args)
        dq, dk, dv = _attn_bwd(qkv, y_attn, dya)
    dpooled, dpool_w, dpool_scale = _pool_bwd_w(u, dyp, pool_w, small["pool_scale"])
    du = _pool_bwd_u(dpooled)
    dw_in = _in_bwd_dw(hn, (du, dq, dk, dv))
    if spread:
        dx, dg1, got_in = _in_bwd_dx(du, dq, dk, dv, w_in, x, dh1, small["norm1_g"], [_pieces(dw_in)])
    else:
        dx, dg1 = _in_bwd_dx(du, dq, dk, dv, w_in, x, dh1, small["norm1_g"])

    big = {"w_in": dw_in, "w_out": dw_out, "w_up": dw_up, "w_down": dw_down}
    if spread:
        big["received"] = {"w_in": got_in, "w_out": got_out, "w_up": got_up, "w_down": got_down}
    little = {"norm1_g": dg1, "pool_w": dpool_w, "pool_scale": dpool_scale, "pool_out_g": dgp,
              "attn_out_g": dga, "norm2_g": dg2, "final_g": dgf, "loss_sq": lsq}
    return dx, big, little


def _place():
    x, y, c = lax.axis_index("x"), lax.axis_index("y"), lax.axis_index("c")
    other_chips = [(1 - x, y), (x, 1 - y), (1 - x, 1 - y)]
    return x, y, c, other_chips


def _chip_index(chip):
    return 2 * chip[0] + chip[1]


def _gather_shapes(shards):
    return [jax.ShapeDtypeStruct((N_CHIPS,) + s.shape, s.dtype) for s in shards]


def _gather_sems(n):
    return [pltpu.SemaphoreType.DMA((n, 7)), pltpu.SemaphoreType.DMA((n, 7))]


def _gather_phases(ins, outs, send_sems, recv_sems):
    n = len(ins)
    x, y, c, chips = _place()
    me = _chip_index((x, y))
    sibling = (x, y, 1 - c)

    def copy(a, k, chip_idx, half, to, src=None):
        dst = outs[a].at[chip_idx, half]
        return pltpu.make_async_remote_copy(
            src_ref=dst if src is None else src, dst_ref=dst,
            send_sem=send_sems.at[a, k], recv_sem=recv_sems.at[a, k],
            device_id=to, device_id_type=MESH)

    def own_chip(a, to):
        return pltpu.make_async_remote_copy(
            src_ref=ins[a], dst_ref=outs[a].at[me],
            send_sem=send_sems.at[a, 0], recv_sem=recv_sems.at[a, 0],
            device_id=to, device_id_type=MESH)

    def first(a):
        return [own_chip(a, sibling)] + [
            copy(a, 1 + j, me, c, (*chip, c), src=ins[a].at[c]) for j, chip in enumerate(chips)]

    def passed(a, j):
        return copy(a, 4 + j, _chip_index(chips[j]), c, sibling)

    def start():
        for a in range(n):
            for cp in first(a):
                cp.start()

    def forward():
        for a in range(n):
            for j, chip in enumerate(chips):
                copy(a, 1 + j, _chip_index(chip), c, (x, y, c)).wait_recv()
                passed(a, j).start()

    def finish():
        for a in range(n):
            own_chip(a, (x, y, c)).wait_recv()
            for j, chip in enumerate(chips):
                copy(a, 4 + j, _chip_index(chip), 1 - c, (x, y, c)).wait_recv()
        for a in range(n):
            for cp in first(a):
                cp.wait_send()
            for j in range(len(chips)):
                passed(a, j).wait_send()

    return start, forward, finish


def _gather_weights(shards):
    n = len(shards)

    def body(*refs):
        start, forward, finish = _gather_phases(refs[:n], refs[n:2 * n], *refs[2 * n:])
        start()
        forward()
        finish()

    return pl.pallas_call(
        body,
        name="gather_weights",
        in_specs=[ANY] * n,
        out_specs=[ANY] * n,
        out_shape=_gather_shapes(shards),
        scratch_shapes=_gather_sems(n),
    )(*shards)


def _reduce_shapes(partials):
    return [jax.ShapeDtypeStruct((N_DEV,) + p.shape[2:], p.dtype) for p in partials]


def _reduce_sems(n):
    return [pltpu.SemaphoreType.DMA((n, N_DEV)), pltpu.SemaphoreType.DMA((n, N_DEV))]


def _reduce_phases(ins, outs, send_sems, recv_sems, hops=None):
    n = len(ins)
    x, y, c, _ = _place()
    me = 4 * x + 2 * y + c

    def to_peer(a, k):
        return pltpu.make_async_remote_copy(
            src_ref=ins[a].at[k // 2, k % 2], dst_ref=outs[a].at[me],
            send_sem=send_sems.at[a, k], recv_sem=recv_sems.at[a, me],
            device_id=(k // 4, (k // 2) % 2, k % 2), device_id_type=MESH)

    def from_peer(a, k):
        return pltpu.make_async_remote_copy(
            src_ref=ins[a].at[k // 2, k % 2], dst_ref=outs[a].at[k],
            send_sem=send_sems.at[a, k], recv_sem=recv_sems.at[a, k],
            device_id=(x, y, c), device_id_type=MESH)

    def taken(k):
        if hops is None:
            return k != me
        hit = (k ^ hops[0]) == me
        for d in hops[1:]:
            hit = jnp.logical_or(hit, (k ^ d) == me)
        return hit

    def start():
        for a in range(n):
            for k in range(N_DEV):
                @pl.when(taken(k))
                def _(a=a, k=k):
                    to_peer(a, k).start()

    def finish():
        for a in range(n):
            for k in range(N_DEV):
                @pl.when(taken(k))
                def _(a=a, k=k):
                    from_peer(a, k).wait_recv()
                    to_peer(a, k).wait_send()

    return start, finish


def _sum_pieces(where, parts, own):
    _, h, cols = parts.shape
    hb = min(h, 256)

    def body(where_ref, *refs):
        me = where_ref[0]
        acc = None
        for k in range(N_DEV):
            piece = jnp.where(k == me, refs[N_DEV][0, 0], refs[k][0]).astype(F32)
            acc = piece if acc is None else acc + piece
        refs[N_DEV + 1][0] = acc

    def sent_by(k):
        return lambda r, w: (jnp.where(w[0] == k, (k + 1) % N_DEV, k), r, 0)

    return pl.pallas_call(
        body,
        name="sum_pieces",
        grid_spec=pltpu.PrefetchScalarGridSpec(
            num_scalar_prefetch=1,
            grid=(h // hb,),
            in_specs=[pl.BlockSpec((1, hb, cols), sent_by(k)) for k in range(N_DEV)]
            + [pl.BlockSpec((1, 1, hb, cols), lambda r, w: (w[1], w[2], r, 0))],
            out_specs=pl.BlockSpec((1, hb, cols), lambda r, w: (w[2], r, 0)),
        ),
        out_shape=jax.ShapeDtypeStruct((2, h, cols), F32),
    )(where, *([parts] * N_DEV), own)


def _join_halves(halves):
    n = len(halves)

    def body(*refs):
        outs = refs[n:2 * n]
        send_sems, recv_sems = refs[2 * n:]
        x, y, c, _ = _place()
        sends = [
            pltpu.make_async_remote_copy(
                src_ref=outs[a].at[c], dst_ref=outs[a].at[c],
                send_sem=send_sems.at[a], recv_sem=recv_sems.at[a],
                device_id=(x, y, 1 - c), device_id_type=MESH)
            for a in range(n)]
        for cp in sends:
            cp.start()
        for a in range(n):
            pltpu.make_async_remote_copy(
                src_ref=outs[a].at[c], dst_ref=outs[a].at[1 - c],
                send_sem=send_sems.at[a], recv_sem=recv_sems.at[a],
                device_id=(x, y, c), device_id_type=MESH).wait_recv()
        for cp in sends:
            cp.wait_send()

    return pl.pallas_call(
        body,
        name="join_halves",
        in_specs=[ANY] * n,
        out_specs=[ANY] * n,
        out_shape=[jax.ShapeDtypeStruct(s.shape, s.dtype) for s in halves],
        input_output_aliases={a: a for a in range(n)},
        scratch_shapes=[pltpu.SemaphoreType.DMA((n,)), pltpu.SemaphoreType.DMA((n,))],
    )(*halves)


def _adamw(w, g, m, v):
    m = ADAM_B1 * m + (1.0 - ADAM_B1) * g
    v = ADAM_B2 * v + (1.0 - ADAM_B2) * jnp.square(g)
    m_hat = m / (1.0 - ADAM_B1 ** ADAM_STEP)
    v_hat = v / (1.0 - ADAM_B2 ** ADAM_STEP)
    delta = -ADAM_LR * (m_hat / (jnp.sqrt(v_hat) + ADAM_EPS) + ADAM_WD * w)
    return delta, m, v


def _adamw_big(w, g, m, v):
    rows, cols = w.shape
    rb = min(rows, 256)

    def body(w_ref, g_ref, m_ref, v_ref, d_ref, mo_ref, vo_ref):
        d_ref[...], mo_ref[...], vo_ref[...] = _adamw(w_ref[...], g_ref[...], m_ref[...], v_ref[...])

    blk = pl.BlockSpec((rb, cols), lambda r: (r, 0))
    return pl.pallas_call(
        body,
        name="adamw_big",
        grid=(rows // rb,),
        in_specs=[blk] * 4,
        out_specs=[blk] * 3,
        out_shape=[jax.ShapeDtypeStruct(w.shape, F32)] * 3,
    )(w, g, m, v)


SMALL_ORDER = ("pool_w", "norm1_g", "pool_scale", "pool_out_g", "attn_out_g", "norm2_g", "final_g")
SUBLANES = 8


def _pack(parts):
    rows = []
    for p in parts:
        p = p.reshape(-1, LANES)
        pad = (-p.shape[0]) % SUBLANES
        if pad:
            p = jnp.pad(p, ((0, pad), (0, 0)))
        rows.append(p)
    return jnp.concatenate(rows, axis=0)


def _unpack(slab, shapes):
    out, r = [], 0
    for shp in shapes:
        size = 1
        for d in shp:
            size *= d
        nrow = size // LANES
        out.append(slab[r:r + nrow].reshape(shp))
        r += nrow + (-nrow) % SUBLANES
    return out


def _small_step(partials, w, m, v, loss_rows):
    rows = partials.shape[0]

    def body(p_ref, w_ref, m_ref, v_ref, g_ref, d_ref, mo_ref, vo_ref, loss_ref, buf, send_sems, recv_sems):
        x, y, c, _ = _place()
        me = 4 * x + 2 * y + c
        for k in range(N_DEV):
            @pl.when(k != me)
            def _(k=k):
                pltpu.make_async_remote_copy(
                    src_ref=p_ref, dst_ref=buf.at[me],
                    send_sem=send_sems.at[k], recv_sem=recv_sems.at[me],
                    device_id=(k // 4, (k // 2) % 2, k % 2), device_id_type=MESH).start()
        buf[me] = p_ref[...]
        for k in range(N_DEV):
            @pl.when(k != me)
            def _(k=k):
                pltpu.make_async_remote_copy(
                    src_ref=p_ref, dst_ref=buf.at[k],
                    send_sem=send_sems.at[k], recv_sem=recv_sems.at[k],
                    device_id=(x, y, c), device_id_type=MESH).wait()
        g = buf[0]
        for k in range(1, N_DEV):
            g = g + buf[k]
        g_ref[...] = g
        d_ref[...], mo_ref[...], vo_ref[...] = _adamw(w_ref[...], g, m_ref[...], v_ref[...])
        loss = (0.5 / D_MODEL) * jnp.sum(g[rows - loss_rows:, :])
        loss_ref[...] = jnp.full(loss_ref.shape, loss, F32)

    vm = pl.BlockSpec(memory_space=pltpu.VMEM)
    slab = jax.ShapeDtypeStruct((rows, LANES), F32)
    return pl.pallas_call(
        body,
        name="small_step",
        in_specs=[vm] * 4,
        out_specs=[vm] * 5,
        out_shape=[slab, slab, slab, slab, jax.ShapeDtypeStruct((SUBLANES, LANES), F32)],
        scratch_shapes=[pltpu.VMEM((N_DEV, rows, LANES), F32),
                        pltpu.SemaphoreType.DMA((N_DEV,)), pltpu.SemaphoreType.DMA((N_DEV,))],
    )(partials, w, m, v)


BIG_ORDER = ("w_in", "w_out", "w_up", "w_down")
WEIGHT_ORDER = ("norm1_g", "w_in", "pool_w", "pool_scale", "pool_out_g", "attn_out_g", "w_out", "norm2_g",
                "w_up", "w_down", "final_g")


def _halves(a):
    return a.reshape(2, a.shape[0] // 2, a.shape[1])


def kernel(x, norm1_g, w_in, pool_w, pool_scale, pool_out_g, attn_out_g, w_out, norm2_g, w_up, w_down, final_g, loss_target, m_norm1_g, m_w_in, m_pool_w, m_pool_scale, m_pool_out_g, m_attn_out_g, m_w_out, m_norm2_g, m_w_up, m_w_down, m_final_g, v_norm1_g, v_w_in, v_pool_w, v_pool_scale, v_pool_out_g, v_attn_out_g, v_w_out, v_norm2_g, v_w_up, v_w_down, v_final_g):
    w = dict(norm1_g=norm1_g, w_in=w_in, pool_w=pool_w, pool_scale=pool_scale, pool_out_g=pool_out_g,
             attn_out_g=attn_out_g, w_out=w_out, norm2_g=norm2_g, w_up=w_up, w_down=w_down, final_g=final_g)
    m = dict(norm1_g=m_norm1_g, w_in=m_w_in, pool_w=m_pool_w, pool_scale=m_pool_scale, pool_out_g=m_pool_out_g,
             attn_out_g=m_attn_out_g, w_out=m_w_out, norm2_g=m_norm2_g, w_up=m_w_up, w_down=m_w_down,
             final_g=m_final_g)
    v = dict(norm1_g=v_norm1_g, w_in=v_w_in, pool_w=v_pool_w, pool_scale=v_pool_scale, pool_out_g=v_pool_out_g,
             attn_out_g=v_attn_out_g, w_out=v_w_out, norm2_g=v_norm2_g, w_up=v_w_up, w_down=v_w_down,
             final_g=v_final_g)

    shards = {n: _halves(w[n].astype(BF16)) for n in BIG_ORDER}
    (w_in_g,) = _gather_weights([shards["w_in"]])
    small = {n: w[n].reshape(1, -1) for n in ("norm1_g", "pool_scale", "pool_out_g", "attn_out_g", "norm2_g", "final_g")}
    dx, big, little = _local_step(
        x[0], loss_target[0], w_in_g.reshape(N_CHIPS, D_MODEL, 512), pool_w.astype(BF16), small,
        shards=[shards["w_out"], shards["w_up"], shards["w_down"]])

    grads, deltas, new_m, new_v = {}, {}, {}, {}
    loss_rows = D_MODEL // LANES
    slab_g = _pack([little[n] for n in SMALL_ORDER] + [little["loss_sq"]])
    zeros = jnp.zeros((loss_rows, LANES), F32)
    slab_w = _pack([w[n] for n in SMALL_ORDER] + [zeros])
    slab_m = _pack([m[n] for n in SMALL_ORDER] + [zeros])
    slab_v = _pack([v[n] for n in SMALL_ORDER] + [zeros])
    received = big.pop("received")
    g_s, d_s, m_s, v_s, loss = _small_step(slab_g, slab_w, slab_m, slab_v, loss_rows)
    shapes = [w[n].shape for n in SMALL_ORDER]
    for slab, dst in ((g_s, grads), (d_s, deltas), (m_s, new_m), (v_s, new_v)):
        for n, val in zip(SMALL_ORDER, _unpack(slab, shapes)):
            dst[n] = val

    xi, yi, ci = lax.axis_index("x"), lax.axis_index("y"), lax.axis_index("c")
    where = jnp.stack([4 * xi + 2 * yi + ci, 2 * xi + yi, ci]).astype(jnp.int32)
    full = _join_halves([_sum_pieces(where, received[n], _pieces(big[n])) for n in BIG_ORDER])
    for n, g in zip(BIG_ORDER, full):
        grads[n] = g.reshape(w[n].shape)
        deltas[n], new_m[n], new_v[n] = _adamw_big(w[n], grads[n], m[n], v[n])

    return (loss[0, 0], dx[None], *[grads[n] for n in WEIGHT_ORDER], *[deltas[n] for n in WEIGHT_ORDER],
            *[new_m[n] for n in WEIGHT_ORDER], *[new_v[n] for n in WEIGHT_ORDER])
```

```python
import functools

import jax
import jax.numpy as jnp
from jax import lax
from jax.experimental import pallas as pl
from jax.experimental.pallas import tpu as pltpu

F32 = jnp.float32
BF16 = jnp.bfloat16

D_MODEL = 1024
D_POOL = 512
D_ATTN = 512
POOL_WINDOWS = (2, 4, 8, 16)
POOL_GROUP_DIM = 128
POOL_HALO = 16
HEAD_DIM = 64
HEADS_PER_BLOCK = 4
ATTN_LANES = HEADS_PER_BLOCK * HEAD_DIM
D_FF = 4096
N_CHIPS = 4
N_DEV = 8
EPS = 1e-6
ATTN_SCALE = 0.125
ATTN_TILE = 256
ATTN_ROW_CHUNKS = 1
DEAD_LOG = -105.0
ROW_TILE = 512
MLP_TILE = 512
DW_TOKEN_TILE = 2048
LANES = 128

ADAM_LR = 0.001
ADAM_B1 = 0.9
ADAM_B2 = 0.999
ADAM_EPS = 1e-08
ADAM_WD = 0.01
ADAM_STEP = 10

HOPS_DIAGONAL = (6, 7)
HOPS_SAME_CORE = (2, 4)
HOPS_REST = (1, 3, 5)

MESH = pl.DeviceIdType.MESH
ANY = pl.BlockSpec(memory_space=pl.ANY)
VMEM_LIMIT = 56 * 1024 * 1024


def _nn(a, b):
    return jnp.dot(a, b, preferred_element_type=F32)


def _nt(a, b):
    return lax.dot_general(a, b, (((1,), (1,)), ((), ())), preferred_element_type=F32)


def _tn(a, b):
    return lax.dot_general(a, b, (((0,), (0,)), ((), ())), preferred_element_type=F32)


def _rms(x):
    return lax.rsqrt(jnp.mean(x * x, axis=-1, keepdims=True) + EPS)


def _rms_bwd(dy, n, r, g):
    dn = dy * g
    return r * (dn - n * jnp.mean(dn * n, axis=-1, keepdims=True))


def _params(**kw):
    return pltpu.CompilerParams(vmem_limit_bytes=VMEM_LIMIT, **kw)


def _run_at(first, middle, last, phases):
    pl.when(first)(phases[0])
    if len(phases) == 3:
        pl.when(middle)(phases[1])
    return lambda: pl.when(last)(phases[-1])


def _in_proj(x, g1, w_in, shards=()):
    S = x.shape[0]
    ts = ROW_TILE
    n = len(shards)

    def body(*refs):
        x_ref, g_ref, w_ref = refs[:3]
        hn_ref, u_ref, qkv_ref = refs[3 + n:6 + n]
        i = pl.program_id(0)
        finish = lambda: None
        if n:
            steps = pl.num_programs(0)
            finish = _run_at(i == 0, i == jnp.maximum(steps - 2, 0), i == steps - 1,
                             _gather_phases(refs[3:3 + n], refs[6 + n:6 + 2 * n], *refs[6 + 2 * n:]))

        xf = x_ref[...]
        hn = (xf * _rms(xf) * g_ref[...]).astype(BF16)
        hn_ref[...] = hn
        u_ref[...] = _nn(hn, w_ref[0])
        for j in range(1, 4):
            qkv_ref[j - 1] = _nn(hn, w_ref[j]).astype(BF16)

        finish()

    return pl.pallas_call(
        body,
        name="in_proj",
        grid=(S // ts,),
        in_specs=[
            pl.BlockSpec((ts, D_MODEL), lambda i: (i, 0)),
            pl.BlockSpec((1, D_MODEL), lambda i: (0, 0)),
            pl.BlockSpec((4, D_MODEL, 512), lambda i: (0, 0, 0), pipeline_mode=pl.Buffered(1)),
        ] + [ANY] * n,
        out_specs=[
            pl.BlockSpec((ts, D_MODEL), lambda i: (i, 0)),
            pl.BlockSpec((ts, D_POOL), lambda i: (i, 0)),
            pl.BlockSpec((3, ts, 512), lambda i: (0, i, 0)),
        ] + [ANY] * n,
        out_shape=[
            jax.ShapeDtypeStruct((S, D_MODEL), BF16),
            jax.ShapeDtypeStruct((S, D_POOL), F32),
            jax.ShapeDtypeStruct((3, S, 512), BF16),
        ] + _gather_shapes(shards),
        scratch_shapes=_gather_sems(n) if n else [],
        compiler_params=_params(),
    )(x, g1, w_in, *shards)


def _pool_counts(first_row, rows):
    t = first_row + lax.broadcasted_iota(jnp.int32, (rows, 1), 0)
    return [1.0 / jnp.minimum(t + 1, w).astype(F32) for w in POOL_WINDOWS]


def _pooled(u_tile, halo, first_row):
    ts = u_tile.shape[0]
    inv = _pool_counts(first_row, ts)
    outs = []
    for g, w in enumerate(POOL_WINDOWS):
        lanes = slice(g * POOL_GROUP_DIM, (g + 1) * POOL_GROUP_DIM)
        xg = u_tile[:, lanes]
        acc = jnp.concatenate([halo[:, lanes], xg], axis=0)
        shift = 1
        while shift < w:
            acc = acc + pltpu.roll(acc, shift, axis=0)
            shift *= 2
        outs.append(acc[POOL_HALO:, :] * inv[g] - xg)
    return outs


def _pool_fwd(u, pool_w, pool_scale):
    S = u.shape[0]
    ts = ROW_TILE
    hb = ts // POOL_HALO

    def body(u_ref, halo_ref, w_ref, s_ref, y_ref):
        i = pl.program_id(0)
        halo = jnp.where(i == 0, 0.0, halo_ref[...])
        pooled = _pooled(u_ref[...], halo, i * ts)
        for g in range(len(POOL_WINDOWS)):
            lanes = slice(g * POOL_GROUP_DIM, (g + 1) * POOL_GROUP_DIM)
            y_ref[:, lanes] = _nn(pooled[g].astype(BF16), w_ref[g]) * s_ref[:, lanes]

    return pl.pallas_call(
        body,
        name="pool_fwd",
        grid=(S // ts,),
        in_specs=[
            pl.BlockSpec((ts, D_POOL), lambda i: (i, 0)),
            pl.BlockSpec((POOL_HALO, D_POOL), lambda i: (jnp.maximum(i * hb - 1, 0), 0)),
            pl.BlockSpec((4, POOL_GROUP_DIM, POOL_GROUP_DIM), lambda i: (0, 0, 0)),
            pl.BlockSpec((1, D_POOL), lambda i: (0, 0)),
        ],
        out_specs=pl.BlockSpec((ts, D_POOL), lambda i: (i, 0)),
        out_shape=jax.ShapeDtypeStruct((S, D_POOL), F32),
        compiler_params=_params(),
    )(u, u, pool_w, pool_scale)


def _head_masks():
    lane = lax.broadcasted_iota(jnp.int32, (1, ATTN_LANES), 1)
    return [jnp.logical_and(lane >= h * HEAD_DIM, lane < (h + 1) * HEAD_DIM) for h in range(HEADS_PER_BLOCK)]


def _tri_masks(t):
    row = lax.broadcasted_iota(jnp.int32, (t, t), 0)
    col = lax.broadcasted_iota(jnp.int32, (t, t), 1)
    return row, col


def _split_bf16(x):
    hi = x.astype(BF16)
    lo = (x - hi.astype(F32)).astype(BF16)
    return hi, lo


def _log_sigmoids(z):
    sp = jnp.log(1.0 + jnp.exp(-jnp.abs(z)))
    ls = jnp.minimum(z, 0.0) - sp
    return ls, ls - z


def _attn_fwd(qkv, shards=()):
    S = qkv.shape[1]
    t = ATTN_TILE
    n = len(shards)
    nblk = D_ATTN // ATTN_LANES

    def body(*refs):
        q_ref, k_ref, v_ref = refs[:3]
        o_ref = refs[3 + n]
        vh_ref, acc_ref, z_ref = refs[4 + 2 * n:7 + 2 * n]
        hp = pl.program_id(0)
        i = pl.program_id(1)
        finish = lambda: None
        if n:
            finish = _run_at(jnp.logical_and(hp == 0, i == 0),
                             jnp.logical_and(hp == nblk - 1, i == (3 * pl.num_programs(1)) // 4),
                             jnp.logical_and(hp == nblk - 1, i == pl.num_programs(1) - 1),
                             _gather_phases(refs[3:3 + n], refs[4 + n:4 + 2 * n], *refs[7 + 2 * n:]))
        masks = _head_masks()

        @pl.when(i == 0)
        def _():
            vv = v_ref[0]
            for h in range(HEADS_PER_BLOCK):
                vh_ref[h] = jnp.where(masks[h], vv, jnp.zeros_like(vv))

        row, col = _tri_masks(t)
        later = (row > col).astype(BF16)
        causal = col < row
        qs = q_ref[0] * ATTN_SCALE
        heads = range(HEADS_PER_BLOCK)
        qh = [jnp.where(masks[h], qs, jnp.zeros_like(qs)) for h in heads]

        def scores(j, slot):
            kj = k_ref[0, pl.ds(pl.multiple_of(j * t, t), t), :]
            for h in heads:
                z_ref[slot, h] = _nt(qh[h], kj)

        def tiles(walk, left, after, carry):
            cs = list(carry)
            ls, tail = {}, {}
            for w, (j, slot, diag, counts) in enumerate(walk):
                for h in heads:
                    ls[w, h], l1m = _log_sigmoids(z_ref[slot, h])
                    if diag:
                        l1m = jnp.where(causal, l1m, 0.0)
                    hi, lo = _split_bf16(l1m)
                    tail[w, h] = _nn(hi, later) + _nn(lo, later) + cs[h]
                    cs[h] = cs[h] + jnp.sum(l1m, axis=1, keepdims=True)
            top = cs[0]
            for h in heads[1:]:
                top = jnp.maximum(top, cs[h])
            go = jnp.logical_and(left > 0, jnp.max(top) > DEAD_LOG)
            for w, (j, slot, diag, counts) in enumerate(walk):
                keys = pl.ds(pl.multiple_of(j * t, t), t)
                for h in heads:
                    a = jnp.exp(ls[w, h] + tail[w, h])
                    if diag:
                        a = jnp.where(causal, a, 0.0)
                    if counts is not None:
                        a = jnp.where(counts, a, 0.0)
                    pv = _nn(a.astype(BF16), vh_ref[h, keys, :])
                    if diag:
                        acc_ref[h] = pv
                    else:
                        acc_ref[h] += pv
            pl.when(go)(lambda: scores(*after))
            return (go, *cs)

        before = jnp.maximum(i - 1, 0)
        scores(i, 0)
        scores(before, 1)
        state = (jnp.int32(0), *tiles([(i, 0, True, None), (before, 1, False, i >= 1)], i - 1,
                                      (jnp.maximum(i - 2, 0), 0), [jnp.zeros((t, 1), F32)] * HEADS_PER_BLOCK))

        def step(state):
            jj = state[0]
            j = i - 2 - jj
            return (jj + 1, *tiles([(j, jj % 2, False, None)], j, (jnp.maximum(j - 1, 0), 1 - jj % 2), state[2:]))

        lax.while_loop(lambda s: s[1], step, state)
        out = acc_ref[0]
        for h in heads[1:]:
            out = out + acc_ref[h]
        o_ref[...] = out
        finish()

    return pl.pallas_call(
        body,
        name="attn_fwd",
        grid=(nblk, S // t),
        in_specs=[
            pl.BlockSpec((1, t, ATTN_LANES), lambda hp, i: (0, i, hp)),
            pl.BlockSpec((1, S, ATTN_LANES), lambda hp, i: (1, 0, hp)),
            pl.BlockSpec((1, S, ATTN_LANES), lambda hp, i: (2, 0, hp)),
        ] + [ANY] * n,
        out_specs=[pl.BlockSpec((t, ATTN_LANES), lambda hp, i: (i, hp))] + [ANY] * n,
        out_shape=[jax.ShapeDtypeStruct((S, D_ATTN), F32)] + _gather_shapes(shards),
        scratch_shapes=[pltpu.VMEM((HEADS_PER_BLOCK, S, ATTN_LANES), BF16),
                        pltpu.VMEM((HEADS_PER_BLOCK * ATTN_ROW_CHUNKS, t // ATTN_ROW_CHUNKS, ATTN_LANES), F32),
                        pltpu.VMEM((2, HEADS_PER_BLOCK * ATTN_ROW_CHUNKS, t // ATTN_ROW_CHUNKS, t), F32)]
        + (_gather_sems(n) if n else []),
        compiler_params=_params(),
    )(qkv, qkv, qkv, *shards)


def _out_proj(y_pool, y_attn, x, g_pool, g_attn, w_out, g2):
    S = x.shape[0]
    ts = ROW_TILE

    def body(yp_ref, ya_ref, x_ref, gp_ref, ga_ref, w_ref, g2_ref, mixed_ref, h1_ref, hn2_ref):
        yp = yp_ref[...]
        ya = ya_ref[...]
        mixed = jnp.concatenate([yp * _rms(yp) * gp_ref[...], ya * _rms(ya) * ga_ref[...]], axis=-1).astype(BF16)
        mixed_ref[...] = mixed
        h1 = x_ref[...] + _nn(mixed, w_ref[...])
        h1_ref[...] = h1
        hn2_ref[...] = (h1 * _rms(h1) * g2_ref[...]).astype(BF16)

    row = lambda w: pl.BlockSpec((ts, w), lambda i: (i, 0))
    vec = lambda w: pl.BlockSpec((1, w), lambda i: (0, 0))
    return pl.pallas_call(
        body,
        name="out_proj",
        grid=(S // ts,),
        in_specs=[row(D_POOL), row(D_ATTN), row(D_MODEL), vec(D_POOL), vec(D_ATTN),
                  pl.BlockSpec((D_MODEL, D_MODEL), lambda i: (0, 0)), vec(D_MODEL)],
        out_specs=[row(D_MODEL), row(D_MODEL), row(D_MODEL)],
        out_shape=[
            jax.ShapeDtypeStruct((S, D_MODEL), BF16),
            jax.ShapeDtypeStruct((S, D_MODEL), F32),
            jax.ShapeDtypeStruct((S, D_MODEL), BF16),
        ],
        compiler_params=_params(),
    )(y_pool, y_attn, x, g_pool, g_attn, w_out, g2)


def _mlp_fwd(hn2, h1, w_up, w_down, g_final, target):
    S = hn2.shape[0]
    ts = MLP_TILE
    nf = D_FF // 1024

    def body(hn2_ref, h1_ref, wu_ref, wd_ref, gf_ref, tg_ref, r_ref, dh2_ref, dh2b_ref, lsq_ref, dgf_ref):
        i = pl.program_id(0)
        hn2v = hn2_ref[...]
        acts = []
        for c in range(nf):
            r = jnp.maximum(_nn(hn2v, wu_ref[c]), 0.0)
            r_ref[:, c * 1024:(c + 1) * 1024] = r.astype(BF16)
            acts.append((r * r).astype(BF16))
        h2 = h1_ref[...] + _nn(jnp.concatenate(acts, axis=1), wd_ref[...])

        @pl.when(i == 0)
        def _():
            lsq_ref[...] = jnp.zeros_like(lsq_ref)
            dgf_ref[...] = jnp.zeros_like(dgf_ref)

        rf = _rms(h2)
        n = h2 * rf
        gf = gf_ref[...]
        e = n * gf - tg_ref[...]
        lsq_ref[...] += jnp.sum(e * e, axis=0, keepdims=True)
        dy = e * (1.0 / D_MODEL)
        dgf_ref[...] += jnp.sum(dy * n, axis=0, keepdims=True)
        dh2 = _rms_bwd(dy, n, rf, gf)
        dh2_ref[...] = dh2
        dh2b_ref[...] = dh2.astype(BF16)

    row = lambda w: pl.BlockSpec((ts, w), lambda i: (i, 0))
    vec = lambda w: pl.BlockSpec((1, w), lambda i: (0, 0))
    once = pl.Buffered(1)
    return pl.pallas_call(
        body,
        name="mlp_fwd",
        grid=(S // ts,),
        in_specs=[row(D_MODEL), row(D_MODEL),
                  pl.BlockSpec((nf, D_MODEL, 1024), lambda i: (0, 0, 0), pipeline_mode=once),
                  pl.BlockSpec((D_FF, D_MODEL), lambda i: (0, 0), pipeline_mode=once),
                  vec(D_MODEL), row(D_MODEL)],
        out_specs=[row(D_FF), row(D_MODEL), row(D_MODEL), vec(D_MODEL), vec(D_MODEL)],
        out_shape=[
            jax.ShapeDtypeStruct((S, D_FF), BF16),
            jax.ShapeDtypeStruct((S, D_MODEL), F32),
            jax.ShapeDtypeStruct((S, D_MODEL), BF16),
            jax.ShapeDtypeStruct((1, D_MODEL), F32),
            jax.ShapeDtypeStruct((1, D_MODEL), F32),
        ],
        compiler_params=_params(),
    )(hn2, h1, w_up, w_down.reshape(D_FF, D_MODEL), g_final, target)


def _mlp_bwd_dx(dh2b, r_act, w_down, w_up, h1, dh2, g2):
    S = h1.shape[0]
    ts = MLP_TILE
    nf = D_FF // 1024

    def body(dh2b_ref, r_ref, wd_ref, wu_ref, h1_ref, dh2_ref, g2_ref, dup_ref, dh1_ref, dg2_ref):
        i = pl.program_id(0)
        dh2b = dh2b_ref[...]
        dhn2 = None
        for c in range(nf):
            chunk = slice(c * 1024, (c + 1) * 1024)
            dup = (_nt(dh2b, wd_ref[c]) * (2.0 * r_ref[:, chunk].astype(F32))).astype(BF16)
            dup_ref[:, chunk] = dup
            part = _nt(dup, wu_ref[c])
            dhn2 = part if dhn2 is None else dhn2 + part

        @pl.when(i == 0)
        def _():
            dg2_ref[...] = jnp.zeros_like(dg2_ref)

        h1v = h1_ref[...]
        r2 = _rms(h1v)
        n2 = h1v * r2
        dg2_ref[...] += jnp.sum(dhn2 * n2, axis=0, keepdims=True)
        dh1_ref[...] = dh2_ref[...] + _rms_bwd(dhn2, n2, r2, g2_ref[...])

    row = lambda w: pl.BlockSpec((ts, w), lambda i: (i, 0))
    vec = lambda w: pl.BlockSpec((1, w), lambda i: (0, 0))
    once = pl.Buffered(1)
    return pl.pallas_call(
        body,
        name="mlp_bwd_dx",
        grid=(S // ts,),
        in_specs=[row(D_MODEL), row(D_FF),
                  pl.BlockSpec((nf, 1024, D_MODEL), lambda i: (0, 0, 0), pipeline_mode=once),
                  pl.BlockSpec((nf, D_MODEL, 1024), lambda i: (0, 0, 0), pipeline_mode=once),
                  row(D_MODEL), row(D_MODEL), vec(D_MODEL)],
        out_specs=[row(D_FF), row(D_MODEL), vec(D_MODEL)],
        out_shape=[
            jax.ShapeDtypeStruct((S, D_FF), BF16),
            jax.ShapeDtypeStruct((S, D_MODEL), F32),
            jax.ShapeDtypeStruct((1, D_MODEL), F32),
        ],
        compiler_params=_params(),
    )(dh2b, r_act, w_down, w_up, h1, dh2, g2)


def _tokens_tn(name, a, b, a_chunked, square_a, partials=()):
    S = a.shape[0]
    ts = min(DW_TOKEN_TILE, S)
    nf = D_FF // 1024
    n = len(partials)

    def body(*refs):
        a_ref, b_ref = refs[:2]
        o_ref = refs[2 + n]
        acc = refs[3 + 2 * n]
        tt = pl.program_id(1)
        finish = lambda: None
        if n:
            blk = pl.program_id(0)
            finish = _run_at(jnp.logical_and(blk == 0, tt == 0), None,
                             jnp.logical_and(blk == nf - 1, tt == pl.num_programs(1) - 1),
                             _reduce_phases(refs[2:2 + n], refs[3 + n:3 + 2 * n], *refs[4 + 2 * n:], hops=HOPS_DIAGONAL))
        av = a_ref[...]
        if square_a:
            af = av.astype(F32)
            av = (af * af).astype(BF16)
        part = _tn(av, b_ref[...])

        @pl.when(tt == 0)
        def _():
            acc[...] = part

        @pl.when(tt > 0)
        def _():
            acc[...] += part

        @pl.when(tt == pl.num_programs(1) - 1)
        def _():
            o_ref[0] = acc[...].astype(BF16)

        finish()

    whole = pl.BlockSpec((ts, 1024), lambda c, tt: (tt, 0))
    chunk = pl.BlockSpec((ts, 1024), lambda c, tt: (tt, c))
    return pl.pallas_call(
        body,
        name=name,
        grid=(nf, S // ts),
        in_specs=([chunk, whole] if a_chunked else [whole, chunk]) + [ANY] * n,
        out_specs=[pl.BlockSpec((1, 1024, 1024), lambda c, tt: (c, 0, 0))] + [ANY] * n,
        out_shape=[jax.ShapeDtypeStruct((nf, 1024, 1024), BF16)] + _reduce_shapes(partials),
        scratch_shapes=[pltpu.VMEM((1024, 1024), F32)] + (_reduce_sems(n) if n else []),
        compiler_params=_params(),
    )(a, b, *partials)


def _out_bwd(dh1, w_out, mixed, y_pool, y_attn, g_pool, g_attn, partials=(), received=()):
    S = dh1.shape[0]
    ts = ROW_TILE
    n = len(partials)

    def body(*refs):
        dh1_ref, w_ref, mixed_ref, yp_ref, ya_ref, gp_ref, ga_ref = refs[:7]
        dyp_ref, dya_ref, dw_ref, dgp_ref, dga_ref = refs[7 + 2 * n:12 + 2 * n]
        dw_acc = refs[12 + 3 * n]
        i = pl.program_id(0)
        finish = lambda: None
        if n:
            finish = _run_at(i == 0, None, i == pl.num_programs(0) - 1,
                             _reduce_phases(refs[7:7 + n], refs[12 + 2 * n:12 + 3 * n], *refs[13 + 3 * n:],
                                            hops=HOPS_SAME_CORE))
        dh1b = dh1_ref[...].astype(BF16)
        dmixed = _nt(dh1b, w_ref[...])
        dw = _tn(mixed_ref[...], dh1b)

        @pl.when(i == 0)
        def _():
            dw_acc[...] = dw
            dgp_ref[...] = jnp.zeros_like(dgp_ref)
            dga_ref[...] = jnp.zeros_like(dga_ref)

        @pl.when(i > 0)
        def _():
            dw_acc[...] += dw

        @pl.when(i == pl.num_programs(0) - 1)
        def _():
            dw_ref[...] = dw_acc[...].astype(BF16)

        for y_ref, g_ref, dy_ref, dg_ref, lanes in (
                (yp_ref, gp_ref, dyp_ref, dgp_ref, slice(0, D_POOL)),
                (ya_ref, ga_ref, dya_ref, dga_ref, slice(D_POOL, D_MODEL))):
            y = y_ref[...]
            r = _rms(y)
            nrm = y * r
            dm = dmixed[:, lanes]
            dg_ref[...] += jnp.sum(dm * nrm, axis=0, keepdims=True)
            dy_ref[...] = _rms_bwd(dm, nrm, r, g_ref[...])

        finish()

    row = lambda w: pl.BlockSpec((ts, w), lambda i: (i, 0))
    vec = lambda w: pl.BlockSpec((1, w), lambda i: (0, 0))
    full = pl.BlockSpec((D_MODEL, D_MODEL), lambda i: (0, 0))
    return pl.pallas_call(
        body,
        name="out_bwd",
        grid=(S // ts,),
        in_specs=[row(D_MODEL), full, row(D_MODEL), row(D_POOL), row(D_ATTN), vec(D_POOL), vec(D_ATTN)]
        + [ANY] * (2 * n),
        out_specs=[row(D_POOL), row(D_ATTN), full, vec(D_POOL), vec(D_ATTN)] + [ANY] * n,
        out_shape=[
            jax.ShapeDtypeStruct((S, D_POOL), F32),
            jax.ShapeDtypeStruct((S, D_ATTN), F32),
            jax.ShapeDtypeStruct((D_MODEL, D_MODEL), BF16),
            jax.ShapeDtypeStruct((1, D_POOL), F32),
            jax.ShapeDtypeStruct((1, D_ATTN), F32),
        ] + [jax.ShapeDtypeStruct(r.shape, r.dtype) for r in received],
        input_output_aliases={7 + n + a: 5 + a for a in range(n)},
        scratch_shapes=[pltpu.VMEM((D_MODEL, D_MODEL), F32)] + (_reduce_sems(n) if n else []),
        compiler_params=_params(),
    )(dh1, w_out, mixed, y_pool, y_attn, g_pool, g_attn, *partials, *received)


def _attn_bwd(qkv, o, do, partials=(), late=(), late_received=()):
    S = qkv.shape[1]
    t = ATTN_TILE
    n = len(partials)
    m = len(late)
    nblk = D_ATTN // ATTN_LANES
    outs_at = 5 + n + 2 * m
    scratch_at = outs_at + 3 + n + m

    def body(*refs):
        q_ref, k_ref, v_ref, o_ref, do_ref = refs[:5]
        dq_ref, dk_ref, dv_ref = refs[outs_at:outs_at + 3]
        kh_ref, dk_acc, dv_acc, dq_acc, z_ref, da_ref = refs[scratch_at:scratch_at + 6]
        hp = pl.program_id(0)
        i = pl.program_id(1)
        first = jnp.logical_and(hp == 0, i == 0)
        last = jnp.logical_and(hp == nblk - 1, i == pl.num_programs(1) - 1)
        finishes = []
        if n:
            finishes.append(_run_at(first, None, last, _reduce_phases(
                refs[5:5 + n], refs[outs_at + 3:outs_at + 3 + n], *refs[scratch_at + 6:scratch_at + 8])))
        if m:
            finishes.append(_run_at(first, None, last, _reduce_phases(
                refs[5 + n:5 + n + m], refs[outs_at + 3 + n:outs_at + 3 + n + m],
                *refs[scratch_at + 6 + (2 if n else 0):], hops=HOPS_REST)))

        def finish():
            for f in finishes:
                f()

        masks = _head_masks()

        @pl.when(i == 0)
        def _():
            kk = k_ref[0]
            for h in range(HEADS_PER_BLOCK):
                kh_ref[h] = jnp.where(masks[h], kk, jnp.zeros_like(kk))
            dk_acc[...] = jnp.zeros_like(dk_acc)
            dv_acc[...] = jnp.zeros_like(dv_acc)

        row, col = _tri_masks(t)
        later = (row > col).astype(BF16)
        from_s = (row >= col).astype(BF16)
        causal = col < row
        qs = q_ref[0] * ATTN_SCALE
        dob = do_ref[...].astype(BF16)
        d_all = dob.astype(F32) * o_ref[...]
        qh = [jnp.where(masks[h], qs, jnp.zeros_like(qs)) for h in range(HEADS_PER_BLOCK)]
        doh = [jnp.where(masks[h], dob, jnp.zeros_like(dob)) for h in range(HEADS_PER_BLOCK)]
        d_row = [jnp.sum(jnp.where(masks[h], d_all, 0.0), axis=1, keepdims=True) for h in range(HEADS_PER_BLOCK)]

        heads = range(HEADS_PER_BLOCK)

        def scores(j, slot):
            keys = pl.ds(pl.multiple_of(j * t, t), t)
            kj = k_ref[0, keys, :]
            vj = v_ref[0, keys, :]
            for h in heads:
                z_ref[slot, h] = _nt(qh[h], kj)
                da_ref[slot, h] = _nt(doh[h], vj)

        def tiles(walk, left, after, carry):
            nh = HEADS_PER_BLOCK
            c_l, c_g = list(carry[:nh]), list(carry[nh:])
            ls, tail, g, before = {}, {}, {}, {}
            for w, (j, slot, diag, counts) in enumerate(walk):
                for h in heads:
                    ls[w, h], l1m = _log_sigmoids(z_ref[slot, h])
                    if diag:
                        l1m = jnp.where(causal, l1m, 0.0)
                    hi, lo = _split_bf16(l1m)
                    tail[w, h] = _nn(hi, later) + _nn(lo, later) + c_l[h]
                    c_l[h] = c_l[h] + jnp.sum(l1m, axis=1, keepdims=True)
            top = c_l[0]
            for h in range(1, nh):
                top = jnp.maximum(top, c_l[h])
            go = jnp.logical_and(left > 0, jnp.max(top) > DEAD_LOG)
            for w, (j, slot, diag, counts) in enumerate(walk):
                keys = pl.ds(pl.multiple_of(j * t, t), t)
                dv = None
                for h in heads:
                    a = jnp.exp(ls[w, h] + tail[w, h])
                    if diag:
                        a = jnp.where(causal, a, 0.0)
                    if counts is not None:
                        a = jnp.where(counts, a, 0.0)
                    ab = a.astype(BF16)
                    g[w, h] = ab.astype(F32) * da_ref[slot, h]
                    ghi, glo = _split_bf16(g[w, h])
                    before[w, h] = d_row[h] - (_nn(ghi, from_s) + _nn(glo, from_s) + c_g[h])
                    c_g[h] = c_g[h] + jnp.sum(g[w, h], axis=1, keepdims=True)
                    part = _tn(ab, doh[h])
                    dv = part if dv is None else dv + part
                dv_acc[keys, :] += dv
            for w, (j, slot, diag, counts) in enumerate(walk):
                keys = pl.ds(pl.multiple_of(j * t, t), t)
                dk = None
                for h in heads:
                    beta = jnp.exp(ls[w, h])
                    dz = g[w, h] * (1.0 - beta) - before[w, h] * beta
                    if diag:
                        dz = jnp.where(causal, dz, 0.0)
                    if counts is not None:
                        dz = jnp.where(counts, dz, 0.0)
                    dzb = dz.astype(BF16)
                    dqh = _nn(dzb, kh_ref[h, keys, :])
                    if diag:
                        dq_acc[h] = dqh
                    else:
                        dq_acc[h] += dqh
                    part = _tn(dzb, qh[h])
                    dk = part if dk is None else dk + part
                dk_acc[keys, :] += dk
            pl.when(go)(lambda: scores(*after))
            return (go, *c_l, *c_g)

        prev = jnp.maximum(i - 1, 0)
        scores(i, 0)
        scores(prev, 1)
        state = (jnp.int32(0), *tiles([(i, 0, True, None), (prev, 1, False, i >= 1)], i - 1,
                                      (jnp.maximum(i - 2, 0), 0),
                                      [jnp.zeros((t, 1), F32)] * (2 * HEADS_PER_BLOCK)))

        def step(state):
            jj = state[0]
            j = i - 2 - jj
            return (jj + 1, *tiles([(j, jj % 2, False, None)], j, (jnp.maximum(j - 1, 0), 1 - jj % 2), state[2:]))

        lax.while_loop(lambda s: s[1], step, state)
        dq = dq_acc[0]
        for h in range(1, HEADS_PER_BLOCK):
            dq = dq + dq_acc[h]
        dq_ref[...] = (dq * ATTN_SCALE).astype(BF16)

        @pl.when(i == pl.num_programs(1) - 1)
        def _():
            dk_ref[...] = dk_acc[...].astype(BF16)
            dv_ref[...] = dv_acc[...].astype(BF16)

        finish()

    qtile = pl.BlockSpec((t, ATTN_LANES), lambda hp, i: (i, hp))
    whole = pl.BlockSpec((S, ATTN_LANES), lambda hp, i: (0, hp))
    return pl.pallas_call(
        body,
        name="attn_bwd",
        grid=(nblk, S // t),
        in_specs=[
            pl.BlockSpec((1, t, ATTN_LANES), lambda hp, i: (0, i, hp)),
            pl.BlockSpec((1, S, ATTN_LANES), lambda hp, i: (1, 0, hp)),
            pl.BlockSpec((1, S, ATTN_LANES), lambda hp, i: (2, 0, hp)),
            qtile, qtile,
        ] + [ANY] * (n + 2 * m),
        out_specs=[qtile, whole, whole] + [ANY] * (n + m),
        out_shape=[jax.ShapeDtypeStruct((S, D_ATTN), BF16)] * 3 + _reduce_shapes(partials)
        + [jax.ShapeDtypeStruct(r.shape, r.dtype) for r in late_received],
        input_output_aliases={5 + n + m + a: 3 + n + a for a in range(m)},
        scratch_shapes=[
            pltpu.VMEM((HEADS_PER_BLOCK, S, ATTN_LANES), BF16),
            pltpu.VMEM((S, ATTN_LANES), F32),
            pltpu.VMEM((S, ATTN_LANES), F32),
            pltpu.VMEM((HEADS_PER_BLOCK, t, ATTN_LANES), F32),
            pltpu.VMEM((2, HEADS_PER_BLOCK, t, t), F32),
            pltpu.VMEM((2, HEADS_PER_BLOCK, t, t), F32),
        ] + (_reduce_sems(n) if n else []) + (_reduce_sems(m) if m else []),
        compiler_params=_params(),
    )(qkv, qkv, qkv, o, do, *partials, *late, *late_received)


def _pool_bwd_w(u, dyp, pool_w, pool_scale):
    S = u.shape[0]
    ts = ROW_TILE
    hb = ts // POOL_HALO

    def body(u_ref, halo_ref, dy_ref, w_ref, s_ref, dp_ref, dw_ref, ds_ref):
        i = pl.program_id(0)
        halo = jnp.where(i == 0, 0.0, halo_ref[...])
        pooled = _pooled(u_ref[...], halo, i * ts)

        @pl.when(i == 0)
        def _():
            dw_ref[...] = jnp.zeros_like(dw_ref)
            ds_ref[...] = jnp.zeros_like(ds_ref)

        for g in range(len(POOL_WINDOWS)):
            lanes = slice(g * POOL_GROUP_DIM, (g + 1) * POOL_GROUP_DIM)
            pg = pooled[g].astype(BF16)
            dy = dy_ref[:, lanes]
            ds_ref[:, lanes] += jnp.sum(dy * _nn(pg, w_ref[g]), axis=0, keepdims=True)
            dmapped = (dy * s_ref[:, lanes]).astype(BF16)
            dp_ref[:, lanes] = _nt(dmapped, w_ref[g])
            dw_ref[g] += _tn(pg, dmapped)

    row = pl.BlockSpec((ts, D_POOL), lambda i: (i, 0))
    vec = pl.BlockSpec((1, D_POOL), lambda i: (0, 0))
    wspec = pl.BlockSpec((4, POOL_GROUP_DIM, POOL_GROUP_DIM), lambda i: (0, 0, 0))
    return pl.pallas_call(
        body,
        name="pool_bwd_w",
        grid=(S // ts,),
        in_specs=[row, pl.BlockSpec((POOL_HALO, D_POOL), lambda i: (jnp.maximum(i * hb - 1, 0), 0)),
                  row, wspec, vec],
        out_specs=[row, wspec, vec],
        out_shape=[
            jax.ShapeDtypeStruct((S, D_POOL), F32),
            jax.ShapeDtypeStruct((4, POOL_GROUP_DIM, POOL_GROUP_DIM), F32),
            jax.ShapeDtypeStruct((1, D_POOL), F32),
        ],
        compiler_params=_params(),
    )(u, u, dyp, pool_w, pool_scale)


def _pool_bwd_u(dpooled):
    S = dpooled.shape[0]
    ts = ROW_TILE
    hb = ts // POOL_HALO
    last = S // ts - 1

    def body(dp_ref, halo_ref, du_ref):
        i = pl.program_id(0)
        dp = dp_ref[...]
        halo = jnp.where(i == last, 0.0, halo_ref[...])
        inv = _pool_counts(i * ts, ts)
        n = ts + POOL_HALO
        for g, w in enumerate(POOL_WINDOWS):
            lanes = slice(g * POOL_GROUP_DIM, (g + 1) * POOL_GROUP_DIM)
            dg = dp[:, lanes]
            acc = jnp.concatenate([dg * inv[g], halo[:, lanes] * (1.0 / w)], axis=0)
            shift = 1
            while shift < w:
                acc = acc + pltpu.roll(acc, n - shift, axis=0)
                shift *= 2
            du_ref[:, lanes] = (acc[:ts, :] - dg).astype(BF16)

    return pl.pallas_call(
        body,
        name="pool_bwd_u",
        grid=(S // ts,),
        in_specs=[pl.BlockSpec((ts, D_POOL), lambda i: (i, 0)),
                  pl.BlockSpec((POOL_HALO, D_POOL), lambda i: (jnp.minimum((i + 1) * hb, (last + 1) * hb - 1), 0))],
        out_specs=pl.BlockSpec((ts, D_POOL), lambda i: (i, 0)),
        out_shape=jax.ShapeDtypeStruct((S, D_POOL), BF16),
        compiler_params=_params(),
    )(dpooled, dpooled)


def _in_bwd_dw(hn, dprojs, late=(), late_received=()):
    S = hn.shape[0]
    ts = min(DW_TOKEN_TILE, S)
    m = len(late)

    def body(*refs):
        hn_ref, du_ref, dq_ref, dk_ref, dv_ref = refs[:5]
        o_ref = refs[5 + 2 * m]
        acc = refs[6 + 3 * m]
        j = pl.program_id(0)
        tt = pl.program_id(1)
        finish = lambda: None
        if m:
            finish = _run_at(jnp.logical_and(j == 0, tt == 0), None,
                             jnp.logical_and(j == 3, tt == pl.num_programs(1) - 1),
                             _reduce_phases(refs[5:5 + m], refs[6 + 2 * m:6 + 3 * m], *refs[7 + 3 * m:],
                                            hops=HOPS_REST))
        for k, dp_ref in enumerate((du_ref, dq_ref, dk_ref, dv_ref)):
            @pl.when(j == k)
            def _(dp_ref=dp_ref):
                part = _tn(hn_ref[...], dp_ref[...])

                @pl.when(tt == 0)
                def _():
                    acc[...] = part

                @pl.when(tt > 0)
                def _():
                    acc[...] += part

        @pl.when(tt == pl.num_programs(1) - 1)
        def _():
            o_ref[0] = acc[...].astype(BF16)

        finish()

    def taken_at(k):
        return lambda j, tt: (jnp.where(j == k, tt, 0), 0)

    return pl.pallas_call(
        body,
        name="in_bwd_dw",
        grid=(4, S // ts),
        in_specs=[pl.BlockSpec((ts, D_MODEL), lambda j, tt: (tt, 0))]
        + [pl.BlockSpec((ts, 512), taken_at(k)) for k in range(4)] + [ANY] * (2 * m),
        out_specs=[pl.BlockSpec((1, D_MODEL, 512), lambda j, tt: (j, 0, 0))] + [ANY] * m,
        out_shape=[jax.ShapeDtypeStruct((4, D_MODEL, 512), BF16)]
        + [jax.ShapeDtypeStruct(r.shape, r.dtype) for r in late_received],
        input_output_aliases={5 + m + a: 1 + a for a in range(m)},
        scratch_shapes=[pltpu.VMEM((D_MODEL, 512), F32)] + (_reduce_sems(m) if m else []),
        compiler_params=_params(),
    )(hn, *dprojs, *late, *late_received)


def _in_bwd_dx(du, dq, dk, dv, w_in, x, dh1, g1, partials=()):
    S = x.shape[0]
    ts = ROW_TILE
    n = len(partials)

    def body(*refs):
        du_ref, dq_ref, dk_ref, dv_ref, w_ref, x_ref, dh1_ref, g_ref = refs[:8]
        dx_ref, dg_ref = refs[8 + n:10 + n]
        i = pl.program_id(0)
        finish = lambda: None
        if n:
            finish = _run_at(i == 0, None, i == pl.num_programs(0) - 1,
                             _reduce_phases(refs[8:8 + n], refs[10 + n:10 + 2 * n], *refs[10 + 2 * n:]))

        @pl.when(i == 0)
        def _():
            dg_ref[...] = jnp.zeros_like(dg_ref)

        dhn = None
        for j, dp_ref in enumerate((du_ref, dq_ref, dk_ref, dv_ref)):
            part = _nt(dp_ref[...], w_ref[j])
            dhn = part if dhn is None else dhn + part
        xv = x_ref[...]
        r1 = _rms(xv)
        n1 = xv * r1
        dg_ref[...] += jnp.sum(dhn * n1, axis=0, keepdims=True)
        dx_ref[...] = dh1_ref[...] + _rms_bwd(dhn, n1, r1, g_ref[...])
        finish()

    row = lambda w: pl.BlockSpec((ts, w), lambda i: (i, 0))
    vec = pl.BlockSpec((1, D_MODEL), lambda i: (0, 0))
    wspec = pl.BlockSpec((4, D_MODEL, 512), lambda i: (0, 0, 0), pipeline_mode=pl.Buffered(1))
    return pl.pallas_call(
        body,
        name="in_bwd_dx",
        grid=(S // ts,),
        in_specs=[row(512), row(512), row(512), row(512), wspec, row(D_MODEL), row(D_MODEL), vec] + [ANY] * n,
        out_specs=[row(D_MODEL), vec] + [ANY] * n,
        out_shape=[
            jax.ShapeDtypeStruct((S, D_MODEL), F32),
            jax.ShapeDtypeStruct((1, D_MODEL), F32),
        ] + _reduce_shapes(partials),
        scratch_shapes=_reduce_sems(n) if n else [],
        compiler_params=_params(),
    )(du, dq, dk, dv, w_in, x, dh1, g1, *partials)


def _pieces(g):
    return g.reshape(N_CHIPS, 2, -1, g.shape[-1])


def _local_step(x, target, w_in, pool_w, small, full=None, shards=None):
    spread = shards is not None
    if spread:
        hn, u, qkv, w_down = _in_proj(x, small["norm1_g"], w_in, shards[2:])
        w_down = w_down.reshape(N_CHIPS, 1024, D_MODEL)
    else:
        hn, u, qkv = _in_proj(x, small["norm1_g"], w_in)
        w_out, w_up, w_down = full
    y_pool = _pool_fwd(u, pool_w, small["pool_scale"])
    if spread:
        y_attn, w_out, w_up = _attn_fwd(qkv, shards[:2])
        w_out = w_out.reshape(D_MODEL, D_MODEL)
        w_up = w_up.reshape(N_CHIPS, D_MODEL, 1024)
    else:
        (y_attn,) = _attn_fwd(qkv)
    mixed, h1, hn2 = _out_proj(y_pool, y_attn, x, small["pool_out_g"], small["attn_out_g"], w_out, small["norm2_g"])
    r_act, dh2, dh2b, lsq, dgf = _mlp_fwd(hn2, h1, w_up, w_down, small["final_g"], target)

    dup, dh1, dg2 = _mlp_bwd_dx(dh2b, r_act, w_down, w_up, h1, dh2, small["norm2_g"])
    (dw_down,) = _tokens_tn("mlp_bwd_dw_down", r_act, dh2b, True, True)
    out_args = (dh1, w_out, mixed, y_pool, y_attn, small["pool_out_g"], small["attn_out_g"])
    if spread:
        dw_up, got_down = _tokens_tn("mlp_bwd_dw_up", hn2, dup, False, False, [_pieces(dw_down)])
        dyp, dya, dw_out, dgp, dga, got_down = _out_bwd(*out_args, [_pieces(dw_down)], [got_down])
        dq, dk, dv, got_out, got_up = _attn_bwd(qkv, y_attn, dya, [_pieces(dw_out), _pieces(dw_up)])
    else:
        (dw_up,) = _tokens_tn("mlp_bwd_dw_up", hn2, dup, False, False)
        dyp, dya, dw_out, dgp, dga = _out_bwd(*out_args)
        dq, dk, dv = _attn_bwd(qkv, y_attn, dya)
    dpooled, dpool_w, dpool_scale = _pool_bwd_w(u, dyp, pool_w, small["pool_scale"])
    du = _pool_bwd_u(dpooled)
    if spread:
        dw_in, got_down = _in_bwd_dw(hn, (du, dq, dk, dv), [_pieces(dw_down)], [got_down])
    else:
        (dw_in,) = _in_bwd_dw(hn, (du, dq, dk, dv))
    if spread:
        dx, dg1, got_in = _in_bwd_dx(du, dq, dk, dv, w_in, x, dh1, small["norm1_g"], [_pieces(dw_in)])
    else:
        dx, dg1 = _in_bwd_dx(du, dq, dk, dv, w_in, x, dh1, small["norm1_g"])

    big = {"w_in": dw_in, "w_out": dw_out, "w_up": dw_up, "w_down": dw_down}
    if spread:
        big["received"] = {"w_in": got_in, "w_out": got_out, "w_up": got_up, "w_down": got_down}
    little = {"norm1_g": dg1, "pool_w": dpool_w, "pool_scale": dpool_scale, "pool_out_g": dgp,
              "attn_out_g": dga, "norm2_g": dg2, "final_g": dgf, "loss_sq": lsq}
    return dx, big, little


def _place():
    x, y, c = lax.axis_index("x"), lax.axis_index("y"), lax.axis_index("c")
    other_chips = [(1 - x, y), (x, 1 - y), (1 - x, 1 - y)]
    return x, y, c, other_chips


def _chip_index(chip):
    return 2 * chip[0] + chip[1]


def _gather_shapes(shards):
    return [jax.ShapeDtypeStruct((N_CHIPS,) + s.shape, s.dtype) for s in shards]


def _gather_sems(n):
    return [pltpu.SemaphoreType.DMA((n, 7)), pltpu.SemaphoreType.DMA((n, 7))]


def _gather_phases(ins, outs, send_sems, recv_sems):
    n = len(ins)
    x, y, c, chips = _place()
    me = _chip_index((x, y))
    sibling = (x, y, 1 - c)

    def copy(a, k, chip_idx, half, to, src=None):
        dst = outs[a].at[chip_idx, half]
        return pltpu.make_async_remote_copy(
            src_ref=dst if src is None else src, dst_ref=dst,
            send_sem=send_sems.at[a, k], recv_sem=recv_sems.at[a, k],
            device_id=to, device_id_type=MESH)

    def own_chip(a, to):
        return pltpu.make_async_remote_copy(
            src_ref=ins[a], dst_ref=outs[a].at[me],
            send_sem=send_sems.at[a, 0], recv_sem=recv_sems.at[a, 0],
            device_id=to, device_id_type=MESH)

    def first(a):
        return [own_chip(a, sibling)] + [
            copy(a, 1 + j, me, c, (*chip, c), src=ins[a].at[c]) for j, chip in enumerate(chips)]

    def passed(a, j):
        return copy(a, 4 + j, _chip_index(chips[j]), c, sibling)

    def start():
        for a in range(n):
            for cp in first(a):
                cp.start()

    def forward():
        for a in range(n):
            for j, chip in enumerate(chips):
                copy(a, 1 + j, _chip_index(chip), c, (x, y, c)).wait_recv()
                passed(a, j).start()

    def finish():
        for a in range(n):
            own_chip(a, (x, y, c)).wait_recv()
            for j, chip in enumerate(chips):
                copy(a, 4 + j, _chip_index(chip), 1 - c, (x, y, c)).wait_recv()
        for a in range(n):
            for cp in first(a):
                cp.wait_send()
            for j in range(len(chips)):
                passed(a, j).wait_send()

    return start, forward, finish


def _gather_weights(shards):
    n = len(shards)

    def body(*refs):
        start, forward, finish = _gather_phases(refs[:n], refs[n:2 * n], *refs[2 * n:])
        start()
        forward()
        finish()

    return pl.pallas_call(
        body,
        name="gather_weights",
        in_specs=[ANY] * n,
        out_specs=[ANY] * n,
        out_shape=_gather_shapes(shards),
        scratch_shapes=_gather_sems(n),
    )(*shards)


def _reduce_shapes(partials):
    return [jax.ShapeDtypeStruct((N_DEV,) + p.shape[2:], p.dtype) for p in partials]


def _reduce_sems(n):
    return [pltpu.SemaphoreType.DMA((n, N_DEV)), pltpu.SemaphoreType.DMA((n, N_DEV))]


def _reduce_phases(ins, outs, send_sems, recv_sems, hops=None):
    n = len(ins)
    x, y, c, _ = _place()
    me = 4 * x + 2 * y + c

    def to_peer(a, k):
        return pltpu.make_async_remote_copy(
            src_ref=ins[a].at[k // 2, k % 2], dst_ref=outs[a].at[me],
            send_sem=send_sems.at[a, k], recv_sem=recv_sems.at[a, me],
            device_id=(k // 4, (k // 2) % 2, k % 2), device_id_type=MESH)

    def from_peer(a, k):
        return pltpu.make_async_remote_copy(
            src_ref=ins[a].at[k // 2, k % 2], dst_ref=outs[a].at[k],
            send_sem=send_sems.at[a, k], recv_sem=recv_sems.at[a, k],
            device_id=(x, y, c), device_id_type=MESH)

    def taken(k):
        if hops is None:
            return k != me
        hit = (k ^ hops[0]) == me
        for d in hops[1:]:
            hit = jnp.logical_or(hit, (k ^ d) == me)
        return hit

    def start():
        for a in range(n):
            for k in range(N_DEV):
                @pl.when(taken(k))
                def _(a=a, k=k):
                    to_peer(a, k).start()

    def finish():
        for a in range(n):
            for k in range(N_DEV):
                @pl.when(taken(k))
                def _(a=a, k=k):
                    from_peer(a, k).wait_recv()
                    to_peer(a, k).wait_send()

    return start, finish


def _sum_pieces(where, parts, own):
    _, h, cols = parts.shape
    hb = min(h, 256)

    def body(where_ref, *refs):
        me = where_ref[0]
        acc = None
        for k in range(N_DEV):
            piece = jnp.where(k == me, refs[N_DEV][0, 0], refs[k][0]).astype(F32)
            acc = piece if acc is None else acc + piece
        refs[N_DEV + 1][0] = acc

    def sent_by(k):
        return lambda r, w: (jnp.where(w[0] == k, (k + 1) % N_DEV, k), r, 0)

    return pl.pallas_call(
        body,
        name="sum_pieces",
        grid_spec=pltpu.PrefetchScalarGridSpec(
            num_scalar_prefetch=1,
            grid=(h // hb,),
            in_specs=[pl.BlockSpec((1, hb, cols), sent_by(k)) for k in range(N_DEV)]
            + [pl.BlockSpec((1, 1, hb, cols), lambda r, w: (w[1], w[2], r, 0))],
            out_specs=pl.BlockSpec((1, hb, cols), lambda r, w: (w[2], r, 0)),
        ),
        out_shape=jax.ShapeDtypeStruct((2, h, cols), F32),
    )(where, *([parts] * N_DEV), own)


def _join_halves(halves):
    n = len(halves)

    def body(*refs):
        outs = refs[n:2 * n]
        send_sems, recv_sems = refs[2 * n:]
        x, y, c, _ = _place()
        sends = [
            pltpu.make_async_remote_copy(
                src_ref=outs[a].at[c], dst_ref=outs[a].at[c],
                send_sem=send_sems.at[a], recv_sem=recv_sems.at[a],
                device_id=(x, y, 1 - c), device_id_type=MESH)
            for a in range(n)]
        for cp in sends:
            cp.start()
        for a in range(n):
            pltpu.make_async_remote_copy(
                src_ref=outs[a].at[c], dst_ref=outs[a].at[1 - c],
                send_sem=send_sems.at[a], recv_sem=recv_sems.at[a],
                device_id=(x, y, c), device_id_type=MESH).wait_recv()
        for cp in sends:
            cp.wait_send()

    return pl.pallas_call(
        body,
        name="join_halves",
        in_specs=[ANY] * n,
        out_specs=[ANY] * n,
        out_shape=[jax.ShapeDtypeStruct(s.shape, s.dtype) for s in halves],
        input_output_aliases={a: a for a in range(n)},
        scratch_shapes=[pltpu.SemaphoreType.DMA((n,)), pltpu.SemaphoreType.DMA((n,))],
    )(*halves)


def _adamw(w, g, m, v):
    m = ADAM_B1 * m + (1.0 - ADAM_B1) * g
    v = ADAM_B2 * v + (1.0 - ADAM_B2) * jnp.square(g)
    m_hat = m / (1.0 - ADAM_B1 ** ADAM_STEP)
    v_hat = v / (1.0 - ADAM_B2 ** ADAM_STEP)
    delta = -ADAM_LR * (m_hat / (jnp.sqrt(v_hat) + ADAM_EPS) + ADAM_WD * w)
    return delta, m, v


def _adamw_big(w, g, m, v):
    rows, cols = w.shape
    rb = min(rows, 256)

    def body(w_ref, g_ref, m_ref, v_ref, d_ref, mo_ref, vo_ref):
        d_ref[...], mo_ref[...], vo_ref[...] = _adamw(w_ref[...], g_ref[...], m_ref[...], v_ref[...])

    blk = pl.BlockSpec((rb, cols), lambda r: (r, 0))
    return pl.pallas_call(
        body,
        name="adamw_big",
        grid=(rows // rb,),
        in_specs=[blk] * 4,
        out_specs=[blk] * 3,
        out_shape=[jax.ShapeDtypeStruct(w.shape, F32)] * 3,
    )(w, g, m, v)


SMALL_ORDER = ("pool_w", "norm1_g", "pool_scale", "pool_out_g", "attn_out_g", "norm2_g", "final_g")
SUBLANES = 8


def _pack(parts):
    rows = []
    for p in parts:
        p = p.reshape(-1, LANES)
        pad = (-p.shape[0]) % SUBLANES
        if pad:
            p = jnp.pad(p, ((0, pad), (0, 0)))
        rows.append(p)
    return jnp.concatenate(rows, axis=0)


def _unpack(slab, shapes):
    out, r = [], 0
    for shp in shapes:
        size = 1
        for d in shp:
            size *= d
        nrow = size // LANES
        out.append(slab[r:r + nrow].reshape(shp))
        r += nrow + (-nrow) % SUBLANES
    return out


def _small_step(partials, w, m, v, loss_rows):
    rows = partials.shape[0]

    def body(p_ref, w_ref, m_ref, v_ref, g_ref, d_ref, mo_ref, vo_ref, loss_ref, buf, send_sems, recv_sems):
        x, y, c, _ = _place()
        me = 4 * x + 2 * y + c
        for k in range(N_DEV):
            @pl.when(k != me)
            def _(k=k):
                pltpu.make_async_remote_copy(
                    src_ref=p_ref, dst_ref=buf.at[me],
                    send_sem=send_sems.at[k], recv_sem=recv_sems.at[me],
                    device_id=(k // 4, (k // 2) % 2, k % 2), device_id_type=MESH).start()
        buf[me] = p_ref[...]
        for k in range(N_DEV):
            @pl.when(k != me)
            def _(k=k):
                pltpu.make_async_remote_copy(
                    src_ref=p_ref, dst_ref=buf.at[k],
                    send_sem=send_sems.at[k], recv_sem=recv_sems.at[k],
                    device_id=(x, y, c), device_id_type=MESH).wait()
        g = buf[0]
        for k in range(1, N_DEV):
            g = g + buf[k]
        g_ref[...] = g
        d_ref[...], mo_ref[...], vo_ref[...] = _adamw(w_ref[...], g, m_ref[...], v_ref[...])
        loss = (0.5 / D_MODEL) * jnp.sum(g[rows - loss_rows:, :])
        loss_ref[...] = jnp.full(loss_ref.shape, loss, F32)

    vm = pl.BlockSpec(memory_space=pltpu.VMEM)
    slab = jax.ShapeDtypeStruct((rows, LANES), F32)
    return pl.pallas_call(
        body,
        name="small_step",
        in_specs=[vm] * 4,
        out_specs=[vm] * 5,
        out_shape=[slab, slab, slab, slab, jax.ShapeDtypeStruct((SUBLANES, LANES), F32)],
        scratch_shapes=[pltpu.VMEM((N_DEV, rows, LANES), F32),
                        pltpu.SemaphoreType.DMA((N_DEV,)), pltpu.SemaphoreType.DMA((N_DEV,))],
    )(partials, w, m, v)


BIG_ORDER = ("w_in", "w_out", "w_up", "w_down")
WEIGHT_ORDER = ("norm1_g", "w_in", "pool_w", "pool_scale", "pool_out_g", "attn_out_g", "w_out", "norm2_g",
                "w_up", "w_down", "final_g")


def _halves(a):
    return a.reshape(2, a.shape[0] // 2, a.shape[1])


def kernel(x, norm1_g, w_in, pool_w, pool_scale, pool_out_g, attn_out_g, w_out, norm2_g, w_up, w_down, final_g, loss_target, m_norm1_g, m_w_in, m_pool_w, m_pool_scale, m_pool_out_g, m_attn_out_g, m_w_out, m_norm2_g, m_w_up, m_w_down, m_final_g, v_norm1_g, v_w_in, v_pool_w, v_pool_scale, v_pool_out_g, v_attn_out_g, v_w_out, v_norm2_g, v_w_up, v_w_down, v_final_g):
    w = dict(norm1_g=norm1_g, w_in=w_in, pool_w=pool_w, pool_scale=pool_scale, pool_out_g=pool_out_g,
             attn_out_g=attn_out_g, w_out=w_out, norm2_g=norm2_g, w_up=w_up, w_down=w_down, final_g=final_g)
    m = dict(norm1_g=m_norm1_g, w_in=m_w_in, pool_w=m_pool_w, pool_scale=m_pool_scale, pool_out_g=m_pool_out_g,
             attn_out_g=m_attn_out_g, w_out=m_w_out, norm2_g=m_norm2_g, w_up=m_w_up, w_down=m_w_down,
             final_g=m_final_g)
    v = dict(norm1_g=v_norm1_g, w_in=v_w_in, pool_w=v_pool_w, pool_scale=v_pool_scale, pool_out_g=v_pool_out_g,
             attn_out_g=v_attn_out_g, w_out=v_w_out, norm2_g=v_norm2_g, w_up=v_w_up, w_down=v_w_down,
             final_g=v_final_g)

    shards = {n: _halves(w[n].astype(BF16)) for n in BIG_ORDER}
    (w_in_g,) = _gather_weights([shards["w_in"]])
    small = {n: w[n].reshape(1, -1) for n in ("norm1_g", "pool_scale", "pool_out_g", "attn_out_g", "norm2_g", "final_g")}
    dx, big, little = _local_step(
        x[0], loss_target[0], w_in_g.reshape(N_CHIPS, D_MODEL, 512), pool_w.astype(BF16), small,
        shards=[shards["w_out"], shards["w_up"], shards["w_down"]])

    grads, deltas, new_m, new_v = {}, {}, {}, {}
    loss_rows = D_MODEL // LANES
    slab_g = _pack([little[n] for n in SMALL_ORDER] + [little["loss_sq"]])
    zeros = jnp.zeros((loss_rows, LANES), F32)
    slab_w = _pack([w[n] for n in SMALL_ORDER] + [zeros])
    slab_m = _pack([m[n] for n in SMALL_ORDER] + [zeros])
    slab_v = _pack([v[n] for n in SMALL_ORDER] + [zeros])
    received = big.pop("received")
    g_s, d_s, m_s, v_s, loss = _small_step(slab_g, slab_w, slab_m, slab_v, loss_rows)
    shapes = [w[n].shape for n in SMALL_ORDER]
    for slab, dst in ((g_s, grads), (d_s, deltas), (m_s, new_m), (v_s, new_v)):
        for n, val in zip(SMALL_ORDER, _unpack(slab, shapes)):
            dst[n] = val

    xi, yi, ci = lax.axis_index("x"), lax.axis_index("y"), lax.axis_index("c")
    where = jnp.stack([4 * xi + 2 * yi + ci, 2 * xi + yi, ci]).astype(jnp.int32)
    full = _join_halves([_sum_pieces(where, received[n], _pieces(big[n])) for n in BIG_ORDER])
    for n, g in zip(BIG_ORDER, full):
        grads[n] = g.reshape(w[n].shape)
        deltas[n], new_m[n], new_v[n] = _adamw_big(w[n], grads[n], m[n], v[n])

    return (loss[0, 0], dx[None], *[grads[n] for n in WEIGHT_ORDER], *[deltas[n] for n in WEIGHT_ORDER],
            *[new_m[n] for n in WEIGHT_ORDER], *[new_v[n] for n in WEIGHT_ORDER])
```

```python
import functools

import jax
import jax.numpy as jnp
from jax import lax
from jax.experimental import pallas as pl
from jax.experimental.pallas import tpu as pltpu

F32 = jnp.float32
BF16 = jnp.bfloat16

D_MODEL = 1024
D_POOL = 512
D_ATTN = 512
POOL_WINDOWS = (2, 4, 8, 16)
POOL_GROUP_DIM = 128
POOL_HALO = 16
HEAD_DIM = 64
HEADS_PER_BLOCK = 4
ATTN_LANES = HEADS_PER_BLOCK * HEAD_DIM
D_FF = 4096
N_CHIPS = 4
N_DEV = 8
EPS = 1e-6
ATTN_SCALE = 0.125
ATTN_TILE = 256
ATTN_ROW_CHUNKS = 1
DEAD_LOG = -105.0
ROW_TILE = 512
MLP_TILE = 512
DW_TOKEN_TILE = 2048
LANES = 128

ADAM_LR = 0.001
ADAM_B1 = 0.9
ADAM_B2 = 0.999
ADAM_EPS = 1e-08
ADAM_WD = 0.01
ADAM_STEP = 10

HOPS_DIAGONAL = (6, 7)
HOPS_SAME_CORE = (2, 4)
HOPS_REST = (1, 3, 5)

MESH = pl.DeviceIdType.MESH
ANY = pl.BlockSpec(memory_space=pl.ANY)
VMEM_LIMIT = 56 * 1024 * 1024


def _nn(a, b):
    return jnp.dot(a, b, preferred_element_type=F32)


def _nt(a, b):
    return lax.dot_general(a, b, (((1,), (1,)), ((), ())), preferred_element_type=F32)


def _tn(a, b):
    return lax.dot_general(a, b, (((0,), (0,)), ((), ())), preferred_element_type=F32)


def _rms(x):
    return lax.rsqrt(jnp.mean(x * x, axis=-1, keepdims=True) + EPS)


def _rms_bwd(dy, n, r, g):
    dn = dy * g
    return r * (dn - n * jnp.mean(dn * n, axis=-1, keepdims=True))


def _params(**kw):
    return pltpu.CompilerParams(vmem_limit_bytes=VMEM_LIMIT, **kw)


def _run_at(first, middle, last, phases):
    pl.when(first)(phases[0])
    if len(phases) == 3:
        pl.when(middle)(phases[1])
    return lambda: pl.when(last)(phases[-1])


def _in_proj(x, g1, w_in, shards=()):
    S = x.shape[0]
    ts = ROW_TILE
    n = len(shards)

    def body(*refs):
        x_ref, g_ref, w_ref = refs[:3]
        hn_ref, u_ref, qkv_ref = refs[3 + n:6 + n]
        i = pl.program_id(0)
        finish = lambda: None
        if n:
            steps = pl.num_programs(0)
            finish = _run_at(i == 0, i == jnp.maximum(steps - 2, 0), i == steps - 1,
                             _gather_phases(refs[3:3 + n], refs[6 + n:6 + 2 * n], *refs[6 + 2 * n:]))

        xf = x_ref[...]
        hn = (xf * _rms(xf) * g_ref[...]).astype(BF16)
        hn_ref[...] = hn
        u_ref[...] = _nn(hn, w_ref[0])
        for j in range(1, 4):
            qkv_ref[j - 1] = _nn(hn, w_ref[j]).astype(BF16)

        finish()

    return pl.pallas_call(
        body,
        name="in_proj",
        grid=(S // ts,),
        in_specs=[
            pl.BlockSpec((ts, D_MODEL), lambda i: (i, 0)),
            pl.BlockSpec((1, D_MODEL), lambda i: (0, 0)),
            pl.BlockSpec((4, D_MODEL, 512), lambda i: (0, 0, 0), pipeline_mode=pl.Buffered(1)),
        ] + [ANY] * n,
        out_specs=[
            pl.BlockSpec((ts, D_MODEL), lambda i: (i, 0)),
            pl.BlockSpec((ts, D_POOL), lambda i: (i, 0)),
            pl.BlockSpec((3, ts, 512), lambda i: (0, i, 0)),
        ] + [ANY] * n,
        out_shape=[
            jax.ShapeDtypeStruct((S, D_MODEL), BF16),
            jax.ShapeDtypeStruct((S, D_POOL), F32),
            jax.ShapeDtypeStruct((3, S, 512), BF16),
        ] + _gather_shapes(shards),
        scratch_shapes=_gather_sems(n) if n else [],
        compiler_params=_params(),
    )(x, g1, w_in, *shards)


def _pool_counts(first_row, rows):
    t = first_row + lax.broadcasted_iota(jnp.int32, (rows, 1), 0)
    return [1.0 / jnp.minimum(t + 1, w).astype(F32) for w in POOL_WINDOWS]


def _pooled(u_tile, halo, first_row):
    ts = u_tile.shape[0]
    inv = _pool_counts(first_row, ts)
    outs = []
    for g, w in enumerate(POOL_WINDOWS):
        lanes = slice(g * POOL_GROUP_DIM, (g + 1) * POOL_GROUP_DIM)
        xg = u_tile[:, lanes]
        acc = jnp.concatenate([halo[:, lanes], xg], axis=0)
        shift = 1
        while shift < w:
            acc = acc + pltpu.roll(acc, shift, axis=0)
            shift *= 2
        outs.append(acc[POOL_HALO:, :] * inv[g] - xg)
    return outs


def _pool_fwd(u, pool_w, pool_scale):
    S = u.shape[0]
    ts = ROW_TILE
    hb = ts // POOL_HALO

    def body(u_ref, halo_ref, w_ref, s_ref, y_ref):
        i = pl.program_id(0)
        halo = jnp.where(i == 0, 0.0, halo_ref[...])
        pooled = _pooled(u_ref[...], halo, i * ts)
        for g in range(len(POOL_WINDOWS)):
            lanes = slice(g * POOL_GROUP_DIM, (g + 1) * POOL_GROUP_DIM)
            y_ref[:, lanes] = _nn(pooled[g].astype(BF16), w_ref[g]) * s_ref[:, lanes]

    return pl.pallas_call(
        body,
        name="pool_fwd",
        grid=(S // ts,),
        in_specs=[
            pl.BlockSpec((ts, D_POOL), lambda i: (i, 0)),
            pl.BlockSpec((POOL_HALO, D_POOL), lambda i: (jnp.maximum(i * hb - 1, 0), 0)),
            pl.BlockSpec((4, POOL_GROUP_DIM, POOL_GROUP_DIM), lambda i: (0, 0, 0)),
            pl.BlockSpec((1, D_POOL), lambda i: (0, 0)),
        ],
        out_specs=pl.BlockSpec((ts, D_POOL), lambda i: (i, 0)),
        out_shape=jax.ShapeDtypeStruct((S, D_POOL), F32),
        compiler_params=_params(),
    )(u, u, pool_w, pool_scale)


def _head_masks():
    lane = lax.broadcasted_iota(jnp.int32, (1, ATTN_LANES), 1)
    return [jnp.logical_and(lane >= h * HEAD_DIM, lane < (h + 1) * HEAD_DIM) for h in range(HEADS_PER_BLOCK)]


def _tri_masks(t):
    row = lax.broadcasted_iota(jnp.int32, (t, t), 0)
    col = lax.broadcasted_iota(jnp.int32, (t, t), 1)
    return row, col


def _split_bf16(x):
    hi = x.astype(BF16)
    lo = (x - hi.astype(F32)).astype(BF16)
    return hi, lo


def _log_sigmoids(z):
    sp = jnp.log(1.0 + jnp.exp(-jnp.abs(z)))
    ls = jnp.minimum(z, 0.0) - sp
    return ls, ls - z


def _attn_fwd(qkv, shards=()):
    S = qkv.shape[1]
    t = ATTN_TILE
    n = len(shards)
    nblk = D_ATTN // ATTN_LANES

    def body(*refs):
        q_ref, k_ref, v_ref = refs[:3]
        o_ref = refs[3 + n]
        vh_ref, acc_ref, z_ref = refs[4 + 2 * n:7 + 2 * n]
        hp = pl.program_id(0)
        i = pl.program_id(1)
        finish = lambda: None
        if n:
            finish = _run_at(jnp.logical_and(hp == 0, i == 0),
                             jnp.logical_and(hp == nblk - 1, i == (3 * pl.num_programs(1)) // 4),
                             jnp.logical_and(hp == nblk - 1, i == pl.num_programs(1) - 1),
                             _gather_phases(refs[3:3 + n], refs[4 + n:4 + 2 * n], *refs[7 + 2 * n:]))
        masks = _head_masks()

        @pl.when(i == 0)
        def _():
            vv = v_ref[0]
            for h in range(HEADS_PER_BLOCK):
                vh_ref[h] = jnp.where(masks[h], vv, jnp.zeros_like(vv))

        row, col = _tri_masks(t)
        later = (row > col).astype(BF16)
        causal = col < row
        qs = q_ref[0] * ATTN_SCALE
        heads = range(HEADS_PER_BLOCK)
        qh = [jnp.where(masks[h], qs, jnp.zeros_like(qs)) for h in heads]

        def scores(j, slot):
            kj = k_ref[0, pl.ds(pl.multiple_of(j * t, t), t), :]
            for h in heads:
                z_ref[slot, h] = _nt(qh[h], kj)

        def tiles(walk, left, after, carry):
            cs = list(carry)
            ls, tail = {}, {}
            for w, (j, slot, diag, counts) in enumerate(walk):
                for h in heads:
                    ls[w, h], l1m = _log_sigmoids(z_ref[slot, h])
                    if diag:
                        l1m = jnp.where(causal, l1m, 0.0)
                    hi, lo = _split_bf16(l1m)
                    tail[w, h] = _nn(hi, later) + _nn(lo, later) + cs[h]
                    cs[h] = cs[h] + jnp.sum(l1m, axis=1, keepdims=True)
            top = cs[0]
            for h in heads[1:]:
                top = jnp.maximum(top, cs[h])
            go = jnp.logical_and(left > 0, jnp.max(top) > DEAD_LOG)
            for w, (j, slot, diag, counts) in enumerate(walk):
                keys = pl.ds(pl.multiple_of(j * t, t), t)
                for h in heads:
                    a = jnp.exp(ls[w, h] + tail[w, h])
                    if diag:
                        a = jnp.where(causal, a, 0.0)
                    if counts is not None:
                        a = jnp.where(counts, a, 0.0)
                    pv = _nn(a.astype(BF16), vh_ref[h, keys, :])
                    if diag:
                        acc_ref[h] = pv
                    else:
                        acc_ref[h] += pv
            pl.when(go)(lambda: scores(*after))
            return (go, *cs)

        before = jnp.maximum(i - 1, 0)
        scores(i, 0)
        scores(before, 1)
        state = (jnp.int32(0), *tiles([(i, 0, True, None), (before, 1, False, i >= 1)], i - 1,
                                      (jnp.maximum(i - 2, 0), 0), [jnp.zeros((t, 1), F32)] * HEADS_PER_BLOCK))

        def step(state):
            jj = state[0]
            j = i - 2 - jj
            return (jj + 1, *tiles([(j, jj % 2, False, None)], j, (jnp.maximum(j - 1, 0), 1 - jj % 2), state[2:]))

        lax.while_loop(lambda s: s[1], step, state)
        out = acc_ref[0]
        for h in heads[1:]:
            out = out + acc_ref[h]
        o_ref[...] = out
        finish()

    return pl.pallas_call(
        body,
        name="attn_fwd",
        grid=(nblk, S // t),
        in_specs=[
            pl.BlockSpec((1, t, ATTN_LANES), lambda hp, i: (0, i, hp)),
            pl.BlockSpec((1, S, ATTN_LANES), lambda hp, i: (1, 0, hp)),
            pl.BlockSpec((1, S, ATTN_LANES), lambda hp, i: (2, 0, hp)),
        ] + [ANY] * n,
        out_specs=[pl.BlockSpec((t, ATTN_LANES), lambda hp, i: (i, hp))] + [ANY] * n,
        out_shape=[jax.ShapeDtypeStruct((S, D_ATTN), F32)] + _gather_shapes(shards),
        scratch_shapes=[pltpu.VMEM((HEADS_PER_BLOCK, S, ATTN_LANES), BF16),
                        pltpu.VMEM((HEADS_PER_BLOCK * ATTN_ROW_CHUNKS, t // ATTN_ROW_CHUNKS, ATTN_LANES), F32),
                        pltpu.VMEM((2, HEADS_PER_BLOCK * ATTN_ROW_CHUNKS, t // ATTN_ROW_CHUNKS, t), F32)]
        + (_gather_sems(n) if n else []),
        compiler_params=_params(),
    )(qkv, qkv, qkv, *shards)


def _out_proj(y_pool, y_attn, x, g_pool, g_attn, w_out, g2):
    S = x.shape[0]
    ts = ROW_TILE

    def body(yp_ref, ya_ref, x_ref, gp_ref, ga_ref, w_ref, g2_ref, mixed_ref, h1_ref, hn2_ref):
        yp = yp_ref[...]
        ya = ya_ref[...]
        mixed = jnp.concatenate([yp * _rms(yp) * gp_ref[...], ya * _rms(ya) * ga_ref[...]], axis=-1).astype(BF16)
        mixed_ref[...] = mixed
        h1 = x_ref[...] + _nn(mixed, w_ref[...])
        h1_ref[...] = h1
        hn2_ref[...] = (h1 * _rms(h1) * g2_ref[...]).astype(BF16)

    row = lambda w: pl.BlockSpec((ts, w), lambda i: (i, 0))
    vec = lambda w: pl.BlockSpec((1, w), lambda i: (0, 0))
    return pl.pallas_call(
        body,
        name="out_proj",
        grid=(S // ts,),
        in_specs=[row(D_POOL), row(D_ATTN), row(D_MODEL), vec(D_POOL), vec(D_ATTN),
                  pl.BlockSpec((D_MODEL, D_MODEL), lambda i: (0, 0)), vec(D_MODEL)],
        out_specs=[row(D_MODEL), row(D_MODEL), row(D_MODEL)],
        out_shape=[
            jax.ShapeDtypeStruct((S, D_MODEL), BF16),
            jax.ShapeDtypeStruct((S, D_MODEL), F32),
            jax.ShapeDtypeStruct((S, D_MODEL), BF16),
        ],
        compiler_params=_params(),
    )(y_pool, y_attn, x, g_pool, g_attn, w_out, g2)


def _mlp_fwd(hn2, h1, w_up, w_down, g_final, target):
    S = hn2.shape[0]
    ts = MLP_TILE
    nf = D_FF // 1024

    def body(hn2_ref, h1_ref, wu_ref, wd_ref, gf_ref, tg_ref, r_ref, dh2_ref, dh2b_ref, lsq_ref, dgf_ref):
        i = pl.program_id(0)
        hn2v = hn2_ref[...]
        acts = []
        for c in range(nf):
            r = jnp.maximum(_nn(hn2v, wu_ref[c]), 0.0)
            r_ref[:, c * 1024:(c + 1) * 1024] = r.astype(BF16)
            acts.append((r * r).astype(BF16))
        h2 = h1_ref[...] + _nn(jnp.concatenate(acts, axis=1), wd_ref[...])

        @pl.when(i == 0)
        def _():
            lsq_ref[...] = jnp.zeros_like(lsq_ref)
            dgf_ref[...] = jnp.zeros_like(dgf_ref)

        rf = _rms(h2)
        n = h2 * rf
        gf = gf_ref[...]
        e = n * gf - tg_ref[...]
        lsq_ref[...] += jnp.sum(e * e, axis=0, keepdims=True)
        dy = e * (1.0 / D_MODEL)
        dgf_ref[...] += jnp.sum(dy * n, axis=0, keepdims=True)
        dh2 = _rms_bwd(dy, n, rf, gf)
        dh2_ref[...] = dh2
        dh2b_ref[...] = dh2.astype(BF16)

    row = lambda w: pl.BlockSpec((ts, w), lambda i: (i, 0))
    vec = lambda w: pl.BlockSpec((1, w), lambda i: (0, 0))
    once = pl.Buffered(1)
    return pl.pallas_call(
        body,
        name="mlp_fwd",
        grid=(S // ts,),
        in_specs=[row(D_MODEL), row(D_MODEL),
                  pl.BlockSpec((nf, D_MODEL, 1024), lambda i: (0, 0, 0), pipeline_mode=once),
                  pl.BlockSpec((D_FF, D_MODEL), lambda i: (0, 0), pipeline_mode=once),
                  vec(D_MODEL), row(D_MODEL)],
        out_specs=[row(D_FF), row(D_MODEL), row(D_MODEL), vec(D_MODEL), vec(D_MODEL)],
        out_shape=[
            jax.ShapeDtypeStruct((S, D_FF), BF16),
            jax.ShapeDtypeStruct((S, D_MODEL), F32),
            jax.ShapeDtypeStruct((S, D_MODEL), BF16),
            jax.ShapeDtypeStruct((1, D_MODEL), F32),
            jax.ShapeDtypeStruct((1, D_MODEL), F32),
        ],
        compiler_params=_params(),
    )(hn2, h1, w_up, w_down.reshape(D_FF, D_MODEL), g_final, target)


def _mlp_bwd_dx(dh2b, r_act, w_down, w_up, h1, dh2, g2):
    S = h1.shape[0]
    ts = MLP_TILE
    nf = D_FF // 1024

    def body(dh2b_ref, r_ref, wd_ref, wu_ref, h1_ref, dh2_ref, g2_ref, dup_ref, dh1_ref, dg2_ref):
        i = pl.program_id(0)
        dh2b = dh2b_ref[...]
        dhn2 = None
        for c in range(nf):
            chunk = slice(c * 1024, (c + 1) * 1024)
            dup = (_nt(dh2b, wd_ref[c]) * (2.0 * r_ref[:, chunk].astype(F32))).astype(BF16)
            dup_ref[:, chunk] = dup
            part = _nt(dup, wu_ref[c])
            dhn2 = part if dhn2 is None else dhn2 + part

        @pl.when(i == 0)
        def _():
            dg2_ref[...] = jnp.zeros_like(dg2_ref)

        h1v = h1_ref[...]
        r2 = _rms(h1v)
        n2 = h1v * r2
        dg2_ref[...] += jnp.sum(dhn2 * n2, axis=0, keepdims=True)
        dh1_ref[...] = dh2_ref[...] + _rms_bwd(dhn2, n2, r2, g2_ref[...])

    row = lambda w: pl.BlockSpec((ts, w), lambda i: (i, 0))
    vec = lambda w: pl.BlockSpec((1, w), lambda i: (0, 0))
    once = pl.Buffered(1)
    return pl.pallas_call(
        body,
        name="mlp_bwd_dx",
        grid=(S // ts,),
        in_specs=[row(D_MODEL), row(D_FF),
                  pl.BlockSpec((nf, 1024, D_MODEL), lambda i: (0, 0, 0), pipeline_mode=once),
                  pl.BlockSpec((nf, D_MODEL, 1024), lambda i: (0, 0, 0), pipeline_mode=once),
                  row(D_MODEL), row(D_MODEL), vec(D_MODEL)],
        out_specs=[row(D_FF), row(D_MODEL), vec(D_MODEL)],
        out_shape=[
            jax.ShapeDtypeStruct((S, D_FF), BF16),
            jax.ShapeDtypeStruct((S, D_MODEL), F32),
            jax.ShapeDtypeStruct((1, D_MODEL), F32),
        ],
        compiler_params=_params(),
    )(dh2b, r_act, w_down, w_up, h1, dh2, g2)


def _tokens_tn(name, a, b, a_chunked, square_a, partials=()):
    S = a.shape[0]
    ts = min(DW_TOKEN_TILE, S)
    nf = D_FF // 1024
    n = len(partials)

    def body(*refs):
        a_ref, b_ref = refs[:2]
        o_ref = refs[2 + n]
        acc = refs[3 + 2 * n]
        tt = pl.program_id(1)
        finish = lambda: None
        if n:
            blk = pl.program_id(0)
            finish = _run_at(jnp.logical_and(blk == 0, tt == 0), None,
                             jnp.logical_and(blk == nf - 1, tt == pl.num_programs(1) - 1),
                             _reduce_phases(refs[2:2 + n], refs[3 + n:3 + 2 * n], *refs[4 + 2 * n:], hops=HOPS_DIAGONAL))
        av = a_ref[...]
        if square_a:
            af = av.astype(F32)
            av = (af * af).astype(BF16)
        part = _tn(av, b_ref[...])

        @pl.when(tt == 0)
        def _():
            acc[...] = part

        @pl.when(tt > 0)
        def _():
            acc[...] += part

        @pl.when(tt == pl.num_programs(1) - 1)
        def _():
            o_ref[0] = acc[...].astype(BF16)

        finish()

    whole = pl.BlockSpec((ts, 1024), lambda c, tt: (tt, 0))
    chunk = pl.BlockSpec((ts, 1024), lambda c, tt: (tt, c))
    return pl.pallas_call(
        body,
        name=name,
        grid=(nf, S // ts),
        in_specs=([chunk, whole] if a_chunked else [whole, chunk]) + [ANY] * n,
        out_specs=[pl.BlockSpec((1, 1024, 1024), lambda c, tt: (c, 0, 0))] + [ANY] * n,
        out_shape=[jax.ShapeDtypeStruct((nf, 1024, 1024), BF16)] + _reduce_shapes(partials),
        scratch_shapes=[pltpu.VMEM((1024, 1024), F32)] + (_reduce_sems(n) if n else []),
        compiler_params=_params(),
    )(a, b, *partials)


def _out_bwd(dh1, w_out, mixed, y_pool, y_attn, g_pool, g_attn, partials=(), received=()):
    S = dh1.shape[0]
    ts = ROW_TILE
    n = len(partials)

    def body(*refs):
        dh1_ref, w_ref, mixed_ref, yp_ref, ya_ref, gp_ref, ga_ref = refs[:7]
        dyp_ref, dya_ref, dw_ref, dgp_ref, dga_ref = refs[7 + 2 * n:12 + 2 * n]
        dw_acc = refs[12 + 3 * n]
        i = pl.program_id(0)
        finish = lambda: None
        if n:
            finish = _run_at(i == 0, None, i == pl.num_programs(0) - 1,
                             _reduce_phases(refs[7:7 + n], refs[12 + 2 * n:12 + 3 * n], *refs[13 + 3 * n:],
                                            hops=HOPS_SAME_CORE))
        dh1b = dh1_ref[...].astype(BF16)
        dmixed = _nt(dh1b, w_ref[...])
        dw = _tn(mixed_ref[...], dh1b)

        @pl.when(i == 0)
        def _():
            dw_acc[...] = dw
            dgp_ref[...] = jnp.zeros_like(dgp_ref)
            dga_ref[...] = jnp.zeros_like(dga_ref)

        @pl.when(i > 0)
        def _():
            dw_acc[...] += dw

        @pl.when(i == pl.num_programs(0) - 1)
        def _():
            dw_ref[...] = dw_acc[...].astype(BF16)

        for y_ref, g_ref, dy_ref, dg_ref, lanes in (
                (yp_ref, gp_ref, dyp_ref, dgp_ref, slice(0, D_POOL)),
                (ya_ref, ga_ref, dya_ref, dga_ref, slice(D_POOL, D_MODEL))):
            y = y_ref[...]
            r = _rms(y)
            nrm = y * r
            dm = dmixed[:, lanes]
            dg_ref[...] += jnp.sum(dm * nrm, axis=0, keepdims=True)
            dy_ref[...] = _rms_bwd(dm, nrm, r, g_ref[...])

        finish()

    row = lambda w: pl.BlockSpec((ts, w), lambda i: (i, 0))
    vec = lambda w: pl.BlockSpec((1, w), lambda i: (0, 0))
    full = pl.BlockSpec((D_MODEL, D_MODEL), lambda i: (0, 0))
    return pl.pallas_call(
        body,
        name="out_bwd",
        grid=(S // ts,),
        in_specs=[row(D_MODEL), full, row(D_MODEL), row(D_POOL), row(D_ATTN), vec(D_POOL), vec(D_ATTN)]
        + [ANY] * (2 * n),
        out_specs=[row(D_POOL), row(D_ATTN), full, vec(D_POOL), vec(D_ATTN)] + [ANY] * n,
        out_shape=[
            jax.ShapeDtypeStruct((S, D_POOL), F32),
            jax.ShapeDtypeStruct((S, D_ATTN), F32),
            jax.ShapeDtypeStruct((D_MODEL, D_MODEL), BF16),
            jax.ShapeDtypeStruct((1, D_POOL), F32),
            jax.ShapeDtypeStruct((1, D_ATTN), F32),
        ] + [jax.ShapeDtypeStruct(r.shape, r.dtype) for r in received],
        input_output_aliases={7 + n + a: 5 + a for a in range(n)},
        scratch_shapes=[pltpu.VMEM((D_MODEL, D_MODEL), F32)] + (_reduce_sems(n) if n else []),
        compiler_params=_params(),
    )(dh1, w_out, mixed, y_pool, y_attn, g_pool, g_attn, *partials, *received)


def _attn_bwd(qkv, o, do, partials=(), late=(), late_received=()):
    S = qkv.shape[1]
    t = ATTN_TILE
    n = len(partials)
    m = len(late)
    nblk = D_ATTN // ATTN_LANES
    outs_at = 5 + n + 2 * m
    scratch_at = outs_at + 3 + n + m

    def body(*refs):
        q_ref, k_ref, v_ref, o_ref, do_ref = refs[:5]
        dq_ref, dk_ref, dv_ref = refs[outs_at:outs_at + 3]
        kh_ref, dk_acc, dv_acc, dq_acc, z_ref, da_ref = refs[scratch_at:scratch_at + 6]
        hp = pl.program_id(0)
        i = pl.program_id(1)
        first = jnp.logical_and(hp == 0, i == 0)
        last = jnp.logical_and(hp == nblk - 1, i == pl.num_programs(1) - 1)
        finishes = []
        if n:
            finishes.append(_run_at(first, None, last, _reduce_phases(
                refs[5:5 + n], refs[outs_at + 3:outs_at + 3 + n], *refs[scratch_at + 6:scratch_at + 8])))
        if m:
            finishes.append(_run_at(first, None, last, _reduce_phases(
                refs[5 + n:5 + n + m], refs[outs_at + 3 + n:outs_at + 3 + n + m],
                *refs[scratch_at + 6 + (2 if n else 0):], hops=HOPS_REST)))

        def finish():
            for f in finishes:
                f()

        masks = _head_masks()

        @pl.when(i == 0)
        def _():
            kk = k_ref[0]
            for h in range(HEADS_PER_BLOCK):
                kh_ref[h] = jnp.where(masks[h], kk, jnp.zeros_like(kk))
            dk_acc[...] = jnp.zeros_like(dk_acc)
            dv_acc[...] = jnp.zeros_like(dv_acc)

        row, col = _tri_masks(t)
        later = (row > col).astype(BF16)
        from_s = (row >= col).astype(BF16)
        causal = col < row
        qs = q_ref[0] * ATTN_SCALE
        dob = do_ref[...].astype(BF16)
        d_all = dob.astype(F32) * o_ref[...]
        qh = [jnp.where(masks[h], qs, jnp.zeros_like(qs)) for h in range(HEADS_PER_BLOCK)]
        doh = [jnp.where(masks[h], dob, jnp.zeros_like(dob)) for h in range(HEADS_PER_BLOCK)]
        d_row = [jnp.sum(jnp.where(masks[h], d_all, 0.0), axis=1, keepdims=True) for h in range(HEADS_PER_BLOCK)]

        heads = range(HEADS_PER_BLOCK)

        def scores(j, slot):
            keys = pl.ds(pl.multiple_of(j * t, t), t)
            kj = k_ref[0, keys, :]
            vj = v_ref[0, keys, :]
            for h in heads:
                z_ref[slot, h] = _nt(qh[h], kj)
                da_ref[slot, h] = _nt(doh[h], vj)

        def tiles(walk, left, after, carry):
            nh = HEADS_PER_BLOCK
            c_l, c_g = list(carry[:nh]), list(carry[nh:])
            ls, tail, g, before = {}, {}, {}, {}
            for w, (j, slot, diag, counts) in enumerate(walk):
                for h in heads:
                    ls[w, h], l1m = _log_sigmoids(z_ref[slot, h])
                    if diag:
                        l1m = jnp.where(causal, l1m, 0.0)
                    hi, lo = _split_bf16(l1m)
                    tail[w, h] = _nn(hi, later) + _nn(lo, later) + c_l[h]
                    c_l[h] = c_l[h] + jnp.sum(l1m, axis=1, keepdims=True)
            top = c_l[0]
            for h in range(1, nh):
                top = jnp.maximum(top, c_l[h])
            go = jnp.logical_and(left > 0, jnp.max(top) > DEAD_LOG)
            for w, (j, slot, diag, counts) in enumerate(walk):
                keys = pl.ds(pl.multiple_of(j * t, t), t)
                dv = None
                for h in heads:
                    a = jnp.exp(ls[w, h] + tail[w, h])
                    if diag:
                        a = jnp.where(causal, a, 0.0)
                    if counts is not None:
                        a = jnp.where(counts, a, 0.0)
                    ab = a.astype(BF16)
                    g[w, h] = ab.astype(F32) * da_ref[slot, h]
                    ghi, glo = _split_bf16(g[w, h])
                    before[w, h] = d_row[h] - (_nn(ghi, from_s) + _nn(glo, from_s) + c_g[h])
                    c_g[h] = c_g[h] + jnp.sum(g[w, h], axis=1, keepdims=True)
                    part = _tn(ab, doh[h])
                    dv = part if dv is None else dv + part
                dv_acc[keys, :] += dv
            for w, (j, slot, diag, counts) in enumerate(walk):
                keys = pl.ds(pl.multiple_of(j * t, t), t)
                dk = None
                for h in heads:
                    beta = jnp.exp(ls[w, h])
                    dz = g[w, h] * (1.0 - beta) - before[w, h] * beta
                    if diag:
                        dz = jnp.where(causal, dz, 0.0)
                    if counts is not None:
                        dz = jnp.where(counts, dz, 0.0)
                    dzb = dz.astype(BF16)
                    dqh = _nn(dzb, kh_ref[h, keys, :])
                    if diag:
                        dq_acc[h] = dqh
                    else:
                        dq_acc[h] += dqh
                    part = _tn(dzb, qh[h])
                    dk = part if dk is None else dk + part
                dk_acc[keys, :] += dk
            pl.when(go)(lambda: scores(*after))
            return (go, *c_l, *c_g)

        prev = jnp.maximum(i - 1, 0)
        scores(i, 0)
        scores(prev, 1)
        state = (jnp.int32(0), *tiles([(i, 0, True, None), (prev, 1, False, i >= 1)], i - 1,
                                      (jnp.maximum(i - 2, 0), 0),
                                      [jnp.zeros((t, 1), F32)] * (2 * HEADS_PER_BLOCK)))

        def step(state):
            jj = state[0]
            j = i - 2 - jj
            return (jj + 1, *tiles([(j, jj % 2, False, None)], j, (jnp.maximum(j - 1, 0), 1 - jj % 2), state[2:]))

        lax.while_loop(lambda s: s[1], step, state)
        dq = dq_acc[0]
        for h in range(1, HEADS_PER_BLOCK):
            dq = dq + dq_acc[h]
        dq_ref[...] = (dq * ATTN_SCALE).astype(BF16)

        @pl.when(i == pl.num_programs(1) - 1)
        def _():
            dk_ref[...] = dk_acc[...].astype(BF16)
            dv_ref[...] = dv_acc[...].astype(BF16)

        finish()

    qtile = pl.BlockSpec((t, ATTN_LANES), lambda hp, i: (i, hp))
    whole = pl.BlockSpec((S, ATTN_LANES), lambda hp, i: (0, hp))
    return pl.pallas_call(
        body,
        name="attn_bwd",
        grid=(nblk, S // t),
        in_specs=[
            pl.BlockSpec((1, t, ATTN_LANES), lambda hp, i: (0, i, hp)),
            pl.BlockSpec((1, S, ATTN_LANES), lambda hp, i: (1, 0, hp)),
            pl.BlockSpec((1, S, ATTN_LANES), lambda hp, i: (2, 0, hp)),
            qtile, qtile,
        ] + [ANY] * (n + 2 * m),
        out_specs=[qtile, whole, whole] + [ANY] * (n + m),
        out_shape=[jax.ShapeDtypeStruct((S, D_ATTN), BF16)] * 3 + _reduce_shapes(partials)
        + [jax.ShapeDtypeStruct(r.shape, r.dtype) for r in late_received],
        input_output_aliases={5 + n + m + a: 3 + n + a for a in range(m)},
        scratch_shapes=[
            pltpu.VMEM((HEADS_PER_BLOCK, S, ATTN_LANES), BF16),
            pltpu.VMEM((S, ATTN_LANES), F32),
            pltpu.VMEM((S, ATTN_LANES), F32),
            pltpu.VMEM((HEADS_PER_BLOCK, t, ATTN_LANES), F32),
            pltpu.VMEM((2, HEADS_PER_BLOCK, t, t), F32),
            pltpu.VMEM((2, HEADS_PER_BLOCK, t, t), F32),
        ] + (_reduce_sems(n) if n else []) + (_reduce_sems(m) if m else []),
        compiler_params=_params(),
    )(qkv, qkv, qkv, o, do, *partials, *late, *late_received)


def _pool_bwd_w(u, dyp, pool_w, pool_scale):
    S = u.shape[0]
    ts = ROW_TILE
    hb = ts // POOL_HALO

    def body(u_ref, halo_ref, dy_ref, w_ref, s_ref, dp_ref, dw_ref, ds_ref):
        i = pl.program_id(0)
        halo = jnp.where(i == 0, 0.0, halo_ref[...])
        pooled = _pooled(u_ref[...], halo, i * ts)

        @pl.when(i == 0)
        def _():
            dw_ref[...] = jnp.zeros_like(dw_ref)
            ds_ref[...] = jnp.zeros_like(ds_ref)

        for g in range(len(POOL_WINDOWS)):
            lanes = slice(g * POOL_GROUP_DIM, (g + 1) * POOL_GROUP_DIM)
            pg = pooled[g].astype(BF16)
            dy = dy_ref[:, lanes]
            ds_ref[:, lanes] += jnp.sum(dy * _nn(pg, w_ref[g]), axis=0, keepdims=True)
            dmapped = (dy * s_ref[:, lanes]).astype(BF16)
            dp_ref[:, lanes] = _nt(dmapped, w_ref[g])
            dw_ref[g] += _tn(pg, dmapped)

    row = pl.BlockSpec((ts, D_POOL), lambda i: (i, 0))
    vec = pl.BlockSpec((1, D_POOL), lambda i: (0, 0))
    wspec = pl.BlockSpec((4, POOL_GROUP_DIM, POOL_GROUP_DIM), lambda i: (0, 0, 0))
    return pl.pallas_call(
        body,
        name="pool_bwd_w",
        grid=(S // ts,),
        in_specs=[row, pl.BlockSpec((POOL_HALO, D_POOL), lambda i: (jnp.maximum(i * hb - 1, 0), 0)),
                  row, wspec, vec],
        out_specs=[row, wspec, vec],
        out_shape=[
            jax.ShapeDtypeStruct((S, D_POOL), F32),
            jax.ShapeDtypeStruct((4, POOL_GROUP_DIM, POOL_GROUP_DIM), F32),
            jax.ShapeDtypeStruct((1, D_POOL), F32),
        ],
        compiler_params=_params(),
    )(u, u, dyp, pool_w, pool_scale)


def _pool_bwd_u(dpooled):
    S = dpooled.shape[0]
    ts = ROW_TILE
    hb = ts // POOL_HALO
    last = S // ts - 1

    def body(dp_ref, halo_ref, du_ref):
        i = pl.program_id(0)
        dp = dp_ref[...]
        halo = jnp.where(i == last, 0.0, halo_ref[...])
        inv = _pool_counts(i * ts, ts)
        n = ts + POOL_HALO
        for g, w in enumerate(POOL_WINDOWS):
            lanes = slice(g * POOL_GROUP_DIM, (g + 1) * POOL_GROUP_DIM)
            dg = dp[:, lanes]
            acc = jnp.concatenate([dg * inv[g], halo[:, lanes] * (1.0 / w)], axis=0)
            shift = 1
            while shift < w:
                acc = acc + pltpu.roll(acc, n - shift, axis=0)
                shift *= 2
            du_ref[:, lanes] = (acc[:ts, :] - dg).astype(BF16)

    return pl.pallas_call(
        body,
        name="pool_bwd_u",
        grid=(S // ts,),
        in_specs=[pl.BlockSpec((ts, D_POOL), lambda i: (i, 0)),
                  pl.BlockSpec((POOL_HALO, D_POOL), lambda i: (jnp.minimum((i + 1) * hb, (last + 1) * hb - 1), 0))],
        out_specs=pl.BlockSpec((ts, D_POOL), lambda i: (i, 0)),
        out_shape=jax.ShapeDtypeStruct((S, D_POOL), BF16),
        compiler_params=_params(),
    )(dpooled, dpooled)


def _in_bwd_dw(hn, dprojs, late=(), late_received=()):
    S = hn.shape[0]
    ts = min(DW_TOKEN_TILE, S)
    m = len(late)

    def body(*refs):
        hn_ref, du_ref, dq_ref, dk_ref, dv_ref = refs[:5]
        o_ref = refs[5 + 2 * m]
        acc = refs[6 + 3 * m]
        j = pl.program_id(0)
        tt = pl.program_id(1)
        finish = lambda: None
        if m:
            finish = _run_at(jnp.logical_and(j == 0, tt == 0), None,
                             jnp.logical_and(j == 3, tt == pl.num_programs(1) - 1),
                             _reduce_phases(refs[5:5 + m], refs[6 + 2 * m:6 + 3 * m], *refs[7 + 3 * m:],
                                            hops=HOPS_REST))
        for k, dp_ref in enumerate((du_ref, dq_ref, dk_ref, dv_ref)):
            @pl.when(j == k)
            def _(dp_ref=dp_ref):
                part = _tn(hn_ref[...], dp_ref[...])

                @pl.when(tt == 0)
                def _():
                    acc[...] = part

                @pl.when(tt > 0)
                def _():
                    acc[...] += part

        @pl.when(tt == pl.num_programs(1) - 1)
        def _():
            o_ref[0] = acc[...].astype(BF16)

        finish()

    def taken_at(k):
        return lambda j, tt: (jnp.where(j == k, tt, 0), 0)

    return pl.pallas_call(
        body,
        name="in_bwd_dw",
        grid=(4, S // ts),
        in_specs=[pl.BlockSpec((ts, D_MODEL), lambda j, tt: (tt, 0))]
        + [pl.BlockSpec((ts, 512), taken_at(k)) for k in range(4)] + [ANY] * (2 * m),
        out_specs=[pl.BlockSpec((1, D_MODEL, 512), lambda j, tt: (j, 0, 0))] + [ANY] * m,
        out_shape=[jax.ShapeDtypeStruct((4, D_MODEL, 512), BF16)]
        + [jax.ShapeDtypeStruct(r.shape, r.dtype) for r in late_received],
        input_output_aliases={5 + m + a: 1 + a for a in range(m)},
        scratch_shapes=[pltpu.VMEM((D_MODEL, 512), F32)] + (_reduce_sems(m) if m else []),
        compiler_params=_params(),
    )(hn, *dprojs, *late, *late_received)


def _in_bwd_dx(du, dq, dk, dv, w_in, x, dh1, g1, partials=()):
    S = x.shape[0]
    ts = ROW_TILE
    n = len(partials)

    def body(*refs):
        du_ref, dq_ref, dk_ref, dv_ref, w_ref, x_ref, dh1_ref, g_ref = refs[:8]
        dx_ref, dg_ref = refs[8 + n:10 + n]
        i = pl.program_id(0)
        finish = lambda: None
        if n:
            finish = _run_at(i == 0, None, i == pl.num_programs(0) - 1,
                             _reduce_phases(refs[8:8 + n], refs[10 + n:10 + 2 * n], *refs[10 + 2 * n:],
                                            hops=HOPS_SAME_CORE + HOPS_REST))

        @pl.when(i == 0)
        def _():
            dg_ref[...] = jnp.zeros_like(dg_ref)

        dhn = None
        for j, dp_ref in enumerate((du_ref, dq_ref, dk_ref, dv_ref)):
            part = _nt(dp_ref[...], w_ref[j])
            dhn = part if dhn is None else dhn + part
        xv = x_ref[...]
        r1 = _rms(xv)
        n1 = xv * r1
        dg_ref[...] += jnp.sum(dhn * n1, axis=0, keepdims=True)
        dx_ref[...] = dh1_ref[...] + _rms_bwd(dhn, n1, r1, g_ref[...])
        finish()

    row = lambda w: pl.BlockSpec((ts, w), lambda i: (i, 0))
    vec = pl.BlockSpec((1, D_MODEL), lambda i: (0, 0))
    wspec = pl.BlockSpec((4, D_MODEL, 512), lambda i: (0, 0, 0), pipeline_mode=pl.Buffered(1))
    return pl.pallas_call(
        body,
        name="in_bwd_dx",
        grid=(S // ts,),
        in_specs=[row(512), row(512), row(512), row(512), wspec, row(D_MODEL), row(D_MODEL), vec] + [ANY] * n,
        out_specs=[row(D_MODEL), vec] + [ANY] * n,
        out_shape=[
            jax.ShapeDtypeStruct((S, D_MODEL), F32),
            jax.ShapeDtypeStruct((1, D_MODEL), F32),
        ] + _reduce_shapes(partials),
        scratch_shapes=_reduce_sems(n) if n else [],
        compiler_params=_params(),
    )(du, dq, dk, dv, w_in, x, dh1, g1, *partials)


def _pieces(g):
    return g.reshape(N_CHIPS, 2, -1, g.shape[-1])


def _local_step(x, target, w_in, pool_w, small, full=None, shards=None):
    spread = shards is not None
    if spread:
        hn, u, qkv, w_down = _in_proj(x, small["norm1_g"], w_in, shards[2:])
        w_down = w_down.reshape(N_CHIPS, 1024, D_MODEL)
    else:
        hn, u, qkv = _in_proj(x, small["norm1_g"], w_in)
        w_out, w_up, w_down = full
    y_pool = _pool_fwd(u, pool_w, small["pool_scale"])
    if spread:
        y_attn, w_out, w_up = _attn_fwd(qkv, shards[:2])
        w_out = w_out.reshape(D_MODEL, D_MODEL)
        w_up = w_up.reshape(N_CHIPS, D_MODEL, 1024)
    else:
        (y_attn,) = _attn_fwd(qkv)
    mixed, h1, hn2 = _out_proj(y_pool, y_attn, x, small["pool_out_g"], small["attn_out_g"], w_out, small["norm2_g"])
    r_act, dh2, dh2b, lsq, dgf = _mlp_fwd(hn2, h1, w_up, w_down, small["final_g"], target)

    dup, dh1, dg2 = _mlp_bwd_dx(dh2b, r_act, w_down, w_up, h1, dh2, small["norm2_g"])
    (dw_down,) = _tokens_tn("mlp_bwd_dw_down", r_act, dh2b, True, True)
    out_args = (dh1, w_out, mixed, y_pool, y_attn, small["pool_out_g"], small["attn_out_g"])
    if spread:
        dw_up, got_down = _tokens_tn("mlp_bwd_dw_up", hn2, dup, False, False, [_pieces(dw_down)])
        dyp, dya, dw_out, dgp, dga, got_down = _out_bwd(*out_args, [_pieces(dw_down)], [got_down])
        dq, dk, dv, got_out, got_up = _attn_bwd(qkv, y_attn, dya, [_pieces(dw_out), _pieces(dw_up)])
    else:
        (dw_up,) = _tokens_tn("mlp_bwd_dw_up", hn2, dup, False, False)
        dyp, dya, dw_out, dgp, dga = _out_bwd(*out_args)
        dq, dk, dv = _attn_bwd(qkv, y_attn, dya)
    dpooled, dpool_w, dpool_scale = _pool_bwd_w(u, dyp, pool_w, small["pool_scale"])
    du = _pool_bwd_u(dpooled)
    if spread:
        dw_in, got_down = _in_bwd_dw(hn, (du, dq, dk, dv), [_pieces(dw_down)], [got_down])
    else:
        (dw_in,) = _in_bwd_dw(hn, (du, dq, dk, dv))
    if spread:
        dx, dg1, got_in = _in_bwd_dx(du, dq, dk, dv, w_in, x, dh1, small["norm1_g"], [_pieces(dw_in)])
    else:
        dx, dg1 = _in_bwd_dx(du, dq, dk, dv, w_in, x, dh1, small["norm1_g"])

    big = {"w_in": dw_in, "w_out": dw_out, "w_up": dw_up, "w_down": dw_down}
    if spread:
        big["received"] = {"w_in": got_in, "w_out": got_out, "w_up": got_up, "w_down": got_down}
    little = {"norm1_g": dg1, "pool_w": dpool_w, "pool_scale": dpool_scale, "pool_out_g": dgp,
              "attn_out_g": dga, "norm2_g": dg2, "final_g": dgf, "loss_sq": lsq}
    return dx, big, little


def _place():
    x, y, c = lax.axis_index("x"), lax.axis_index("y"), lax.axis_index("c")
    other_chips = [(1 - x, y), (x, 1 - y), (1 - x, 1 - y)]
    return x, y, c, other_chips


def _chip_index(chip):
    return 2 * chip[0] + chip[1]


def _gather_shapes(shards):
    return [jax.ShapeDtypeStruct((N_CHIPS,) + s.shape, s.dtype) for s in shards]


def _gather_sems(n):
    return [pltpu.SemaphoreType.DMA((n, 7)), pltpu.SemaphoreType.DMA((n, 7))]


def _gather_phases(ins, outs, send_sems, recv_sems):
    n = len(ins)
    x, y, c, chips = _place()
    me = _chip_index((x, y))
    sibling = (x, y, 1 - c)

    def copy(a, k, chip_idx, half, to, src=None):
        dst = outs[a].at[chip_idx, half]
        return pltpu.make_async_remote_copy(
            src_ref=dst if src is None else src, dst_ref=dst,
            send_sem=send_sems.at[a, k], recv_sem=recv_sems.at[a, k],
            device_id=to, device_id_type=MESH)

    def own_chip(a, to):
        return pltpu.make_async_remote_copy(
            src_ref=ins[a], dst_ref=outs[a].at[me],
            send_sem=send_sems.at[a, 0], recv_sem=recv_sems.at[a, 0],
            device_id=to, device_id_type=MESH)

    def first(a):
        return [own_chip(a, sibling)] + [
            copy(a, 1 + j, me, c, (*chip, c), src=ins[a].at[c]) for j, chip in enumerate(chips)]

    def passed(a, j):
        return copy(a, 4 + j, _chip_index(chips[j]), c, sibling)

    def start():
        for a in range(n):
            for cp in first(a):
                cp.start()

    def forward():
        for a in range(n):
            for j, chip in enumerate(chips):
                copy(a, 1 + j, _chip_index(chip), c, (x, y, c)).wait_recv()
                passed(a, j).start()

    def finish():
        for a in range(n):
            own_chip(a, (x, y, c)).wait_recv()
            for j, chip in enumerate(chips):
                copy(a, 4 + j, _chip_index(chip), 1 - c, (x, y, c)).wait_recv()
        for a in range(n):
            for cp in first(a):
                cp.wait_send()
            for j in range(len(chips)):
                passed(a, j).wait_send()

    return start, forward, finish


def _gather_weights(shards):
    n = len(shards)

    def body(*refs):
        start, forward, finish = _gather_phases(refs[:n], refs[n:2 * n], *refs[2 * n:])
        start()
        forward()
        finish()

    return pl.pallas_call(
        body,
        name="gather_weights",
        in_specs=[ANY] * n,
        out_specs=[ANY] * n,
        out_shape=_gather_shapes(shards),
        scratch_shapes=_gather_sems(n),
    )(*shards)


def _reduce_shapes(partials):
    return [jax.ShapeDtypeStruct((N_DEV,) + p.shape[2:], p.dtype) for p in partials]


def _reduce_sems(n):
    return [pltpu.SemaphoreType.DMA((n, N_DEV)), pltpu.SemaphoreType.DMA((n, N_DEV))]


def _reduce_phases(ins, outs, send_sems, recv_sems, hops=None):
    n = len(ins)
    x, y, c, _ = _place()
    me = 4 * x + 2 * y + c

    def to_peer(a, k):
        return pltpu.make_async_remote_copy(
            src_ref=ins[a].at[k // 2, k % 2], dst_ref=outs[a].at[me],
            send_sem=send_sems.at[a, k], recv_sem=recv_sems.at[a, me],
            device_id=(k // 4, (k // 2) % 2, k % 2), device_id_type=MESH)

    def from_peer(a, k):
        return pltpu.make_async_remote_copy(
            src_ref=ins[a].at[k // 2, k % 2], dst_ref=outs[a].at[k],
            send_sem=send_sems.at[a, k], recv_sem=recv_sems.at[a, k],
            device_id=(x, y, c), device_id_type=MESH)

    def taken(k):
        if hops is None:
            return k != me
        hit = (k ^ hops[0]) == me
        for d in hops[1:]:
            hit = jnp.logical_or(hit, (k ^ d) == me)
        return hit

    def start():
        for a in range(n):
            for k in range(N_DEV):
                @pl.when(taken(k))
                def _(a=a, k=k):
                    to_peer(a, k).start()

    def finish():
        for a in range(n):
            for k in range(N_DEV):
                @pl.when(taken(k))
                def _(a=a, k=k):
                    from_peer(a, k).wait_recv()
                    to_peer(a, k).wait_send()

    return start, finish


def _sum_pieces(where, parts, own):
    _, h, cols = parts.shape
    hb = min(h, 256)

    def body(where_ref, *refs):
        me = where_ref[0]
        acc = None
        for k in range(N_DEV):
            piece = jnp.where(k == me, refs[N_DEV][0, 0], refs[k][0]).astype(F32)
            acc = piece if acc is None else acc + piece
        refs[N_DEV + 1][0] = acc

    def sent_by(k):
        return lambda r, w: (jnp.where(w[0] == k, (k + 1) % N_DEV, k), r, 0)

    return pl.pallas_call(
        body,
        name="sum_pieces",
        grid_spec=pltpu.PrefetchScalarGridSpec(
            num_scalar_prefetch=1,
            grid=(h // hb,),
            in_specs=[pl.BlockSpec((1, hb, cols), sent_by(k)) for k in range(N_DEV)]
            + [pl.BlockSpec((1, 1, hb, cols), lambda r, w: (w[1], w[2], r, 0))],
            out_specs=pl.BlockSpec((1, hb, cols), lambda r, w: (w[2], r, 0)),
        ),
        out_shape=jax.ShapeDtypeStruct((2, h, cols), F32),
    )(where, *([parts] * N_DEV), own)


def _join_halves(halves):
    n = len(halves)

    def body(*refs):
        outs = refs[n:2 * n]
        send_sems, recv_sems = refs[2 * n:]
        x, y, c, _ = _place()
        sends = [
            pltpu.make_async_remote_copy(
                src_ref=outs[a].at[c], dst_ref=outs[a].at[c],
                send_sem=send_sems.at[a], recv_sem=recv_sems.at[a],
                device_id=(x, y, 1 - c), device_id_type=MESH)
            for a in range(n)]
        for cp in sends:
            cp.start()
        for a in range(n):
            pltpu.make_async_remote_copy(
                src_ref=outs[a].at[c], dst_ref=outs[a].at[1 - c],
                send_sem=send_sems.at[a], recv_sem=recv_sems.at[a],
                device_id=(x, y, c), device_id_type=MESH).wait_recv()
        for cp in sends:
            cp.wait_send()

    return pl.pallas_call(
        body,
        name="join_halves",
        in_specs=[ANY] * n,
        out_specs=[ANY] * n,
        out_shape=[jax.ShapeDtypeStruct(s.shape, s.dtype) for s in halves],
        input_output_aliases={a: a for a in range(n)},
        scratch_shapes=[pltpu.SemaphoreType.DMA((n,)), pltpu.SemaphoreType.DMA((n,))],
    )(*halves)


def _adamw(w, g, m, v):
    m = ADAM_B1 * m + (1.0 - ADAM_B1) * g
    v = ADAM_B2 * v + (1.0 - ADAM_B2) * jnp.square(g)
    m_hat = m / (1.0 - ADAM_B1 ** ADAM_STEP)
    v_hat = v / (1.0 - ADAM_B2 ** ADAM_STEP)
    delta = -ADAM_LR * (m_hat / (jnp.sqrt(v_hat) + ADAM_EPS) + ADAM_WD * w)
    return delta, m, v


def _adamw_big(w, g, m, v):
    rows, cols = w.shape
    rb = min(rows, 256)

    def body(w_ref, g_ref, m_ref, v_ref, d_ref, mo_ref, vo_ref):
        d_ref[...], mo_ref[...], vo_ref[...] = _adamw(w_ref[...], g_ref[...], m_ref[...], v_ref[...])

    blk = pl.BlockSpec((rb, cols), lambda r: (r, 0))
    return pl.pallas_call(
        body,
        name="adamw_big",
        grid=(rows // rb,),
        in_specs=[blk] * 4,
        out_specs=[blk] * 3,
        out_shape=[jax.ShapeDtypeStruct(w.shape, F32)] * 3,
    )(w, g, m, v)


SMALL_ORDER = ("pool_w", "norm1_g", "pool_scale", "pool_out_g", "attn_out_g", "norm2_g", "final_g")
SUBLANES = 8


def _pack(parts):
    rows = []
    for p in parts:
        p = p.reshape(-1, LANES)
        pad = (-p.shape[0]) % SUBLANES
        if pad:
            p = jnp.pad(p, ((0, pad), (0, 0)))
        rows.append(p)
    return jnp.concatenate(rows, axis=0)


def _unpack(slab, shapes):
    out, r = [], 0
    for shp in shapes:
        size = 1
        for d in shp:
            size *= d
        nrow = size // LANES
        out.append(slab[r:r + nrow].reshape(shp))
        r += nrow + (-nrow) % SUBLANES
    return out


def _small_step(partials, w, m, v, loss_rows, late=(), late_received=()):
    rows = partials.shape[0]
    nl = len(late)

    def body(*refs):
        p_ref, w_ref, m_ref, v_ref = refs[:4]
        g_ref, d_ref, mo_ref, vo_ref, loss_ref = refs[4 + 2 * nl:9 + 2 * nl]
        buf, send_sems, recv_sems = refs[9 + 3 * nl:12 + 3 * nl]
        start_late, finish_late = (lambda: None), (lambda: None)
        if nl:
            start_late, finish_late = _reduce_phases(
                refs[4:4 + nl], refs[9 + 2 * nl:9 + 3 * nl], *refs[12 + 3 * nl:], hops=HOPS_DIAGONAL)
        start_late()
        x, y, c, _ = _place()
        me = 4 * x + 2 * y + c
        for k in range(N_DEV):
            @pl.when(k != me)
            def _(k=k):
                pltpu.make_async_remote_copy(
                    src_ref=p_ref, dst_ref=buf.at[me],
                    send_sem=send_sems.at[k], recv_sem=recv_sems.at[me],
                    device_id=(k // 4, (k // 2) % 2, k % 2), device_id_type=MESH).start()
        buf[me] = p_ref[...]
        for k in range(N_DEV):
            @pl.when(k != me)
            def _(k=k):
                pltpu.make_async_remote_copy(
                    src_ref=p_ref, dst_ref=buf.at[k],
                    send_sem=send_sems.at[k], recv_sem=recv_sems.at[k],
                    device_id=(x, y, c), device_id_type=MESH).wait()
        g = buf[0]
        for k in range(1, N_DEV):
            g = g + buf[k]
        g_ref[...] = g
        d_ref[...], mo_ref[...], vo_ref[...] = _adamw(w_ref[...], g, m_ref[...], v_ref[...])
        loss = (0.5 / D_MODEL) * jnp.sum(g[rows - loss_rows:, :])
        loss_ref[...] = jnp.full(loss_ref.shape, loss, F32)
        finish_late()

    vm = pl.BlockSpec(memory_space=pltpu.VMEM)
    slab = jax.ShapeDtypeStruct((rows, LANES), F32)
    return pl.pallas_call(
        body,
        name="small_step",
        in_specs=[vm] * 4 + [ANY] * (2 * nl),
        out_specs=[vm] * 5 + [ANY] * nl,
        out_shape=[slab, slab, slab, slab, jax.ShapeDtypeStruct((SUBLANES, LANES), F32)]
        + [jax.ShapeDtypeStruct(r.shape, r.dtype) for r in late_received],
        input_output_aliases={4 + nl + a: 5 + a for a in range(nl)},
        scratch_shapes=[pltpu.VMEM((N_DEV, rows, LANES), F32),
                        pltpu.SemaphoreType.DMA((N_DEV,)), pltpu.SemaphoreType.DMA((N_DEV,))]
        + (_reduce_sems(nl) if nl else []),
    )(partials, w, m, v, *late, *late_received)


BIG_ORDER = ("w_in", "w_out", "w_up", "w_down")
WEIGHT_ORDER = ("norm1_g", "w_in", "pool_w", "pool_scale", "pool_out_g", "attn_out_g", "w_out", "norm2_g",
                "w_up", "w_down", "final_g")


def _halves(a):
    return a.reshape(2, a.shape[0] // 2, a.shape[1])


def kernel(x, norm1_g, w_in, pool_w, pool_scale, pool_out_g, attn_out_g, w_out, norm2_g, w_up, w_down, final_g, loss_target, m_norm1_g, m_w_in, m_pool_w, m_pool_scale, m_pool_out_g, m_attn_out_g, m_w_out, m_norm2_g, m_w_up, m_w_down, m_final_g, v_norm1_g, v_w_in, v_pool_w, v_pool_scale, v_pool_out_g, v_attn_out_g, v_w_out, v_norm2_g, v_w_up, v_w_down, v_final_g):
    w = dict(norm1_g=norm1_g, w_in=w_in, pool_w=pool_w, pool_scale=pool_scale, pool_out_g=pool_out_g,
             attn_out_g=attn_out_g, w_out=w_out, norm2_g=norm2_g, w_up=w_up, w_down=w_down, final_g=final_g)
    m = dict(norm1_g=m_norm1_g, w_in=m_w_in, pool_w=m_pool_w, pool_scale=m_pool_scale, pool_out_g=m_pool_out_g,
             attn_out_g=m_attn_out_g, w_out=m_w_out, norm2_g=m_norm2_g, w_up=m_w_up, w_down=m_w_down,
             final_g=m_final_g)
    v = dict(norm1_g=v_norm1_g, w_in=v_w_in, pool_w=v_pool_w, pool_scale=v_pool_scale, pool_out_g=v_pool_out_g,
             attn_out_g=v_attn_out_g, w_out=v_w_out, norm2_g=v_norm2_g, w_up=v_w_up, w_down=v_w_down,
             final_g=v_final_g)

    shards = {n: _halves(w[n].astype(BF16)) for n in BIG_ORDER}
    (w_in_g,) = _gather_weights([shards["w_in"]])
    small = {n: w[n].reshape(1, -1) for n in ("norm1_g", "pool_scale", "pool_out_g", "attn_out_g", "norm2_g", "final_g")}
    dx, big, little = _local_step(
        x[0], loss_target[0], w_in_g.reshape(N_CHIPS, D_MODEL, 512), pool_w.astype(BF16), small,
        shards=[shards["w_out"], shards["w_up"], shards["w_down"]])

    grads, deltas, new_m, new_v = {}, {}, {}, {}
    loss_rows = D_MODEL // LANES
    slab_g = _pack([little[n] for n in SMALL_ORDER] + [little["loss_sq"]])
    zeros = jnp.zeros((loss_rows, LANES), F32)
    slab_w = _pack([w[n] for n in SMALL_ORDER] + [zeros])
    slab_m = _pack([m[n] for n in SMALL_ORDER] + [zeros])
    slab_v = _pack([v[n] for n in SMALL_ORDER] + [zeros])
    received = big.pop("received")
    g_s, d_s, m_s, v_s, loss, received["w_in"] = _small_step(
        slab_g, slab_w, slab_m, slab_v, loss_rows, [_pieces(big["w_in"])], [received["w_in"]])
    shapes = [w[n].shape for n in SMALL_ORDER]
    for slab, dst in ((g_s, grads), (d_s, deltas), (m_s, new_m), (v_s, new_v)):
        for n, val in zip(SMALL_ORDER, _unpack(slab, shapes)):
            dst[n] = val

    xi, yi, ci = lax.axis_index("x"), lax.axis_index("y"), lax.axis_index("c")
    where = jnp.stack([4 * xi + 2 * yi + ci, 2 * xi + yi, ci]).astype(jnp.int32)
    full = _join_halves([_sum_pieces(where, received[n], _pieces(big[n])) for n in BIG_ORDER])
    for n, g in zip(BIG_ORDER, full):
        grads[n] = g.reshape(w[n].shape)
        deltas[n], new_m[n], new_v[n] = _adamw_big(w[n], grads[n], m[n], v[n])

    return (loss[0, 0], dx[None], *[grads[n] for n in WEIGHT_ORDER], *[deltas[n] for n in WEIGHT_ORDER],
            *[new_m[n] for n in WEIGHT_ORDER], *[new_v[n] for n in WEIGHT_ORDER])
```

```python
import functools

import jax
import jax.numpy as jnp
from jax import lax
from jax.experimental import pallas as pl
from jax.experimental.pallas import tpu as pltpu

F32 = jnp.float32
BF16 = jnp.bfloat16

D_MODEL = 1024
D_POOL = 512
D_ATTN = 512
POOL_WINDOWS = (2, 4, 8, 16)
POOL_GROUP_DIM = 128
POOL_HALO = 16
HEAD_DIM = 64
HEADS_PER_BLOCK = 4
ATTN_LANES = HEADS_PER_BLOCK * HEAD_DIM
D_FF = 4096
N_CHIPS = 4
N_DEV = 8
EPS = 1e-6
ATTN_SCALE = 0.125
ATTN_TILE = 256
ATTN_ROW_CHUNKS = 1
DEAD_LOG = -105.0
LOG2E = 1.4426950408889634
Q_PRESCALE = ATTN_SCALE * LOG2E
DEAD_LOG2 = DEAD_LOG * LOG2E
ROW_TILE = 512
MLP_TILE = 512
DW_TOKEN_TILE = 2048
LANES = 128

ADAM_LR = 0.001
ADAM_B1 = 0.9
ADAM_B2 = 0.999
ADAM_EPS = 1e-08
ADAM_WD = 0.01
ADAM_STEP = 10

HOPS_DIAGONAL = (6, 7)
HOPS_SAME_CORE = (2, 4)
HOPS_REST = (1, 3, 5)

MESH = pl.DeviceIdType.MESH
ANY = pl.BlockSpec(memory_space=pl.ANY)
VMEM_LIMIT = 56 * 1024 * 1024


def _nn(a, b):
    return jnp.dot(a, b, preferred_element_type=F32)


def _nt(a, b):
    return lax.dot_general(a, b, (((1,), (1,)), ((), ())), preferred_element_type=F32)


def _tn(a, b):
    return lax.dot_general(a, b, (((0,), (0,)), ((), ())), preferred_element_type=F32)


def _rms(x):
    return lax.rsqrt(jnp.mean(x * x, axis=-1, keepdims=True) + EPS)


def _rms_bwd(dy, n, r, g):
    dn = dy * g
    return r * (dn - n * jnp.mean(dn * n, axis=-1, keepdims=True))


def _params(**kw):
    return pltpu.CompilerParams(vmem_limit_bytes=VMEM_LIMIT, **kw)


def _run_at(first, middle, last, phases):
    pl.when(first)(phases[0])
    if len(phases) == 3:
        pl.when(middle)(phases[1])
    return lambda: pl.when(last)(phases[-1])


def _in_proj(x, g1, w_in, shards=()):
    S = x.shape[0]
    ts = ROW_TILE
    n = len(shards)

    def body(*refs):
        x_ref, g_ref, w_ref = refs[:3]
        hn_ref, u_ref, qkv_ref = refs[3 + n:6 + n]
        i = pl.program_id(0)
        finish = lambda: None
        if n:
            steps = pl.num_programs(0)
            finish = _run_at(i == 0, i == jnp.maximum(steps - 2, 0), i == steps - 1,
                             _gather_phases(refs[3:3 + n], refs[6 + n:6 + 2 * n], *refs[6 + 2 * n:]))

        xf = x_ref[...]
        hn = (xf * _rms(xf) * g_ref[...]).astype(BF16)
        hn_ref[...] = hn
        u_ref[...] = _nn(hn, w_ref[0])
        qkv_ref[0] = (_nn(hn, w_ref[1]) * Q_PRESCALE).astype(BF16)
        for j in range(2, 4):
            qkv_ref[j - 1] = _nn(hn, w_ref[j]).astype(BF16)

        finish()

    return pl.pallas_call(
        body,
        name="in_proj",
        grid=(S // ts,),
        in_specs=[
            pl.BlockSpec((ts, D_MODEL), lambda i: (i, 0)),
            pl.BlockSpec((1, D_MODEL), lambda i: (0, 0)),
            pl.BlockSpec((4, D_MODEL, 512), lambda i: (0, 0, 0), pipeline_mode=pl.Buffered(1)),
        ] + [ANY] * n,
        out_specs=[
            pl.BlockSpec((ts, D_MODEL), lambda i: (i, 0)),
            pl.BlockSpec((ts, D_POOL), lambda i: (i, 0)),
            pl.BlockSpec((3, ts, 512), lambda i: (0, i, 0)),
        ] + [ANY] * n,
        out_shape=[
            jax.ShapeDtypeStruct((S, D_MODEL), BF16),
            jax.ShapeDtypeStruct((S, D_POOL), F32),
            jax.ShapeDtypeStruct((3, S, 512), BF16),
        ] + _gather_shapes(shards),
        scratch_shapes=_gather_sems(n) if n else [],
        compiler_params=_params(),
    )(x, g1, w_in, *shards)


def _pool_counts(first_row, rows):
    t = first_row + lax.broadcasted_iota(jnp.int32, (rows, 1), 0)
    return [1.0 / jnp.minimum(t + 1, w).astype(F32) for w in POOL_WINDOWS]


def _pooled(u_tile, halo, first_row):
    ts = u_tile.shape[0]
    inv = _pool_counts(first_row, ts)
    outs = []
    for g, w in enumerate(POOL_WINDOWS):
        lanes = slice(g * POOL_GROUP_DIM, (g + 1) * POOL_GROUP_DIM)
        xg = u_tile[:, lanes]
        acc = jnp.concatenate([halo[:, lanes], xg], axis=0)
        shift = 1
        while shift < w:
            acc = acc + pltpu.roll(acc, shift, axis=0)
            shift *= 2
        outs.append(acc[POOL_HALO:, :] * inv[g] - xg)
    return outs


def _pool_fwd(u, pool_w, pool_scale):
    S = u.shape[0]
    ts = ROW_TILE
    hb = ts // POOL_HALO

    def body(u_ref, halo_ref, w_ref, s_ref, y_ref):
        i = pl.program_id(0)
        halo = jnp.where(i == 0, 0.0, halo_ref[...])
        pooled = _pooled(u_ref[...], halo, i * ts)
        for g in range(len(POOL_WINDOWS)):
            lanes = slice(g * POOL_GROUP_DIM, (g + 1) * POOL_GROUP_DIM)
            y_ref[:, lanes] = _nn(pooled[g].astype(BF16), w_ref[g]) * s_ref[:, lanes]

    return pl.pallas_call(
        body,
        name="pool_fwd",
        grid=(S // ts,),
        in_specs=[
            pl.BlockSpec((ts, D_POOL), lambda i: (i, 0)),
            pl.BlockSpec((POOL_HALO, D_POOL), lambda i: (jnp.maximum(i * hb - 1, 0), 0)),
            pl.BlockSpec((4, POOL_GROUP_DIM, POOL_GROUP_DIM), lambda i: (0, 0, 0)),
            pl.BlockSpec((1, D_POOL), lambda i: (0, 0)),
        ],
        out_specs=pl.BlockSpec((ts, D_POOL), lambda i: (i, 0)),
        out_shape=jax.ShapeDtypeStruct((S, D_POOL), F32),
        compiler_params=_params(),
    )(u, u, pool_w, pool_scale)


def _head_masks():
    lane = lax.broadcasted_iota(jnp.int32, (1, ATTN_LANES), 1)
    return [jnp.logical_and(lane >= h * HEAD_DIM, lane < (h + 1) * HEAD_DIM) for h in range(HEADS_PER_BLOCK)]


def _tri_masks(t):
    row = lax.broadcasted_iota(jnp.int32, (t, t), 0)
    col = lax.broadcasted_iota(jnp.int32, (t, t), 1)
    return row, col


def _split_bf16(x):
    hi = x.astype(BF16)
    lo = (x - hi.astype(F32)).astype(BF16)
    return hi, lo


def _log_sigmoids(z):
    sp = jnp.log2(1.0 + jnp.exp2(-jnp.abs(z)))
    ls = jnp.minimum(z, 0.0) - sp
    return ls, ls - z


def _attn_fwd(qkv, shards=()):
    S = qkv.shape[1]
    t = ATTN_TILE
    n = len(shards)
    nblk = D_ATTN // ATTN_LANES

    def body(*refs):
        q_ref, k_ref, v_ref = refs[:3]
        o_ref = refs[3 + n]
        vh_ref, acc_ref, z_ref = refs[4 + 2 * n:7 + 2 * n]
        hp = pl.program_id(0)
        i = pl.program_id(1)
        finish = lambda: None
        if n:
            finish = _run_at(jnp.logical_and(hp == 0, i == 0),
                             jnp.logical_and(hp == nblk - 1, i == (3 * pl.num_programs(1)) // 4),
                             jnp.logical_and(hp == nblk - 1, i == pl.num_programs(1) - 1),
                             _gather_phases(refs[3:3 + n], refs[4 + n:4 + 2 * n], *refs[7 + 2 * n:]))
        masks = _head_masks()

        @pl.when(i == 0)
        def _():
            vv = v_ref[0]
            for h in range(HEADS_PER_BLOCK):
                vh_ref[h] = jnp.where(masks[h], vv, jnp.zeros_like(vv))

        row, col = _tri_masks(t)
        later = (row > col).astype(BF16)
        causal = col < row
        qs = q_ref[0]
        heads = range(HEADS_PER_BLOCK)
        qh = [jnp.where(masks[h], qs, jnp.zeros_like(qs)) for h in heads]

        def scores(j, slot):
            kj = k_ref[0, pl.ds(pl.multiple_of(j * t, t), t), :]
            for h in heads:
                z_ref[slot, h] = _nt(qh[h], kj)

        def tiles(walk, left, after, carry):
            cs = list(carry)
            ls, tail = {}, {}
            for w, (j, slot, diag, counts) in enumerate(walk):
                for h in heads:
                    ls[w, h], l1m = _log_sigmoids(z_ref[slot, h])
                    if diag:
                        l1m = jnp.where(causal, l1m, 0.0)
                    hi, lo = _split_bf16(l1m)
                    tail[w, h] = _nn(hi, later) + _nn(lo, later) + cs[h]
                    cs[h] = cs[h] + jnp.sum(l1m, axis=1, keepdims=True)
            top = cs[0]
            for h in heads[1:]:
                top = jnp.maximum(top, cs[h])
            go = jnp.logical_and(left > 0, jnp.max(top) > DEAD_LOG2)
            for w, (j, slot, diag, counts) in enumerate(walk):
                keys = pl.ds(pl.multiple_of(j * t, t), t)
                for h in heads:
                    a = jnp.exp2(ls[w, h] + tail[w, h])
                    if diag:
                        a = jnp.where(causal, a, 0.0)
                    if counts is not None:
                        a = jnp.where(counts, a, 0.0)
                    pv = _nn(a.astype(BF16), vh_ref[h, keys, :])
                    if diag:
                        acc_ref[h] = pv
                    else:
                        acc_ref[h] += pv
            pl.when(go)(lambda: scores(*after))
            return (go, *cs)

        before = jnp.maximum(i - 1, 0)
        scores(i, 0)
        scores(before, 1)
        state = (jnp.int32(0), *tiles([(i, 0, True, None), (before, 1, False, i >= 1)], i - 1,
                                      (jnp.maximum(i - 2, 0), 0), [jnp.zeros((t, 1), F32)] * HEADS_PER_BLOCK))

        def step(state):
            jj = state[0]
            j = i - 2 - jj
            return (jj + 1, *tiles([(j, jj % 2, False, None)], j, (jnp.maximum(j - 1, 0), 1 - jj % 2), state[2:]))

        lax.while_loop(lambda s: s[1], step, state)
        out = acc_ref[0]
        for h in heads[1:]:
            out = out + acc_ref[h]
        o_ref[...] = out
        finish()

    return pl.pallas_call(
        body,
        name="attn_fwd",
        grid=(nblk, S // t),
        in_specs=[
            pl.BlockSpec((1, t, ATTN_LANES), lambda hp, i: (0, i, hp)),
            pl.BlockSpec((1, S, ATTN_LANES), lambda hp, i: (1, 0, hp)),
            pl.BlockSpec((1, S, ATTN_LANES), lambda hp, i: (2, 0, hp)),
        ] + [ANY] * n,
        out_specs=[pl.BlockSpec((t, ATTN_LANES), lambda hp, i: (i, hp))] + [ANY] * n,
        out_shape=[jax.ShapeDtypeStruct((S, D_ATTN), F32)] + _gather_shapes(shards),
        scratch_shapes=[pltpu.VMEM((HEADS_PER_BLOCK, S, ATTN_LANES), BF16),
                        pltpu.VMEM((HEADS_PER_BLOCK * ATTN_ROW_CHUNKS, t // ATTN_ROW_CHUNKS, ATTN_LANES), F32),
                        pltpu.VMEM((2, HEADS_PER_BLOCK * ATTN_ROW_CHUNKS, t // ATTN_ROW_CHUNKS, t), F32)]
        + (_gather_sems(n) if n else []),
        compiler_params=_params(),
    )(qkv, qkv, qkv, *shards)


def _out_proj(y_pool, y_attn, x, g_pool, g_attn, w_out, g2):
    S = x.shape[0]
    ts = ROW_TILE

    def body(yp_ref, ya_ref, x_ref, gp_ref, ga_ref, w_ref, g2_ref, mixed_ref, h1_ref, hn2_ref):
        yp = yp_ref[...]
        ya = ya_ref[...]
        mixed = jnp.concatenate([yp * _rms(yp) * gp_ref[...], ya * _rms(ya) * ga_ref[...]], axis=-1).astype(BF16)
        mixed_ref[...] = mixed
        h1 = x_ref[...] + _nn(mixed, w_ref[...])
        h1_ref[...] = h1
        hn2_ref[...] = (h1 * _rms(h1) * g2_ref[...]).astype(BF16)

    row = lambda w: pl.BlockSpec((ts, w), lambda i: (i, 0))
    vec = lambda w: pl.BlockSpec((1, w), lambda i: (0, 0))
    return pl.pallas_call(
        body,
        name="out_proj",
        grid=(S // ts,),
        in_specs=[row(D_POOL), row(D_ATTN), row(D_MODEL), vec(D_POOL), vec(D_ATTN),
                  pl.BlockSpec((D_MODEL, D_MODEL), lambda i: (0, 0)), vec(D_MODEL)],
        out_specs=[row(D_MODEL), row(D_MODEL), row(D_MODEL)],
        out_shape=[
            jax.ShapeDtypeStruct((S, D_MODEL), BF16),
            jax.ShapeDtypeStruct((S, D_MODEL), F32),
            jax.ShapeDtypeStruct((S, D_MODEL), BF16),
        ],
        compiler_params=_params(),
    )(y_pool, y_attn, x, g_pool, g_attn, w_out, g2)


def _mlp_fwd(hn2, h1, w_up, w_down, g_final, target):
    S = hn2.shape[0]
    ts = MLP_TILE
    nf = D_FF // 1024

    def body(hn2_ref, h1_ref, wu_ref, wd_ref, gf_ref, tg_ref, r_ref, dh2_ref, dh2b_ref, lsq_ref, dgf_ref):
        i = pl.program_id(0)
        hn2v = hn2_ref[...]
        acts = []
        for c in range(nf):
            r = jnp.maximum(_nn(hn2v, wu_ref[c]), 0.0)
            r_ref[:, c * 1024:(c + 1) * 1024] = r.astype(BF16)
            acts.append((r * r).astype(BF16))
        h2 = h1_ref[...] + _nn(jnp.concatenate(acts, axis=1), wd_ref[...])

        @pl.when(i == 0)
        def _():
            lsq_ref[...] = jnp.zeros_like(lsq_ref)
            dgf_ref[...] = jnp.zeros_like(dgf_ref)

        rf = _rms(h2)
        n = h2 * rf
        gf = gf_ref[...]
        e = n * gf - tg_ref[...]
        lsq_ref[...] += jnp.sum(e * e, axis=0, keepdims=True)
        dy = e * (1.0 / D_MODEL)
        dgf_ref[...] += jnp.sum(dy * n, axis=0, keepdims=True)
        dh2 = _rms_bwd(dy, n, rf, gf)
        dh2_ref[...] = dh2
        dh2b_ref[...] = dh2.astype(BF16)

    row = lambda w: pl.BlockSpec((ts, w), lambda i: (i, 0))
    vec = lambda w: pl.BlockSpec((1, w), lambda i: (0, 0))
    once = pl.Buffered(1)
    return pl.pallas_call(
        body,
        name="mlp_fwd",
        grid=(S // ts,),
        in_specs=[row(D_MODEL), row(D_MODEL),
                  pl.BlockSpec((nf, D_MODEL, 1024), lambda i: (0, 0, 0), pipeline_mode=once),
                  pl.BlockSpec((D_FF, D_MODEL), lambda i: (0, 0), pipeline_mode=once),
                  vec(D_MODEL), row(D_MODEL)],
        out_specs=[row(D_FF), row(D_MODEL), row(D_MODEL), vec(D_MODEL), vec(D_MODEL)],
        out_shape=[
            jax.ShapeDtypeStruct((S, D_FF), BF16),
            jax.ShapeDtypeStruct((S, D_MODEL), F32),
            jax.ShapeDtypeStruct((S, D_MODEL), BF16),
            jax.ShapeDtypeStruct((1, D_MODEL), F32),
            jax.ShapeDtypeStruct((1, D_MODEL), F32),
        ],
        compiler_params=_params(),
    )(hn2, h1, w_up, w_down.reshape(D_FF, D_MODEL), g_final, target)


def _mlp_bwd_dx(dh2b, r_act, w_down, w_up, h1, dh2, g2):
    S = h1.shape[0]
    ts = MLP_TILE
    nf = D_FF // 1024

    def body(dh2b_ref, r_ref, wd_ref, wu_ref, h1_ref, dh2_ref, g2_ref, dup_ref, dh1_ref, dg2_ref):
        i = pl.program_id(0)
        dh2b = dh2b_ref[...]
        dhn2 = None
        for c in range(nf):
            chunk = slice(c * 1024, (c + 1) * 1024)
            dup = (_nt(dh2b, wd_ref[c]) * (2.0 * r_ref[:, chunk].astype(F32))).astype(BF16)
            dup_ref[:, chunk] = dup
            part = _nt(dup, wu_ref[c])
            dhn2 = part if dhn2 is None else dhn2 + part

        @pl.when(i == 0)
        def _():
            dg2_ref[...] = jnp.zeros_like(dg2_ref)

        h1v = h1_ref[...]
        r2 = _rms(h1v)
        n2 = h1v * r2
        dg2_ref[...] += jnp.sum(dhn2 * n2, axis=0, keepdims=True)
        dh1_ref[...] = dh2_ref[...] + _rms_bwd(dhn2, n2, r2, g2_ref[...])

    row = lambda w: pl.BlockSpec((ts, w), lambda i: (i, 0))
    vec = lambda w: pl.BlockSpec((1, w), lambda i: (0, 0))
    once = pl.Buffered(1)
    return pl.pallas_call(
        body,
        name="mlp_bwd_dx",
        grid=(S // ts,),
        in_specs=[row(D_MODEL), row(D_FF),
                  pl.BlockSpec((nf, 1024, D_MODEL), lambda i: (0, 0, 0), pipeline_mode=once),
                  pl.BlockSpec((nf, D_MODEL, 1024), lambda i: (0, 0, 0), pipeline_mode=once),
                  row(D_MODEL), row(D_MODEL), vec(D_MODEL)],
        out_specs=[row(D_FF), row(D_MODEL), vec(D_MODEL)],
        out_shape=[
            jax.ShapeDtypeStruct((S, D_FF), BF16),
            jax.ShapeDtypeStruct((S, D_MODEL), F32),
            jax.ShapeDtypeStruct((1, D_MODEL), F32),
        ],
        compiler_params=_params(),
    )(dh2b, r_act, w_down, w_up, h1, dh2, g2)


def _tokens_tn(name, a, b, a_chunked, square_a, partials=()):
    S = a.shape[0]
    ts = min(DW_TOKEN_TILE, S)
    nf = D_FF // 1024
    n = len(partials)

    def body(*refs):
        a_ref, b_ref = refs[:2]
        o_ref = refs[2 + n]
        acc = refs[3 + 2 * n]
        tt = pl.program_id(1)
        finish = lambda: None
        if n:
            blk = pl.program_id(0)
            finish = _run_at(jnp.logical_and(blk == 0, tt == 0), None,
                             jnp.logical_and(blk == nf - 1, tt == pl.num_programs(1) - 1),
                             _reduce_phases(refs[2:2 + n], refs[3 + n:3 + 2 * n], *refs[4 + 2 * n:], hops=HOPS_DIAGONAL))
        av = a_ref[...]
        if square_a:
            af = av.astype(F32)
            av = (af * af).astype(BF16)
        part = _tn(av, b_ref[...])

        @pl.when(tt == 0)
        def _():
            acc[...] = part

        @pl.when(tt > 0)
        def _():
            acc[...] += part

        @pl.when(tt == pl.num_programs(1) - 1)
        def _():
            o_ref[0] = acc[...].astype(BF16)

        finish()

    whole = pl.BlockSpec((ts, 1024), lambda c, tt: (tt, 0))
    chunk = pl.BlockSpec((ts, 1024), lambda c, tt: (tt, c))
    return pl.pallas_call(
        body,
        name=name,
        grid=(nf, S // ts),
        in_specs=([chunk, whole] if a_chunked else [whole, chunk]) + [ANY] * n,
        out_specs=[pl.BlockSpec((1, 1024, 1024), lambda c, tt: (c, 0, 0))] + [ANY] * n,
        out_shape=[jax.ShapeDtypeStruct((nf, 1024, 1024), BF16)] + _reduce_shapes(partials),
        scratch_shapes=[pltpu.VMEM((1024, 1024), F32)] + (_reduce_sems(n) if n else []),
        compiler_params=_params(),
    )(a, b, *partials)


def _out_bwd(dh1, w_out, mixed, y_pool, y_attn, g_pool, g_attn, partials=(), received=()):
    S = dh1.shape[0]
    ts = ROW_TILE
    n = len(partials)

    def body(*refs):
        dh1_ref, w_ref, mixed_ref, yp_ref, ya_ref, gp_ref, ga_ref = refs[:7]
        dyp_ref, dya_ref, dw_ref, dgp_ref, dga_ref = refs[7 + 2 * n:12 + 2 * n]
        dw_acc = refs[12 + 3 * n]
        i = pl.program_id(0)
        finish = lambda: None
        if n:
            finish = _run_at(i == 0, None, i == pl.num_programs(0) - 1,
                             _reduce_phases(refs[7:7 + n], refs[12 + 2 * n:12 + 3 * n], *refs[13 + 3 * n:],
                                            hops=HOPS_SAME_CORE))
        dh1b = dh1_ref[...].astype(BF16)
        dmixed = _nt(dh1b, w_ref[...])
        dw = _tn(mixed_ref[...], dh1b)

        @pl.when(i == 0)
        def _():
            dw_acc[...] = dw
            dgp_ref[...] = jnp.zeros_like(dgp_ref)
            dga_ref[...] = jnp.zeros_like(dga_ref)

        @pl.when(i > 0)
        def _():
            dw_acc[...] += dw

        @pl.when(i == pl.num_programs(0) - 1)
        def _():
            dw_ref[...] = dw_acc[...].astype(BF16)

        for y_ref, g_ref, dy_ref, dg_ref, lanes in (
                (yp_ref, gp_ref, dyp_ref, dgp_ref, slice(0, D_POOL)),
                (ya_ref, ga_ref, dya_ref, dga_ref, slice(D_POOL, D_MODEL))):
            y = y_ref[...]
            r = _rms(y)
            nrm = y * r
            dm = dmixed[:, lanes]
            dg_ref[...] += jnp.sum(dm * nrm, axis=0, keepdims=True)
            dy_ref[...] = _rms_bwd(dm, nrm, r, g_ref[...])

        finish()

    row = lambda w: pl.BlockSpec((ts, w), lambda i: (i, 0))
    vec = lambda w: pl.BlockSpec((1, w), lambda i: (0, 0))
    full = pl.BlockSpec((D_MODEL, D_MODEL), lambda i: (0, 0))
    return pl.pallas_call(
        body,
        name="out_bwd",
        grid=(S // ts,),
        in_specs=[row(D_MODEL), full, row(D_MODEL), row(D_POOL), row(D_ATTN), vec(D_POOL), vec(D_ATTN)]
        + [ANY] * (2 * n),
        out_specs=[row(D_POOL), row(D_ATTN), full, vec(D_POOL), vec(D_ATTN)] + [ANY] * n,
        out_shape=[
            jax.ShapeDtypeStruct((S, D_POOL), F32),
            jax.ShapeDtypeStruct((S, D_ATTN), F32),
            jax.ShapeDtypeStruct((D_MODEL, D_MODEL), BF16),
            jax.ShapeDtypeStruct((1, D_POOL), F32),
            jax.ShapeDtypeStruct((1, D_ATTN), F32),
        ] + [jax.ShapeDtypeStruct(r.shape, r.dtype) for r in received],
        input_output_aliases={7 + n + a: 5 + a for a in range(n)},
        scratch_shapes=[pltpu.VMEM((D_MODEL, D_MODEL), F32)] + (_reduce_sems(n) if n else []),
        compiler_params=_params(),
    )(dh1, w_out, mixed, y_pool, y_attn, g_pool, g_attn, *partials, *received)


def _attn_bwd(qkv, o, do, partials=(), late=(), late_received=()):
    S = qkv.shape[1]
    t = ATTN_TILE
    n = len(partials)
    m = len(late)
    nblk = D_ATTN // ATTN_LANES
    outs_at = 5 + n + 2 * m
    scratch_at = outs_at + 3 + n + m

    def body(*refs):
        q_ref, k_ref, v_ref, o_ref, do_ref = refs[:5]
        dq_ref, dk_ref, dv_ref = refs[outs_at:outs_at + 3]
        kh_ref, dk_acc, dv_acc, dq_acc, z_ref, da_ref = refs[scratch_at:scratch_at + 6]
        hp = pl.program_id(0)
        i = pl.program_id(1)
        first = jnp.logical_and(hp == 0, i == 0)
        last = jnp.logical_and(hp == nblk - 1, i == pl.num_programs(1) - 1)
        finishes = []
        if n:
            finishes.append(_run_at(first, None, last, _reduce_phases(
                refs[5:5 + n], refs[outs_at + 3:outs_at + 3 + n], *refs[scratch_at + 6:scratch_at + 8])))
        if m:
            finishes.append(_run_at(first, None, last, _reduce_phases(
                refs[5 + n:5 + n + m], refs[outs_at + 3 + n:outs_at + 3 + n + m],
                *refs[scratch_at + 6 + (2 if n else 0):], hops=HOPS_REST)))

        def finish():
            for f in finishes:
                f()

        masks = _head_masks()

        @pl.when(i == 0)
        def _():
            kk = k_ref[0]
            for h in range(HEADS_PER_BLOCK):
                kh_ref[h] = jnp.where(masks[h], kk, jnp.zeros_like(kk))
            dk_acc[...] = jnp.zeros_like(dk_acc)
            dv_acc[...] = jnp.zeros_like(dv_acc)

        row, col = _tri_masks(t)
        later = (row > col).astype(BF16)
        from_s = (row >= col).astype(BF16)
        causal = col < row
        qs = q_ref[0]
        dob = do_ref[...].astype(BF16)
        d_all = dob.astype(F32) * o_ref[...]
        qh = [jnp.where(masks[h], qs, jnp.zeros_like(qs)) for h in range(HEADS_PER_BLOCK)]
        doh = [jnp.where(masks[h], dob, jnp.zeros_like(dob)) for h in range(HEADS_PER_BLOCK)]
        d_row = [jnp.sum(jnp.where(masks[h], d_all, 0.0), axis=1, keepdims=True) for h in range(HEADS_PER_BLOCK)]

        heads = range(HEADS_PER_BLOCK)

        def scores(j, slot):
            keys = pl.ds(pl.multiple_of(j * t, t), t)
            kj = k_ref[0, keys, :]
            vj = v_ref[0, keys, :]
            for h in heads:
                z_ref[slot, h] = _nt(qh[h], kj)
                da_ref[slot, h] = _nt(doh[h], vj)

        def tiles(walk, left, after, carry):
            nh = HEADS_PER_BLOCK
            c_l, c_g = list(carry[:nh]), list(carry[nh:])
            ls, tail, g, before = {}, {}, {}, {}
            for w, (j, slot, diag, counts) in enumerate(walk):
                for h in heads:
                    ls[w, h], l1m = _log_sigmoids(z_ref[slot, h])
                    if diag:
                        l1m = jnp.where(causal, l1m, 0.0)
                    hi, lo = _split_bf16(l1m)
                    tail[w, h] = _nn(hi, later) + _nn(lo, later) + c_l[h]
                    c_l[h] = c_l[h] + jnp.sum(l1m, axis=1, keepdims=True)
            top = c_l[0]
            for h in range(1, nh):
                top = jnp.maximum(top, c_l[h])
            go = jnp.logical_and(left > 0, jnp.max(top) > DEAD_LOG2)
            for w, (j, slot, diag, counts) in enumerate(walk):
                keys = pl.ds(pl.multiple_of(j * t, t), t)
                dv = None
                for h in heads:
                    a = jnp.exp2(ls[w, h] + tail[w, h])
                    if diag:
                        a = jnp.where(causal, a, 0.0)
                    if counts is not None:
                        a = jnp.where(counts, a, 0.0)
                    ab = a.astype(BF16)
                    g[w, h] = ab.astype(F32) * da_ref[slot, h]
                    ghi, glo = _split_bf16(g[w, h])
                    before[w, h] = d_row[h] - (_nn(ghi, from_s) + _nn(glo, from_s) + c_g[h])
                    c_g[h] = c_g[h] + jnp.sum(g[w, h], axis=1, keepdims=True)
                    part = _tn(ab, doh[h])
                    dv = part if dv is None else dv + part
                dv_acc[keys, :] += dv
            for w, (j, slot, diag, counts) in enumerate(walk):
                keys = pl.ds(pl.multiple_of(j * t, t), t)
                dk = None
                for h in heads:
                    beta = jnp.exp2(ls[w, h])
                    dz = g[w, h] * (1.0 - beta) - before[w, h] * beta
                    if diag:
                        dz = jnp.where(causal, dz, 0.0)
                    if counts is not None:
                        dz = jnp.where(counts, dz, 0.0)
                    dzb = dz.astype(BF16)
                    dqh = _nn(dzb, kh_ref[h, keys, :])
                    if diag:
                        dq_acc[h] = dqh
                    else:
                        dq_acc[h] += dqh
                    part = _tn(dzb, qh[h])
                    dk = part if dk is None else dk + part
                dk_acc[keys, :] += dk
            pl.when(go)(lambda: scores(*after))
            return (go, *c_l, *c_g)

        prev = jnp.maximum(i - 1, 0)
        scores(i, 0)
        scores(prev, 1)
        state = (jnp.int32(0), *tiles([(i, 0, True, None), (prev, 1, False, i >= 1)], i - 1,
                                      (jnp.maximum(i - 2, 0), 0),
                                      [jnp.zeros((t, 1), F32)] * (2 * HEADS_PER_BLOCK)))

        def step(state):
            jj = state[0]
            j = i - 2 - jj
            return (jj + 1, *tiles([(j, jj % 2, False, None)], j, (jnp.maximum(j - 1, 0), 1 - jj % 2), state[2:]))

        lax.while_loop(lambda s: s[1], step, state)
        dq = dq_acc[0]
        for h in range(1, HEADS_PER_BLOCK):
            dq = dq + dq_acc[h]
        dq_ref[...] = (dq * ATTN_SCALE).astype(BF16)

        @pl.when(i == pl.num_programs(1) - 1)
        def _():
            dk_ref[...] = (dk_acc[...] * (1.0 / LOG2E)).astype(BF16)
            dv_ref[...] = dv_acc[...].astype(BF16)

        finish()

    qtile = pl.BlockSpec((t, ATTN_LANES), lambda hp, i: (i, hp))
    whole = pl.BlockSpec((S, ATTN_LANES), lambda hp, i: (0, hp))
    return pl.pallas_call(
        body,
        name="attn_bwd",
        grid=(nblk, S // t),
        in_specs=[
            pl.BlockSpec((1, t, ATTN_LANES), lambda hp, i: (0, i, hp)),
            pl.BlockSpec((1, S, ATTN_LANES), lambda hp, i: (1, 0, hp)),
            pl.BlockSpec((1, S, ATTN_LANES), lambda hp, i: (2, 0, hp)),
            qtile, qtile,
        ] + [ANY] * (n + 2 * m),
        out_specs=[qtile, whole, whole] + [ANY] * (n + m),
        out_shape=[jax.ShapeDtypeStruct((S, D_ATTN), BF16)] * 3 + _reduce_shapes(partials)
        + [jax.ShapeDtypeStruct(r.shape, r.dtype) for r in late_received],
        input_output_aliases={5 + n + m + a: 3 + n + a for a in range(m)},
        scratch_shapes=[
            pltpu.VMEM((HEADS_PER_BLOCK, S, ATTN_LANES), BF16),
            pltpu.VMEM((S, ATTN_LANES), F32),
            pltpu.VMEM((S, ATTN_LANES), F32),
            pltpu.VMEM((HEADS_PER_BLOCK, t, ATTN_LANES), F32),
            pltpu.VMEM((2, HEADS_PER_BLOCK, t, t), F32),
            pltpu.VMEM((2, HEADS_PER_BLOCK, t, t), F32),
        ] + (_reduce_sems(n) if n else []) + (_reduce_sems(m) if m else []),
        compiler_params=_params(),
    )(qkv, qkv, qkv, o, do, *partials, *late, *late_received)


def _pool_bwd_w(u, dyp, pool_w, pool_scale):
    S = u.shape[0]
    ts = ROW_TILE
    hb = ts // POOL_HALO

    def body(u_ref, halo_ref, dy_ref, w_ref, s_ref, dp_ref, dw_ref, ds_ref):
        i = pl.program_id(0)
        halo = jnp.where(i == 0, 0.0, halo_ref[...])
        pooled = _pooled(u_ref[...], halo, i * ts)

        @pl.when(i == 0)
        def _():
            dw_ref[...] = jnp.zeros_like(dw_ref)
            ds_ref[...] = jnp.zeros_like(ds_ref)

        for g in range(len(POOL_WINDOWS)):
            lanes = slice(g * POOL_GROUP_DIM, (g + 1) * POOL_GROUP_DIM)
            pg = pooled[g].astype(BF16)
            dy = dy_ref[:, lanes]
            ds_ref[:, lanes] += jnp.sum(dy * _nn(pg, w_ref[g]), axis=0, keepdims=True)
            dmapped = (dy * s_ref[:, lanes]).astype(BF16)
            dp_ref[:, lanes] = _nt(dmapped, w_ref[g])
            dw_ref[g] += _tn(pg, dmapped)

    row = pl.BlockSpec((ts, D_POOL), lambda i: (i, 0))
    vec = pl.BlockSpec((1, D_POOL), lambda i: (0, 0))
    wspec = pl.BlockSpec((4, POOL_GROUP_DIM, POOL_GROUP_DIM), lambda i: (0, 0, 0))
    return pl.pallas_call(
        body,
        name="pool_bwd_w",
        grid=(S // ts,),
        in_specs=[row, pl.BlockSpec((POOL_HALO, D_POOL), lambda i: (jnp.maximum(i * hb - 1, 0), 0)),
                  row, wspec, vec],
        out_specs=[row, wspec, vec],
        out_shape=[
            jax.ShapeDtypeStruct((S, D_POOL), F32),
            jax.ShapeDtypeStruct((4, POOL_GROUP_DIM, POOL_GROUP_DIM), F32),
            jax.ShapeDtypeStruct((1, D_POOL), F32),
        ],
        compiler_params=_params(),
    )(u, u, dyp, pool_w, pool_scale)


def _pool_bwd_u(dpooled):
    S = dpooled.shape[0]
    ts = ROW_TILE
    hb = ts // POOL_HALO
    last = S // ts - 1

    def body(dp_ref, halo_ref, du_ref):
        i = pl.program_id(0)
        dp = dp_ref[...]
        halo = jnp.where(i == last, 0.0, halo_ref[...])
        inv = _pool_counts(i * ts, ts)
        n = ts + POOL_HALO
        for g, w in enumerate(POOL_WINDOWS):
            lanes = slice(g * POOL_GROUP_DIM, (g + 1) * POOL_GROUP_DIM)
            dg = dp[:, lanes]
            acc = jnp.concatenate([dg * inv[g], halo[:, lanes] * (1.0 / w)], axis=0)
            shift = 1
            while shift < w:
                acc = acc + pltpu.roll(acc, n - shift, axis=0)
                shift *= 2
            du_ref[:, lanes] = (acc[:ts, :] - dg).astype(BF16)

    return pl.pallas_call(
        body,
        name="pool_bwd_u",
        grid=(S // ts,),
        in_specs=[pl.BlockSpec((ts, D_POOL), lambda i: (i, 0)),
                  pl.BlockSpec((POOL_HALO, D_POOL), lambda i: (jnp.minimum((i + 1) * hb, (last + 1) * hb - 1), 0))],
        out_specs=pl.BlockSpec((ts, D_POOL), lambda i: (i, 0)),
        out_shape=jax.ShapeDtypeStruct((S, D_POOL), BF16),
        compiler_params=_params(),
    )(dpooled, dpooled)


def _in_bwd_dw(hn, dprojs, late=(), late_received=()):
    S = hn.shape[0]
    ts = min(DW_TOKEN_TILE, S)
    m = len(late)

    def body(*refs):
        hn_ref, du_ref, dq_ref, dk_ref, dv_ref = refs[:5]
        o_ref = refs[5 + 2 * m]
        acc = refs[6 + 3 * m]
        j = pl.program_id(0)
        tt = pl.program_id(1)
        finish = lambda: None
        if m:
            finish = _run_at(jnp.logical_and(j == 0, tt == 0), None,
                             jnp.logical_and(j == 3, tt == pl.num_programs(1) - 1),
                             _reduce_phases(refs[5:5 + m], refs[6 + 2 * m:6 + 3 * m], *refs[7 + 3 * m:],
                                            hops=HOPS_REST))
        for k, dp_ref in enumerate((du_ref, dq_ref, dk_ref, dv_ref)):
            @pl.when(j == k)
            def _(dp_ref=dp_ref):
                part = _tn(hn_ref[...], dp_ref[...])

                @pl.when(tt == 0)
                def _():
                    acc[...] = part

                @pl.when(tt > 0)
                def _():
                    acc[...] += part

        @pl.when(tt == pl.num_programs(1) - 1)
        def _():
            o_ref[0] = acc[...].astype(BF16)

        finish()

    def taken_at(k):
        return lambda j, tt: (jnp.where(j == k, tt, 0), 0)

    return pl.pallas_call(
        body,
        name="in_bwd_dw",
        grid=(4, S // ts),
        in_specs=[pl.BlockSpec((ts, D_MODEL), lambda j, tt: (tt, 0))]
        + [pl.BlockSpec((ts, 512), taken_at(k)) for k in range(4)] + [ANY] * (2 * m),
        out_specs=[pl.BlockSpec((1, D_MODEL, 512), lambda j, tt: (j, 0, 0))] + [ANY] * m,
        out_shape=[jax.ShapeDtypeStruct((4, D_MODEL, 512), BF16)]
        + [jax.ShapeDtypeStruct(r.shape, r.dtype) for r in late_received],
        input_output_aliases={5 + m + a: 1 + a for a in range(m)},
        scratch_shapes=[pltpu.VMEM((D_MODEL, 512), F32)] + (_reduce_sems(m) if m else []),
        compiler_params=_params(),
    )(hn, *dprojs, *late, *late_received)


def _in_bwd_dx(du, dq, dk, dv, w_in, x, dh1, g1, partials=()):
    S = x.shape[0]
    ts = ROW_TILE
    n = len(partials)

    def body(*refs):
        du_ref, dq_ref, dk_ref, dv_ref, w_ref, x_ref, dh1_ref, g_ref = refs[:8]
        dx_ref, dg_ref = refs[8 + n:10 + n]
        i = pl.program_id(0)
        finish = lambda: None
        if n:
            finish = _run_at(i == 0, None, i == pl.num_programs(0) - 1,
                             _reduce_phases(refs[8:8 + n], refs[10 + n:10 + 2 * n], *refs[10 + 2 * n:]))

        @pl.when(i == 0)
        def _():
            dg_ref[...] = jnp.zeros_like(dg_ref)

        dhn = None
        for j, dp_ref in enumerate((du_ref, dq_ref, dk_ref, dv_ref)):
            part = _nt(dp_ref[...], w_ref[j])
            dhn = part if dhn is None else dhn + part
        xv = x_ref[...]
        r1 = _rms(xv)
        n1 = xv * r1
        dg_ref[...] += jnp.sum(dhn * n1, axis=0, keepdims=True)
        dx_ref[...] = dh1_ref[...] + _rms_bwd(dhn, n1, r1, g_ref[...])
        finish()

    row = lambda w: pl.BlockSpec((ts, w), lambda i: (i, 0))
    vec = pl.BlockSpec((1, D_MODEL), lambda i: (0, 0))
    wspec = pl.BlockSpec((4, D_MODEL, 512), lambda i: (0, 0, 0), pipeline_mode=pl.Buffered(1))
    return pl.pallas_call(
        body,
        name="in_bwd_dx",
        grid=(S // ts,),
        in_specs=[row(512), row(512), row(512), row(512), wspec, row(D_MODEL), row(D_MODEL), vec] + [ANY] * n,
        out_specs=[row(D_MODEL), vec] + [ANY] * n,
        out_shape=[
            jax.ShapeDtypeStruct((S, D_MODEL), F32),
            jax.ShapeDtypeStruct((1, D_MODEL), F32),
        ] + _reduce_shapes(partials),
        scratch_shapes=_reduce_sems(n) if n else [],
        compiler_params=_params(),
    )(du, dq, dk, dv, w_in, x, dh1, g1, *partials)


def _pieces(g):
    return g.reshape(N_CHIPS, 2, -1, g.shape[-1])


def _local_step(x, target, w_in, pool_w, small, full=None, shards=None):
    spread = shards is not None
    if spread:
        hn, u, qkv, w_down = _in_proj(x, small["norm1_g"], w_in, shards[2:])
        w_down = w_down.reshape(N_CHIPS, 1024, D_MODEL)
    else:
        hn, u, qkv = _in_proj(x, small["norm1_g"], w_in)
        w_out, w_up, w_down = full
    y_pool = _pool_fwd(u, pool_w, small["pool_scale"])
    if spread:
        y_attn, w_out, w_up = _attn_fwd(qkv, shards[:2])
        w_out = w_out.reshape(D_MODEL, D_MODEL)
        w_up = w_up.reshape(N_CHIPS, D_MODEL, 1024)
    else:
        (y_attn,) = _attn_fwd(qkv)
    mixed, h1, hn2 = _out_proj(y_pool, y_attn, x, small["pool_out_g"], small["attn_out_g"], w_out, small["norm2_g"])
    r_act, dh2, dh2b, lsq, dgf = _mlp_fwd(hn2, h1, w_up, w_down, small["final_g"], target)

    dup, dh1, dg2 = _mlp_bwd_dx(dh2b, r_act, w_down, w_up, h1, dh2, small["norm2_g"])
    (dw_down,) = _tokens_tn("mlp_bwd_dw_down", r_act, dh2b, True, True)
    out_args = (dh1, w_out, mixed, y_pool, y_attn, small["pool_out_g"], small["attn_out_g"])
    if spread:
        dw_up, got_down = _tokens_tn("mlp_bwd_dw_up", hn2, dup, False, False, [_pieces(dw_down)])
        dyp, dya, dw_out, dgp, dga, got_down = _out_bwd(*out_args, [_pieces(dw_down)], [got_down])
        dq, dk, dv, got_out, got_up = _attn_bwd(qkv, y_attn, dya, [_pieces(dw_out), _pieces(dw_up)])
    else:
        (dw_up,) = _tokens_tn("mlp_bwd_dw_up", hn2, dup, False, False)
        dyp, dya, dw_out, dgp, dga = _out_bwd(*out_args)
        dq, dk, dv = _attn_bwd(qkv, y_attn, dya)
    dpooled, dpool_w, dpool_scale = _pool_bwd_w(u, dyp, pool_w, small["pool_scale"])
    du = _pool_bwd_u(dpooled)
    if spread:
        dw_in, got_down = _in_bwd_dw(hn, (du, dq, dk, dv), [_pieces(dw_down)], [got_down])
    else:
        (dw_in,) = _in_bwd_dw(hn, (du, dq, dk, dv))
    if spread:
        dx, dg1, got_in = _in_bwd_dx(du, dq, dk, dv, w_in, x, dh1, small["norm1_g"], [_pieces(dw_in)])
    else:
        dx, dg1 = _in_bwd_dx(du, dq, dk, dv, w_in, x, dh1, small["norm1_g"])

    big = {"w_in": dw_in, "w_out": dw_out, "w_up": dw_up, "w_down": dw_down}
    if spread:
        big["received"] = {"w_in": got_in, "w_out": got_out, "w_up": got_up, "w_down": got_down}
    little = {"norm1_g": dg1, "pool_w": dpool_w, "pool_scale": dpool_scale, "pool_out_g": dgp,
              "attn_out_g": dga, "norm2_g": dg2, "final_g": dgf, "loss_sq": lsq}
    return dx, big, little


def _place():
    x, y, c = lax.axis_index("x"), lax.axis_index("y"), lax.axis_index("c")
    other_chips = [(1 - x, y), (x, 1 - y), (1 - x, 1 - y)]
    return x, y, c, other_chips


def _chip_index(chip):
    return 2 * chip[0] + chip[1]


def _gather_shapes(shards):
    return [jax.ShapeDtypeStruct((N_CHIPS,) + s.shape, s.dtype) for s in shards]


def _gather_sems(n):
    return [pltpu.SemaphoreType.DMA((n, 7)), pltpu.SemaphoreType.DMA((n, 7))]


def _gather_phases(ins, outs, send_sems, recv_sems):
    n = len(ins)
    x, y, c, chips = _place()
    me = _chip_index((x, y))
    sibling = (x, y, 1 - c)

    def copy(a, k, chip_idx, half, to, src=None):
        dst = outs[a].at[chip_idx, half]
        return pltpu.make_async_remote_copy(
            src_ref=dst if src is None else src, dst_ref=dst,
            send_sem=send_sems.at[a, k], recv_sem=recv_sems.at[a, k],
            device_id=to, device_id_type=MESH)

    def own_chip(a, to):
        return pltpu.make_async_remote_copy(
            src_ref=ins[a], dst_ref=outs[a].at[me],
            send_sem=send_sems.at[a, 0], recv_sem=recv_sems.at[a, 0],
            device_id=to, device_id_type=MESH)

    def first(a):
        return [own_chip(a, sibling)] + [
            copy(a, 1 + j, me, c, (*chip, c), src=ins[a].at[c]) for j, chip in enumerate(chips)]

    def passed(a, j):
        return copy(a, 4 + j, _chip_index(chips[j]), c, sibling)

    def start():
        for a in range(n):
            for cp in first(a):
                cp.start()

    def forward():
        for a in range(n):
            for j, chip in enumerate(chips):
                copy(a, 1 + j, _chip_index(chip), c, (x, y, c)).wait_recv()
                passed(a, j).start()

    def finish():
        for a in range(n):
            own_chip(a, (x, y, c)).wait_recv()
            for j, chip in enumerate(chips):
                copy(a, 4 + j, _chip_index(chip), 1 - c, (x, y, c)).wait_recv()
        for a in range(n):
            for cp in first(a):
                cp.wait_send()
            for j in range(len(chips)):
                passed(a, j).wait_send()

    return start, forward, finish


def _gather_weights(shards):
    n = len(shards)

    def body(*refs):
        start, forward, finish = _gather_phases(refs[:n], refs[n:2 * n], *refs[2 * n:])
        start()
        forward()
        finish()

    return pl.pallas_call(
        body,
        name="gather_weights",
        in_specs=[ANY] * n,
        out_specs=[ANY] * n,
        out_shape=_gather_shapes(shards),
        scratch_shapes=_gather_sems(n),
    )(*shards)


def _reduce_shapes(partials):
    return [jax.ShapeDtypeStruct((N_DEV,) + p.shape[2:], p.dtype) for p in partials]


def _reduce_sems(n):
    return [pltpu.SemaphoreType.DMA((n, N_DEV)), pltpu.SemaphoreType.DMA((n, N_DEV))]


def _reduce_phases(ins, outs, send_sems, recv_sems, hops=None):
    n = len(ins)
    x, y, c, _ = _place()
    me = 4 * x + 2 * y + c

    def to_peer(a, k):
        return pltpu.make_async_remote_copy(
            src_ref=ins[a].at[k // 2, k % 2], dst_ref=outs[a].at[me],
            send_sem=send_sems.at[a, k], recv_sem=recv_sems.at[a, me],
            device_id=(k // 4, (k // 2) % 2, k % 2), device_id_type=MESH)

    def from_peer(a, k):
        return pltpu.make_async_remote_copy(
            src_ref=ins[a].at[k // 2, k % 2], dst_ref=outs[a].at[k],
            send_sem=send_sems.at[a, k], recv_sem=recv_sems.at[a, k],
            device_id=(x, y, c), device_id_type=MESH)

    def taken(k):
        if hops is None:
            return k != me
        hit = (k ^ hops[0]) == me
        for d in hops[1:]:
            hit = jnp.logical_or(hit, (k ^ d) == me)
        return hit

    def start():
        for a in range(n):
            for k in range(N_DEV):
                @pl.when(taken(k))
                def _(a=a, k=k):
                    to_peer(a, k).start()

    def finish():
        for a in range(n):
            for k in range(N_DEV):
                @pl.when(taken(k))
                def _(a=a, k=k):
                    from_peer(a, k).wait_recv()
                    to_peer(a, k).wait_send()

    return start, finish


def _sum_pieces(where, parts, own):
    _, h, cols = parts.shape
    hb = min(h, 256)

    def body(where_ref, *refs):
        me = where_ref[0]
        acc = None
        for k in range(N_DEV):
            piece = jnp.where(k == me, refs[N_DEV][0, 0], refs[k][0]).astype(F32)
            acc = piece if acc is None else acc + piece
        refs[N_DEV + 1][0] = acc

    def sent_by(k):
        return lambda r, w: (jnp.where(w[0] == k, (k + 1) % N_DEV, k), r, 0)

    return pl.pallas_call(
        body,
        name="sum_pieces",
        grid_spec=pltpu.PrefetchScalarGridSpec(
            num_scalar_prefetch=1,
            grid=(h // hb,),
            in_specs=[pl.BlockSpec((1, hb, cols), sent_by(k)) for k in range(N_DEV)]
            + [pl.BlockSpec((1, 1, hb, cols), lambda r, w: (w[1], w[2], r, 0))],
            out_specs=pl.BlockSpec((1, hb, cols), lambda r, w: (w[2], r, 0)),
        ),
        out_shape=jax.ShapeDtypeStruct((2, h, cols), F32),
    )(where, *([parts] * N_DEV), own)


def _join_halves(halves):
    n = len(halves)

    def body(*refs):
        outs = refs[n:2 * n]
        send_sems, recv_sems = refs[2 * n:]
        x, y, c, _ = _place()
        sends = [
            pltpu.make_async_remote_copy(
                src_ref=outs[a].at[c], dst_ref=outs[a].at[c],
                send_sem=send_sems.at[a], recv_sem=recv_sems.at[a],
                device_id=(x, y, 1 - c), device_id_type=MESH)
            for a in range(n)]
        for cp in sends:
            cp.start()
        for a in range(n):
            pltpu.make_async_remote_copy(
                src_ref=outs[a].at[c], dst_ref=outs[a].at[1 - c],
                send_sem=send_sems.at[a], recv_sem=recv_sems.at[a],
                device_id=(x, y, c), device_id_type=MESH).wait_recv()
        for cp in sends:
            cp.wait_send()

    return pl.pallas_call(
        body,
        name="join_halves",
        in_specs=[ANY] * n,
        out_specs=[ANY] * n,
        out_shape=[jax.ShapeDtypeStruct(s.shape, s.dtype) for s in halves],
        input_output_aliases={a: a for a in range(n)},
        scratch_shapes=[pltpu.SemaphoreType.DMA((n,)), pltpu.SemaphoreType.DMA((n,))],
    )(*halves)


def _adamw(w, g, m, v):
    m = ADAM_B1 * m + (1.0 - ADAM_B1) * g
    v = ADAM_B2 * v + (1.0 - ADAM_B2) * jnp.square(g)
    m_hat = m / (1.0 - ADAM_B1 ** ADAM_STEP)
    v_hat = v / (1.0 - ADAM_B2 ** ADAM_STEP)
    delta = -ADAM_LR * (m_hat / (jnp.sqrt(v_hat) + ADAM_EPS) + ADAM_WD * w)
    return delta, m, v


def _adamw_big(w, g, m, v):
    rows, cols = w.shape
    rb = min(rows, 256)

    def body(w_ref, g_ref, m_ref, v_ref, d_ref, mo_ref, vo_ref):
        d_ref[...], mo_ref[...], vo_ref[...] = _adamw(w_ref[...], g_ref[...], m_ref[...], v_ref[...])

    blk = pl.BlockSpec((rb, cols), lambda r: (r, 0))
    return pl.pallas_call(
        body,
        name="adamw_big",
        grid=(rows // rb,),
        in_specs=[blk] * 4,
        out_specs=[blk] * 3,
        out_shape=[jax.ShapeDtypeStruct(w.shape, F32)] * 3,
    )(w, g, m, v)


SMALL_ORDER = ("pool_w", "norm1_g", "pool_scale", "pool_out_g", "attn_out_g", "norm2_g", "final_g")
SUBLANES = 8


def _pack(parts):
    rows = []
    for p in parts:
        p = p.reshape(-1, LANES)
        pad = (-p.shape[0]) % SUBLANES
        if pad:
            p = jnp.pad(p, ((0, pad), (0, 0)))
        rows.append(p)
    return jnp.concatenate(rows, axis=0)


def _unpack(slab, shapes):
    out, r = [], 0
    for shp in shapes:
        size = 1
        for d in shp:
            size *= d
        nrow = size // LANES
        out.append(slab[r:r + nrow].reshape(shp))
        r += nrow + (-nrow) % SUBLANES
    return out


def _small_step(partials, w, m, v, loss_rows):
    rows = partials.shape[0]

    def body(p_ref, w_ref, m_ref, v_ref, g_ref, d_ref, mo_ref, vo_ref, loss_ref, buf, send_sems, recv_sems):
        x, y, c, _ = _place()
        me = 4 * x + 2 * y + c
        for k in range(N_DEV):
            @pl.when(k != me)
            def _(k=k):
                pltpu.make_async_remote_copy(
                    src_ref=p_ref, dst_ref=buf.at[me],
                    send_sem=send_sems.at[k], recv_sem=recv_sems.at[me],
                    device_id=(k // 4, (k // 2) % 2, k % 2), device_id_type=MESH).start()
        buf[me] = p_ref[...]
        for k in range(N_DEV):
            @pl.when(k != me)
            def _(k=k):
                pltpu.make_async_remote_copy(
                    src_ref=p_ref, dst_ref=buf.at[k],
                    send_sem=send_sems.at[k], recv_sem=recv_sems.at[k],
                    device_id=(x, y, c), device_id_type=MESH).wait()
        g = buf[0]
        for k in range(1, N_DEV):
            g = g + buf[k]
        g_ref[...] = g
        d_ref[...], mo_ref[...], vo_ref[...] = _adamw(w_ref[...], g, m_ref[...], v_ref[...])
        loss = (0.5 / D_MODEL) * jnp.sum(g[rows - loss_rows:, :])
        loss_ref[...] = jnp.full(loss_ref.shape, loss, F32)

    vm = pl.BlockSpec(memory_space=pltpu.VMEM)
    slab = jax.ShapeDtypeStruct((rows, LANES), F32)
    return pl.pallas_call(
        body,
        name="small_step",
        in_specs=[vm] * 4,
        out_specs=[vm] * 5,
        out_shape=[slab, slab, slab, slab, jax.ShapeDtypeStruct((SUBLANES, LANES), F32)],
        scratch_shapes=[pltpu.VMEM((N_DEV, rows, LANES), F32),
                        pltpu.SemaphoreType.DMA((N_DEV,)), pltpu.SemaphoreType.DMA((N_DEV,))],
    )(partials, w, m, v)


BIG_ORDER = ("w_in", "w_out", "w_up", "w_down")
WEIGHT_ORDER = ("norm1_g", "w_in", "pool_w", "pool_scale", "pool_out_g", "attn_out_g", "w_out", "norm2_g",
                "w_up", "w_down", "final_g")


def _halves(a):
    return a.reshape(2, a.shape[0] // 2, a.shape[1])


def kernel(x, norm1_g, w_in, pool_w, pool_scale, pool_out_g, attn_out_g, w_out, norm2_g, w_up, w_down, final_g, loss_target, m_norm1_g, m_w_in, m_pool_w, m_pool_scale, m_pool_out_g, m_attn_out_g, m_w_out, m_norm2_g, m_w_up, m_w_down, m_final_g, v_norm1_g, v_w_in, v_pool_w, v_pool_scale, v_pool_out_g, v_attn_out_g, v_w_out, v_norm2_g, v_w_up, v_w_down, v_final_g):
    w = dict(norm1_g=norm1_g, w_in=w_in, pool_w=pool_w, pool_scale=pool_scale, pool_out_g=pool_out_g,
             attn_out_g=attn_out_g, w_out=w_out, norm2_g=norm2_g, w_up=w_up, w_down=w_down, final_g=final_g)
    m = dict(norm1_g=m_norm1_g, w_in=m_w_in, pool_w=m_pool_w, pool_scale=m_pool_scale, pool_out_g=m_pool_out_g,
             attn_out_g=m_attn_out_g, w_out=m_w_out, norm2_g=m_norm2_g, w_up=m_w_up, w_down=m_w_down,
             final_g=m_final_g)
    v = dict(norm1_g=v_norm1_g, w_in=v_w_in, pool_w=v_pool_w, pool_scale=v_pool_scale, pool_out_g=v_pool_out_g,
             attn_out_g=v_attn_out_g, w_out=v_w_out, norm2_g=v_norm2_g, w_up=v_w_up, w_down=v_w_down,
             final_g=v_final_g)

    shards = {n: _halves(w[n].astype(BF16)) for n in BIG_ORDER}
    (w_in_g,) = _gather_weights([shards["w_in"]])
    small = {n: w[n].reshape(1, -1) for n in ("norm1_g", "pool_scale", "pool_out_g", "attn_out_g", "norm2_g", "final_g")}
    dx, big, little = _local_step(
        x[0], loss_target[0], w_in_g.reshape(N_CHIPS, D_MODEL, 512), pool_w.astype(BF16), small,
        shards=[shards["w_out"], shards["w_up"], shards["w_down"]])

    grads, deltas, new_m, new_v = {}, {}, {}, {}
    loss_rows = D_MODEL // LANES
    slab_g = _pack([little[n] for n in SMALL_ORDER] + [little["loss_sq"]])
    zeros = jnp.zeros((loss_rows, LANES), F32)
    slab_w = _pack([w[n] for n in SMALL_ORDER] + [zeros])
    slab_m = _pack([m[n] for n in SMALL_ORDER] + [zeros])
    slab_v = _pack([v[n] for n in SMALL_ORDER] + [zeros])
    received = big.pop("received")
    g_s, d_s, m_s, v_s, loss = _small_step(slab_g, slab_w, slab_m, slab_v, loss_rows)
    shapes = [w[n].shape for n in SMALL_ORDER]
    for slab, dst in ((g_s, grads), (d_s, deltas), (m_s, new_m), (v_s, new_v)):
        for n, val in zip(SMALL_ORDER, _unpack(slab, shapes)):
            dst[n] = val

    xi, yi, ci = lax.axis_index("x"), lax.axis_index("y"), lax.axis_index("c")
    where = jnp.stack([4 * xi + 2 * yi + ci, 2 * xi + yi, ci]).astype(jnp.int32)
    full = _join_halves([_sum_pieces(where, received[n], _pieces(big[n])) for n in BIG_ORDER])
    for n, g in zip(BIG_ORDER, full):
        grads[n] = g.reshape(w[n].shape)
        deltas[n], new_m[n], new_v[n] = _adamw_big(w[n], grads[n], m[n], v[n])

    return (loss[0, 0], dx[None], *[grads[n] for n in WEIGHT_ORDER], *[deltas[n] for n in WEIGHT_ORDER],
            *[new_m[n] for n in WEIGHT_ORDER], *[new_v[n] for n in WEIGHT_ORDER])
```
